```python
import jax, jax.numpy as jnp
from jax import lax
import numpy as np

D_MODEL = 4096
BATCH = 4
SEQ = 2048
DEPTH = 1
DEC_BATCH = 128
DEC_SEQ = 1
PAST_LEN = 16384
PAGE_SIZE = 128

POOL_WIDTH = D_MODEL // 2
POOL_WINDOWS = (2, 4, 8, 16)
N_POOL_GROUPS = len(POOL_WINDOWS)
POOL_GROUP = POOL_WIDTH // N_POOL_GROUPS
POOL_BUF = max(POOL_WINDOWS) - 1
RWKV_WIDTH = D_MODEL // 2
RWKV_HEAD_DIM = 64
RWKV_HEADS = RWKV_WIDTH // RWKV_HEAD_DIM
DECAY_LORA = max(32, int(round(RWKV_WIDTH ** 0.5 * 1.8 / 32)) * 32)
AAA_LORA = max(32, int(round(RWKV_WIDTH ** 0.5 * 1.8 / 32)) * 32)
SHIFT_WIDTH = 3 * RWKV_WIDTH + DECAY_LORA + AAA_LORA
GN_EPS = 64e-5
N_MEM = 256
MEM_HEADS = 4
MEM_WIDTH = 3 * D_MODEL // 8
MEM_HEAD_DIM = MEM_WIDTH // MEM_HEADS
IN_SPLITS = (POOL_WIDTH, POOL_WIDTH, SHIFT_WIDTH, RWKV_WIDTH, MEM_WIDTH, MEM_WIDTH, 3 * D_MODEL)
IN_COLS = sum(IN_SPLITS)
DEEPNORM_ALPHA = (2.0 * DEPTH) ** 0.25
DEEPNORM_BETA = (8.0 * DEPTH) ** -0.25
LN_EPS = 1e-5

kernel_name = "gated_pool_rwkv7_memattn_decoder_step"


def _split(z, sizes):
    out, start = [], 0
    for s in sizes:
        out.append(z[..., start:start + s])
        start += s
    return out


def _layernorm(x, g, b):
    xf = x.astype(jnp.float32)
    mu = jnp.mean(xf, -1, keepdims=True)
    var = jnp.mean(jnp.square(xf - mu), -1, keepdims=True)
    return (xf - mu) * lax.rsqrt(var + LN_EPS) * g + b


def _pool_branch(u, buf, pos0, pool_w, pool_scale):
    n, t, _ = u.shape
    uf = jnp.concatenate([buf.astype(jnp.float32), u.astype(jnp.float32)], axis=1)
    c = jnp.pad(jnp.cumsum(uf, axis=1), ((0, 0), (1, 0), (0, 0)))
    end = c[:, POOL_BUF + 1:]
    pos = pos0 + jnp.arange(t)
    means = []
    for g, win in enumerate(POOL_WINDOWS):
        sl = slice(g * POOL_GROUP, (g + 1) * POOL_GROUP)
        start = c[:, POOL_BUF + 1 - win:POOL_BUF + 1 - win + t, sl]
        count = jnp.minimum(pos + 1, win).astype(jnp.float32)[None, :, None]
        means.append((end[..., sl] - start) / count)
    pooled = (jnp.concatenate(means, -1) - uf[:, POOL_BUF:]).reshape(n, t, N_POOL_GROUPS, POOL_GROUP)
    mixed = jnp.einsum('btgc,gcd->btgd', pooled, pool_w.astype(jnp.float32)).reshape(n, t, POOL_WIDTH)
    return mixed * pool_scale, uf[:, -POOL_BUF:]


def _rwkv_branch(sh, shift_buf, state0, mu, w0, w2, a0, a2, k_k, k_a, r_k, ln_w, ln_b):
    f32 = jnp.float32
    n, t, _ = sh.shape
    sh = sh.astype(f32)
    prev = jnp.concatenate([shift_buf.astype(f32), sh[:, :-1]], axis=1)
    mixed = sh + (prev - sh) * mu
    r, k, v, wd, ad = _split(mixed, (RWKV_WIDTH, RWKV_WIDTH, RWKV_WIDTH, DECAY_LORA, AAA_LORA))
    w_log = -jax.nn.softplus(-(w0 + jnp.tanh(wd) @ w2)) - 0.5
    decay = jnp.exp(-jnp.exp(w_log))
    a = jax.nn.sigmoid(a0 + ad @ a2)
    heads = lambda z: z.reshape(n, t, RWKV_HEADS, RWKV_HEAD_DIM)
    kk = heads(k * k_k)
    kk = kk / jnp.maximum(jnp.sqrt(jnp.sum(kk * kk, -1, keepdims=True)), 1e-12)
    k = k * (1.0 + (a - 1.0) * k_a)
    r_h, k_h, v_h, w_h, a_h = heads(r), heads(k), heads(v), heads(decay), heads(a)

    def step(S, inp):
        r_t, w_t, k_t, v_t, kk_t, a_t = inp
        sa = jnp.einsum('bhij,bhj->bhi', S, -kk_t)
        S = (S * w_t[:, :, None, :] + sa[..., None] * (kk_t * a_t)[:, :, None, :]
             + v_t[..., None] * k_t[:, :, None, :])
        return S, jnp.einsum('bhij,bhj->bhi', S, r_t)

    xs = tuple(jnp.moveaxis(z, 1, 0) for z in (r_h, w_h, k_h, v_h, kk, a_h))
    S, ys = lax.scan(step, state0.astype(f32), xs)
    y = jnp.moveaxis(ys, 0, 1)
    ym = jnp.mean(y, -1, keepdims=True)
    yv = jnp.mean(jnp.square(y - ym), -1, keepdims=True)
    yn = ((y - ym) * lax.rsqrt(yv + GN_EPS)).reshape(n, t, RWKV_WIDTH) * ln_w + ln_b
    bonus = (jnp.sum(r_h * k_h * r_k, -1, keepdims=True) * v_h).reshape(n, t, RWKV_WIDTH)
    return yn + bonus, sh[:, -1:], S


def _project_memory(mem, w_mem_kv):
    n = mem.shape[0]
    mk, mv = _split(mem @ w_mem_kv, (MEM_WIDTH, MEM_WIDTH))
    return (mk.reshape(n, N_MEM, MEM_HEADS, MEM_HEAD_DIM),
            mv.reshape(n, N_MEM, MEM_HEADS, MEM_HEAD_DIM))


def _memory_attention(q, mem_k, mem_v):
    n, t, _ = q.shape
    qh = q.reshape(n, t, MEM_HEADS, MEM_HEAD_DIM).astype(jnp.float32)
    s = jnp.einsum('bthd,bmhd->bhtm', qh, mem_k.astype(jnp.float32)) * (MEM_HEAD_DIM ** -0.5)
    p = jax.nn.softmax(s, axis=-1)
    o = jnp.einsum('bhtm,bmhd->bthd', p, mem_v.astype(jnp.float32))
    return o.reshape(n, t, MEM_WIDTH)


def _layer(x, mem_k, mem_v, pool_buf, shift_buf, rwkv_state, pos0, lp):
    z = x @ lp['w_in']
    u_pool, z_pool, sh, z_rwkv, q_mem, z_mem, gate_in = _split(z, IN_SPLITS)
    pool_out, new_pool = _pool_branch(u_pool, pool_buf, pos0, lp['pool_w'], lp['pool_scale'])
    pool_out = pool_out * jax.nn.silu(z_pool.astype(jnp.float32))
    rwkv_out, new_shift, new_state = _rwkv_branch(
        sh, shift_buf, rwkv_state, lp['rwkv_mu'], lp['rwkv_w0'], lp['rwkv_w2'], lp['rwkv_a0'],
        lp['rwkv_a2'], lp['rwkv_k_k'], lp['rwkv_k_a'], lp['rwkv_r_k'], lp['rwkv_ln_w'], lp['rwkv_ln_b'])
    rwkv_out = rwkv_out * jax.nn.silu(z_rwkv.astype(jnp.float32))
    mem_out = _memory_attention(q_mem, mem_k, mem_v) * jax.nn.silu(z_mem.astype(jnp.float32))
    g_pool, g_rwkv, g_mem = _split(jax.nn.sigmoid(gate_in.astype(jnp.float32) + lp['b_gate']),
                                   (D_MODEL, D_MODEL, D_MODEL))
    dt = x.dtype
    h = (g_pool * (pool_out.astype(dt) @ lp['w_branch_pool'])
         + g_rwkv * (rwkv_out.astype(dt) @ lp['w_branch_rwkv'])
         + g_mem * (mem_out.astype(dt) @ lp['w_branch_mem']))
    sub = h.astype(dt) @ lp['w_out']
    y = _layernorm(DEEPNORM_ALPHA * x.astype(jnp.float32) + sub, lp['ln_g'], lp['ln_b'])
    return y.astype(dt), new_pool, new_shift, new_state


def setup_inputs(seed: int = 0) -> dict:
    key = jax.random.key(seed)
    ks = jax.random.split(key, 40)
    f32 = jnp.float32
    nrm = lambda k, shape, s: (jax.random.normal(k, shape, f32) * s)
    lin = jnp.linspace(0.0, 1.0, RWKV_WIDTH, dtype=f32)[None, :]
    return {
        'x_prompt': nrm(ks[0], (BATCH, SEQ, D_MODEL), 1.0),
        'x_sample': nrm(ks[1], (DEC_BATCH, DEC_SEQ, D_MODEL), 1.0),
        'cache_mem_k': nrm(ks[2], (DEPTH, DEC_BATCH, N_MEM, MEM_HEADS, MEM_HEAD_DIM), 1.0),
        'cache_mem_v': nrm(ks[3], (DEPTH, DEC_BATCH, N_MEM, MEM_HEADS, MEM_HEAD_DIM), 1.0),
        'state_pool': nrm(ks[4], (DEPTH, DEC_BATCH, POOL_BUF, POOL_WIDTH), 1.0),
        'state_shift': nrm(ks[5], (DEPTH, DEC_BATCH, 1, SHIFT_WIDTH), 1.0),
        'state_rwkv': nrm(ks[6], (DEPTH, DEC_BATCH, RWKV_HEADS, RWKV_HEAD_DIM, RWKV_HEAD_DIM), 0.5),
        'mem_prompt': nrm(ks[7], (BATCH, N_MEM, D_MODEL), 1.0),
        'w_in': nrm(ks[8], (DEPTH, D_MODEL, IN_COLS), D_MODEL ** -0.5),
        'b_gate': nrm(ks[9], (DEPTH, 3 * D_MODEL), 0.01),
        'pool_w': nrm(ks[10], (DEPTH, N_POOL_GROUPS, POOL_GROUP, POOL_GROUP), POOL_GROUP ** -0.5),
        'pool_scale': 1.0 + nrm(ks[11], (DEPTH, POOL_WIDTH), 0.01),
        'rwkv_mu': jax.random.uniform(ks[12], (DEPTH, SHIFT_WIDTH), f32),
        'rwkv_w0': (-7.0 + 5.0 * lin ** 0.85 + 0.5) + nrm(ks[13], (DEPTH, RWKV_WIDTH), 0.01),
        'rwkv_w2': nrm(ks[14], (DEPTH, DECAY_LORA, RWKV_WIDTH), 0.1 * DECAY_LORA ** -0.5),
        'rwkv_a0': nrm(ks[15], (DEPTH, RWKV_WIDTH), 0.01),
        'rwkv_a2': nrm(ks[16], (DEPTH, AAA_LORA, RWKV_WIDTH), 0.1 * AAA_LORA ** -0.5),
        'rwkv_k_k': 0.85 + nrm(ks[17], (DEPTH, RWKV_WIDTH), 0.01),
        'rwkv_k_a': 1.0 + nrm(ks[18], (DEPTH, RWKV_WIDTH), 0.01),
        'rwkv_r_k': nrm(ks[19], (DEPTH, RWKV_HEADS, RWKV_HEAD_DIM), 0.1),
        'rwkv_ln_w': 1.0 + nrm(ks[20], (DEPTH, RWKV_WIDTH), 0.01),
        'rwkv_ln_b': nrm(ks[21], (DEPTH, RWKV_WIDTH), 0.01),
        'w_mem_kv': nrm(ks[22], (DEPTH, D_MODEL, 2 * MEM_WIDTH), D_MODEL ** -0.5),
        'w_branch_pool': nrm(ks[23], (DEPTH, POOL_WIDTH, D_MODEL), DEEPNORM_BETA * POOL_WIDTH ** -0.5),
        'w_branch_rwkv': nrm(ks[24], (DEPTH, RWKV_WIDTH, D_MODEL), DEEPNORM_BETA * RWKV_WIDTH ** -0.5),
        'w_branch_mem': nrm(ks[25], (DEPTH, MEM_WIDTH, D_MODEL), DEEPNORM_BETA * MEM_WIDTH ** -0.5),
        'w_out': nrm(ks[26], (DEPTH, D_MODEL, D_MODEL), DEEPNORM_BETA * D_MODEL ** -0.5),
        'ln_g': 1.0 + nrm(ks[27], (DEPTH, D_MODEL), 0.01),
        'ln_b': nrm(ks[28], (DEPTH, D_MODEL), 0.01),
    }


def reference(x_prompt, x_sample, cache_mem_k, cache_mem_v, state_pool, state_shift, state_rwkv,
              mem_prompt, w_in, b_gate, pool_w, pool_scale, rwkv_mu, rwkv_w0, rwkv_w2, rwkv_a0,
              rwkv_a2, rwkv_k_k, rwkv_k_a, rwkv_r_k, rwkv_ln_w, rwkv_ln_b, w_mem_kv,
              w_branch_pool, w_branch_rwkv, w_branch_mem, w_out, ln_g, ln_b):
    f32 = jnp.float32
    yp, ys = x_prompt, x_sample
    nb = x_prompt.shape[0]
    mk_l, mv_l, pp_l, shp_l, sp_l, ps_l, shs_l, ss_l = [], [], [], [], [], [], [], []
    for l in range(DEPTH):
        lp = dict(w_in=w_in[l], b_gate=b_gate[l], pool_w=pool_w[l], pool_scale=pool_scale[l],
                  rwkv_mu=rwkv_mu[l], rwkv_w0=rwkv_w0[l], rwkv_w2=rwkv_w2[l], rwkv_a0=rwkv_a0[l],
                  rwkv_a2=rwkv_a2[l], rwkv_k_k=rwkv_k_k[l], rwkv_k_a=rwkv_k_a[l],
                  rwkv_r_k=rwkv_r_k[l], rwkv_ln_w=rwkv_ln_w[l], rwkv_ln_b=rwkv_ln_b[l],
                  w_branch_pool=w_branch_pool[l], w_branch_rwkv=w_branch_rwkv[l],
                  w_branch_mem=w_branch_mem[l], w_out=w_out[l], ln_g=ln_g[l], ln_b=ln_b[l])
        mk_p, mv_p = _project_memory(mem_prompt, w_mem_kv[l])
        yp, pool_p, shift_p, st_p = _layer(
            yp, mk_p, mv_p,
            jnp.zeros((nb, POOL_BUF, POOL_WIDTH), f32),
            jnp.zeros((nb, 1, SHIFT_WIDTH), f32),
            jnp.zeros((nb, RWKV_HEADS, RWKV_HEAD_DIM, RWKV_HEAD_DIM), f32),
            0, lp)
        ys, pool_s, shift_s, st_s = _layer(
            ys, cache_mem_k[l], cache_mem_v[l], state_pool[l], state_shift[l], state_rwkv[l],
            PAST_LEN, lp)
        mk_l.append(mk_p); mv_l.append(mv_p); pp_l.append(pool_p); shp_l.append(shift_p)
        sp_l.append(st_p); ps_l.append(pool_s); shs_l.append(shift_s); ss_l.append(st_s)
    return (yp, ys, jnp.stack(mk_l), jnp.stack(mv_l), jnp.stack(pp_l), jnp.stack(shp_l),
            jnp.stack(sp_l), jnp.stack(ps_l), jnp.stack(shs_l), jnp.stack(ss_l))
```

```python
import functools

import jax
import jax.numpy as jnp
from jax import lax
from jax.experimental import pallas as pl
from jax.experimental.pallas import tpu as pltpu

_F32 = jnp.float32
_BF16 = jnp.bfloat16

_POOL_WINDOWS = (2, 4, 8, 16)
_PAST_LEN = 16384
_GN_EPS = 64e-5
_LN_EPS = 1e-5
_L2_EPS = 1e-12

_LANES = 128
_MIB = 1024 * 1024
_VMEM_LIMIT = 56 * _MIB

_HEAD_PAIR = _LANES
_CHUNK = 64


def _params(sem):
    return pltpu.CompilerParams(dimension_semantics=sem, vmem_limit_bytes=_VMEM_LIMIT)


def _dot(a, b):
    return jnp.dot(a, b, preferred_element_type=_F32)


def _dot_nt(a, b):
    return lax.dot_general(a, b, (((1,), (1,)), ((), ())), preferred_element_type=_F32)


def _dot_tn(a, b):
    return lax.dot_general(a, b, (((0,), (0,)), ((), ())), preferred_element_type=_F32)


def _split(x):
    hi = x.astype(_BF16)
    lo = (x - hi.astype(_F32)).astype(_BF16)
    return hi, lo


def _mm3(dot, a, b):
    a_hi, a_lo = _split(a)
    b_hi, b_lo = _split(b)
    return dot(a_hi, b_hi) + (dot(a_hi, b_lo) + dot(a_lo, b_hi))


def _mm1(dot, a, b):
    return dot(a.astype(_BF16), b.astype(_BF16))


def _mm_exact_rhs(a, b_bf16):
    a_hi, a_lo = _split(a)
    return _dot(a_hi, b_bf16) + _dot(a_lo, b_bf16)


def _silu(x):
    return x * jax.nn.sigmoid(x)


def _head_block_matrix(value, head_dim):
    r = lax.broadcasted_iota(jnp.int32, (_LANES, _LANES), 0) // head_dim
    c = lax.broadcasted_iota(jnp.int32, (_LANES, _LANES), 1) // head_dim
    return jnp.where(r == c, value, 0.0).astype(_BF16)


def _matmul_kernel(x_ref, w_ref, o_ref):
    o_ref[...] = _dot(x_ref[...], w_ref[...]).astype(o_ref.dtype)


def _matmul(x, w, *, tm, tn, out_dtype, name):
    m, k = x.shape
    n = w.shape[1]
    return pl.pallas_call(
        _matmul_kernel,
        grid=(m // tm, n // tn),
        in_specs=[pl.BlockSpec((tm, k), lambda i, j: (i, 0)), pl.BlockSpec((k, tn), lambda i, j: (0, j))],
        out_specs=pl.BlockSpec((tm, tn), lambda i, j: (i, j)),
        out_shape=jax.ShapeDtypeStruct((m, n), out_dtype),
        compiler_params=_params(("parallel", "parallel")),
        name=name,
    )(x, w)


def _pool_mix(pooled_fn, zp_ref, pw_ref, ps_ref, o_ref, group):
    for g in range(len(_POOL_WINDOWS)):
        cols = slice(g * group, (g + 1) * group)
        mixed = _dot(pooled_fn(g, cols).astype(_BF16), pw_ref[g])
        o_ref[:, cols] = (mixed * ps_ref[:, cols] * _silu(zp_ref[:, cols])).astype(o_ref.dtype)


def _pool_prompt_kernel(u_ref, halo_ref, zp_ref, pw_ref, ps_ref, o_ref, e_ref, *, tb, hist, group):
    t = pl.program_id(1)
    e_ref[0:hist, :] = jnp.where(t == 0, 0.0, halo_ref[...])
    e_ref[hist:hist + tb, :] = u_ref[...]
    pos = t * tb + lax.broadcasted_iota(jnp.int32, (tb, group), 0)

    def pooled(g, cols):
        win = _POOL_WINDOWS[g]
        x = e_ref[hist:hist + tb, cols]
        acc = x
        for d in range(1, win):
            acc = acc + e_ref[hist - d:hist - d + tb, cols]
        cnt = jnp.minimum(pos + 1, win).astype(_F32)
        return acc / cnt - x

    _pool_mix(pooled, zp_ref, pw_ref, ps_ref, o_ref, group)


def _pool_prompt(z, pool_w, pool_scale, *, batch, seq, width, tb):
    hist = 16
    n_t = seq // tb
    group = width // len(_POOL_WINDOWS)
    kern = functools.partial(_pool_prompt_kernel, tb=tb, hist=hist, group=group)
    return pl.pallas_call(
        kern,
        grid=(batch, n_t),
        in_specs=[
            pl.BlockSpec((tb, width), lambda b, t: (b * n_t + t, 0)),
            pl.BlockSpec((hist, width), lambda b, t: (jnp.maximum((b * seq + t * tb) // hist - 1, 0), 0)),
            pl.BlockSpec((tb, width), lambda b, t: (b * n_t + t, 1)),
            pl.BlockSpec(pool_w.shape, lambda b, t: (0, 0, 0)),
            pl.BlockSpec((1, width), lambda b, t: (0, 0)),
        ],
        out_specs=pl.BlockSpec((tb, width), lambda b, t: (b * n_t + t, 0)),
        out_shape=jax.ShapeDtypeStruct((batch * seq, width), _BF16),
        scratch_shapes=[pltpu.VMEM((hist + tb, width), _F32)],
        compiler_params=_params(("parallel", "parallel")),
        name="pool_prompt",
    )(z, z, z, pool_w, pool_scale)


def _pool_sample_kernel(u_ref, buf_ref, zp_ref, pw_ref, ps_ref, o_ref, *, width, nbuf, group):
    def pooled(g, cols):
        win = _POOL_WINDOWS[g]
        x = u_ref[:, cols]
        acc = x
        for d in range(1, win):
            row = nbuf - d
            acc = acc + buf_ref[:, row * width + cols.start:row * width + cols.stop]
        cnt = float(min(_PAST_LEN + 1, win))
        return acc / cnt - x

    _pool_mix(pooled, zp_ref, pw_ref, ps_ref, o_ref, group)


def _pool_sample(z, buf2d, pool_w, pool_scale, *, row0, rows, width, bb):
    nbuf = buf2d.shape[1] // width
    group = width // len(_POOL_WINDOWS)
    off = row0 // bb
    kern = functools.partial(_pool_sample_kernel, width=width, nbuf=nbuf, group=group)
    return pl.pallas_call(
        kern,
        grid=(rows // bb,),
        in_specs=[
            pl.BlockSpec((bb, width), lambda i: (off + i, 0)),
            pl.BlockSpec((bb, nbuf * width), lambda i: (i, 0)),
            pl.BlockSpec((bb, width), lambda i: (off + i, 1)),
            pl.BlockSpec(pool_w.shape, lambda i: (0, 0, 0)),
            pl.BlockSpec((1, width), lambda i: (0, 0)),
        ],
        out_specs=pl.BlockSpec((bb, width), lambda i: (i, 0)),
        out_shape=jax.ShapeDtypeStruct((rows, width), _BF16),
        compiler_params=_params(("parallel",)),
        name="pool_sample",
    )(z, buf2d, z, pool_w, pool_scale)


def _rwkv_mix(cur, prev, mu, w0, a0, kk_scale, ka, w2p, a2p, head_dim, outs):
    o_r, o_ld, o_k, o_v, o_kk, o_be = outs
    mix = lambda n: cur[n] + (prev[n] - cur[n]) * mu[n]
    xl = mix("l")
    wl = w0 + _mm1(_dot, jnp.tanh(xl), w2p)
    w_log = -jax.nn.softplus(-wl) - 0.5
    a = jax.nn.sigmoid(a0 + _mm1(_dot, xl, a2p))
    xk = mix("k")
    kkp = xk * kk_scale
    ones_bd = _head_block_matrix(1.0, head_dim)
    width = kkp.shape[1]
    o_r[...] = mix("r")
    o_v[...] = mix("v")
    o_ld[...] = -jnp.exp(w_log)
    o_k[...] = xk * (1.0 + (a - 1.0) * ka)
    for p in range(width // _LANES):
        cols = slice(p * _LANES, (p + 1) * _LANES)
        kp = kkp[:, cols]
        ss = _mm_exact_rhs(kp * kp, ones_bd)
        kk = kp / jnp.maximum(jnp.sqrt(ss), _L2_EPS)
        o_kk[:, cols] = kk
        o_be[:, cols] = kk * a[:, cols]


def _rwkv_prep_prompt_kernel(zr, zk, zv, zl, hr, hk, hv, hl, mur, muk, muv, mul, w0, a0, kks, ka, w2p, a2p,
                             o_r, o_ld, o_k, o_v, o_kk, o_be, er, ek, ev, el, *, tb, head_dim):
    t = pl.program_id(1)
    halo = 8
    cur, prev = {}, {}
    for name, z_ref, h_ref, e_ref in (("r", zr, hr, er), ("k", zk, hk, ek), ("v", zv, hv, ev), ("l", zl, hl, el)):
        e_ref[0:halo, :] = jnp.where(t == 0, 0.0, h_ref[...])
        e_ref[halo:halo + tb, :] = z_ref[...]
        cur[name] = z_ref[...]
        prev[name] = e_ref[halo - 1:halo - 1 + tb, :]
    mu = {"r": mur[...], "k": muk[...], "v": muv[...], "l": mul[...]}
    _rwkv_mix(cur, prev, mu, w0[...], a0[...], kks[...], ka[...], w2p[...], a2p[...], head_dim,
              (o_r, o_ld, o_k, o_v, o_kk, o_be))


def _rwkv_prep_prompt(z, zl, consts, *, batch, seq, width, tb, head_dim, col0):
    n_t = seq // tb
    c0 = col0 // width
    assert c0 * width == col0
    halo = 8
    lw = zl.shape[1]
    row = lambda b, t: b * n_t + t
    hrow = lambda b, t: jnp.maximum((b * seq + t * tb) // halo - 1, 0)
    cblk = lambda c: pl.BlockSpec((tb, width), lambda b, t, c=c: (row(b, t), c))
    hblk = lambda c: pl.BlockSpec((halo, width), lambda b, t, c=c: (hrow(b, t), c))
    vec = lambda n: pl.BlockSpec((1, n), lambda b, t: (0, 0))
    mat = pl.BlockSpec((lw, width), lambda b, t: (0, 0))
    out_spec = pl.BlockSpec((tb, width), lambda b, t: (row(b, t), 0))
    out_shape = jax.ShapeDtypeStruct((batch * seq, width), _F32)
    kern = functools.partial(_rwkv_prep_prompt_kernel, tb=tb, head_dim=head_dim)
    return pl.pallas_call(
        kern,
        grid=(batch, n_t),
        in_specs=[cblk(c0), cblk(c0 + 1), cblk(c0 + 2), pl.BlockSpec((tb, lw), lambda b, t: (row(b, t), 0)),
                  hblk(c0), hblk(c0 + 1), hblk(c0 + 2), pl.BlockSpec((halo, lw), lambda b, t: (hrow(b, t), 0)),
                  vec(width), vec(width), vec(width), vec(lw),
                  vec(width), vec(width), vec(width), vec(width), mat, mat],
        out_specs=[out_spec] * 6,
        out_shape=[out_shape] * 6,
        scratch_shapes=[pltpu.VMEM((halo + tb, width), _F32)] * 3 + [pltpu.VMEM((halo + tb, lw), _F32)],
        compiler_params=_params(("parallel", "parallel")),
        name="rwkv_prep_prompt",
    )(z, z, z, zl, z, z, z, zl, *consts)


def _rwkv_prep_sample_kernel(zr, zk, zv, zl, pr, pk, pv, plr, mur, muk, muv, mul, w0, a0, kks, ka, w2p, a2p,
                             o_r, o_ld, o_k, o_v, o_kk, o_be, *, head_dim):
    cur = {"r": zr[...], "k": zk[...], "v": zv[...], "l": zl[...]}
    prev = {"r": pr[...], "k": pk[...], "v": pv[...], "l": plr[...]}
    mu = {"r": mur[...], "k": muk[...], "v": muv[...], "l": mul[...]}
    _rwkv_mix(cur, prev, mu, w0[...], a0[...], kks[...], ka[...], w2p[...], a2p[...], head_dim,
              (o_r, o_ld, o_k, o_v, o_kk, o_be))


def _rwkv_prep_sample(z, zl, prev_rkv, prev_l, consts, *, row0, rows, width, head_dim, col0):
    lw = zl.shape[1]
    c0 = col0 // width
    assert c0 * width == col0
    off = row0 // rows
    cblk = lambda c: pl.BlockSpec((rows, width), lambda i, c=c: (off, c))
    pblk = lambda c: pl.BlockSpec((rows, width), lambda i, c=c: (0, c))
    vec = lambda n: pl.BlockSpec((1, n), lambda i: (0, 0))
    mat = pl.BlockSpec((lw, width), lambda i: (0, 0))
    out_spec = pl.BlockSpec((rows, width), lambda i: (0, 0))
    out_shape = jax.ShapeDtypeStruct((rows, width), _F32)
    kern = functools.partial(_rwkv_prep_sample_kernel, head_dim=head_dim)
    return pl.pallas_call(
        kern,
        grid=(1,),
        in_specs=[cblk(c0), cblk(c0 + 1), cblk(c0 + 2), pl.BlockSpec((rows, lw), lambda i: (off, 0)),
                  pblk(0), pblk(1), pblk(2), pl.BlockSpec((rows, lw), lambda i: (0, 0)),
                  vec(width), vec(width), vec(width), vec(lw),
                  vec(width), vec(width), vec(width), vec(width), mat, mat],
        out_specs=[out_spec] * 6,
        out_shape=[out_shape] * 6,
        compiler_params=_params(("arbitrary",)),
        name="rwkv_prep_sample",
    )(z, z, z, zl, prev_rkv, prev_rkv, prev_rkv, prev_l, *consts)


def _scan_chunk(r, ld, k2, v, kk, be, st, c):
    mm = c["mm"]
    cum = _mm_exact_rhs_t(c["ltri"], ld)
    tot = cum[_CHUNK - 1:_CHUNK, :]
    e_in = jnp.exp(cum)
    e_ex = jnp.exp(cum - ld)
    e_neg = jnp.exp(-cum)
    e_rem = jnp.exp(tot - cum)
    m0, m1 = c["m0"], c["m1"]
    expand = lambda x: jnp.concatenate([x * m0, x * m1], axis=0)
    xa = jnp.concatenate([expand(-kk * e_ex), expand(r * e_in)], axis=0)
    xb = jnp.concatenate([expand(be * e_neg), expand(k2 * e_neg)], axis=0)
    g = mm(_dot_nt, xa, xb)
    n2 = _LANES
    a_ab = jnp.where(c["strict"], g[:n2, :n2], 0.0)
    a_ak = jnp.where(c["strict"], g[:n2, n2:], 0.0)
    a_rb = jnp.where(c["incl"], g[n2:, :n2], 0.0)
    a_rk = jnp.where(c["incl"], g[n2:, n2:], 0.0)
    npow = a_ab
    tinv = c["eye"] + npow
    for _ in range(c["squarings"]):
        npow = mm(_dot, npow, npow)
        tinv = tinv + mm(_dot, tinv, npow)
    ve = expand(v)
    ps = mm(_dot_nt, xa, st)
    av = mm(_dot, jnp.concatenate([a_ak, a_rk], axis=0), ve)
    ue = mm(_dot, tinv, ps[:n2] + av[:n2])
    ye = ps[n2:] + av[n2:] + mm(_dot, a_rb, ue)
    y = ye[:_CHUNK] + ye[_CHUNK:]
    bk = jnp.concatenate([expand(be * e_rem), expand(k2 * e_rem)], axis=0)
    st_new = st * jnp.exp(tot) + mm(_dot_tn, jnp.concatenate([ue, ve], axis=0), bk)
    return y, st_new


def _mm_exact_rhs_t(l_bf16, x):
    x_hi, x_lo = _split(x)
    x_lo2 = (x - x_hi.astype(_F32) - x_lo.astype(_F32)).astype(_BF16)
    return _dot(l_bf16, x_hi) + (_dot(l_bf16, x_lo) + _dot(l_bf16, x_lo2))


def _scan_consts(mm):
    n = _LANES
    half = _CHUNK
    ri = lax.broadcasted_iota(jnp.int32, (n, n), 0)
    ci = lax.broadcasted_iota(jnp.int32, (n, n), 1)
    same = (ri // half) == (ci // half)
    lane = lax.broadcasted_iota(jnp.int32, (half, n), 1)
    tr = lax.broadcasted_iota(jnp.int32, (half, half), 0)
    tc = lax.broadcasted_iota(jnp.int32, (half, half), 1)
    squarings = 0
    while (2 << squarings) < half:
        squarings += 1
    return {
        "mm": mm,
        "strict": same & ((ri % half) > (ci % half)),
        "incl": same & ((ri % half) >= (ci % half)),
        "eye": jnp.where(ri == ci, 1.0, 0.0).astype(_F32),
        "m0": jnp.where(lane < half, 1.0, 0.0).astype(_F32),
        "m1": jnp.where(lane >= half, 1.0, 0.0).astype(_F32),
        "ltri": jnp.where(tr >= tc, 1.0, 0.0).astype(_BF16),
        "squarings": squarings,
    }


def _rwkv_scan_kernel(r_ref, ld_ref, k_ref, v_ref, kk_ref, be_ref, y_ref, s_ref, st_ref, *, tb, pp, mm):
    t = pl.program_id(2)

    @pl.when(t == 0)
    def _():
        st_ref[...] = jnp.zeros_like(st_ref)

    c = _scan_consts(mm)

    def body(ci, carry):
        rows = pl.ds(pl.multiple_of(ci * _CHUNK, _CHUNK), _CHUNK)
        for q in range(pp):
            cols = slice(q * _LANES, (q + 1) * _LANES)
            y, st_new = _scan_chunk(r_ref[rows, cols], ld_ref[rows, cols], k_ref[rows, cols], v_ref[rows, cols],
                                    kk_ref[rows, cols], be_ref[rows, cols], st_ref[q], c)
            y_ref[rows, cols] = y
            st_ref[q] = st_new
        return carry

    lax.fori_loop(0, tb // _CHUNK, body, 0)

    @pl.when(t == pl.num_programs(2) - 1)
    def _():
        s_ref[0] = st_ref[...]


def _rwkv_scan(arrs, *, batch, seq, width, tb, pp, mm):
    n_t = seq // tb
    n_p = width // _LANES
    blk = pl.BlockSpec((tb, pp * _LANES), lambda b, p, t: (b * n_t + t, p))
    kern = functools.partial(_rwkv_scan_kernel, tb=tb, pp=pp, mm=mm)
    return pl.pallas_call(
        kern,
        grid=(batch, n_p // pp, n_t),
        in_specs=[blk] * 6,
        out_specs=[blk, pl.BlockSpec((1, pp, _LANES, _LANES), lambda b, p, t: (b, p, 0, 0))],
        out_shape=[jax.ShapeDtypeStruct((batch * seq, width), _F32),
                   jax.ShapeDtypeStruct((batch, n_p, _LANES, _LANES), _F32)],
        scratch_shapes=[pltpu.VMEM((pp, _LANES, _LANES), _F32)],
        compiler_params=_params(("parallel", "parallel", "arbitrary")),
        name="rwkv_scan",
    )(*arrs)


def _rwkv_step_kernel(r_ref, ld_ref, k_ref, v_ref, kk_ref, be_ref, s_ref, y_ref, so_ref, *, head_dim):
    s = s_ref[...]
    eye = (lax.broadcasted_iota(jnp.int32, (head_dim, head_dim), 0)
           == lax.broadcasted_iota(jnp.int32, (head_dim, head_dim), 1))
    col = lambda row: jnp.sum(jnp.where(eye, row, 0.0), axis=-1, keepdims=True)
    sa = -jnp.sum(s * kk_ref[...], axis=-1, keepdims=True)
    s_new = s * jnp.exp(ld_ref[...]) + sa * be_ref[...] + col(v_ref[...]) * k_ref[...]
    y_col = jnp.sum(s_new * r_ref[...], axis=-1, keepdims=True)
    y_ref[...] = jnp.sum(jnp.where(eye, y_col, 0.0), axis=-2, keepdims=True)
    so_ref[...] = s_new


def _rwkv_step(arrs, state, *, bb):
    n, h, hd, _ = state.shape
    vec = pl.BlockSpec((bb, h, 1, hd), lambda i: (i, 0, 0, 0))
    mat = pl.BlockSpec((bb, h, hd, hd), lambda i: (i, 0, 0, 0))
    kern = functools.partial(_rwkv_step_kernel, head_dim=hd)
    return pl.pallas_call(
        kern,
        grid=(n // bb,),
        in_specs=[vec] * 6 + [mat],
        out_specs=[vec, mat],
        out_shape=[jax.ShapeDtypeStruct((n, h, 1, hd), _F32), jax.ShapeDtypeStruct(state.shape, _F32)],
        compiler_params=_params(("parallel",)),
        name="rwkv_step",
    )(*[a.reshape(n, h, 1, hd) for a in arrs], state)


def _rwkv_post_kernel(y_ref, r_ref, k_ref, v_ref, zg_ref, rk_ref, lw_ref, lb_ref, o_ref, *, head_dim):
    avg_bd = _head_block_matrix(1.0 / head_dim, head_dim)
    ones_bd = _head_block_matrix(1.0, head_dim)
    width = y_ref.shape[1]
    for p in range(width // _LANES):
        cols = slice(p * _LANES, (p + 1) * _LANES)
        y = y_ref[:, cols]
        d = y - _mm_exact_rhs(y, avg_bd)
        var = _mm_exact_rhs(d * d, avg_bd)
        yn = d * lax.rsqrt(var + _GN_EPS) * lw_ref[:, cols] + lb_ref[:, cols]
        bonus = _mm_exact_rhs(r_ref[:, cols] * k_ref[:, cols] * rk_ref[:, cols], ones_bd) * v_ref[:, cols]
        o_ref[:, cols] = ((yn + bonus) * _silu(zg_ref[:, cols])).astype(o_ref.dtype)


def _rwkv_post(y, r, k2, v, z, consts, *, row0, rows, width, tb, head_dim, col0):
    off = row0 // tb
    cg = col0 // width
    assert cg * width == col0
    blk = pl.BlockSpec((tb, width), lambda i: (i, 0))
    vec = pl.BlockSpec((1, width), lambda i: (0, 0))
    kern = functools.partial(_rwkv_post_kernel, head_dim=head_dim)
    return pl.pallas_call(
        kern,
        grid=(rows // tb,),
        in_specs=[blk, blk, blk, blk, pl.BlockSpec((tb, width), lambda i: (off + i, cg)), vec, vec, vec],
        out_specs=blk,
        out_shape=jax.ShapeDtypeStruct((rows, width), _BF16),
        compiler_params=_params(("parallel",)),
        name="rwkv_post",
    )(y, r, k2, v, z, *consts)


def _mem_attn_prompt_kernel(q_ref, zg_ref, k_ref, v_ref, o_ref, *, heads, head_dim):
    scale = head_dim ** -0.5
    for h in range(heads):
        cols = slice(h * head_dim, (h + 1) * head_dim)
        s = _dot_nt(q_ref[:, cols].astype(_BF16), k_ref[:, cols].astype(_BF16)) * scale
        p = jnp.exp(s - jnp.max(s, axis=-1, keepdims=True))
        o = _dot(p.astype(_BF16), v_ref[:, cols].astype(_BF16)) / jnp.sum(p, axis=-1, keepdims=True)
        o_ref[:, cols] = (o * _silu(zg_ref[:, cols])).astype(o_ref.dtype)


def _mem_attn_prompt(z, mkv, *, batch, seq, n_mem, heads, head_dim, tq, col0):
    n_t = seq // tq
    mw = heads * head_dim
    q_col = col0 // mw
    assert q_col * mw == col0
    kern = functools.partial(_mem_attn_prompt_kernel, heads=heads, head_dim=head_dim)
    return pl.pallas_call(
        kern,
        grid=(batch, n_t),
        in_specs=[
            pl.BlockSpec((tq, mw), lambda b, t: (b * n_t + t, q_col)),
            pl.BlockSpec((tq, mw), lambda b, t: (b * n_t + t, q_col + 1)),
            pl.BlockSpec((n_mem, mw), lambda b, t: (b, 0)),
            pl.BlockSpec((n_mem, mw), lambda b, t: (b, 1)),
        ],
        out_specs=pl.BlockSpec((tq, mw), lambda b, t: (b * n_t + t, 0)),
        out_shape=jax.ShapeDtypeStruct((batch * seq, mw), _BF16),
        compiler_params=_params(("parallel", "parallel")),
        name="mem_attn_prompt",
    )(z, z, mkv, mkv)


def _mem_attn_sample_kernel(q_ref, zg_ref, k_ref, v_ref, o_ref, *, bb, heads, head_dim):
    scale = head_dim ** -0.5
    for b in range(bb):
        q = q_ref[b]
        prod = k_ref[b] * q
        for h in range(heads):
            cols = slice(h * head_dim, (h + 1) * head_dim)
            s = jnp.sum(prod[:, cols], axis=-1, keepdims=True) * scale
            p = jnp.exp(s - jnp.max(s, axis=0, keepdims=True))
            o = jnp.sum(p * v_ref[b, :, cols], axis=0, keepdims=True) / jnp.sum(p, axis=0, keepdims=True)
            o_ref[b, :, cols] = (o * _silu(zg_ref[b, :, cols])).astype(o_ref.dtype)


def _mem_attn_sample(q, zg, mem_k, mem_v, *, heads, head_dim, bb):
    n, n_mem, mw = mem_k.shape
    vec = pl.BlockSpec((bb, 1, mw), lambda i: (i, 0, 0))
    mat = pl.BlockSpec((bb, n_mem, mw), lambda i: (i, 0, 0))
    kern = functools.partial(_mem_attn_sample_kernel, bb=bb, heads=heads, head_dim=head_dim)
    return pl.pallas_call(
        kern,
        grid=(n // bb,),
        in_specs=[vec, vec, mat, mat],
        out_specs=vec,
        out_shape=jax.ShapeDtypeStruct((n, 1, mw), _BF16),
        compiler_params=_params(("parallel",)),
        name="mem_attn_sample",
    )(q, zg, mem_k, mem_v)


def _branch_kernel(po_ref, ro_ref, mo_ref, wp_ref, wr_ref, wm_ref, gp_ref, gr_ref, gm_ref,
                   bp_ref, br_ref, bm_ref, o_ref):
    gate = lambda g_ref, b_ref: jax.nn.sigmoid(g_ref[...] + b_ref[...])
    h = gate(gp_ref, bp_ref) * _dot(po_ref[...], wp_ref[...])
    h = h + gate(gr_ref, br_ref) * _dot(ro_ref[...], wr_ref[...])
    h = h + gate(gm_ref, bm_ref) * _dot(mo_ref[...], wm_ref[...])
    o_ref[...] = h.astype(o_ref.dtype)


def _branch(po, ro, mo, wp, wr, wm, z, b_gate, *, row0, tm, tn, gate_col0):
    rows = po.shape[0]
    d = wp.shape[1]
    off = row0 // tm
    g0 = gate_col0 // tn
    nb = d // tn
    act = lambda a: pl.BlockSpec((tm, a.shape[1]), lambda i, j: (i, 0))
    wgt = lambda w: pl.BlockSpec((w.shape[0], tn), lambda i, j: (0, j))
    gat = lambda k: pl.BlockSpec((tm, tn), lambda i, j, k=k: (off + i, g0 + k * nb + j))
    bia = lambda k: pl.BlockSpec((1, tn), lambda i, j, k=k: (0, k * nb + j))
    return pl.pallas_call(
        _branch_kernel,
        grid=(rows // tm, nb),
        in_specs=[act(po), act(ro), act(mo), wgt(wp), wgt(wr), wgt(wm), gat(0), gat(1), gat(2),
                  bia(0), bia(1), bia(2)],
        out_specs=pl.BlockSpec((tm, tn), lambda i, j: (i, j)),
        out_shape=jax.ShapeDtypeStruct((rows, d), _BF16),
        compiler_params=_params(("parallel", "parallel")),
        name="branch_proj",
    )(po, ro, mo, wp, wr, wm, z, z, z, b_gate, b_gate, b_gate)


def _out_kernel(h_ref, w_ref, x_ref, g_ref, b_ref, o_ref, acc_ref, *, alpha):
    k = pl.program_id(1)

    @pl.when(k == 0)
    def _():
        acc_ref[...] = alpha * x_ref[...]

    acc_ref[...] += _dot(h_ref[...], w_ref[...])

    @pl.when(k == pl.num_programs(1) - 1)
    def _():
        xf = acc_ref[...]
        mu = jnp.mean(xf, axis=-1, keepdims=True)
        d = xf - mu
        var = jnp.mean(d * d, axis=-1, keepdims=True)
        o_ref[...] = d * lax.rsqrt(var + _LN_EPS) * g_ref[...] + b_ref[...]


def _out_proj(h, w_out, x, ln_g, ln_b, *, tm, tk, alpha):
    rows, d = x.shape
    kern = functools.partial(_out_kernel, alpha=alpha)
    return pl.pallas_call(
        kern,
        grid=(rows // tm, d // tk),
        in_specs=[
            pl.BlockSpec((tm, tk), lambda i, k: (i, k)),
            pl.BlockSpec((tk, d), lambda i, k: (k, 0)),
            pl.BlockSpec((tm, d), lambda i, k: (i, 0)),
            pl.BlockSpec((1, d), lambda i, k: (0, 0)),
            pl.BlockSpec((1, d), lambda i, k: (0, 0)),
        ],
        out_specs=pl.BlockSpec((tm, d), lambda i, k: (i, 0)),
        out_shape=jax.ShapeDtypeStruct((rows, d), _F32),
        scratch_shapes=[pltpu.VMEM((tm, d), _F32)],
        compiler_params=_params(("parallel", "arbitrary")),
        name="out_proj_ln",
    )(h, w_out, x, ln_g, ln_b)


def _tiles(batch, seq, n_sample):
    n_prompt = batch * seq
    m_all = n_prompt + n_sample
    tm_in = next(t for t in (832, 640, 512, 256, 128, 64, 32, 16) if m_all % t == 0)
    return {
        "in_tm": tm_in, "in_tn": 1024,
        "row_tb": min(256, seq), "prep_tb": min(128, seq),
        "scan_tb": min(256, seq), "scan_pp": 4,
        "attn_tq": min(512, seq),
        "proj_tm": min(512, n_prompt), "proj_tn": 512, "out_tm": min(256, n_prompt), "out_tk": 1024,
        "pool_bb": min(32, n_sample), "step_bb": min(8, n_sample), "attn_bb": 2,
    }


def _layer(xp, xs, mem_p, cache_k, cache_v, st_pool, st_shift, st_rwkv, lp, *, scan_mm):
    batch, seq, d = xp.shape
    ns = xs.shape[0]
    n_p = batch * seq
    m_all = n_p + ns
    pw = lp["pool_scale"].shape[-1]
    rw = lp["rwkv_w0"].shape[-1]
    heads, hd = lp["rwkv_r_k"].shape
    lora = lp["rwkv_w2"].shape[0]
    sw = lp["rwkv_mu"].shape[-1]
    n_mem, mh, md = cache_k.shape[1:]
    mw = mh * md
    tl = _tiles(batch, seq, ns)

    w_in = lp["w_in"]
    lo0, lo1 = 2 * pw + 3 * rw, 2 * pw + sw
    lw = 2 * _LANES
    w_main = jnp.concatenate([w_in[:, :lo0], w_in[:, lo1:]], axis=1).astype(_BF16)
    w_lora = jnp.pad(w_in[:, lo0:lo1], ((0, 0), (0, lw - 2 * lora))).astype(_BF16)
    gate_col0 = lo0 + rw + 2 * mw

    x_all = jnp.concatenate([xp.reshape(n_p, d), xs.reshape(ns, d)], axis=0)
    x_bf = x_all.astype(_BF16)
    z = _matmul(x_bf, w_main, tm=tl["in_tm"], tn=tl["in_tn"], out_dtype=_F32, name="in_proj")
    zl = _matmul(x_bf, w_lora, tm=tl["in_tm"], tn=lw, out_dtype=_F32, name="in_proj_lora")

    pool_w = lp["pool_w"].astype(_BF16)
    pool_scale = lp["pool_scale"].reshape(1, pw)
    po_p = _pool_prompt(z, pool_w, pool_scale, batch=batch, seq=seq, width=pw, tb=tl["row_tb"])
    po_s = _pool_sample(z, st_pool.reshape(ns, -1), pool_w, pool_scale, row0=n_p, rows=ns, width=pw,
                        bb=tl["pool_bb"])
    u_p = z[:n_p, :pw].reshape(batch, seq, pw)
    new_pool_p = u_p[:, seq - st_pool.shape[1]:, :]
    new_pool_s = jnp.concatenate([st_pool[:, 1:, :], z[n_p:, :pw].reshape(ns, 1, pw)], axis=1)

    mu = lp["rwkv_mu"]
    pad_l = lambda a: jnp.pad(a, ((0, 0), (0, lw - 2 * lora)))
    w2p = jnp.zeros((lw, rw), _F32).at[:lora].set(lp["rwkv_w2"]).astype(_BF16)
    a2p = jnp.zeros((lw, rw), _F32).at[lora:2 * lora].set(lp["rwkv_a2"]).astype(_BF16)
    row = lambda a: a.reshape(1, -1)
    mix_consts = (row(mu[:rw]), row(mu[rw:2 * rw]), row(mu[2 * rw:3 * rw]), pad_l(row(mu[3 * rw:])),
                  row(lp["rwkv_w0"]), row(lp["rwkv_a0"]), row(lp["rwkv_k_k"]), row(lp["rwkv_k_a"]), w2p, a2p)
    post_consts = (row(lp["rwkv_r_k"]), row(lp["rwkv_ln_w"]), row(lp["rwkv_ln_b"]))

    r_p, ld_p, k_p, v_p, kk_p, be_p = _rwkv_prep_prompt(z, zl, mix_consts, batch=batch, seq=seq, width=rw,
                                                        tb=tl["prep_tb"], head_dim=hd, col0=2 * pw)
    y_p, st_pairs = _rwkv_scan((r_p, ld_p, k_p, v_p, kk_p, be_p), batch=batch, seq=seq, width=rw,
                               tb=tl["scan_tb"], pp=tl["scan_pp"], mm=scan_mm)
    ro_p = _rwkv_post(y_p, r_p, k_p, v_p, z, post_consts, row0=0, rows=n_p, width=rw, tb=tl["row_tb"],
                      head_dim=hd, col0=lo0)
    st5 = st_pairs.reshape(batch, heads // 2, 2, hd, 2, hd)
    new_state_p = jnp.stack([st5[:, :, 0, :, 0, :], st5[:, :, 1, :, 1, :]], axis=2).reshape(batch, heads, hd, hd)

    shift_s = st_shift.reshape(ns, sw)
    r_s, ld_s, k_s, v_s, kk_s, be_s = _rwkv_prep_sample(z, zl, shift_s[:, :3 * rw], pad_l(shift_s[:, 3 * rw:]),
                                                        mix_consts, row0=n_p, rows=ns, width=rw, head_dim=hd,
                                                        col0=2 * pw)
    y_s4, new_state_s = _rwkv_step((r_s, ld_s, k_s, v_s, kk_s, be_s), st_rwkv, bb=tl["step_bb"])
    ro_s = _rwkv_post(y_s4.reshape(ns, rw), r_s, k_s, v_s, z, post_consts, row0=n_p, rows=ns, width=rw, tb=ns,
                      head_dim=hd, col0=lo0)
    sh_all = jnp.concatenate([z[:, 2 * pw:lo0], zl[:, :2 * lora]], axis=1)
    new_shift_p = sh_all[:n_p].reshape(batch, seq, sw)[:, seq - 1:, :]
    new_shift_s = sh_all[n_p:].reshape(ns, 1, sw)

    mkv = _matmul(mem_p.reshape(batch * n_mem, d).astype(_BF16), lp["w_mem_kv"].astype(_BF16),
                  tm=min(512, batch * n_mem), tn=min(1024, 2 * mw), out_dtype=_F32, name="mem_kv_proj")
    q_col0 = lo0 + rw
    mo_p = _mem_attn_prompt(z, mkv, batch=batch, seq=seq, n_mem=n_mem, heads=mh, head_dim=md, tq=tl["attn_tq"],
                            col0=q_col0)
    mo_s = _mem_attn_sample(z[n_p:, q_col0:q_col0 + mw].reshape(ns, 1, mw),
                            z[n_p:, q_col0 + mw:q_col0 + 2 * mw].reshape(ns, 1, mw),
                            cache_k.reshape(ns, n_mem, mw), cache_v.reshape(ns, n_mem, mw),
                            heads=mh, head_dim=md, bb=tl["attn_bb"]).reshape(ns, mw)
    mk_p = mkv[:, :mw].reshape(batch, n_mem, mh, md)
    mv_p = mkv[:, mw:].reshape(batch, n_mem, mh, md)

    wp = lp["w_branch_pool"].astype(_BF16)
    wr = lp["w_branch_rwkv"].astype(_BF16)
    wm = lp["w_branch_mem"].astype(_BF16)
    w_out = lp["w_out"].astype(_BF16)
    b_gate = row(lp["b_gate"])
    ln_g, ln_b = row(lp["ln_g"]), row(lp["ln_b"])
    alpha = lp["alpha"]
    outs = []
    for po, ro, mo, x2d, row0 in ((po_p, ro_p, mo_p, xp.reshape(n_p, d), 0), (po_s, ro_s, mo_s, xs.reshape(ns, d), n_p)):
        rows = x2d.shape[0]
        h = _branch(po, ro, mo, wp, wr, wm, z, b_gate, row0=row0, tm=min(tl["proj_tm"], rows), tn=tl["proj_tn"],
                    gate_col0=gate_col0)
        outs.append(_out_proj(h, w_out, x2d, ln_g, ln_b, tm=min(tl["out_tm"], rows), tk=tl["out_tk"], alpha=alpha))
    y_prompt = outs[0].reshape(batch, seq, d)
    y_sample = outs[1].reshape(ns, 1, d)
    return (y_prompt, y_sample, mk_p, mv_p, new_pool_p, new_shift_p, new_state_p, new_pool_s, new_shift_s,
            new_state_s)


def kernel(x_prompt, x_sample, cache_mem_k, cache_mem_v, state_pool, state_shift, state_rwkv, mem_prompt, w_in,
           b_gate, pool_w, pool_scale, rwkv_mu, rwkv_w0, rwkv_w2, rwkv_a0, rwkv_a2, rwkv_k_k, rwkv_k_a, rwkv_r_k,
           rwkv_ln_w, rwkv_ln_b, w_mem_kv, w_branch_pool, w_branch_rwkv, w_branch_mem, w_out, ln_g, ln_b):
    depth = w_in.shape[0]
    alpha = (2.0 * depth) ** 0.25
    weights = dict(w_in=w_in, b_gate=b_gate, pool_w=pool_w, pool_scale=pool_scale, rwkv_mu=rwkv_mu, rwkv_w0=rwkv_w0,
                   rwkv_w2=rwkv_w2, rwkv_a0=rwkv_a0, rwkv_a2=rwkv_a2, rwkv_k_k=rwkv_k_k, rwkv_k_a=rwkv_k_a,
                   rwkv_r_k=rwkv_r_k, rwkv_ln_w=rwkv_ln_w, rwkv_ln_b=rwkv_ln_b, w_mem_kv=w_mem_kv,
                   w_branch_pool=w_branch_pool, w_branch_rwkv=w_branch_rwkv, w_branch_mem=w_branch_mem,
                   w_out=w_out, ln_g=ln_g, ln_b=ln_b)
    yp, ys = x_prompt, x_sample
    per_layer = []
    for l in range(depth):
        lp = {k: v[l] for k, v in weights.items()}
        lp["alpha"] = alpha
        res = _layer(yp, ys, mem_prompt, cache_mem_k[l], cache_mem_v[l], state_pool[l], state_shift[l],
                     state_rwkv[l], lp, scan_mm=_mm3)
        yp, ys = res[0], res[1]
        per_layer.append(res[2:])
    stacked = tuple(jnp.stack([pl_[i] for pl_ in per_layer]) for i in range(8))
    return (yp, ys) + stacked
```

```python
import functools

import jax
import jax.numpy as jnp
from jax import lax
from jax.experimental import pallas as pl
from jax.experimental.pallas import tpu as pltpu

_F32 = jnp.float32
_BF16 = jnp.bfloat16

_POOL_WINDOWS = (2, 4, 8, 16)
_PAST_LEN = 16384
_GN_EPS = 64e-5
_LN_EPS = 1e-5
_L2_EPS = 1e-12

_LANES = 128
_MIB = 1024 * 1024
_VMEM_LIMIT = 56 * _MIB

_HEAD_PAIR = _LANES
_CHUNK = 64


def _params(sem):
    return pltpu.CompilerParams(dimension_semantics=sem, vmem_limit_bytes=_VMEM_LIMIT)


def _dot(a, b):
    return jnp.dot(a, b, preferred_element_type=_F32)


def _dot_nt(a, b):
    return lax.dot_general(a, b, (((1,), (1,)), ((), ())), preferred_element_type=_F32)


def _dot_tn(a, b):
    return lax.dot_general(a, b, (((0,), (0,)), ((), ())), preferred_element_type=_F32)


def _split(x):
    hi = x.astype(_BF16)
    lo = (x - hi.astype(_F32)).astype(_BF16)
    return hi, lo


def _mm1(dot, a, b):
    return dot(a.astype(_BF16), b.astype(_BF16))


def _mm_exact_rhs(a, b_bf16):
    a_hi, a_lo = _split(a)
    return _dot(a_hi, b_bf16) + _dot(a_lo, b_bf16)


def _silu(x):
    return x * jax.nn.sigmoid(x)


def _head_block_matrix(value, head_dim):
    r = lax.broadcasted_iota(jnp.int32, (_LANES, _LANES), 0) // head_dim
    c = lax.broadcasted_iota(jnp.int32, (_LANES, _LANES), 1) // head_dim
    return jnp.where(r == c, value, 0.0).astype(_BF16)


def _matmul_kernel(x_ref, w_ref, o_ref):
    o_ref[...] = _dot(x_ref[...], w_ref[...]).astype(o_ref.dtype)


def _matmul(x, w, *, tm, tn, out_dtype, name):
    m, k = x.shape
    n = w.shape[1]
    return pl.pallas_call(
        _matmul_kernel,
        grid=(m // tm, n // tn),
        in_specs=[pl.BlockSpec((tm, k), lambda i, j: (i, 0)), pl.BlockSpec((k, tn), lambda i, j: (0, j))],
        out_specs=pl.BlockSpec((tm, tn), lambda i, j: (i, j)),
        out_shape=jax.ShapeDtypeStruct((m, n), out_dtype),
        compiler_params=_params(("parallel", "parallel")),
        name=name,
    )(x, w)


def _pool_mix(pooled_fn, zp_ref, pw_ref, ps_ref, o_ref, group):
    for g in range(len(_POOL_WINDOWS)):
        cols = slice(g * group, (g + 1) * group)
        mixed = _dot(pooled_fn(g, cols).astype(_BF16), pw_ref[g])
        o_ref[:, cols] = (mixed * ps_ref[:, cols] * _silu(zp_ref[:, cols])).astype(o_ref.dtype)


def _pool_prompt_kernel(u_ref, halo_ref, zp_ref, pw_ref, ps_ref, o_ref, e_ref, *, tb, hist, group):
    t = pl.program_id(1)
    e_ref[0:hist, :] = jnp.where(t == 0, 0.0, halo_ref[...])
    e_ref[hist:hist + tb, :] = u_ref[...]
    pos = t * tb + lax.broadcasted_iota(jnp.int32, (tb, group), 0)

    def pooled(g, cols):
        win = _POOL_WINDOWS[g]
        x = e_ref[hist:hist + tb, cols]
        acc = x
        for d in range(1, win):
            acc = acc + e_ref[hist - d:hist - d + tb, cols]
        cnt = jnp.minimum(pos + 1, win).astype(_F32)
        return acc / cnt - x

    _pool_mix(pooled, zp_ref, pw_ref, ps_ref, o_ref, group)


def _pool_prompt(z, pool_w, pool_scale, *, batch, seq, width, tb):
    hist = 16
    n_t = seq // tb
    group = width // len(_POOL_WINDOWS)
    kern = functools.partial(_pool_prompt_kernel, tb=tb, hist=hist, group=group)
    return pl.pallas_call(
        kern,
        grid=(batch, n_t),
        in_specs=[
            pl.BlockSpec((tb, width), lambda b, t: (b * n_t + t, 0)),
            pl.BlockSpec((hist, width), lambda b, t: (jnp.maximum((b * seq + t * tb) // hist - 1, 0), 0)),
            pl.BlockSpec((tb, width), lambda b, t: (b * n_t + t, 1)),
            pl.BlockSpec(pool_w.shape, lambda b, t: (0, 0, 0)),
            pl.BlockSpec((1, width), lambda b, t: (0, 0)),
        ],
        out_specs=pl.BlockSpec((tb, width), lambda b, t: (b * n_t + t, 0)),
        out_shape=jax.ShapeDtypeStruct((batch * seq, width), _BF16),
        scratch_shapes=[pltpu.VMEM((hist + tb, width), _F32)],
        compiler_params=_params(("parallel", "parallel")),
        name="pool_prompt",
    )(z, z, z, pool_w, pool_scale)


def _pool_sample_kernel(u_ref, buf_ref, zp_ref, pw_ref, ps_ref, o_ref, *, width, nbuf, group):
    def pooled(g, cols):
        win = _POOL_WINDOWS[g]
        x = u_ref[:, cols]
        acc = x
        for d in range(1, win):
            row = nbuf - d
            acc = acc + buf_ref[:, row * width + cols.start:row * width + cols.stop]
        cnt = float(min(_PAST_LEN + 1, win))
        return acc / cnt - x

    _pool_mix(pooled, zp_ref, pw_ref, ps_ref, o_ref, group)


def _pool_sample(z, buf2d, pool_w, pool_scale, *, row0, rows, width, bb):
    nbuf = buf2d.shape[1] // width
    group = width // len(_POOL_WINDOWS)
    off = row0 // bb
    kern = functools.partial(_pool_sample_kernel, width=width, nbuf=nbuf, group=group)
    return pl.pallas_call(
        kern,
        grid=(rows // bb,),
        in_specs=[
            pl.BlockSpec((bb, width), lambda i: (off + i, 0)),
            pl.BlockSpec((bb, nbuf * width), lambda i: (i, 0)),
            pl.BlockSpec((bb, width), lambda i: (off + i, 1)),
            pl.BlockSpec(pool_w.shape, lambda i: (0, 0, 0)),
            pl.BlockSpec((1, width), lambda i: (0, 0)),
        ],
        out_specs=pl.BlockSpec((bb, width), lambda i: (i, 0)),
        out_shape=jax.ShapeDtypeStruct((rows, width), _BF16),
        compiler_params=_params(("parallel",)),
        name="pool_sample",
    )(z, buf2d, z, pool_w, pool_scale)


def _rwkv_mix(cur, prev, mu, w0, a0, kk_scale, ka, w2p, a2p, head_dim, outs):
    o_r, o_ld, o_k, o_v, o_kk, o_be = outs
    mix = lambda n: cur[n] + (prev[n] - cur[n]) * mu[n]
    xl = mix("l")
    wl = w0 + _mm1(_dot, jnp.tanh(xl), w2p)
    w_log = -jax.nn.softplus(-wl) - 0.5
    a = jax.nn.sigmoid(a0 + _mm1(_dot, xl, a2p))
    xk = mix("k")
    kkp = xk * kk_scale
    ones_bd = _head_block_matrix(1.0, head_dim)
    width = kkp.shape[1]
    o_r[...] = mix("r")
    o_v[...] = mix("v")
    o_ld[...] = -jnp.exp(w_log)
    o_k[...] = xk * (1.0 + (a - 1.0) * ka)
    for p in range(width // _LANES):
        cols = slice(p * _LANES, (p + 1) * _LANES)
        kp = kkp[:, cols]
        ss = _mm_exact_rhs(kp * kp, ones_bd)
        kk = kp / jnp.maximum(jnp.sqrt(ss), _L2_EPS)
        o_kk[:, cols] = kk
        o_be[:, cols] = kk * a[:, cols]


def _rwkv_prep_prompt_kernel(zr, zk, zv, zl, hr, hk, hv, hl, mur, muk, muv, mul, w0, a0, kks, ka, w2p, a2p,
                             o_r, o_ld, o_k, o_v, o_kk, o_be, er, ek, ev, el, *, tb, head_dim):
    t = pl.program_id(1)
    halo = 8
    cur, prev = {}, {}
    for name, z_ref, h_ref, e_ref in (("r", zr, hr, er), ("k", zk, hk, ek), ("v", zv, hv, ev), ("l", zl, hl, el)):
        e_ref[0:halo, :] = jnp.where(t == 0, 0.0, h_ref[...])
        e_ref[halo:halo + tb, :] = z_ref[...]
        cur[name] = z_ref[...]
        prev[name] = e_ref[halo - 1:halo - 1 + tb, :]
    mu = {"r": mur[...], "k": muk[...], "v": muv[...], "l": mul[...]}
    _rwkv_mix(cur, prev, mu, w0[...], a0[...], kks[...], ka[...], w2p[...], a2p[...], head_dim,
              (o_r, o_ld, o_k, o_v, o_kk, o_be))


def _rwkv_prep_prompt(z, zl, consts, *, batch, seq, width, tb, head_dim, col0):
    n_t = seq // tb
    c0 = col0 // width
    assert c0 * width == col0
    halo = 8
    lw = zl.shape[1]
    row = lambda b, t: b * n_t + t
    hrow = lambda b, t: jnp.maximum((b * seq + t * tb) // halo - 1, 0)
    cblk = lambda c: pl.BlockSpec((tb, width), lambda b, t, c=c: (row(b, t), c))
    hblk = lambda c: pl.BlockSpec((halo, width), lambda b, t, c=c: (hrow(b, t), c))
    vec = lambda n: pl.BlockSpec((1, n), lambda b, t: (0, 0))
    mat = pl.BlockSpec((lw, width), lambda b, t: (0, 0))
    out_spec = pl.BlockSpec((tb, width), lambda b, t: (row(b, t), 0))
    out_shape = jax.ShapeDtypeStruct((batch * seq, width), _F32)
    kern = functools.partial(_rwkv_prep_prompt_kernel, tb=tb, head_dim=head_dim)
    return pl.pallas_call(
        kern,
        grid=(batch, n_t),
        in_specs=[cblk(c0), cblk(c0 + 1), cblk(c0 + 2), pl.BlockSpec((tb, lw), lambda b, t: (row(b, t), 0)),
                  hblk(c0), hblk(c0 + 1), hblk(c0 + 2), pl.BlockSpec((halo, lw), lambda b, t: (hrow(b, t), 0)),
                  vec(width), vec(width), vec(width), vec(lw),
                  vec(width), vec(width), vec(width), vec(width), mat, mat],
        out_specs=[out_spec] * 6,
        out_shape=[out_shape] * 6,
        scratch_shapes=[pltpu.VMEM((halo + tb, width), _F32)] * 3 + [pltpu.VMEM((halo + tb, lw), _F32)],
        compiler_params=_params(("parallel", "parallel")),
        name="rwkv_prep_prompt",
    )(z, z, z, zl, z, z, z, zl, *consts)


def _rwkv_prep_sample_kernel(zr, zk, zv, zl, pr, pk, pv, plr, mur, muk, muv, mul, w0, a0, kks, ka, w2p, a2p,
                             o_r, o_ld, o_k, o_v, o_kk, o_be, *, head_dim):
    cur = {"r": zr[...], "k": zk[...], "v": zv[...], "l": zl[...]}
    prev = {"r": pr[...], "k": pk[...], "v": pv[...], "l": plr[...]}
    mu = {"r": mur[...], "k": muk[...], "v": muv[...], "l": mul[...]}
    _rwkv_mix(cur, prev, mu, w0[...], a0[...], kks[...], ka[...], w2p[...], a2p[...], head_dim,
              (o_r, o_ld, o_k, o_v, o_kk, o_be))


def _rwkv_prep_sample(z, zl, prev_rkv, prev_l, consts, *, row0, rows, width, head_dim, col0):
    lw = zl.shape[1]
    c0 = col0 // width
    assert c0 * width == col0
    off = row0 // rows
    cblk = lambda c: pl.BlockSpec((rows, width), lambda i, c=c: (off, c))
    pblk = lambda c: pl.BlockSpec((rows, width), lambda i, c=c: (0, c))
    vec = lambda n: pl.BlockSpec((1, n), lambda i: (0, 0))
    mat = pl.BlockSpec((lw, width), lambda i: (0, 0))
    out_spec = pl.BlockSpec((rows, width), lambda i: (0, 0))
    out_shape = jax.ShapeDtypeStruct((rows, width), _F32)
    kern = functools.partial(_rwkv_prep_sample_kernel, head_dim=head_dim)
    return pl.pallas_call(
        kern,
        grid=(1,),
        in_specs=[cblk(c0), cblk(c0 + 1), cblk(c0 + 2), pl.BlockSpec((rows, lw), lambda i: (off, 0)),
                  pblk(0), pblk(1), pblk(2), pl.BlockSpec((rows, lw), lambda i: (0, 0)),
                  vec(width), vec(width), vec(width), vec(lw),
                  vec(width), vec(width), vec(width), vec(width), mat, mat],
        out_specs=[out_spec] * 6,
        out_shape=[out_shape] * 6,
        compiler_params=_params(("arbitrary",)),
        name="rwkv_prep_sample",
    )(z, z, z, zl, prev_rkv, prev_rkv, prev_rkv, prev_l, *consts)


def _scan_chunks(inputs, states, c):
    pairs = range(len(inputs))
    n2 = _LANES
    bf = lambda x: x.astype(_BF16)
    m0, m1 = c["m0"], c["m1"]
    expand = lambda x: jnp.concatenate([x * m0, x * m1], axis=0)
    stack = lambda a, b: jnp.concatenate([expand(a), expand(b)], axis=0)

    cum = [_mm_exact_rhs_t(c["ltri"], inputs[q][1]) for q in pairs]
    tot = [cum[q][_CHUNK - 1:_CHUNK, :] for q in pairs]
    xa, xb, bk, ve = [], [], [], []
    for q in pairs:
        r, ld, k2, v, kk, be = inputs[q]
        e_neg = jnp.exp(-cum[q])
        e_rem = jnp.exp(tot[q] - cum[q])
        xa.append(bf(stack(-kk * jnp.exp(cum[q] - ld), r * jnp.exp(cum[q]))))
        xb.append(bf(stack(be * e_neg, k2 * e_neg)))
        bk.append(bf(stack(be * e_rem, k2 * e_rem)))
        ve.append(bf(expand(v)))
    g = [_dot_nt(xa[q], xb[q]) for q in pairs]
    ps = [_dot_nt(xa[q], bf(states[q])) for q in pairs]
    a_kk = [bf(jnp.concatenate([jnp.where(c["strict"], g[q][:n2, n2:], 0.0),
                                jnp.where(c["incl"], g[q][n2:, n2:], 0.0)], axis=0)) for q in pairs]
    av = [_dot(a_kk[q], ve[q]) for q in pairs]
    npow = [jnp.where(c["strict"], g[q][:n2, :n2], 0.0) for q in pairs]
    u = [ps[q][:n2] + av[q][:n2] for q in pairs]
    for k in range(c["squarings"]):
        out = [_dot(bf(npow[q]), bf(jnp.concatenate([npow[q], u[q]], axis=1))) for q in pairs]
        npow = [out[q][:, :n2] for q in pairs]
        u = [u[q] + out[q][:, n2:] for q in pairs]
    ub = [bf(u[q]) for q in pairs]
    u = [u[q] + _dot(bf(npow[q]), ub[q]) for q in pairs]
    ub = [bf(u[q]) for q in pairs]
    a_rb = [bf(jnp.where(c["incl"], g[q][n2:, :n2], 0.0)) for q in pairs]
    ye = [ps[q][n2:] + av[q][n2:] + _dot(a_rb[q], ub[q]) for q in pairs]
    ys = [ye[q][:_CHUNK] + ye[q][_CHUNK:] for q in pairs]
    new_states = [states[q] * jnp.exp(tot[q]) + _dot_tn(jnp.concatenate([ub[q], ve[q]], axis=0), bk[q])
                  for q in pairs]
    return ys, new_states


def _mm_exact_rhs_t(l_bf16, x):
    x_hi, x_lo = _split(x)
    x_lo2 = (x - x_hi.astype(_F32) - x_lo.astype(_F32)).astype(_BF16)
    return _dot(l_bf16, x_hi) + (_dot(l_bf16, x_lo) + _dot(l_bf16, x_lo2))


def _scan_consts():
    n = _LANES
    half = _CHUNK
    ri = lax.broadcasted_iota(jnp.int32, (n, n), 0)
    ci = lax.broadcasted_iota(jnp.int32, (n, n), 1)
    same = (ri // half) == (ci // half)
    lane = lax.broadcasted_iota(jnp.int32, (half, n), 1)
    tr = lax.broadcasted_iota(jnp.int32, (half, half), 0)
    tc = lax.broadcasted_iota(jnp.int32, (half, half), 1)
    squarings = 0
    while (2 << squarings) < half:
        squarings += 1
    return {
        "strict": same & ((ri % half) > (ci % half)),
        "incl": same & ((ri % half) >= (ci % half)),
        "m0": jnp.where(lane < half, 1.0, 0.0).astype(_F32),
        "m1": jnp.where(lane >= half, 1.0, 0.0).astype(_F32),
        "ltri": jnp.where(tr >= tc, 1.0, 0.0).astype(_BF16),
        "squarings": squarings,
    }


def _rwkv_scan_kernel(r_ref, ld_ref, k_ref, v_ref, kk_ref, be_ref, y_ref, s_ref, st_ref, *, tb, pp):
    t = pl.program_id(2)

    @pl.when(t == 0)
    def _():
        st_ref[...] = jnp.zeros_like(st_ref)

    c = _scan_consts()
    in_refs = (r_ref, ld_ref, k_ref, v_ref, kk_ref, be_ref)

    def body(ci, carry):
        rows = pl.ds(pl.multiple_of(ci * _CHUNK, _CHUNK), _CHUNK)
        cols = [slice(q * _LANES, (q + 1) * _LANES) for q in range(pp)]
        ys, new_states = _scan_chunks([tuple(ref[rows, cols[q]] for ref in in_refs) for q in range(pp)],
                                      [st_ref[q] for q in range(pp)], c)
        for q in range(pp):
            y_ref[rows, cols[q]] = ys[q]
            st_ref[q] = new_states[q]
        return carry

    lax.fori_loop(0, tb // _CHUNK, body, 0)

    @pl.when(t == pl.num_programs(2) - 1)
    def _():
        s_ref[0] = st_ref[...]


def _rwkv_scan(arrs, *, batch, seq, width, tb, pp):
    n_t = seq // tb
    n_p = width // _LANES
    blk = pl.BlockSpec((tb, pp * _LANES), lambda b, p, t: (b * n_t + t, p))
    kern = functools.partial(_rwkv_scan_kernel, tb=tb, pp=pp)
    return pl.pallas_call(
        kern,
        grid=(batch, n_p // pp, n_t),
        in_specs=[blk] * 6,
        out_specs=[blk, pl.BlockSpec((1, pp, _LANES, _LANES), lambda b, p, t: (b, p, 0, 0))],
        out_shape=[jax.ShapeDtypeStruct((batch * seq, width), _F32),
                   jax.ShapeDtypeStruct((batch, n_p, _LANES, _LANES), _F32)],
        scratch_shapes=[pltpu.VMEM((pp, _LANES, _LANES), _F32)],
        compiler_params=_params(("parallel", "parallel", "arbitrary")),
        name="rwkv_scan",
    )(*arrs)


def _rwkv_step_kernel(r_ref, ld_ref, k_ref, v_ref, kk_ref, be_ref, s_ref, y_ref, so_ref, *, head_dim):
    s = s_ref[...]
    eye = (lax.broadcasted_iota(jnp.int32, (head_dim, head_dim), 0)
           == lax.broadcasted_iota(jnp.int32, (head_dim, head_dim), 1))
    col = lambda row: jnp.sum(jnp.where(eye, row, 0.0), axis=-1, keepdims=True)
    sa = -jnp.sum(s * kk_ref[...], axis=-1, keepdims=True)
    s_new = s * jnp.exp(ld_ref[...]) + sa * be_ref[...] + col(v_ref[...]) * k_ref[...]
    y_col = jnp.sum(s_new * r_ref[...], axis=-1, keepdims=True)
    y_ref[...] = jnp.sum(jnp.where(eye, y_col, 0.0), axis=-2, keepdims=True)
    so_ref[...] = s_new


def _rwkv_step(arrs, state, *, bb):
    n, h, hd, _ = state.shape
    vec = pl.BlockSpec((bb, h, 1, hd), lambda i: (i, 0, 0, 0))
    mat = pl.BlockSpec((bb, h, hd, hd), lambda i: (i, 0, 0, 0))
    kern = functools.partial(_rwkv_step_kernel, head_dim=hd)
    return pl.pallas_call(
        kern,
        grid=(n // bb,),
        in_specs=[vec] * 6 + [mat],
        out_specs=[vec, mat],
        out_shape=[jax.ShapeDtypeStruct((n, h, 1, hd), _F32), jax.ShapeDtypeStruct(state.shape, _F32)],
        compiler_params=_params(("parallel",)),
        name="rwkv_step",
    )(*[a.reshape(n, h, 1, hd) for a in arrs], state)


def _rwkv_post_kernel(y_ref, r_ref, k_ref, v_ref, zg_ref, rk_ref, lw_ref, lb_ref, o_ref, *, head_dim):
    avg_bd = _head_block_matrix(1.0 / head_dim, head_dim)
    ones_bd = _head_block_matrix(1.0, head_dim)
    width = y_ref.shape[1]
    for p in range(width // _LANES):
        cols = slice(p * _LANES, (p + 1) * _LANES)
        y = y_ref[:, cols]
        d = y - _mm_exact_rhs(y, avg_bd)
        var = _mm_exact_rhs(d * d, avg_bd)
        yn = d * lax.rsqrt(var + _GN_EPS) * lw_ref[:, cols] + lb_ref[:, cols]
        bonus = _mm_exact_rhs(r_ref[:, cols] * k_ref[:, cols] * rk_ref[:, cols], ones_bd) * v_ref[:, cols]
        o_ref[:, cols] = ((yn + bonus) * _silu(zg_ref[:, cols])).astype(o_ref.dtype)


def _rwkv_post(y, r, k2, v, z, consts, *, row0, rows, width, tb, head_dim, col0):
    off = row0 // tb
    cg = col0 // width
    assert cg * width == col0
    blk = pl.BlockSpec((tb, width), lambda i: (i, 0))
    vec = pl.BlockSpec((1, width), lambda i: (0, 0))
    kern = functools.partial(_rwkv_post_kernel, head_dim=head_dim)
    return pl.pallas_call(
        kern,
        grid=(rows // tb,),
        in_specs=[blk, blk, blk, blk, pl.BlockSpec((tb, width), lambda i: (off + i, cg)), vec, vec, vec],
        out_specs=blk,
        out_shape=jax.ShapeDtypeStruct((rows, width), _BF16),
        compiler_params=_params(("parallel",)),
        name="rwkv_post",
    )(y, r, k2, v, z, *consts)


def _mem_attn_prompt_kernel(q_ref, zg_ref, k_ref, v_ref, o_ref, *, heads, head_dim):
    scale = head_dim ** -0.5
    for h in range(heads):
        cols = slice(h * head_dim, (h + 1) * head_dim)
        s = _dot_nt(q_ref[:, cols].astype(_BF16), k_ref[:, cols].astype(_BF16)) * scale
        p = jnp.exp(s - jnp.max(s, axis=-1, keepdims=True))
        o = _dot(p.astype(_BF16), v_ref[:, cols].astype(_BF16)) / jnp.sum(p, axis=-1, keepdims=True)
        o_ref[:, cols] = (o * _silu(zg_ref[:, cols])).astype(o_ref.dtype)


def _mem_attn_prompt(z, mkv, *, batch, seq, n_mem, heads, head_dim, tq, col0):
    n_t = seq // tq
    mw = heads * head_dim
    q_col = col0 // mw
    assert q_col * mw == col0
    kern = functools.partial(_mem_attn_prompt_kernel, heads=heads, head_dim=head_dim)
    return pl.pallas_call(
        kern,
        grid=(batch, n_t),
        in_specs=[
            pl.BlockSpec((tq, mw), lambda b, t: (b * n_t + t, q_col)),
            pl.BlockSpec((tq, mw), lambda b, t: (b * n_t + t, q_col + 1)),
            pl.BlockSpec((n_mem, mw), lambda b, t: (b, 0)),
            pl.BlockSpec((n_mem, mw), lambda b, t: (b, 1)),
        ],
        out_specs=pl.BlockSpec((tq, mw), lambda b, t: (b * n_t + t, 0)),
        out_shape=jax.ShapeDtypeStruct((batch * seq, mw), _BF16),
        compiler_params=_params(("parallel", "parallel")),
        name="mem_attn_prompt",
    )(z, z, mkv, mkv)


def _mem_attn_sample_kernel(q_ref, zg_ref, k_ref, v_ref, o_ref, *, bb, head_dim):
    scale = head_dim ** -0.5
    for b in range(bb):
        s = jnp.sum(k_ref[b] * q_ref[b][None], axis=-1, keepdims=True) * scale
        p = jnp.exp(s - jnp.max(s, axis=0, keepdims=True))
        o = jnp.sum(p * v_ref[b], axis=0) / jnp.sum(p, axis=0)
        o_ref[b] = o * _silu(zg_ref[b])


def _mem_attn_sample(q, zg, mem_k, mem_v, *, bb):
    n, n_mem, heads, head_dim = mem_k.shape
    vec = pl.BlockSpec((bb, heads, head_dim), lambda i: (i, 0, 0))
    mat = pl.BlockSpec((bb, n_mem, heads, head_dim), lambda i: (i, 0, 0, 0))
    kern = functools.partial(_mem_attn_sample_kernel, bb=bb, head_dim=head_dim)
    return pl.pallas_call(
        kern,
        grid=(n // bb,),
        in_specs=[vec, vec, mat, mat],
        out_specs=vec,
        out_shape=jax.ShapeDtypeStruct((n, heads, head_dim), _F32),
        compiler_params=_params(("parallel",)),
        name="mem_attn_sample",
    )(q, zg, mem_k, mem_v)


def _branch_kernel(po_ref, ro_ref, mo_ref, wp_ref, wr_ref, wm_ref, gp_ref, gr_ref, gm_ref,
                   bp_ref, br_ref, bm_ref, o_ref):
    gate = lambda g_ref, b_ref: jax.nn.sigmoid(g_ref[...] + b_ref[...])
    h = gate(gp_ref, bp_ref) * _dot(po_ref[...], wp_ref[...])
    h = h + gate(gr_ref, br_ref) * _dot(ro_ref[...], wr_ref[...])
    h = h + gate(gm_ref, bm_ref) * _dot(mo_ref[...], wm_ref[...])
    o_ref[...] = h.astype(o_ref.dtype)


def _branch(po, ro, mo, wp, wr, wm, z, b_gate, *, row0, tm, tn, gate_col0):
    rows = po.shape[0]
    d = wp.shape[1]
    off = row0 // tm
    g0 = gate_col0 // tn
    nb = d // tn
    act = lambda a: pl.BlockSpec((tm, a.shape[1]), lambda i, j: (i, 0))
    wgt = lambda w: pl.BlockSpec((w.shape[0], tn), lambda i, j: (0, j))
    gat = lambda k: pl.BlockSpec((tm, tn), lambda i, j, k=k: (off + i, g0 + k * nb + j))
    bia = lambda k: pl.BlockSpec((1, tn), lambda i, j, k=k: (0, k * nb + j))
    return pl.pallas_call(
        _branch_kernel,
        grid=(rows // tm, nb),
        in_specs=[act(po), act(ro), act(mo), wgt(wp), wgt(wr), wgt(wm), gat(0), gat(1), gat(2),
                  bia(0), bia(1), bia(2)],
        out_specs=pl.BlockSpec((tm, tn), lambda i, j: (i, j)),
        out_shape=jax.ShapeDtypeStruct((rows, d), _BF16),
        compiler_params=_params(("parallel", "parallel")),
        name="branch_proj",
    )(po, ro, mo, wp, wr, wm, z, z, z, b_gate, b_gate, b_gate)


def _out_kernel(h_ref, w_ref, x_ref, g_ref, b_ref, o_ref, acc_ref, *, alpha):
    k = pl.program_id(1)

    @pl.when(k == 0)
    def _():
        acc_ref[...] = alpha * x_ref[...]

    acc_ref[...] += _dot(h_ref[...], w_ref[...])

    @pl.when(k == pl.num_programs(1) - 1)
    def _():
        xf = acc_ref[...]
        mu = jnp.mean(xf, axis=-1, keepdims=True)
        d = xf - mu
        var = jnp.mean(d * d, axis=-1, keepdims=True)
        o_ref[...] = d * lax.rsqrt(var + _LN_EPS) * g_ref[...] + b_ref[...]


def _out_proj(h, w_out, x, ln_g, ln_b, *, tm, tk, alpha):
    rows, d = x.shape
    kern = functools.partial(_out_kernel, alpha=alpha)
    return pl.pallas_call(
        kern,
        grid=(rows // tm, d // tk),
        in_specs=[
            pl.BlockSpec((tm, tk), lambda i, k: (i, k)),
            pl.BlockSpec((tk, d), lambda i, k: (k, 0)),
            pl.BlockSpec((tm, d), lambda i, k: (i, 0)),
            pl.BlockSpec((1, d), lambda i, k: (0, 0)),
            pl.BlockSpec((1, d), lambda i, k: (0, 0)),
        ],
        out_specs=pl.BlockSpec((tm, d), lambda i, k: (i, 0)),
        out_shape=jax.ShapeDtypeStruct((rows, d), _F32),
        scratch_shapes=[pltpu.VMEM((tm, d), _F32)],
        compiler_params=_params(("parallel", "arbitrary")),
        name="out_proj_ln",
    )(h, w_out, x, ln_g, ln_b)


def _tiles(batch, seq, n_sample):
    n_prompt = batch * seq
    m_all = n_prompt + n_sample
    tm_in = next(t for t in (832, 640, 512, 256, 128, 64, 32, 16) if m_all % t == 0)
    return {
        "in_tm": tm_in, "in_tn": 1024,
        "row_tb": min(256, seq), "prep_tb": min(128, seq),
        "scan_tb": min(256, seq), "scan_pp": 4,
        "attn_tq": min(512, seq),
        "proj_tm": min(512, n_prompt), "proj_tn": 512, "out_tm": min(256, n_prompt), "out_tk": 1024,
        "pool_bb": min(32, n_sample), "step_bb": min(8, n_sample), "attn_bb": 2,
    }


def _layer(xp, xs, mem_p, cache_k, cache_v, st_pool, st_shift, st_rwkv, lp):
    batch, seq, d = xp.shape
    ns = xs.shape[0]
    n_p = batch * seq
    m_all = n_p + ns
    pw = lp["pool_scale"].shape[-1]
    rw = lp["rwkv_w0"].shape[-1]
    heads, hd = lp["rwkv_r_k"].shape
    lora = lp["rwkv_w2"].shape[0]
    sw = lp["rwkv_mu"].shape[-1]
    n_mem, mh, md = cache_k.shape[1:]
    mw = mh * md
    tl = _tiles(batch, seq, ns)

    w_in = lp["w_in"]
    lo0, lo1 = 2 * pw + 3 * rw, 2 * pw + sw
    lw = 2 * _LANES
    w_main = jnp.concatenate([w_in[:, :lo0], w_in[:, lo1:]], axis=1).astype(_BF16)
    w_lora = jnp.pad(w_in[:, lo0:lo1], ((0, 0), (0, lw - 2 * lora))).astype(_BF16)
    gate_col0 = lo0 + rw + 2 * mw

    x_all = jnp.concatenate([xp.reshape(n_p, d), xs.reshape(ns, d)], axis=0)
    x_bf = x_all.astype(_BF16)
    z = _matmul(x_bf, w_main, tm=tl["in_tm"], tn=tl["in_tn"], out_dtype=_F32, name="in_proj")
    zl = _matmul(x_bf, w_lora, tm=tl["in_tm"], tn=lw, out_dtype=_F32, name="in_proj_lora")

    pool_w = lp["pool_w"].astype(_BF16)
    pool_scale = lp["pool_scale"].reshape(1, pw)
    po_p = _pool_prompt(z, pool_w, pool_scale, batch=batch, seq=seq, width=pw, tb=tl["row_tb"])
    po_s = _pool_sample(z, st_pool.reshape(ns, -1), pool_w, pool_scale, row0=n_p, rows=ns, width=pw,
                        bb=tl["pool_bb"])
    nbuf = st_pool.shape[1]
    new_pool_p = jnp.stack([z[(b + 1) * seq - nbuf:(b + 1) * seq, :pw] for b in range(batch)])
    new_pool_s = jnp.concatenate([st_pool[:, 1:, :], z[n_p:, :pw].reshape(ns, 1, pw)], axis=1)

    mu = lp["rwkv_mu"]
    pad_l = lambda a: jnp.pad(a, ((0, 0), (0, lw - 2 * lora)))
    w2p = jnp.zeros((lw, rw), _F32).at[:lora].set(lp["rwkv_w2"]).astype(_BF16)
    a2p = jnp.zeros((lw, rw), _F32).at[lora:2 * lora].set(lp["rwkv_a2"]).astype(_BF16)
    row = lambda a: a.reshape(1, -1)
    mix_consts = (row(mu[:rw]), row(mu[rw:2 * rw]), row(mu[2 * rw:3 * rw]), pad_l(row(mu[3 * rw:])),
                  row(lp["rwkv_w0"]), row(lp["rwkv_a0"]), row(lp["rwkv_k_k"]), row(lp["rwkv_k_a"]), w2p, a2p)
    post_consts = (row(lp["rwkv_r_k"]), row(lp["rwkv_ln_w"]), row(lp["rwkv_ln_b"]))

    r_p, ld_p, k_p, v_p, kk_p, be_p = _rwkv_prep_prompt(z, zl, mix_consts, batch=batch, seq=seq, width=rw,
                                                        tb=tl["prep_tb"], head_dim=hd, col0=2 * pw)
    y_p, st_pairs = _rwkv_scan((r_p, ld_p, k_p, v_p, kk_p, be_p), batch=batch, seq=seq, width=rw,
                               tb=tl["scan_tb"], pp=tl["scan_pp"])
    ro_p = _rwkv_post(y_p, r_p, k_p, v_p, z, post_consts, row0=0, rows=n_p, width=rw, tb=tl["row_tb"],
                      head_dim=hd, col0=lo0)
    st5 = st_pairs.reshape(batch, heads // 2, 2, hd, 2, hd)
    new_state_p = jnp.stack([st5[:, :, 0, :, 0, :], st5[:, :, 1, :, 1, :]], axis=2).reshape(batch, heads, hd, hd)

    shift_s = st_shift.reshape(ns, sw)
    r_s, ld_s, k_s, v_s, kk_s, be_s = _rwkv_prep_sample(z, zl, shift_s[:, :3 * rw], pad_l(shift_s[:, 3 * rw:]),
                                                        mix_consts, row0=n_p, rows=ns, width=rw, head_dim=hd,
                                                        col0=2 * pw)
    y_s4, new_state_s = _rwkv_step((r_s, ld_s, k_s, v_s, kk_s, be_s), st_rwkv, bb=tl["step_bb"])
    ro_s = _rwkv_post(y_s4.reshape(ns, rw), r_s, k_s, v_s, z, post_consts, row0=n_p, rows=ns, width=rw, tb=ns,
                      head_dim=hd, col0=lo0)
    shift_rows = lambda lo, hi: jnp.concatenate([z[lo:hi, 2 * pw:lo0], zl[lo:hi, :2 * lora]], axis=1)
    new_shift_p = jnp.stack([shift_rows((b + 1) * seq - 1, (b + 1) * seq) for b in range(batch)])
    new_shift_s = shift_rows(n_p, m_all).reshape(ns, 1, sw)

    mkv = _matmul(mem_p.reshape(batch * n_mem, d).astype(_BF16), lp["w_mem_kv"].astype(_BF16),
                  tm=min(512, batch * n_mem), tn=min(1024, 2 * mw), out_dtype=_F32, name="mem_kv_proj")
    q_col0 = lo0 + rw
    mo_p = _mem_attn_prompt(z, mkv, batch=batch, seq=seq, n_mem=n_mem, heads=mh, head_dim=md, tq=tl["attn_tq"],
                            col0=q_col0)
    mo_s = _mem_attn_sample(z[n_p:, q_col0:q_col0 + mw].reshape(ns, mh, md),
                            z[n_p:, q_col0 + mw:q_col0 + 2 * mw].reshape(ns, mh, md),
                            cache_k, cache_v, bb=tl["attn_bb"]).reshape(ns, mw).astype(_BF16)
    mk_p = mkv[:, :mw].reshape(batch, n_mem, mh, md)
    mv_p = mkv[:, mw:].reshape(batch, n_mem, mh, md)

    wp = lp["w_branch_pool"].astype(_BF16)
    wr = lp["w_branch_rwkv"].astype(_BF16)
    wm = lp["w_branch_mem"].astype(_BF16)
    w_out = lp["w_out"].astype(_BF16)
    b_gate = row(lp["b_gate"])
    ln_g, ln_b = row(lp["ln_g"]), row(lp["ln_b"])
    alpha = lp["alpha"]
    outs = []
    for po, ro, mo, x2d, row0 in ((po_p, ro_p, mo_p, xp.reshape(n_p, d), 0), (po_s, ro_s, mo_s, xs.reshape(ns, d), n_p)):
        rows = x2d.shape[0]
        h = _branch(po, ro, mo, wp, wr, wm, z, b_gate, row0=row0, tm=min(tl["proj_tm"], rows), tn=tl["proj_tn"],
                    gate_col0=gate_col0)
        outs.append(_out_proj(h, w_out, x2d, ln_g, ln_b, tm=min(tl["out_tm"], rows), tk=tl["out_tk"], alpha=alpha))
    y_prompt = outs[0].reshape(batch, seq, d)
    y_sample = outs[1].reshape(ns, 1, d)
    return (y_prompt, y_sample, mk_p, mv_p, new_pool_p, new_shift_p, new_state_p, new_pool_s, new_shift_s,
            new_state_s)


def kernel(x_prompt, x_sample, cache_mem_k, cache_mem_v, state_pool, state_shift, state_rwkv, mem_prompt, w_in,
           b_gate, pool_w, pool_scale, rwkv_mu, rwkv_w0, rwkv_w2, rwkv_a0, rwkv_a2, rwkv_k_k, rwkv_k_a, rwkv_r_k,
           rwkv_ln_w, rwkv_ln_b, w_mem_kv, w_branch_pool, w_branch_rwkv, w_branch_mem, w_out, ln_g, ln_b):
    depth = w_in.shape[0]
    alpha = (2.0 * depth) ** 0.25
    weights = dict(w_in=w_in, b_gate=b_gate, pool_w=pool_w, pool_scale=pool_scale, rwkv_mu=rwkv_mu, rwkv_w0=rwkv_w0,
                   rwkv_w2=rwkv_w2, rwkv_a0=rwkv_a0, rwkv_a2=rwkv_a2, rwkv_k_k=rwkv_k_k, rwkv_k_a=rwkv_k_a,
                   rwkv_r_k=rwkv_r_k, rwkv_ln_w=rwkv_ln_w, rwkv_ln_b=rwkv_ln_b, w_mem_kv=w_mem_kv,
                   w_branch_pool=w_branch_pool, w_branch_rwkv=w_branch_rwkv, w_branch_mem=w_branch_mem,
                   w_out=w_out, ln_g=ln_g, ln_b=ln_b)
    yp, ys = x_prompt, x_sample
    per_layer = []
    for l in range(depth):
        lp = {k: v[l] for k, v in weights.items()}
        lp["alpha"] = alpha
        res = _layer(yp, ys, mem_prompt, cache_mem_k[l], cache_mem_v[l], state_pool[l], state_shift[l],
                     state_rwkv[l], lp)
        yp, ys = res[0], res[1]
        per_layer.append(res[2:])
    stacked = tuple(jnp.stack([pl_[i] for pl_ in per_layer]) for i in range(8))
    return (yp, ys) + stacked
```

```python
import functools

import jax
import jax.numpy as jnp
from jax import lax
from jax.experimental import pallas as pl
from jax.experimental.pallas import tpu as pltpu

_F32 = jnp.float32
_BF16 = jnp.bfloat16

_POOL_WINDOWS = (2, 4, 8, 16)
_PAST_LEN = 16384
_GN_EPS = 64e-5
_LN_EPS = 1e-5
_L2_EPS = 1e-12

_LANES = 128
_MIB = 1024 * 1024
_VMEM_LIMIT = 56 * _MIB
_VMEM_LIMIT_RESIDENT_WEIGHT = 62 * _MIB

_HEAD_PAIR = _LANES
_CHUNK = 64


def _params(sem, vmem_limit=_VMEM_LIMIT):
    return pltpu.CompilerParams(dimension_semantics=sem, vmem_limit_bytes=vmem_limit)


def _dot(a, b):
    return jnp.dot(a, b, preferred_element_type=_F32)


def _dot_nt(a, b):
    return lax.dot_general(a, b, (((1,), (1,)), ((), ())), preferred_element_type=_F32)


def _dot_tn(a, b):
    return lax.dot_general(a, b, (((0,), (0,)), ((), ())), preferred_element_type=_F32)


def _split(x):
    hi = x.astype(_BF16)
    lo = (x - hi.astype(_F32)).astype(_BF16)
    return hi, lo


def _mm1(dot, a, b):
    return dot(a.astype(_BF16), b.astype(_BF16))


def _mm_exact_rhs(a, b_bf16):
    a_hi, a_lo = _split(a)
    return _dot(a_hi, b_bf16) + _dot(a_lo, b_bf16)


def _silu(x):
    return x * jax.nn.sigmoid(x)


def _head_block_matrix(value, head_dim):
    r = lax.broadcasted_iota(jnp.int32, (_LANES, _LANES), 0) // head_dim
    c = lax.broadcasted_iota(jnp.int32, (_LANES, _LANES), 1) // head_dim
    return jnp.where(r == c, value, 0.0).astype(_BF16)


def _matmul_kernel(x_ref, w_ref, o_ref):
    o_ref[...] = _dot(x_ref[...], w_ref[...]).astype(o_ref.dtype)


def _matmul(x, w, *, tm, tn, out_dtype, name):
    m, k = x.shape
    n = w.shape[1]
    return pl.pallas_call(
        _matmul_kernel,
        grid=(m // tm, n // tn),
        in_specs=[pl.BlockSpec((tm, k), lambda i, j: (i, 0)), pl.BlockSpec((k, tn), lambda i, j: (0, j))],
        out_specs=pl.BlockSpec((tm, tn), lambda i, j: (i, j)),
        out_shape=jax.ShapeDtypeStruct((m, n), out_dtype),
        compiler_params=_params(("parallel", "parallel")),
        name=name,
    )(x, w)


def _in_proj_kernel(x_ref, w_ref, o_ref, wb_ref):
    @pl.when(pl.program_id(1) == 0)
    def _():
        wb_ref[...] = w_ref[...].astype(_BF16)

    o_ref[...] = _dot(x_ref[...], wb_ref[...])


def _in_proj(x, w, *, tm, tn):
    m, k = x.shape
    n = w.shape[1]
    return pl.pallas_call(
        _in_proj_kernel,
        grid=(pl.cdiv(n, tn), m // tm),
        in_specs=[pl.BlockSpec((tm, k), lambda j, i: (i, 0)), pl.BlockSpec((k, tn), lambda j, i: (0, j))],
        out_specs=pl.BlockSpec((tm, tn), lambda j, i: (i, j)),
        out_shape=jax.ShapeDtypeStruct((m, n), _F32),
        scratch_shapes=[pltpu.VMEM((k, tn), _BF16)],
        compiler_params=_params(("parallel", "arbitrary")),
        name="in_proj",
    )(x, w)


def _pool_mix(pooled_fn, zp_ref, pw_ref, ps_ref, o_ref, group):
    for g in range(len(_POOL_WINDOWS)):
        cols = slice(g * group, (g + 1) * group)
        mixed = _dot(pooled_fn(g, cols).astype(_BF16), pw_ref[g])
        o_ref[:, cols] = (mixed * ps_ref[:, cols] * _silu(zp_ref[:, cols])).astype(o_ref.dtype)


def _pool_prompt_kernel(u_ref, halo_ref, zp_ref, pw_ref, ps_ref, o_ref, e_ref, *, tb, hist, group):
    t = pl.program_id(1)
    e_ref[0:hist, :] = jnp.where(t == 0, 0.0, halo_ref[...])
    e_ref[hist:hist + tb, :] = u_ref[...]
    pos = t * tb + lax.broadcasted_iota(jnp.int32, (tb, group), 0)

    def pooled(g, cols):
        win = _POOL_WINDOWS[g]
        x = e_ref[hist:hist + tb, cols]
        acc = x
        for d in range(1, win):
            acc = acc + e_ref[hist - d:hist - d + tb, cols]
        cnt = jnp.minimum(pos + 1, win).astype(_F32)
        return acc / cnt - x

    _pool_mix(pooled, zp_ref, pw_ref, ps_ref, o_ref, group)


def _pool_prompt(z, pool_w, pool_scale, *, batch, seq, width, tb):
    hist = 16
    n_t = seq // tb
    group = width // len(_POOL_WINDOWS)
    kern = functools.partial(_pool_prompt_kernel, tb=tb, hist=hist, group=group)
    return pl.pallas_call(
        kern,
        grid=(batch, n_t),
        in_specs=[
            pl.BlockSpec((tb, width), lambda b, t: (b * n_t + t, 0)),
            pl.BlockSpec((hist, width), lambda b, t: (jnp.maximum((b * seq + t * tb) // hist - 1, 0), 0)),
            pl.BlockSpec((tb, width), lambda b, t: (b * n_t + t, 1)),
            pl.BlockSpec(pool_w.shape, lambda b, t: (0, 0, 0)),
            pl.BlockSpec((1, width), lambda b, t: (0, 0)),
        ],
        out_specs=pl.BlockSpec((tb, width), lambda b, t: (b * n_t + t, 0)),
        out_shape=jax.ShapeDtypeStruct((batch * seq, width), _BF16),
        scratch_shapes=[pltpu.VMEM((hist + tb, width), _F32)],
        compiler_params=_params(("parallel", "parallel")),
        name="pool_prompt",
    )(z, z, z, pool_w, pool_scale)


def _pool_sample_kernel(u_ref, buf_ref, zp_ref, pw_ref, ps_ref, o_ref, *, width, nbuf, group):
    def pooled(g, cols):
        win = _POOL_WINDOWS[g]
        x = u_ref[:, cols]
        acc = x
        for d in range(1, win):
            row = nbuf - d
            acc = acc + buf_ref[:, row * width + cols.start:row * width + cols.stop]
        cnt = float(min(_PAST_LEN + 1, win))
        return acc / cnt - x

    _pool_mix(pooled, zp_ref, pw_ref, ps_ref, o_ref, group)


def _pool_sample(z, buf2d, pool_w, pool_scale, *, row0, rows, width, bb):
    nbuf = buf2d.shape[1] // width
    group = width // len(_POOL_WINDOWS)
    off = row0 // bb
    kern = functools.partial(_pool_sample_kernel, width=width, nbuf=nbuf, group=group)
    return pl.pallas_call(
        kern,
        grid=(rows // bb,),
        in_specs=[
            pl.BlockSpec((bb, width), lambda i: (off + i, 0)),
            pl.BlockSpec((bb, nbuf * width), lambda i: (i, 0)),
            pl.BlockSpec((bb, width), lambda i: (off + i, 1)),
            pl.BlockSpec(pool_w.shape, lambda i: (0, 0, 0)),
            pl.BlockSpec((1, width), lambda i: (0, 0)),
        ],
        out_specs=pl.BlockSpec((bb, width), lambda i: (i, 0)),
        out_shape=jax.ShapeDtypeStruct((rows, width), _BF16),
        compiler_params=_params(("parallel",)),
        name="pool_sample",
    )(z, buf2d, z, pool_w, pool_scale)


def _rwkv_mix(cur, prev, mu, w0, a0, kk_scale, ka, w2p, a2p, head_dim, outs):
    o_r, o_ld, o_k, o_v, o_kk, o_be = outs
    mix = lambda n: cur[n] + (prev[n] - cur[n]) * mu[n]
    xl = mix("l")
    wl = w0 + _mm1(_dot, jnp.tanh(xl), w2p)
    w_log = -jax.nn.softplus(-wl) - 0.5
    a = jax.nn.sigmoid(a0 + _mm1(_dot, xl, a2p))
    xk = mix("k")
    kkp = xk * kk_scale
    ones_bd = _head_block_matrix(1.0, head_dim)
    width = kkp.shape[1]
    o_r[...] = mix("r")
    o_v[...] = mix("v")
    o_ld[...] = -jnp.exp(w_log)
    o_k[...] = xk * (1.0 + (a - 1.0) * ka)
    for p in range(width // _LANES):
        cols = slice(p * _LANES, (p + 1) * _LANES)
        kp = kkp[:, cols]
        ss = _mm_exact_rhs(kp * kp, ones_bd)
        kk = kp / jnp.maximum(jnp.sqrt(ss), _L2_EPS)
        o_kk[:, cols] = kk
        o_be[:, cols] = kk * a[:, cols]


def _rwkv_prep_prompt_kernel(zr, zk, zv, zl, hr, hk, hv, hl, mur, muk, muv, mul, w0, a0, kks, ka, w2p, a2p,
                             o_r, o_ld, o_k, o_v, o_kk, o_be, er, ek, ev, el, *, tb, head_dim):
    t = pl.program_id(1)
    halo = 8
    cur, prev = {}, {}
    for name, z_ref, h_ref, e_ref in (("r", zr, hr, er), ("k", zk, hk, ek), ("v", zv, hv, ev), ("l", zl, hl, el)):
        e_ref[0:halo, :] = jnp.where(t == 0, 0.0, h_ref[...])
        e_ref[halo:halo + tb, :] = z_ref[...]
        cur[name] = z_ref[...]
        prev[name] = e_ref[halo - 1:halo - 1 + tb, :]
    mu = {"r": mur[...], "k": muk[...], "v": muv[...], "l": mul[...]}
    _rwkv_mix(cur, prev, mu, w0[...], a0[...], kks[...], ka[...], w2p[...], a2p[...], head_dim,
              (o_r, o_ld, o_k, o_v, o_kk, o_be))


def _rwkv_prep_prompt(z, consts, *, batch, seq, width, tb, head_dim, col0, lw):
    n_t = seq // tb
    c0 = col0 // width
    lcol = (col0 + 3 * width) // lw
    assert c0 * width == col0 and lcol * lw == col0 + 3 * width
    halo = 8
    row = lambda b, t: b * n_t + t
    hrow = lambda b, t: jnp.maximum((b * seq + t * tb) // halo - 1, 0)
    cblk = lambda c: pl.BlockSpec((tb, width), lambda b, t, c=c: (row(b, t), c))
    hblk = lambda c: pl.BlockSpec((halo, width), lambda b, t, c=c: (hrow(b, t), c))
    vec = lambda n: pl.BlockSpec((1, n), lambda b, t: (0, 0))
    mat = pl.BlockSpec((lw, width), lambda b, t: (0, 0))
    out_spec = pl.BlockSpec((tb, width), lambda b, t: (row(b, t), 0))
    out_shape = jax.ShapeDtypeStruct((batch * seq, width), _F32)
    kern = functools.partial(_rwkv_prep_prompt_kernel, tb=tb, head_dim=head_dim)
    return pl.pallas_call(
        kern,
        grid=(batch, n_t),
        in_specs=[cblk(c0), cblk(c0 + 1), cblk(c0 + 2), pl.BlockSpec((tb, lw), lambda b, t: (row(b, t), lcol)),
                  hblk(c0), hblk(c0 + 1), hblk(c0 + 2), pl.BlockSpec((halo, lw), lambda b, t: (hrow(b, t), lcol)),
                  vec(width), vec(width), vec(width), vec(lw),
                  vec(width), vec(width), vec(width), vec(width), mat, mat],
        out_specs=[out_spec] * 6,
        out_shape=[out_shape] * 6,
        scratch_shapes=[pltpu.VMEM((halo + tb, width), _F32)] * 3 + [pltpu.VMEM((halo + tb, lw), _F32)],
        compiler_params=_params(("parallel", "parallel")),
        name="rwkv_prep_prompt",
    )(z, z, z, z, z, z, z, z, *consts)


def _rwkv_prep_sample_kernel(zr, zk, zv, zl, pr, pk, pv, plr, mur, muk, muv, mul, w0, a0, kks, ka, w2p, a2p,
                             o_r, o_ld, o_k, o_v, o_kk, o_be, *, head_dim):
    cur = {"r": zr[...], "k": zk[...], "v": zv[...], "l": zl[...]}
    prev = {"r": pr[...], "k": pk[...], "v": pv[...], "l": plr[...]}
    mu = {"r": mur[...], "k": muk[...], "v": muv[...], "l": mul[...]}
    _rwkv_mix(cur, prev, mu, w0[...], a0[...], kks[...], ka[...], w2p[...], a2p[...], head_dim,
              (o_r, o_ld, o_k, o_v, o_kk, o_be))


def _rwkv_prep_sample(z, prev_rkv, prev_l, consts, *, row0, rows, width, head_dim, col0, lw):
    c0 = col0 // width
    lcol = (col0 + 3 * width) // lw
    assert c0 * width == col0 and lcol * lw == col0 + 3 * width
    off = row0 // rows
    cblk = lambda c: pl.BlockSpec((rows, width), lambda i, c=c: (off, c))
    pblk = lambda c: pl.BlockSpec((rows, width), lambda i, c=c: (0, c))
    vec = lambda n: pl.BlockSpec((1, n), lambda i: (0, 0))
    mat = pl.BlockSpec((lw, width), lambda i: (0, 0))
    out_spec = pl.BlockSpec((rows, width), lambda i: (0, 0))
    out_shape = jax.ShapeDtypeStruct((rows, width), _F32)
    kern = functools.partial(_rwkv_prep_sample_kernel, head_dim=head_dim)
    return pl.pallas_call(
        kern,
        grid=(1,),
        in_specs=[cblk(c0), cblk(c0 + 1), cblk(c0 + 2), pl.BlockSpec((rows, lw), lambda i: (off, lcol)),
                  pblk(0), pblk(1), pblk(2), pl.BlockSpec((rows, lw), lambda i: (0, 0)),
                  vec(width), vec(width), vec(width), vec(lw),
                  vec(width), vec(width), vec(width), vec(width), mat, mat],
        out_specs=[out_spec] * 6,
        out_shape=[out_shape] * 6,
        compiler_params=_params(("arbitrary",)),
        name="rwkv_prep_sample",
    )(z, z, z, z, prev_rkv, prev_rkv, prev_rkv, prev_l, *consts)


def _scan_chunks(inputs, states, c):
    pairs = range(len(inputs))
    n2 = _LANES
    bf = lambda x: x.astype(_BF16)
    m0, m1 = c["m0"], c["m1"]
    expand = lambda x: jnp.concatenate([x * m0, x * m1], axis=0)
    stack = lambda a, b: jnp.concatenate([expand(a), expand(b)], axis=0)

    cum = [_mm_exact_rhs_t(c["ltri"], inputs[q][1]) for q in pairs]
    tot = [cum[q][_CHUNK - 1:_CHUNK, :] for q in pairs]
    xa, xb, bk, ve = [], [], [], []
    for q in pairs:
        r, ld, k2, v, kk, be = inputs[q]
        e_neg = jnp.exp(-cum[q])
        e_rem = jnp.exp(tot[q] - cum[q])
        xa.append(bf(stack(-kk * jnp.exp(cum[q] - ld), r * jnp.exp(cum[q]))))
        xb.append(bf(stack(be * e_neg, k2 * e_neg)))
        bk.append(bf(stack(be * e_rem, k2 * e_rem)))
        ve.append(bf(expand(v)))
    g = [_dot_nt(xa[q], xb[q]) for q in pairs]
    ps = [_dot_nt(xa[q], bf(states[q])) for q in pairs]
    a_kk = [bf(jnp.concatenate([jnp.where(c["strict"], g[q][:n2, n2:], 0.0),
                                jnp.where(c["incl"], g[q][n2:, n2:], 0.0)], axis=0)) for q in pairs]
    av = [_dot(a_kk[q], ve[q]) for q in pairs]
    npow = [jnp.where(c["strict"], g[q][:n2, :n2], 0.0) for q in pairs]
    u = [ps[q][:n2] + av[q][:n2] for q in pairs]
    for k in range(c["squarings"]):
        out = [_dot(bf(npow[q]), bf(jnp.concatenate([npow[q], u[q]], axis=1))) for q in pairs]
        npow = [out[q][:, :n2] for q in pairs]
        u = [u[q] + out[q][:, n2:] for q in pairs]
    ub = [bf(u[q]) for q in pairs]
    u = [u[q] + _dot(bf(npow[q]), ub[q]) for q in pairs]
    ub = [bf(u[q]) for q in pairs]
    a_rb = [bf(jnp.where(c["incl"], g[q][n2:, :n2], 0.0)) for q in pairs]
    ye = [ps[q][n2:] + av[q][n2:] + _dot(a_rb[q], ub[q]) for q in pairs]
    ys = [ye[q][:_CHUNK] + ye[q][_CHUNK:] for q in pairs]
    new_states = [states[q] * jnp.exp(tot[q]) + _dot_tn(jnp.concatenate([ub[q], ve[q]], axis=0), bk[q])
                  for q in pairs]
    return ys, new_states


def _mm_exact_rhs_t(l_bf16, x):
    x_hi, x_lo = _split(x)
    x_lo2 = (x - x_hi.astype(_F32) - x_lo.astype(_F32)).astype(_BF16)
    return _dot(l_bf16, x_hi) + (_dot(l_bf16, x_lo) + _dot(l_bf16, x_lo2))


def _scan_consts():
    n = _LANES
    half = _CHUNK
    ri = lax.broadcasted_iota(jnp.int32, (n, n), 0)
    ci = lax.broadcasted_iota(jnp.int32, (n, n), 1)
    same = (ri // half) == (ci // half)
    lane = lax.broadcasted_iota(jnp.int32, (half, n), 1)
    tr = lax.broadcasted_iota(jnp.int32, (half, half), 0)
    tc = lax.broadcasted_iota(jnp.int32, (half, half), 1)
    squarings = 0
    while (2 << squarings) < half:
        squarings += 1
    return {
        "strict": same & ((ri % half) > (ci % half)),
        "incl": same & ((ri % half) >= (ci % half)),
        "m0": jnp.where(lane < half, 1.0, 0.0).astype(_F32),
        "m1": jnp.where(lane >= half, 1.0, 0.0).astype(_F32),
        "ltri": jnp.where(tr >= tc, 1.0, 0.0).astype(_BF16),
        "squarings": squarings,
    }


def _rwkv_scan_kernel(r_ref, ld_ref, k_ref, v_ref, kk_ref, be_ref, y_ref, s_ref, st_ref, *, tb, pp):
    t = pl.program_id(2)

    @pl.when(t == 0)
    def _():
        st_ref[...] = jnp.zeros_like(st_ref)

    c = _scan_consts()
    in_refs = (r_ref, ld_ref, k_ref, v_ref, kk_ref, be_ref)

    def body(ci, carry):
        rows = pl.ds(pl.multiple_of(ci * _CHUNK, _CHUNK), _CHUNK)
        cols = [slice(q * _LANES, (q + 1) * _LANES) for q in range(pp)]
        ys, new_states = _scan_chunks([tuple(ref[rows, cols[q]] for ref in in_refs) for q in range(pp)],
                                      [st_ref[q] for q in range(pp)], c)
        for q in range(pp):
            y_ref[rows, cols[q]] = ys[q]
            st_ref[q] = new_states[q]
        return carry

    lax.fori_loop(0, tb // _CHUNK, body, 0)

    @pl.when(t == pl.num_programs(2) - 1)
    def _():
        s_ref[0] = st_ref[...]


def _rwkv_scan(arrs, *, batch, seq, width, tb, pp):
    n_t = seq // tb
    n_p = width // _LANES
    blk = pl.BlockSpec((tb, pp * _LANES), lambda b, p, t: (b * n_t + t, p))
    kern = functools.partial(_rwkv_scan_kernel, tb=tb, pp=pp)
    return pl.pallas_call(
        kern,
        grid=(batch, n_p // pp, n_t),
        in_specs=[blk] * 6,
        out_specs=[blk, pl.BlockSpec((1, pp, _LANES, _LANES), lambda b, p, t: (b, p, 0, 0))],
        out_shape=[jax.ShapeDtypeStruct((batch * seq, width), _F32),
                   jax.ShapeDtypeStruct((batch, n_p, _LANES, _LANES), _F32)],
        scratch_shapes=[pltpu.VMEM((pp, _LANES, _LANES), _F32)],
        compiler_params=_params(("parallel", "parallel", "arbitrary")),
        name="rwkv_scan",
    )(*arrs)


def _rwkv_step_kernel(r_ref, ld_ref, k_ref, v_ref, kk_ref, be_ref, s_ref, y_ref, so_ref, *, head_dim):
    s = s_ref[...]
    eye = (lax.broadcasted_iota(jnp.int32, (head_dim, head_dim), 0)
           == lax.broadcasted_iota(jnp.int32, (head_dim, head_dim), 1))
    col = lambda row: jnp.sum(jnp.where(eye, row, 0.0), axis=-1, keepdims=True)
    sa = -jnp.sum(s * kk_ref[...], axis=-1, keepdims=True)
    s_new = s * jnp.exp(ld_ref[...]) + sa * be_ref[...] + col(v_ref[...]) * k_ref[...]
    y_col = jnp.sum(s_new * r_ref[...], axis=-1, keepdims=True)
    y_ref[...] = jnp.sum(jnp.where(eye, y_col, 0.0), axis=-2, keepdims=True)
    so_ref[...] = s_new


def _rwkv_step(arrs, state, *, bb):
    n, h, hd, _ = state.shape
    vec = pl.BlockSpec((bb, h, 1, hd), lambda i: (i, 0, 0, 0))
    mat = pl.BlockSpec((bb, h, hd, hd), lambda i: (i, 0, 0, 0))
    kern = functools.partial(_rwkv_step_kernel, head_dim=hd)
    return pl.pallas_call(
        kern,
        grid=(n // bb,),
        in_specs=[vec] * 6 + [mat],
        out_specs=[vec, mat],
        out_shape=[jax.ShapeDtypeStruct((n, h, 1, hd), _F32), jax.ShapeDtypeStruct(state.shape, _F32)],
        compiler_params=_params(("parallel",)),
        name="rwkv_step",
    )(*[a.reshape(n, h, 1, hd) for a in arrs], state)


def _rwkv_post_kernel(y_ref, r_ref, k_ref, v_ref, za_ref, zb_ref, rk_ref, lw_ref, lb_ref, o_ref, *, head_dim, delta):
    avg_bd = _head_block_matrix(1.0 / head_dim, head_dim)
    ones_bd = _head_block_matrix(1.0, head_dim)
    width = y_ref.shape[1]
    zwin = jnp.concatenate([za_ref[...], zb_ref[...]], axis=1)
    for p in range(width // _LANES):
        cols = slice(p * _LANES, (p + 1) * _LANES)
        y = y_ref[:, cols]
        d = y - _mm_exact_rhs(y, avg_bd)
        var = _mm_exact_rhs(d * d, avg_bd)
        yn = d * lax.rsqrt(var + _GN_EPS) * lw_ref[:, cols] + lb_ref[:, cols]
        bonus = _mm_exact_rhs(r_ref[:, cols] * k_ref[:, cols] * rk_ref[:, cols], ones_bd) * v_ref[:, cols]
        zg = zwin[:, delta + p * _LANES:delta + (p + 1) * _LANES]
        o_ref[:, cols] = ((yn + bonus) * _silu(zg)).astype(o_ref.dtype)


def _rwkv_post(y, r, k2, v, z, consts, *, row0, rows, width, tb, head_dim, col0):
    off = row0 // tb
    tail = 2 * _LANES
    cg, delta = col0 // width, col0 % width
    assert delta <= tail and width % tail == 0
    blk = pl.BlockSpec((tb, width), lambda i: (i, 0))
    vec = pl.BlockSpec((1, width), lambda i: (0, 0))
    kern = functools.partial(_rwkv_post_kernel, head_dim=head_dim, delta=delta)
    return pl.pallas_call(
        kern,
        grid=(rows // tb,),
        in_specs=[blk, blk, blk, blk,
                  pl.BlockSpec((tb, width), lambda i: (off + i, cg)),
                  pl.BlockSpec((tb, tail), lambda i: (off + i, (cg + 1) * (width // tail))),
                  vec, vec, vec],
        out_specs=blk,
        out_shape=jax.ShapeDtypeStruct((rows, width), _BF16),
        compiler_params=_params(("parallel",)),
        name="rwkv_post",
    )(y, r, k2, v, z, z, *consts)


def _mem_attn_prompt_kernel(za_ref, zb_ref, zc_ref, k_ref, v_ref, o_ref, *, heads, head_dim, delta):
    scale = head_dim ** -0.5
    mw = heads * head_dim
    zwin = jnp.concatenate([za_ref[...], zb_ref[...], zc_ref[...]], axis=1)
    for h in range(heads):
        cols = slice(h * head_dim, (h + 1) * head_dim)
        q = zwin[:, delta + cols.start:delta + cols.stop]
        zg = zwin[:, delta + mw + cols.start:delta + mw + cols.stop]
        s = _dot_nt(q.astype(_BF16), k_ref[:, cols].astype(_BF16)) * scale
        p = jnp.exp(s - jnp.max(s, axis=-1, keepdims=True))
        o = _dot(p.astype(_BF16), v_ref[:, cols].astype(_BF16)) / jnp.sum(p, axis=-1, keepdims=True)
        o_ref[:, cols] = (o * _silu(zg)).astype(o_ref.dtype)


def _mem_attn_prompt(z, mkv, *, batch, seq, n_mem, heads, head_dim, tq, col0):
    n_t = seq // tq
    mw = heads * head_dim
    tail = 2 * _LANES
    q_col, delta = col0 // mw, col0 % mw
    assert delta <= tail and mw % tail == 0
    kern = functools.partial(_mem_attn_prompt_kernel, heads=heads, head_dim=head_dim, delta=delta)
    return pl.pallas_call(
        kern,
        grid=(batch, n_t),
        in_specs=[
            pl.BlockSpec((tq, mw), lambda b, t: (b * n_t + t, q_col)),
            pl.BlockSpec((tq, mw), lambda b, t: (b * n_t + t, q_col + 1)),
            pl.BlockSpec((tq, tail), lambda b, t: (b * n_t + t, (q_col + 2) * (mw // tail))),
            pl.BlockSpec((n_mem, mw), lambda b, t: (b, 0)),
            pl.BlockSpec((n_mem, mw), lambda b, t: (b, 1)),
        ],
        out_specs=pl.BlockSpec((tq, mw), lambda b, t: (b * n_t + t, 0)),
        out_shape=jax.ShapeDtypeStruct((batch * seq, mw), _BF16),
        compiler_params=_params(("parallel", "parallel")),
        name="mem_attn_prompt",
    )(z, z, z, mkv, mkv)


def _mem_attn_sample_kernel(q_ref, zg_ref, k_ref, v_ref, o_ref, *, bb, head_dim):
    scale = head_dim ** -0.5
    for b in range(bb):
        s = jnp.sum(k_ref[b] * q_ref[b][None], axis=-1, keepdims=True) * scale
        p = jnp.exp(s - jnp.max(s, axis=0, keepdims=True))
        o = jnp.sum(p * v_ref[b], axis=0) / jnp.sum(p, axis=0)
        o_ref[b] = o * _silu(zg_ref[b])


def _mem_attn_sample(q, zg, mem_k, mem_v, *, bb):
    n, n_mem, heads, head_dim = mem_k.shape
    vec = pl.BlockSpec((bb, heads, head_dim), lambda i: (i, 0, 0))
    mat = pl.BlockSpec((bb, n_mem, heads, head_dim), lambda i: (i, 0, 0, 0))
    kern = functools.partial(_mem_attn_sample_kernel, bb=bb, head_dim=head_dim)
    return pl.pallas_call(
        kern,
        grid=(n // bb,),
        in_specs=[vec, vec, mat, mat],
        out_specs=vec,
        out_shape=jax.ShapeDtypeStruct((n, heads, head_dim), _F32),
        compiler_params=_params(("parallel",)),
        name="mem_attn_sample",
    )(q, zg, mem_k, mem_v)


def _branch_kernel(po_ref, ro_ref, mo_ref, wp_ref, wr_ref, wm_ref, gp_ref, gr_ref, gm_ref, xp_ref, xr_ref, xm_ref,
                   bp_ref, br_ref, bm_ref, o_ref, *, rot):
    tm, tn = o_ref.shape
    col = pl.program_id(1) * tn + lax.broadcasted_iota(jnp.int32, (tm, tn), 1)
    wrap = col < rot
    gate = lambda g_ref, x_ref, b_ref: jax.nn.sigmoid(jnp.where(wrap, x_ref[...], g_ref[...]) + b_ref[...])
    h = gate(gp_ref, xp_ref, bp_ref) * _dot(po_ref[...], wp_ref[...])
    h = h + gate(gr_ref, xr_ref, br_ref) * _dot(ro_ref[...], wr_ref[...])
    h = h + gate(gm_ref, xm_ref, bm_ref) * _dot(mo_ref[...], wm_ref[...])
    o_ref[...] = h.astype(o_ref.dtype)


def _branch(po, ro, mo, wp, wr, wm, z, b_gate, *, row0, tm, tn, gate_col0, rot):
    rows = po.shape[0]
    d = wp.shape[1]
    off = row0 // tm
    g0 = (gate_col0 - rot) // tn
    assert g0 * tn == gate_col0 - rot and 0 <= rot < tn
    nb = d // tn
    act = lambda a: pl.BlockSpec((tm, a.shape[1]), lambda i, j: (i, 0))
    wgt = lambda w: pl.BlockSpec((w.shape[0], tn), lambda i, j: (0, j))
    gat = lambda k: pl.BlockSpec((tm, tn), lambda i, j, k=k: (off + i, g0 + k * nb + j))
    wrp = lambda k: pl.BlockSpec((tm, tn), lambda i, j, k=k: (off + i, g0 + (k + 1) * nb))
    bia = lambda k: pl.BlockSpec((1, tn), lambda i, j, k=k: (0, k * nb + j))
    return pl.pallas_call(
        functools.partial(_branch_kernel, rot=rot),
        grid=(rows // tm, nb),
        in_specs=[act(po), act(ro), act(mo), wgt(wp), wgt(wr), wgt(wm), gat(0), gat(1), gat(2),
                  wrp(0), wrp(1), wrp(2), bia(0), bia(1), bia(2)],
        out_specs=pl.BlockSpec((tm, tn), lambda i, j: (i, j)),
        out_shape=jax.ShapeDtypeStruct((rows, d), _BF16),
        compiler_params=_params(("parallel", "parallel")),
        name="branch_proj",
    )(po, ro, mo, wp, wr, wm, z, z, z, z, z, z, b_gate, b_gate, b_gate)


def _out_kernel(h_ref, w_ref, x_ref, g_ref, b_ref, o_ref, *, alpha):
    xf = alpha * x_ref[...] + _dot(h_ref[...], w_ref[...])
    mu = jnp.mean(xf, axis=-1, keepdims=True)
    d = xf - mu
    var = jnp.mean(d * d, axis=-1, keepdims=True)
    o_ref[...] = d * lax.rsqrt(var + _LN_EPS) * g_ref[...] + b_ref[...]


def _out_proj(h, w_out, x, ln_g, ln_b, *, tm, alpha):
    rows, d = x.shape
    kern = functools.partial(_out_kernel, alpha=alpha)
    return pl.pallas_call(
        kern,
        grid=(rows // tm,),
        in_specs=[
            pl.BlockSpec((tm, d), lambda i: (i, 0)),
            pl.BlockSpec((d, d), lambda i: (0, 0), pipeline_mode=pl.Buffered(1)),
            pl.BlockSpec((tm, d), lambda i: (i, 0)),
            pl.BlockSpec((1, d), lambda i: (0, 0)),
            pl.BlockSpec((1, d), lambda i: (0, 0)),
        ],
        out_specs=pl.BlockSpec((tm, d), lambda i: (i, 0)),
        out_shape=jax.ShapeDtypeStruct((rows, d), _F32),
        compiler_params=_params(("parallel",), vmem_limit=_VMEM_LIMIT_RESIDENT_WEIGHT),
        name="out_proj_ln",
    )(h, w_out, x, ln_g, ln_b)


def _tiles(batch, seq, n_sample):
    n_prompt = batch * seq
    m_all = n_prompt + n_sample
    tm_in = next(t for t in (832, 640, 512, 256, 128, 64, 32, 16) if m_all % t == 0)
    return {
        "in_tm": tm_in, "in_tn": 768,
        "row_tb": min(256, seq), "prep_tb": min(128, seq),
        "scan_tb": min(256, seq), "scan_pp": 8,
        "attn_tq": min(512, seq),
        "proj_tm": min(512, n_prompt), "proj_tn": 512, "out_tm": min(256, n_prompt),
        "pool_bb": min(32, n_sample), "step_bb": min(8, n_sample), "attn_bb": 2,
    }


def _layer(xp, xs, mem_p, cache_k, cache_v, st_pool, st_shift, st_rwkv, lp):
    batch, seq, d = xp.shape
    ns = xs.shape[0]
    n_p = batch * seq
    m_all = n_p + ns
    pw = lp["pool_scale"].shape[-1]
    rw = lp["rwkv_w0"].shape[-1]
    heads, hd = lp["rwkv_r_k"].shape
    lora = lp["rwkv_w2"].shape[0]
    sw = lp["rwkv_mu"].shape[-1]
    n_mem, mh, md = cache_k.shape[1:]
    mw = mh * md
    tl = _tiles(batch, seq, ns)

    lo0, lo1 = 2 * pw + 3 * rw, 2 * pw + sw
    lw = 2 * _LANES
    assert 2 * lora <= lw
    q_col0 = lo1 + rw
    gate_col0 = q_col0 + 2 * mw
    rot = gate_col0 % tl["proj_tn"]

    x_all = jnp.concatenate([xp.reshape(n_p, d), xs.reshape(ns, d)], axis=0)
    x_bf = x_all.astype(_BF16)
    z = _in_proj(x_bf, lp["w_in"], tm=tl["in_tm"], tn=tl["in_tn"])

    pool_w = lp["pool_w"].astype(_BF16)
    pool_scale = lp["pool_scale"].reshape(1, pw)
    po_p = _pool_prompt(z, pool_w, pool_scale, batch=batch, seq=seq, width=pw, tb=tl["row_tb"])
    po_s = _pool_sample(z, st_pool.reshape(ns, -1), pool_w, pool_scale, row0=n_p, rows=ns, width=pw,
                        bb=tl["pool_bb"])
    nbuf = st_pool.shape[1]
    new_pool_p = jnp.stack([z[(b + 1) * seq - nbuf:(b + 1) * seq, :pw] for b in range(batch)])
    new_pool_s = jnp.concatenate([st_pool[:, 1:, :], z[n_p:, :pw].reshape(ns, 1, pw)], axis=1)

    mu = lp["rwkv_mu"]
    pad_l = lambda a: jnp.pad(a, ((0, 0), (0, lw - 2 * lora)))
    w2p = jnp.zeros((lw, rw), _F32).at[:lora].set(lp["rwkv_w2"]).astype(_BF16)
    a2p = jnp.zeros((lw, rw), _F32).at[lora:2 * lora].set(lp["rwkv_a2"]).astype(_BF16)
    row = lambda a: a.reshape(1, -1)
    mix_consts = (row(mu[:rw]), row(mu[rw:2 * rw]), row(mu[2 * rw:3 * rw]), pad_l(row(mu[3 * rw:])),
                  row(lp["rwkv_w0"]), row(lp["rwkv_a0"]), row(lp["rwkv_k_k"]), row(lp["rwkv_k_a"]), w2p, a2p)
    post_consts = (row(lp["rwkv_r_k"]), row(lp["rwkv_ln_w"]), row(lp["rwkv_ln_b"]))

    r_p, ld_p, k_p, v_p, kk_p, be_p = _rwkv_prep_prompt(z, mix_consts, batch=batch, seq=seq, width=rw,
                                                        tb=tl["prep_tb"], head_dim=hd, col0=2 * pw, lw=lw)
    y_p, st_pairs = _rwkv_scan((r_p, ld_p, k_p, v_p, kk_p, be_p), batch=batch, seq=seq, width=rw,
                               tb=tl["scan_tb"], pp=tl["scan_pp"])
    ro_p = _rwkv_post(y_p, r_p, k_p, v_p, z, post_consts, row0=0, rows=n_p, width=rw, tb=tl["row_tb"],
                      head_dim=hd, col0=lo1)
    st5 = st_pairs.reshape(batch, heads // 2, 2, hd, 2, hd)
    new_state_p = jnp.stack([st5[:, :, 0, :, 0, :], st5[:, :, 1, :, 1, :]], axis=2).reshape(batch, heads, hd, hd)

    shift_s = st_shift.reshape(ns, sw)
    r_s, ld_s, k_s, v_s, kk_s, be_s = _rwkv_prep_sample(z, shift_s[:, :3 * rw], pad_l(shift_s[:, 3 * rw:]),
                                                        mix_consts, row0=n_p, rows=ns, width=rw, head_dim=hd,
                                                        col0=2 * pw, lw=lw)
    y_s4, new_state_s = _rwkv_step((r_s, ld_s, k_s, v_s, kk_s, be_s), st_rwkv, bb=tl["step_bb"])
    ro_s = _rwkv_post(y_s4.reshape(ns, rw), r_s, k_s, v_s, z, post_consts, row0=n_p, rows=ns, width=rw, tb=ns,
                      head_dim=hd, col0=lo1)
    new_shift_p = jnp.stack([z[(b + 1) * seq - 1:(b + 1) * seq, 2 * pw:lo1] for b in range(batch)])
    new_shift_s = z[n_p:, 2 * pw:lo1].reshape(ns, 1, sw)

    mkv = _matmul(mem_p.reshape(batch * n_mem, d).astype(_BF16), lp["w_mem_kv"].astype(_BF16),
                  tm=min(512, batch * n_mem), tn=min(1024, 2 * mw), out_dtype=_F32, name="mem_kv_proj")
    mo_p = _mem_attn_prompt(z, mkv, batch=batch, seq=seq, n_mem=n_mem, heads=mh, head_dim=md, tq=tl["attn_tq"],
                            col0=q_col0)
    mo_s = _mem_attn_sample(z[n_p:, q_col0:q_col0 + mw].reshape(ns, mh, md),
                            z[n_p:, q_col0 + mw:q_col0 + 2 * mw].reshape(ns, mh, md),
                            cache_k, cache_v, bb=tl["attn_bb"]).reshape(ns, mw).astype(_BF16)
    mk_p = mkv[:, :mw].reshape(batch, n_mem, mh, md)
    mv_p = mkv[:, mw:].reshape(batch, n_mem, mh, md)

    wp = jnp.roll(lp["w_branch_pool"], rot, axis=1).astype(_BF16)
    wr = jnp.roll(lp["w_branch_rwkv"], rot, axis=1).astype(_BF16)
    wm = jnp.roll(lp["w_branch_mem"], rot, axis=1).astype(_BF16)
    w_out = jnp.roll(lp["w_out"], rot, axis=0).astype(_BF16)
    b_gate = jnp.roll(lp["b_gate"].reshape(3, d), rot, axis=1).reshape(1, 3 * d)
    ln_g, ln_b = row(lp["ln_g"]), row(lp["ln_b"])
    alpha = lp["alpha"]
    outs = []
    for po, ro, mo, x2d, row0 in ((po_p, ro_p, mo_p, xp.reshape(n_p, d), 0), (po_s, ro_s, mo_s, xs.reshape(ns, d), n_p)):
        rows = x2d.shape[0]
        h = _branch(po, ro, mo, wp, wr, wm, z, b_gate, row0=row0, tm=min(tl["proj_tm"], rows), tn=tl["proj_tn"],
                    gate_col0=gate_col0, rot=rot)
        outs.append(_out_proj(h, w_out, x2d, ln_g, ln_b, tm=min(tl["out_tm"], rows), alpha=alpha))
    y_prompt = outs[0].reshape(batch, seq, d)
    y_sample = outs[1].reshape(ns, 1, d)
    return (y_prompt, y_sample, mk_p, mv_p, new_pool_p, new_shift_p, new_state_p, new_pool_s, new_shift_s,
            new_state_s)


def kernel(x_prompt, x_sample, cache_mem_k, cache_mem_v, state_pool, state_shift, state_rwkv, mem_prompt, w_in,
           b_gate, pool_w, pool_scale, rwkv_mu, rwkv_w0, rwkv_w2, rwkv_a0, rwkv_a2, rwkv_k_k, rwkv_k_a, rwkv_r_k,
           rwkv_ln_w, rwkv_ln_b, w_mem_kv, w_branch_pool, w_branch_rwkv, w_branch_mem, w_out, ln_g, ln_b):
    depth = w_in.shape[0]
    alpha = (2.0 * depth) ** 0.25
    weights = dict(w_in=w_in, b_gate=b_gate, pool_w=pool_w, pool_scale=pool_scale, rwkv_mu=rwkv_mu, rwkv_w0=rwkv_w0,
                   rwkv_w2=rwkv_w2, rwkv_a0=rwkv_a0, rwkv_a2=rwkv_a2, rwkv_k_k=rwkv_k_k, rwkv_k_a=rwkv_k_a,
                   rwkv_r_k=rwkv_r_k, rwkv_ln_w=rwkv_ln_w, rwkv_ln_b=rwkv_ln_b, w_mem_kv=w_mem_kv,
                   w_branch_pool=w_branch_pool, w_branch_rwkv=w_branch_rwkv, w_branch_mem=w_branch_mem,
                   w_out=w_out, ln_g=ln_g, ln_b=ln_b)
    yp, ys = x_prompt, x_sample
    per_layer = []
    for l in range(depth):
        lp = {k: v[l] for k, v in weights.items()}
        lp["alpha"] = alpha
        res = _layer(yp, ys, mem_prompt, cache_mem_k[l], cache_mem_v[l], state_pool[l], state_shift[l],
                     state_rwkv[l], lp)
        yp, ys = res[0], res[1]
        per_layer.append(res[2:])
    if depth == 1:
        stacked = tuple(a[None] for a in per_layer[0])
    else:
        stacked = tuple(jnp.stack([res[i] for res in per_layer]) for i in range(len(per_layer[0])))
    return (yp, ys) + stacked
```

```python
import functools

import jax
import jax.numpy as jnp
from jax import lax
from jax.experimental import pallas as pl
from jax.experimental.pallas import tpu as pltpu

_F32 = jnp.float32
_BF16 = jnp.bfloat16

_POOL_WINDOWS = (2, 4, 8, 16)
_PAST_LEN = 16384
_GN_EPS = 64e-5
_LN_EPS = 1e-5
_L2_EPS = 1e-12

_LANES = 128
_MIB = 1024 * 1024
_VMEM_LIMIT = 56 * _MIB
_VMEM_LIMIT_RESIDENT_WEIGHT = 62 * _MIB

_HEAD_PAIR = _LANES
_CHUNK = 64


def _params(sem, vmem_limit=_VMEM_LIMIT):
    return pltpu.CompilerParams(dimension_semantics=sem, vmem_limit_bytes=vmem_limit)


def _dot(a, b):
    return jnp.dot(a, b, preferred_element_type=_F32)


def _dot_nt(a, b):
    return lax.dot_general(a, b, (((1,), (1,)), ((), ())), preferred_element_type=_F32)


def _dot_tn(a, b):
    return lax.dot_general(a, b, (((0,), (0,)), ((), ())), preferred_element_type=_F32)


def _split(x):
    hi = x.astype(_BF16)
    lo = (x - hi.astype(_F32)).astype(_BF16)
    return hi, lo


def _mm1(dot, a, b):
    return dot(a.astype(_BF16), b.astype(_BF16))


def _mm_exact_rhs(a, b_bf16):
    a_hi, a_lo = _split(a)
    return _dot(a_hi, b_bf16) + _dot(a_lo, b_bf16)


def _silu(x):
    return x * jax.nn.sigmoid(x)


def _head_block_matrix(value, head_dim):
    r = lax.broadcasted_iota(jnp.int32, (_LANES, _LANES), 0) // head_dim
    c = lax.broadcasted_iota(jnp.int32, (_LANES, _LANES), 1) // head_dim
    return jnp.where(r == c, value, 0.0).astype(_BF16)


def _matmul_kernel(x_ref, w_ref, o_ref):
    o_ref[...] = _dot(x_ref[...], w_ref[...]).astype(o_ref.dtype)


def _matmul(x, w, *, tm, tn, out_dtype, name):
    m, k = x.shape
    n = w.shape[1]
    return pl.pallas_call(
        _matmul_kernel,
        grid=(m // tm, n // tn),
        in_specs=[pl.BlockSpec((tm, k), lambda i, j: (i, 0)), pl.BlockSpec((k, tn), lambda i, j: (0, j))],
        out_specs=pl.BlockSpec((tm, tn), lambda i, j: (i, j)),
        out_shape=jax.ShapeDtypeStruct((m, n), out_dtype),
        compiler_params=_params(("parallel", "parallel")),
        name=name,
    )(x, w)


def _in_proj_kernel(x_ref, wt_ref, o_ref, wb_ref):
    @pl.when(pl.program_id(1) == 0)
    def _():
        wb_ref[...] = wt_ref[...].astype(_BF16)

    o_ref[...] = _dot_nt(x_ref[...], wb_ref[...])


def _in_proj(x, wt, *, tm, tn):
    m, k = x.shape
    n = wt.shape[0]
    return pl.pallas_call(
        _in_proj_kernel,
        grid=(pl.cdiv(n, tn), m // tm),
        in_specs=[pl.BlockSpec((tm, k), lambda j, i: (i, 0)), pl.BlockSpec((tn, k), lambda j, i: (j, 0))],
        out_specs=pl.BlockSpec((tm, tn), lambda j, i: (i, j)),
        out_shape=jax.ShapeDtypeStruct((m, n), _F32),
        scratch_shapes=[pltpu.VMEM((tn, k), _BF16)],
        compiler_params=_params(("parallel", "arbitrary")),
        name="in_proj",
    )(x, wt)


def _pool_mix(pooled_fn, zp_ref, pw_ref, ps_ref, o_ref, group):
    for g in range(len(_POOL_WINDOWS)):
        cols = slice(g * group, (g + 1) * group)
        mixed = _dot(pooled_fn(g, cols).astype(_BF16), pw_ref[g])
        o_ref[:, cols] = (mixed * ps_ref[:, cols] * _silu(zp_ref[:, cols])).astype(o_ref.dtype)


def _pool_prompt_kernel(u_ref, halo_ref, zp_ref, pw_ref, ps_ref, o_ref, e_ref, *, tb, hist, group):
    t = pl.program_id(1)
    e_ref[0:hist, :] = jnp.where(t == 0, 0.0, halo_ref[...])
    e_ref[hist:hist + tb, :] = u_ref[...]
    pos = t * tb + lax.broadcasted_iota(jnp.int32, (tb, group), 0)

    def pooled(g, cols):
        win = _POOL_WINDOWS[g]
        x = e_ref[hist:hist + tb, cols]
        acc = x
        for d in range(1, win):
            acc = acc + e_ref[hist - d:hist - d + tb, cols]
        cnt = jnp.minimum(pos + 1, win).astype(_F32)
        return acc / cnt - x

    _pool_mix(pooled, zp_ref, pw_ref, ps_ref, o_ref, group)


def _pool_prompt(z, pool_w, pool_scale, *, batch, seq, width, tb):
    hist = 16
    n_t = seq // tb
    group = width // len(_POOL_WINDOWS)
    kern = functools.partial(_pool_prompt_kernel, tb=tb, hist=hist, group=group)
    return pl.pallas_call(
        kern,
        grid=(batch, n_t),
        in_specs=[
            pl.BlockSpec((tb, width), lambda b, t: (b * n_t + t, 0)),
            pl.BlockSpec((hist, width), lambda b, t: (jnp.maximum((b * seq + t * tb) // hist - 1, 0), 0)),
            pl.BlockSpec((tb, width), lambda b, t: (b * n_t + t, 1)),
            pl.BlockSpec(pool_w.shape, lambda b, t: (0, 0, 0)),
            pl.BlockSpec((1, width), lambda b, t: (0, 0)),
        ],
        out_specs=pl.BlockSpec((tb, width), lambda b, t: (b * n_t + t, 0)),
        out_shape=jax.ShapeDtypeStruct((batch * seq, width), _BF16),
        scratch_shapes=[pltpu.VMEM((hist + tb, width), _F32)],
        compiler_params=_params(("parallel", "parallel")),
        name="pool_prompt",
    )(z, z, z, pool_w, pool_scale)


def _pool_sample_kernel(u_ref, buf_ref, zp_ref, pw_ref, ps_ref, o_ref, *, nbuf, group):
    def pooled(g, cols):
        win = _POOL_WINDOWS[g]
        x = u_ref[:, cols]
        acc = x
        for d in range(1, win):
            acc = acc + buf_ref[nbuf - d, :, cols]
        cnt = float(min(_PAST_LEN + 1, win))
        return acc / cnt - x

    _pool_mix(pooled, zp_ref, pw_ref, ps_ref, o_ref, group)


def _pool_sample(z, buf, pool_w, pool_scale, *, row0, bb):
    nbuf, rows, width = buf.shape
    group = width // len(_POOL_WINDOWS)
    off = row0 // bb
    kern = functools.partial(_pool_sample_kernel, nbuf=nbuf, group=group)
    return pl.pallas_call(
        kern,
        grid=(rows // bb,),
        in_specs=[
            pl.BlockSpec((bb, width), lambda i: (off + i, 0)),
            pl.BlockSpec((nbuf, bb, width), lambda i: (0, i, 0)),
            pl.BlockSpec((bb, width), lambda i: (off + i, 1)),
            pl.BlockSpec(pool_w.shape, lambda i: (0, 0, 0)),
            pl.BlockSpec((1, width), lambda i: (0, 0)),
        ],
        out_specs=pl.BlockSpec((bb, width), lambda i: (i, 0)),
        out_shape=jax.ShapeDtypeStruct((rows, width), _BF16),
        compiler_params=_params(("parallel",)),
        name="pool_sample",
    )(z, buf, z, pool_w, pool_scale)


def _rwkv_mix(cur, prev, mu, w0, a0, kk_scale, ka, w2p, a2p, head_dim, outs):
    o_r, o_ld, o_k, o_v, o_kk, o_be = outs
    mix = lambda n: cur[n] + (prev[n] - cur[n]) * mu[n]
    xl = mix("l")
    wl = w0 + _mm1(_dot, jnp.tanh(xl), w2p)
    w_log = -jax.nn.softplus(-wl) - 0.5
    a = jax.nn.sigmoid(a0 + _mm1(_dot, xl, a2p))
    xk = mix("k")
    kkp = xk * kk_scale
    ones_bd = _head_block_matrix(1.0, head_dim)
    width = kkp.shape[1]
    o_r[...] = mix("r")
    o_v[...] = mix("v")
    o_ld[...] = -jnp.exp(w_log)
    o_k[...] = xk * (1.0 + (a - 1.0) * ka)
    for p in range(width // _LANES):
        cols = slice(p * _LANES, (p + 1) * _LANES)
        kp = kkp[:, cols]
        ss = _mm_exact_rhs(kp * kp, ones_bd)
        kk = kp / jnp.maximum(jnp.sqrt(ss), _L2_EPS)
        o_kk[:, cols] = kk
        o_be[:, cols] = kk * a[:, cols]


def _rwkv_prep_prompt_kernel(zr, zk, zv, zl, hr, hk, hv, hl, mur, muk, muv, mul, w0, a0, kks, ka, w2p, a2p,
                             o_r, o_ld, o_k, o_v, o_kk, o_be, er, ek, ev, el, *, tb, head_dim):
    t = pl.program_id(1)
    halo = 8
    cur, prev = {}, {}
    for name, z_ref, h_ref, e_ref in (("r", zr, hr, er), ("k", zk, hk, ek), ("v", zv, hv, ev), ("l", zl, hl, el)):
        e_ref[0:halo, :] = jnp.where(t == 0, 0.0, h_ref[...])
        e_ref[halo:halo + tb, :] = z_ref[...]
        cur[name] = z_ref[...]
        prev[name] = e_ref[halo - 1:halo - 1 + tb, :]
    mu = {"r": mur[...], "k": muk[...], "v": muv[...], "l": mul[...]}
    _rwkv_mix(cur, prev, mu, w0[...], a0[...], kks[...], ka[...], w2p[...], a2p[...], head_dim,
              (o_r, o_ld, o_k, o_v, o_kk, o_be))


def _rwkv_prep_prompt(z, consts, *, batch, seq, width, tb, head_dim, col0, lw):
    n_t = seq // tb
    c0 = col0 // width
    lcol = (col0 + 3 * width) // lw
    assert c0 * width == col0 and lcol * lw == col0 + 3 * width
    halo = 8
    row = lambda b, t: b * n_t + t
    hrow = lambda b, t: jnp.maximum((b * seq + t * tb) // halo - 1, 0)
    cblk = lambda c: pl.BlockSpec((tb, width), lambda b, t, c=c: (row(b, t), c))
    hblk = lambda c: pl.BlockSpec((halo, width), lambda b, t, c=c: (hrow(b, t), c))
    vec = lambda n: pl.BlockSpec((1, n), lambda b, t: (0, 0))
    mat = pl.BlockSpec((lw, width), lambda b, t: (0, 0))
    out_spec = pl.BlockSpec((tb, width), lambda b, t: (row(b, t), 0))
    out_shape = jax.ShapeDtypeStruct((batch * seq, width), _F32)
    kern = functools.partial(_rwkv_prep_prompt_kernel, tb=tb, head_dim=head_dim)
    return pl.pallas_call(
        kern,
        grid=(batch, n_t),
        in_specs=[cblk(c0), cblk(c0 + 1), cblk(c0 + 2), pl.BlockSpec((tb, lw), lambda b, t: (row(b, t), lcol)),
                  hblk(c0), hblk(c0 + 1), hblk(c0 + 2), pl.BlockSpec((halo, lw), lambda b, t: (hrow(b, t), lcol)),
                  vec(width), vec(width), vec(width), vec(lw),
                  vec(width), vec(width), vec(width), vec(width), mat, mat],
        out_specs=[out_spec] * 6,
        out_shape=[out_shape] * 6,
        scratch_shapes=[pltpu.VMEM((halo + tb, width), _F32)] * 3 + [pltpu.VMEM((halo + tb, lw), _F32)],
        compiler_params=_params(("parallel", "parallel")),
        name="rwkv_prep_prompt",
    )(z, z, z, z, z, z, z, z, *consts)


def _rwkv_prep_sample_kernel(zr, zk, zv, zl, pr, pk, pv, plr, mur, muk, muv, mul, w0, a0, kks, ka, w2p, a2p,
                             o_r, o_ld, o_k, o_v, o_kk, o_be, t_r, t_ld, t_k, t_v, t_kk, t_be, *, head_dim):
    cur = {"r": zr[...], "k": zk[...], "v": zv[...], "l": zl[...]}
    prev = {"r": pr[...], "k": pk[...], "v": pv[...], "l": plr[...]}
    mu = {"r": mur[...], "k": muk[...], "v": muv[...], "l": mul[...]}
    _rwkv_mix(cur, prev, mu, w0[...], a0[...], kks[...], ka[...], w2p[...], a2p[...], head_dim,
              (o_r, o_ld, o_k, o_v, o_kk, o_be))
    for o_ref, t_ref in ((o_r, t_r), (o_ld, t_ld), (o_k, t_k), (o_v, t_v), (o_kk, t_kk), (o_be, t_be)):
        t_ref[...] = o_ref[...].T


def _rwkv_prep_sample(z, prev_rkv, prev_l, consts, *, row0, rows, width, head_dim, col0, lw):
    c0 = col0 // width
    lcol = (col0 + 3 * width) // lw
    assert c0 * width == col0 and lcol * lw == col0 + 3 * width
    off = row0 // rows
    cblk = lambda c: pl.BlockSpec((rows, width), lambda i, c=c: (off, c))
    pblk = lambda c: pl.BlockSpec((rows, width), lambda i, c=c: (0, c))
    vec = lambda n: pl.BlockSpec((1, n), lambda i: (0, 0))
    mat = pl.BlockSpec((lw, width), lambda i: (0, 0))
    out_spec = pl.BlockSpec((rows, width), lambda i: (0, 0))
    out_shape = jax.ShapeDtypeStruct((rows, width), _F32)
    kern = functools.partial(_rwkv_prep_sample_kernel, head_dim=head_dim)
    return pl.pallas_call(
        kern,
        grid=(1,),
        in_specs=[cblk(c0), cblk(c0 + 1), cblk(c0 + 2), pl.BlockSpec((rows, lw), lambda i: (off, lcol)),
                  pblk(0), pblk(1), pblk(2), pl.BlockSpec((rows, lw), lambda i: (0, 0)),
                  vec(width), vec(width), vec(width), vec(lw),
                  vec(width), vec(width), vec(width), vec(width), mat, mat],
        out_specs=[out_spec] * 6 + [pl.BlockSpec((width, rows), lambda i: (0, 0))] * 6,
        out_shape=[out_shape] * 6 + [jax.ShapeDtypeStruct((width, rows), _F32)] * 6,
        compiler_params=_params(("arbitrary",)),
        name="rwkv_prep_sample",
    )(z, z, z, z, prev_rkv, prev_rkv, prev_rkv, prev_l, *consts)


def _scan_chunks(inputs, states, c):
    pairs = range(len(inputs))
    n2 = _LANES
    bf = lambda x: x.astype(_BF16)
    m0, m1 = c["m0"], c["m1"]
    expand = lambda x: jnp.concatenate([x * m0, x * m1], axis=0)
    stack = lambda a, b: jnp.concatenate([expand(a), expand(b)], axis=0)

    cum = [_mm_exact_rhs_t(c["ltri"], inputs[q][1]) for q in pairs]
    tot = [cum[q][_CHUNK - 1:_CHUNK, :] for q in pairs]
    xa, xb, bk, ve = [], [], [], []
    for q in pairs:
        r, ld, k2, v, kk, be = inputs[q]
        e_neg = jnp.exp(-cum[q])
        e_rem = jnp.exp(tot[q] - cum[q])
        xa.append(bf(stack(-kk * jnp.exp(cum[q] - ld), r * jnp.exp(cum[q]))))
        xb.append(bf(stack(be * e_neg, k2 * e_neg)))
        bk.append(bf(stack(be * e_rem, k2 * e_rem)))
        ve.append(bf(expand(v)))
    g = [_dot_nt(xa[q], xb[q]) for q in pairs]
    ps = [_dot_nt(xa[q], bf(states[q])) for q in pairs]
    a_kk = [bf(jnp.concatenate([jnp.where(c["strict"], g[q][:n2, n2:], 0.0),
                                jnp.where(c["incl"], g[q][n2:, n2:], 0.0)], axis=0)) for q in pairs]
    av = [_dot(a_kk[q], ve[q]) for q in pairs]
    npow = [jnp.where(c["strict"], g[q][:n2, :n2], 0.0) for q in pairs]
    u = [ps[q][:n2] + av[q][:n2] for q in pairs]
    for k in range(c["squarings"]):
        out = [_dot(bf(npow[q]), bf(jnp.concatenate([npow[q], u[q]], axis=1))) for q in pairs]
        npow = [out[q][:, :n2] for q in pairs]
        u = [u[q] + out[q][:, n2:] for q in pairs]
    ub = [bf(u[q]) for q in pairs]
    u = [u[q] + _dot(bf(npow[q]), ub[q]) for q in pairs]
    ub = [bf(u[q]) for q in pairs]
    a_rb = [bf(jnp.where(c["incl"], g[q][n2:, :n2], 0.0)) for q in pairs]
    ye = [ps[q][n2:] + av[q][n2:] + _dot(a_rb[q], ub[q]) for q in pairs]
    ys = [ye[q][:_CHUNK] + ye[q][_CHUNK:] for q in pairs]
    new_states = [states[q] * jnp.exp(tot[q]) + _dot_tn(jnp.concatenate([ub[q], ve[q]], axis=0), bk[q])
                  for q in pairs]
    return ys, new_states


def _mm_exact_rhs_t(l_bf16, x):
    x_hi, x_lo = _split(x)
    x_lo2 = (x - x_hi.astype(_F32) - x_lo.astype(_F32)).astype(_BF16)
    return _dot(l_bf16, x_hi) + (_dot(l_bf16, x_lo) + _dot(l_bf16, x_lo2))


def _scan_consts():
    n = _LANES
    half = _CHUNK
    ri = lax.broadcasted_iota(jnp.int32, (n, n), 0)
    ci = lax.broadcasted_iota(jnp.int32, (n, n), 1)
    same = (ri // half) == (ci // half)
    lane = lax.broadcasted_iota(jnp.int32, (half, n), 1)
    tr = lax.broadcasted_iota(jnp.int32, (half, half), 0)
    tc = lax.broadcasted_iota(jnp.int32, (half, half), 1)
    squarings = 0
    while (2 << squarings) < half:
        squarings += 1
    return {
        "strict": same & ((ri % half) > (ci % half)),
        "incl": same & ((ri % half) >= (ci % half)),
        "m0": jnp.where(lane < half, 1.0, 0.0).astype(_F32),
        "m1": jnp.where(lane >= half, 1.0, 0.0).astype(_F32),
        "ltri": jnp.where(tr >= tc, 1.0, 0.0).astype(_BF16),
        "squarings": squarings,
    }


def _rwkv_scan_kernel(r_ref, ld_ref, k_ref, v_ref, kk_ref, be_ref, y_ref, s_ref, st_ref, *, tb, pp):
    t = pl.program_id(2)

    @pl.when(t == 0)
    def _():
        st_ref[...] = jnp.zeros_like(st_ref)

    c = _scan_consts()
    in_refs = (r_ref, ld_ref, k_ref, v_ref, kk_ref, be_ref)

    def body(ci, carry):
        rows = pl.ds(pl.multiple_of(ci * _CHUNK, _CHUNK), _CHUNK)
        cols = [slice(q * _LANES, (q + 1) * _LANES) for q in range(pp)]
        ys, new_states = _scan_chunks([tuple(ref[rows, cols[q]] for ref in in_refs) for q in range(pp)],
                                      [st_ref[q] for q in range(pp)], c)
        for q in range(pp):
            y_ref[rows, cols[q]] = ys[q]
            st_ref[q] = new_states[q]
        return carry

    lax.fori_loop(0, tb // _CHUNK, body, 0)

    @pl.when(t == pl.num_programs(2) - 1)
    def _():
        s_ref[0] = st_ref[...]


def _rwkv_scan(arrs, *, batch, seq, width, tb, pp):
    n_t = seq // tb
    n_p = width // _LANES
    blk = pl.BlockSpec((tb, pp * _LANES), lambda b, p, t: (b * n_t + t, p))
    kern = functools.partial(_rwkv_scan_kernel, tb=tb, pp=pp)
    return pl.pallas_call(
        kern,
        grid=(batch, n_p // pp, n_t),
        in_specs=[blk] * 6,
        out_specs=[blk, pl.BlockSpec((1, pp, _LANES, _LANES), lambda b, p, t: (b, p, 0, 0))],
        out_shape=[jax.ShapeDtypeStruct((batch * seq, width), _F32),
                   jax.ShapeDtypeStruct((batch, n_p, _LANES, _LANES), _F32)],
        scratch_shapes=[pltpu.VMEM((pp, _LANES, _LANES), _F32)],
        compiler_params=_params(("parallel", "parallel", "arbitrary")),
        name="rwkv_scan",
    )(*arrs)


def _rwkv_step_kernel(r_ref, ld_ref, k_ref, v_ref, kk_ref, be_ref, s_ref, y_ref, so_ref, *, hb, head_dim, vi):
    for h in range(hb):
        kk = kk_ref[h][None]
        dec = jnp.exp(ld_ref[h])[None]
        be = be_ref[h][None]
        k = k_ref[h][None]
        r = r_ref[h][None]

        def body(c, carry, h=h, kk=kk, dec=dec, be=be, k=k, r=r):
            rows = pl.ds(pl.multiple_of(c * vi, vi), vi)
            s = s_ref[h, rows]
            sa = -jnp.sum(s * kk, axis=1, keepdims=True)
            s_new = s * dec + sa * be + v_ref[h, rows] * k
            y_ref[h, rows] = jnp.sum(s_new * r, axis=1, keepdims=True)
            so_ref[h, rows] = s_new
            return carry

        lax.fori_loop(0, head_dim // vi, body, 0)


def _rwkv_step(arrs_t, state_t, *, hb):
    heads, hd, _, n = state_t.shape
    r, ld, k, v, kk, be = arrs_t
    per_key = lambda a: a.reshape(heads, hd, n)
    vec = pl.BlockSpec((hb, hd, n), lambda i: (i, 0, 0))
    col = pl.BlockSpec((hb, hd, 1, n), lambda i: (i, 0, 0, 0))
    mat = pl.BlockSpec((hb, hd, hd, n), lambda i: (i, 0, 0, 0))
    kern = functools.partial(_rwkv_step_kernel, hb=hb, head_dim=hd, vi=8)
    return pl.pallas_call(
        kern,
        grid=(heads // hb,),
        in_specs=[vec, vec, vec, col, vec, vec, mat],
        out_specs=[col, mat],
        out_shape=[jax.ShapeDtypeStruct((heads, hd, 1, n), _F32), jax.ShapeDtypeStruct(state_t.shape, _F32)],
        compiler_params=_params(("parallel",)),
        name="rwkv_step",
    )(per_key(r), per_key(ld), per_key(k), v.reshape(heads, hd, 1, n), per_key(kk), per_key(be), state_t)


def _rwkv_post_kernel(y_ref, r_ref, k_ref, v_ref, za_ref, zb_ref, rk_ref, lw_ref, lb_ref, o_ref, *, head_dim, delta):
    avg_bd = _head_block_matrix(1.0 / head_dim, head_dim)
    ones_bd = _head_block_matrix(1.0, head_dim)
    width = y_ref.shape[1]
    zwin = jnp.concatenate([za_ref[...], zb_ref[...]], axis=1)
    for p in range(width // _LANES):
        cols = slice(p * _LANES, (p + 1) * _LANES)
        y = y_ref[:, cols]
        d = y - _mm_exact_rhs(y, avg_bd)
        var = _mm_exact_rhs(d * d, avg_bd)
        yn = d * lax.rsqrt(var + _GN_EPS) * lw_ref[:, cols] + lb_ref[:, cols]
        bonus = _mm_exact_rhs(r_ref[:, cols] * k_ref[:, cols] * rk_ref[:, cols], ones_bd) * v_ref[:, cols]
        zg = zwin[:, delta + p * _LANES:delta + (p + 1) * _LANES]
        o_ref[:, cols] = ((yn + bonus) * _silu(zg)).astype(o_ref.dtype)


def _rwkv_post(y, r, k2, v, z, consts, *, row0, rows, width, tb, head_dim, col0):
    off = row0 // tb
    tail = 2 * _LANES
    cg, delta = col0 // width, col0 % width
    assert delta <= tail and width % tail == 0
    blk = pl.BlockSpec((tb, width), lambda i: (i, 0))
    vec = pl.BlockSpec((1, width), lambda i: (0, 0))
    kern = functools.partial(_rwkv_post_kernel, head_dim=head_dim, delta=delta)
    return pl.pallas_call(
        kern,
        grid=(rows // tb,),
        in_specs=[blk, blk, blk, blk,
                  pl.BlockSpec((tb, width), lambda i: (off + i, cg)),
                  pl.BlockSpec((tb, tail), lambda i: (off + i, (cg + 1) * (width // tail))),
                  vec, vec, vec],
        out_specs=blk,
        out_shape=jax.ShapeDtypeStruct((rows, width), _BF16),
        compiler_params=_params(("parallel",)),
        name="rwkv_post",
    )(y, r, k2, v, z, z, *consts)


def _mem_attn_prompt_kernel(za_ref, zb_ref, zc_ref, k_ref, v_ref, o_ref, *, heads, head_dim, delta):
    scale = head_dim ** -0.5
    mw = heads * head_dim
    zwin = jnp.concatenate([za_ref[...], zb_ref[...], zc_ref[...]], axis=1)
    for h in range(heads):
        cols = slice(h * head_dim, (h + 1) * head_dim)
        q = zwin[:, delta + cols.start:delta + cols.stop]
        zg = zwin[:, delta + mw + cols.start:delta + mw + cols.stop]
        s = _dot_nt(q.astype(_BF16), k_ref[:, cols].astype(_BF16)) * scale
        p = jnp.exp(s - jnp.max(s, axis=-1, keepdims=True))
        o = _dot(p.astype(_BF16), v_ref[:, cols].astype(_BF16)) / jnp.sum(p, axis=-1, keepdims=True)
        o_ref[:, cols] = (o * _silu(zg)).astype(o_ref.dtype)


def _mem_attn_prompt(z, mkv, *, batch, seq, n_mem, heads, head_dim, tq, col0):
    n_t = seq // tq
    mw = heads * head_dim
    tail = 2 * _LANES
    q_col, delta = col0 // mw, col0 % mw
    assert delta <= tail and mw % tail == 0
    kern = functools.partial(_mem_attn_prompt_kernel, heads=heads, head_dim=head_dim, delta=delta)
    return pl.pallas_call(
        kern,
        grid=(batch, n_t),
        in_specs=[
            pl.BlockSpec((tq, mw), lambda b, t: (b * n_t + t, q_col)),
            pl.BlockSpec((tq, mw), lambda b, t: (b * n_t + t, q_col + 1)),
            pl.BlockSpec((tq, tail), lambda b, t: (b * n_t + t, (q_col + 2) * (mw // tail))),
            pl.BlockSpec((n_mem, mw), lambda b, t: (b, 0)),
            pl.BlockSpec((n_mem, mw), lambda b, t: (b, 1)),
        ],
        out_specs=pl.BlockSpec((tq, mw), lambda b, t: (b * n_t + t, 0)),
        out_shape=jax.ShapeDtypeStruct((batch * seq, mw), _BF16),
        compiler_params=_params(("parallel", "parallel")),
        name="mem_attn_prompt",
    )(z, z, z, mkv, mkv)


def _mem_attn_sample_kernel(q_ref, zg_ref, k_ref, v_ref, o_ref, *, bb, head_dim):
    scale = head_dim ** -0.5
    for b in range(bb):
        s = jnp.sum(k_ref[b] * q_ref[b][None], axis=-1, keepdims=True) * scale
        p = jnp.exp(s - jnp.max(s, axis=0, keepdims=True))
        o = jnp.sum(p * v_ref[b], axis=0) / jnp.sum(p, axis=0)
        o_ref[b] = o * _silu(zg_ref[b])


def _mem_attn_sample(q, zg, mem_k, mem_v, *, bb):
    n, n_mem, heads, head_dim = mem_k.shape
    vec = pl.BlockSpec((bb, heads, head_dim), lambda i: (i, 0, 0))
    mat = pl.BlockSpec((bb, n_mem, heads, head_dim), lambda i: (i, 0, 0, 0))
    kern = functools.partial(_mem_attn_sample_kernel, bb=bb, head_dim=head_dim)
    return pl.pallas_call(
        kern,
        grid=(n // bb,),
        in_specs=[vec, vec, mat, mat],
        out_specs=vec,
        out_shape=jax.ShapeDtypeStruct((n, heads, head_dim), _F32),
        compiler_params=_params(("parallel",)),
        name="mem_attn_sample",
    )(q, zg, mem_k, mem_v)


def _branch_kernel(po_ref, ro_ref, mo_ref, wp_ref, wr_ref, wm_ref, gp_ref, gr_ref, gm_ref, xp_ref, xr_ref, xm_ref,
                   bp_ref, br_ref, bm_ref, o_ref, *, rot):
    tm, tn = o_ref.shape
    col = pl.program_id(1) * tn + lax.broadcasted_iota(jnp.int32, (tm, tn), 1)
    wrap = col < rot
    gate = lambda g_ref, x_ref, b_ref: jax.nn.sigmoid(jnp.where(wrap, x_ref[...], g_ref[...]) + b_ref[...])
    h = gate(gp_ref, xp_ref, bp_ref) * _dot(po_ref[...], wp_ref[...])
    h = h + gate(gr_ref, xr_ref, br_ref) * _dot(ro_ref[...], wr_ref[...])
    h = h + gate(gm_ref, xm_ref, bm_ref) * _dot(mo_ref[...], wm_ref[...])
    o_ref[...] = h.astype(o_ref.dtype)


def _branch(po, ro, mo, wp, wr, wm, z, b_gate, *, row0, tm, tn, gate_col0, rot):
    rows = po.shape[0]
    d = wp.shape[1]
    off = row0 // tm
    g0 = (gate_col0 - rot) // tn
    assert g0 * tn == gate_col0 - rot and 0 <= rot < tn
    nb = d // tn
    act = lambda a: pl.BlockSpec((tm, a.shape[1]), lambda i, j: (i, 0))
    wgt = lambda w: pl.BlockSpec((w.shape[0], tn), lambda i, j: (0, j))
    gat = lambda k: pl.BlockSpec((tm, tn), lambda i, j, k=k: (off + i, g0 + k * nb + j))
    wrp = lambda k: pl.BlockSpec((tm, tn), lambda i, j, k=k: (off + i, g0 + (k + 1) * nb))
    bia = lambda k: pl.BlockSpec((1, tn), lambda i, j, k=k: (0, k * nb + j))
    return pl.pallas_call(
        functools.partial(_branch_kernel, rot=rot),
        grid=(rows // tm, nb),
        in_specs=[act(po), act(ro), act(mo), wgt(wp), wgt(wr), wgt(wm), gat(0), gat(1), gat(2),
                  wrp(0), wrp(1), wrp(2), bia(0), bia(1), bia(2)],
        out_specs=pl.BlockSpec((tm, tn), lambda i, j: (i, j)),
        out_shape=jax.ShapeDtypeStruct((rows, d), _BF16),
        compiler_params=_params(("parallel", "parallel")),
        name="branch_proj",
    )(po, ro, mo, wp, wr, wm, z, z, z, z, z, z, b_gate, b_gate, b_gate)


def _out_kernel(h_ref, w_ref, x_ref, g_ref, b_ref, o_ref, *, alpha):
    xf = alpha * x_ref[...] + _dot(h_ref[...], w_ref[...])
    mu = jnp.mean(xf, axis=-1, keepdims=True)
    d = xf - mu
    var = jnp.mean(d * d, axis=-1, keepdims=True)
    o_ref[...] = d * lax.rsqrt(var + _LN_EPS) * g_ref[...] + b_ref[...]


def _out_proj(h, w_out, x, ln_g, ln_b, *, tm, alpha):
    rows, d = x.shape
    kern = functools.partial(_out_kernel, alpha=alpha)
    return pl.pallas_call(
        kern,
        grid=(rows // tm,),
        in_specs=[
            pl.BlockSpec((tm, d), lambda i: (i, 0)),
            pl.BlockSpec((d, d), lambda i: (0, 0), pipeline_mode=pl.Buffered(1)),
            pl.BlockSpec((tm, d), lambda i: (i, 0)),
            pl.BlockSpec((1, d), lambda i: (0, 0)),
            pl.BlockSpec((1, d), lambda i: (0, 0)),
        ],
        out_specs=pl.BlockSpec((tm, d), lambda i: (i, 0)),
        out_shape=jax.ShapeDtypeStruct((rows, d), _F32),
        compiler_params=_params(("parallel",), vmem_limit=_VMEM_LIMIT_RESIDENT_WEIGHT),
        name="out_proj_ln",
    )(h, w_out, x, ln_g, ln_b)


def _tiles(batch, seq, n_sample):
    n_prompt = batch * seq
    m_all = n_prompt + n_sample
    tm_in = next(t for t in (832, 640, 512, 256, 128, 64, 32, 16) if m_all % t == 0)
    return {
        "in_tm": tm_in, "in_tn": 768,
        "row_tb": min(256, seq), "prep_tb": min(128, seq),
        "scan_tb": min(256, seq), "scan_pp": 8,
        "attn_tq": min(512, seq),
        "proj_tm": min(512, n_prompt), "proj_tn": 512, "out_tm": min(256, n_prompt),
        "pool_bb": min(32, n_sample), "step_hb": 2, "attn_bb": 2,
    }


def _layer(xp, xs, mem_p, cache_k, cache_v, st_pool, st_shift, st_rwkv, lp):
    batch, seq, d = xp.shape
    ns = xs.shape[0]
    n_p = batch * seq
    m_all = n_p + ns
    pw = lp["pool_scale"].shape[-1]
    rw = lp["rwkv_w0"].shape[-1]
    heads, hd = lp["rwkv_r_k"].shape
    lora = lp["rwkv_w2"].shape[0]
    sw = lp["rwkv_mu"].shape[-1]
    n_mem, mh, md = cache_k.shape[1:]
    mw = mh * md
    tl = _tiles(batch, seq, ns)

    lo0, lo1 = 2 * pw + 3 * rw, 2 * pw + sw
    lw = 2 * _LANES
    assert 2 * lora <= lw
    q_col0 = lo1 + rw
    gate_col0 = q_col0 + 2 * mw
    rot = gate_col0 % tl["proj_tn"]

    x_all = jnp.concatenate([xp.reshape(n_p, d), xs.reshape(ns, d)], axis=0)
    x_bf = x_all.astype(_BF16)
    z = _in_proj(x_bf, jnp.swapaxes(lp["w_in"], 0, 1), tm=tl["in_tm"], tn=tl["in_tn"])

    pool_w = lp["pool_w"].astype(_BF16)
    pool_scale = lp["pool_scale"].reshape(1, pw)
    po_p = _pool_prompt(z, pool_w, pool_scale, batch=batch, seq=seq, width=pw, tb=tl["row_tb"])
    po_s = _pool_sample(z, jnp.swapaxes(st_pool, 0, 1), pool_w, pool_scale, row0=n_p, bb=tl["pool_bb"])
    nbuf = st_pool.shape[1]
    new_pool_p = jnp.stack([z[(b + 1) * seq - nbuf:(b + 1) * seq, :pw] for b in range(batch)])
    new_pool_s = jnp.concatenate([st_pool[:, 1:, :], z[n_p:, :pw].reshape(ns, 1, pw)], axis=1)

    mu = lp["rwkv_mu"]
    pad_l = lambda a: jnp.pad(a, ((0, 0), (0, lw - 2 * lora)))
    w2p = jnp.zeros((lw, rw), _F32).at[:lora].set(lp["rwkv_w2"]).astype(_BF16)
    a2p = jnp.zeros((lw, rw), _F32).at[lora:2 * lora].set(lp["rwkv_a2"]).astype(_BF16)
    row = lambda a: a.reshape(1, -1)
    mix_consts = (row(mu[:rw]), row(mu[rw:2 * rw]), row(mu[2 * rw:3 * rw]), pad_l(row(mu[3 * rw:])),
                  row(lp["rwkv_w0"]), row(lp["rwkv_a0"]), row(lp["rwkv_k_k"]), row(lp["rwkv_k_a"]), w2p, a2p)
    post_consts = (row(lp["rwkv_r_k"]), row(lp["rwkv_ln_w"]), row(lp["rwkv_ln_b"]))

    r_p, ld_p, k_p, v_p, kk_p, be_p = _rwkv_prep_prompt(z, mix_consts, batch=batch, seq=seq, width=rw,
                                                        tb=tl["prep_tb"], head_dim=hd, col0=2 * pw, lw=lw)
    y_p, st_pairs = _rwkv_scan((r_p, ld_p, k_p, v_p, kk_p, be_p), batch=batch, seq=seq, width=rw,
                               tb=tl["scan_tb"], pp=tl["scan_pp"])
    ro_p = _rwkv_post(y_p, r_p, k_p, v_p, z, post_consts, row0=0, rows=n_p, width=rw, tb=tl["row_tb"],
                      head_dim=hd, col0=lo1)
    st5 = st_pairs.reshape(batch, heads // 2, 2, hd, 2, hd)
    new_state_p = jnp.stack([st5[:, :, 0, :, 0, :], st5[:, :, 1, :, 1, :]], axis=2).reshape(batch, heads, hd, hd)

    shift_s = st_shift.reshape(ns, sw)
    prep_s = _rwkv_prep_sample(z, shift_s[:, :3 * rw], pad_l(shift_s[:, 3 * rw:]), mix_consts, row0=n_p, rows=ns,
                               width=rw, head_dim=hd, col0=2 * pw, lw=lw)
    r_s, _, k_s, v_s, _, _ = prep_s[:6]
    y_t, state_t = _rwkv_step(prep_s[6:], jnp.transpose(st_rwkv, (1, 2, 3, 0)), hb=tl["step_hb"])
    new_state_s = jnp.transpose(state_t, (3, 0, 1, 2))
    ro_s = _rwkv_post(y_t.reshape(rw, ns).T, r_s, k_s, v_s, z, post_consts, row0=n_p, rows=ns, width=rw, tb=ns,
                      head_dim=hd, col0=lo1)
    new_shift_p = jnp.stack([z[(b + 1) * seq - 1:(b + 1) * seq, 2 * pw:lo1] for b in range(batch)])
    new_shift_s = z[n_p:, 2 * pw:lo1].reshape(ns, 1, sw)

    mkv = _matmul(mem_p.reshape(batch * n_mem, d).astype(_BF16), lp["w_mem_kv"].astype(_BF16),
                  tm=min(512, batch * n_mem), tn=min(1024, 2 * mw), out_dtype=_F32, name="mem_kv_proj")
    mo_p = _mem_attn_prompt(z, mkv, batch=batch, seq=seq, n_mem=n_mem, heads=mh, head_dim=md, tq=tl["attn_tq"],
                            col0=q_col0)
    mo_s = _mem_attn_sample(z[n_p:, q_col0:q_col0 + mw].reshape(ns, mh, md),
                            z[n_p:, q_col0 + mw:q_col0 + 2 * mw].reshape(ns, mh, md),
                            cache_k, cache_v, bb=tl["attn_bb"]).reshape(ns, mw).astype(_BF16)
    mk_p = mkv[:, :mw].reshape(batch, n_mem, mh, md)
    mv_p = mkv[:, mw:].reshape(batch, n_mem, mh, md)

    wp = jnp.roll(lp["w_branch_pool"], rot, axis=1).astype(_BF16)
    wr = jnp.roll(lp["w_branch_rwkv"], rot, axis=1).astype(_BF16)
    wm = jnp.roll(lp["w_branch_mem"], rot, axis=1).astype(_BF16)
    w_out = jnp.roll(lp["w_out"], rot, axis=0).astype(_BF16)
    b_gate = jnp.roll(lp["b_gate"].reshape(3, d), rot, axis=1).reshape(1, 3 * d)
    ln_g, ln_b = row(lp["ln_g"]), row(lp["ln_b"])
    alpha = lp["alpha"]
    outs = []
    for po, ro, mo, x2d, row0 in ((po_p, ro_p, mo_p, xp.reshape(n_p, d), 0), (po_s, ro_s, mo_s, xs.reshape(ns, d), n_p)):
        rows = x2d.shape[0]
        h = _branch(po, ro, mo, wp, wr, wm, z, b_gate, row0=row0, tm=min(tl["proj_tm"], rows), tn=tl["proj_tn"],
                    gate_col0=gate_col0, rot=rot)
        outs.append(_out_proj(h, w_out, x2d, ln_g, ln_b, tm=min(tl["out_tm"], rows), alpha=alpha))
    y_prompt = outs[0].reshape(batch, seq, d)
    y_sample = outs[1].reshape(ns, 1, d)
    return (y_prompt, y_sample, mk_p, mv_p, new_pool_p, new_shift_p, new_state_p, new_pool_s, new_shift_s,
            new_state_s)


def kernel(x_prompt, x_sample, cache_mem_k, cache_mem_v, state_pool, state_shift, state_rwkv, mem_prompt, w_in,
           b_gate, pool_w, pool_scale, rwkv_mu, rwkv_w0, rwkv_w2, rwkv_a0, rwkv_a2, rwkv_k_k, rwkv_k_a, rwkv_r_k,
           rwkv_ln_w, rwkv_ln_b, w_mem_kv, w_branch_pool, w_branch_rwkv, w_branch_mem, w_out, ln_g, ln_b):
    depth = w_in.shape[0]
    alpha = (2.0 * depth) ** 0.25
    weights = dict(w_in=w_in, b_gate=b_gate, pool_w=pool_w, pool_scale=pool_scale, rwkv_mu=rwkv_mu, rwkv_w0=rwkv_w0,
                   rwkv_w2=rwkv_w2, rwkv_a0=rwkv_a0, rwkv_a2=rwkv_a2, rwkv_k_k=rwkv_k_k, rwkv_k_a=rwkv_k_a,
                   rwkv_r_k=rwkv_r_k, rwkv_ln_w=rwkv_ln_w, rwkv_ln_b=rwkv_ln_b, w_mem_kv=w_mem_kv,
                   w_branch_pool=w_branch_pool, w_branch_rwkv=w_branch_rwkv, w_branch_mem=w_branch_mem,
                   w_out=w_out, ln_g=ln_g, ln_b=ln_b)
    yp, ys = x_prompt, x_sample
    per_layer = []
    for l in range(depth):
        lp = {k: v[l] for k, v in weights.items()}
        lp["alpha"] = alpha
        res = _layer(yp, ys, mem_prompt, cache_mem_k[l], cache_mem_v[l], state_pool[l], state_shift[l],
                     state_rwkv[l], lp)
        yp, ys = res[0], res[1]
        per_layer.append(res[2:])
    if depth == 1:
        stacked = tuple(a[None] for a in per_layer[0])
    else:
        stacked = tuple(jnp.stack([res[i] for res in per_layer]) for i in range(len(per_layer[0])))
    return (yp, ys) + stacked
```

```python
import functools

import jax
import jax.numpy as jnp
from jax import lax
from jax.experimental import pallas as pl
from jax.experimental.pallas import tpu as pltpu

_F32 = jnp.float32
_BF16 = jnp.bfloat16

_POOL_WINDOWS = (2, 4, 8, 16)
_PAST_LEN = 16384
_GN_EPS = 64e-5
_LN_EPS = 1e-5
_L2_EPS = 1e-12
_EXP_MINUS_HALF = 0.6065306597126334

_LANES = 128
_MIB = 1024 * 1024
_VMEM_LIMIT = 56 * _MIB
_VMEM_LIMIT_RESIDENT_WEIGHT = 62 * _MIB

_HEAD_PAIR = _LANES
_CHUNK = 64


def _params(sem, vmem_limit=_VMEM_LIMIT):
    return pltpu.CompilerParams(dimension_semantics=sem, vmem_limit_bytes=vmem_limit)


def _dot(a, b):
    return jnp.dot(a, b, preferred_element_type=_F32)


def _dot_nt(a, b):
    return lax.dot_general(a, b, (((1,), (1,)), ((), ())), preferred_element_type=_F32)


def _dot_tn(a, b):
    return lax.dot_general(a, b, (((0,), (0,)), ((), ())), preferred_element_type=_F32)


def _split(x):
    hi = x.astype(_BF16)
    lo = (x - hi.astype(_F32)).astype(_BF16)
    return hi, lo


def _mm1(dot, a, b):
    return dot(a.astype(_BF16), b.astype(_BF16))


def _mm_exact_rhs(a, b_bf16):
    a_hi, a_lo = _split(a)
    return _dot(a_hi, b_bf16) + _dot(a_lo, b_bf16)


def _silu(x):
    return x * jax.nn.sigmoid(x)


def _head_block_matrix(value, head_dim):
    r = lax.broadcasted_iota(jnp.int32, (_LANES, _LANES), 0) // head_dim
    c = lax.broadcasted_iota(jnp.int32, (_LANES, _LANES), 1) // head_dim
    return jnp.where(r == c, value, 0.0).astype(_BF16)


def _matmul_kernel(x_ref, w_ref, o_ref):
    o_ref[...] = _dot(x_ref[...], w_ref[...]).astype(o_ref.dtype)


def _matmul(x, w, *, tm, tn, out_dtype, name):
    m, k = x.shape
    n = w.shape[1]
    return pl.pallas_call(
        _matmul_kernel,
        grid=(m // tm, n // tn),
        in_specs=[pl.BlockSpec((tm, k), lambda i, j: (i, 0)), pl.BlockSpec((k, tn), lambda i, j: (0, j))],
        out_specs=pl.BlockSpec((tm, tn), lambda i, j: (i, j)),
        out_shape=jax.ShapeDtypeStruct((m, n), out_dtype),
        compiler_params=_params(("parallel", "parallel")),
        name=name,
    )(x, w)


def _in_proj_kernel(x_ref, xs_ref, wt_ref, o_ref, os_ref, wb_ref):
    @pl.when(pl.program_id(1) == 0)
    def _():
        wb_ref[...] = wt_ref[...].astype(_BF16)
        os_ref[...] = _dot_nt(xs_ref[...], wb_ref[...])

    o_ref[...] = _dot_nt(x_ref[...], wb_ref[...])


def _in_proj(x, xs, wt, *, tm, tn):
    m, k = x.shape
    ms = xs.shape[0]
    n = wt.shape[0]
    return pl.pallas_call(
        _in_proj_kernel,
        grid=(pl.cdiv(n, tn), m // tm),
        in_specs=[pl.BlockSpec((tm, k), lambda j, i: (i, 0)), pl.BlockSpec((ms, k), lambda j, i: (0, 0)),
                  pl.BlockSpec((tn, k), lambda j, i: (j, 0))],
        out_specs=[pl.BlockSpec((tm, tn), lambda j, i: (i, j)), pl.BlockSpec((ms, tn), lambda j, i: (0, j))],
        out_shape=[jax.ShapeDtypeStruct((m, n), _F32), jax.ShapeDtypeStruct((ms, n), _F32)],
        scratch_shapes=[pltpu.VMEM((tn, k), _BF16)],
        compiler_params=_params(("parallel", "arbitrary"), vmem_limit=_VMEM_LIMIT_RESIDENT_WEIGHT),
        name="in_proj",
    )(x, xs, wt)


def _pool_mix(pooled_fn, zp_ref, pw_ref, ps_ref, o_ref, group):
    for g in range(len(_POOL_WINDOWS)):
        cols = slice(g * group, (g + 1) * group)
        mixed = _dot(pooled_fn(g, cols).astype(_BF16), pw_ref[g])
        o_ref[:, cols] = (mixed * ps_ref[:, cols] * _silu(zp_ref[:, cols])).astype(o_ref.dtype)


def _pool_prompt_kernel(u_ref, halo_ref, zp_ref, pw_ref, ps_ref, o_ref, e_ref, *, tb, hist, group):
    t = pl.program_id(1)
    e_ref[0:hist, :] = jnp.where(t == 0, 0.0, halo_ref[...])
    e_ref[hist:hist + tb, :] = u_ref[...]
    pos = t * tb + lax.broadcasted_iota(jnp.int32, (tb, group), 0)

    def pooled(g, cols):
        win = _POOL_WINDOWS[g]
        x = e_ref[hist:hist + tb, cols]
        acc = x
        for d in range(1, win):
            acc = acc + e_ref[hist - d:hist - d + tb, cols]
        cnt = jnp.minimum(pos + 1, win).astype(_F32)
        return acc / cnt - x

    _pool_mix(pooled, zp_ref, pw_ref, ps_ref, o_ref, group)


def _pool_prompt(z, pool_w, pool_scale, *, batch, seq, width, tb):
    hist = 16
    n_t = seq // tb
    group = width // len(_POOL_WINDOWS)
    kern = functools.partial(_pool_prompt_kernel, tb=tb, hist=hist, group=group)
    return pl.pallas_call(
        kern,
        grid=(batch, n_t),
        in_specs=[
            pl.BlockSpec((tb, width), lambda b, t: (b * n_t + t, 0)),
            pl.BlockSpec((hist, width), lambda b, t: (jnp.maximum((b * seq + t * tb) // hist - 1, 0), 0)),
            pl.BlockSpec((tb, width), lambda b, t: (b * n_t + t, 1)),
            pl.BlockSpec(pool_w.shape, lambda b, t: (0, 0, 0)),
            pl.BlockSpec((1, width), lambda b, t: (0, 0)),
        ],
        out_specs=pl.BlockSpec((tb, width), lambda b, t: (b * n_t + t, 0)),
        out_shape=jax.ShapeDtypeStruct((batch * seq, width), _BF16),
        scratch_shapes=[pltpu.VMEM((hist + tb, width), _F32)],
        compiler_params=_params(("parallel", "parallel")),
        name="pool_prompt",
    )(z, z, z, pool_w, pool_scale)


def _pool_sample_kernel(u_ref, buf_ref, zp_ref, pw_ref, ps_ref, o_ref, *, nbuf, group):
    def pooled(g, cols):
        win = _POOL_WINDOWS[g]
        x = u_ref[:, cols]
        acc = x
        for d in range(1, win):
            acc = acc + buf_ref[nbuf - d, :, cols]
        cnt = float(min(_PAST_LEN + 1, win))
        return acc / cnt - x

    _pool_mix(pooled, zp_ref, pw_ref, ps_ref, o_ref, group)


def _pool_sample(z, buf, pool_w, pool_scale, *, row0, bb):
    nbuf, rows, width = buf.shape
    group = width // len(_POOL_WINDOWS)
    off = row0 // bb
    kern = functools.partial(_pool_sample_kernel, nbuf=nbuf, group=group)
    return pl.pallas_call(
        kern,
        grid=(rows // bb,),
        in_specs=[
            pl.BlockSpec((bb, width), lambda i: (off + i, 0)),
            pl.BlockSpec((nbuf, bb, width), lambda i: (0, i, 0)),
            pl.BlockSpec((bb, width), lambda i: (off + i, 1)),
            pl.BlockSpec(pool_w.shape, lambda i: (0, 0, 0)),
            pl.BlockSpec((1, width), lambda i: (0, 0)),
        ],
        out_specs=pl.BlockSpec((bb, width), lambda i: (i, 0)),
        out_shape=jax.ShapeDtypeStruct((rows, width), _BF16),
        compiler_params=_params(("parallel",)),
        name="pool_sample",
    )(z, buf, z, pool_w, pool_scale)


def _rwkv_mix(cur, prev, mu, w0, a0, kk_scale, ka, w2p, a2p, head_dim, outs):
    o_r, o_ld, o_k, o_v, o_kk, o_be = outs
    mix = lambda n: cur[n] + (prev[n] - cur[n]) * mu[n]
    xl = mix("l")
    wl = w0 + _mm1(_dot, jnp.tanh(xl), w2p)
    a = jax.nn.sigmoid(a0 + _mm1(_dot, xl, a2p))
    xk = mix("k")
    kkp = xk * kk_scale
    ones_bd = _head_block_matrix(1.0, head_dim)
    width = kkp.shape[1]
    o_r[...] = mix("r")
    o_v[...] = mix("v")
    o_ld[...] = -_EXP_MINUS_HALF * jax.nn.sigmoid(wl)
    o_k[...] = xk * (1.0 + (a - 1.0) * ka)
    for p in range(width // _LANES):
        cols = slice(p * _LANES, (p + 1) * _LANES)
        kp = kkp[:, cols]
        ss = _mm_exact_rhs(kp * kp, ones_bd)
        kk = kp * lax.rsqrt(jnp.maximum(ss, _L2_EPS * _L2_EPS))
        o_kk[:, cols] = kk
        o_be[:, cols] = kk * a[:, cols]


def _rwkv_prep_prompt_kernel(zr, zk, zv, zl, hr, hk, hv, hl, mur, muk, muv, mul, w0, a0, kks, ka, w2p, a2p,
                             o_r, o_ld, o_k, o_v, o_kk, o_be, er, ek, ev, el, *, tb, head_dim):
    t = pl.program_id(1)
    halo = 8
    cur, prev = {}, {}
    for name, z_ref, h_ref, e_ref in (("r", zr, hr, er), ("k", zk, hk, ek), ("v", zv, hv, ev), ("l", zl, hl, el)):
        e_ref[0:halo, :] = jnp.where(t == 0, 0.0, h_ref[...])
        e_ref[halo:halo + tb, :] = z_ref[...]
        cur[name] = z_ref[...]
        prev[name] = e_ref[halo - 1:halo - 1 + tb, :]
    mu = {"r": mur[...], "k": muk[...], "v": muv[...], "l": mul[...]}
    _rwkv_mix(cur, prev, mu, w0[...], a0[...], kks[...], ka[...], w2p[...], a2p[...], head_dim,
              (o_r, o_ld, o_k, o_v, o_kk, o_be))


def _rwkv_prep_prompt(z, consts, *, batch, seq, width, tb, head_dim, col0, lw):
    n_t = seq // tb
    c0 = col0 // width
    lcol = (col0 + 3 * width) // lw
    assert c0 * width == col0 and lcol * lw == col0 + 3 * width
    halo = 8
    row = lambda b, t: b * n_t + t
    hrow = lambda b, t: jnp.maximum((b * seq + t * tb) // halo - 1, 0)
    cblk = lambda c: pl.BlockSpec((tb, width), lambda b, t, c=c: (row(b, t), c))
    hblk = lambda c: pl.BlockSpec((halo, width), lambda b, t, c=c: (hrow(b, t), c))
    vec = lambda n: pl.BlockSpec((1, n), lambda b, t: (0, 0))
    mat = pl.BlockSpec((lw, width), lambda b, t: (0, 0))
    out_spec = pl.BlockSpec((tb, width), lambda b, t: (row(b, t), 0))
    out_shape = jax.ShapeDtypeStruct((batch * seq, width), _F32)
    kern = functools.partial(_rwkv_prep_prompt_kernel, tb=tb, head_dim=head_dim)
    return pl.pallas_call(
        kern,
        grid=(batch, n_t),
        in_specs=[cblk(c0), cblk(c0 + 1), cblk(c0 + 2), pl.BlockSpec((tb, lw), lambda b, t: (row(b, t), lcol)),
                  hblk(c0), hblk(c0 + 1), hblk(c0 + 2), pl.BlockSpec((halo, lw), lambda b, t: (hrow(b, t), lcol)),
                  vec(width), vec(width), vec(width), vec(lw),
                  vec(width), vec(width), vec(width), vec(width), mat, mat],
        out_specs=[out_spec] * 6,
        out_shape=[out_shape] * 6,
        scratch_shapes=[pltpu.VMEM((halo + tb, width), _F32)] * 3 + [pltpu.VMEM((halo + tb, lw), _F32)],
        compiler_params=_params(("parallel", "parallel")),
        name="rwkv_prep_prompt",
    )(z, z, z, z, z, z, z, z, *consts)


def _rwkv_prep_sample_kernel(zr, zk, zv, zl, pr, pk, pv, plr, mur, muk, muv, mul, w0, a0, kks, ka, w2p, a2p,
                             o_r, o_ld, o_k, o_v, o_kk, o_be, t_r, t_ld, t_k, t_v, t_kk, t_be, *, head_dim):
    cur = {"r": zr[...], "k": zk[...], "v": zv[...], "l": zl[...]}
    prev = {"r": pr[...], "k": pk[...], "v": pv[...], "l": plr[...]}
    mu = {"r": mur[...], "k": muk[...], "v": muv[...], "l": mul[...]}
    _rwkv_mix(cur, prev, mu, w0[...], a0[...], kks[...], ka[...], w2p[...], a2p[...], head_dim,
              (o_r, o_ld, o_k, o_v, o_kk, o_be))
    for o_ref, t_ref in ((o_r, t_r), (o_ld, t_ld), (o_k, t_k), (o_v, t_v), (o_kk, t_kk), (o_be, t_be)):
        t_ref[...] = o_ref[...].T


def _rwkv_prep_sample(z, prev_rkv, prev_l, consts, *, row0, rows, width, head_dim, col0, lw):
    c0 = col0 // width
    lcol = (col0 + 3 * width) // lw
    assert c0 * width == col0 and lcol * lw == col0 + 3 * width
    off = row0 // rows
    cblk = lambda c: pl.BlockSpec((rows, width), lambda i, c=c: (off, c))
    pblk = lambda c: pl.BlockSpec((rows, width), lambda i, c=c: (0, c))
    vec = lambda n: pl.BlockSpec((1, n), lambda i: (0, 0))
    mat = pl.BlockSpec((lw, width), lambda i: (0, 0))
    out_spec = pl.BlockSpec((rows, width), lambda i: (0, 0))
    out_shape = jax.ShapeDtypeStruct((rows, width), _F32)
    kern = functools.partial(_rwkv_prep_sample_kernel, head_dim=head_dim)
    return pl.pallas_call(
        kern,
        grid=(1,),
        in_specs=[cblk(c0), cblk(c0 + 1), cblk(c0 + 2), pl.BlockSpec((rows, lw), lambda i: (off, lcol)),
                  pblk(0), pblk(1), pblk(2), pl.BlockSpec((rows, lw), lambda i: (0, 0)),
                  vec(width), vec(width), vec(width), vec(lw),
                  vec(width), vec(width), vec(width), vec(width), mat, mat],
        out_specs=[out_spec] * 6 + [pl.BlockSpec((width, rows), lambda i: (0, 0))] * 6,
        out_shape=[out_shape] * 6 + [jax.ShapeDtypeStruct((width, rows), _F32)] * 6,
        compiler_params=_params(("arbitrary",)),
        name="rwkv_prep_sample",
    )(z, z, z, z, prev_rkv, prev_rkv, prev_rkv, prev_l, *consts)


def _scan_chunks(inputs, states, c):
    pairs = range(len(inputs))
    n2 = _LANES
    bf = lambda x: x.astype(_BF16)
    m0, m1 = c["m0"], c["m1"]
    expand = lambda x: jnp.concatenate([x * m0, x * m1], axis=0)
    stack = lambda a, b: jnp.concatenate([expand(a), expand(b)], axis=0)

    cum = [_mm_exact_rhs_t(c["ltri"], inputs[q][1]) for q in pairs]
    tot = [cum[q][_CHUNK - 1:_CHUNK, :] for q in pairs]
    xa, xb, bk, ve = [], [], [], []
    for q in pairs:
        r, ld, k2, v, kk, be = inputs[q]
        e_neg = jnp.exp(-cum[q])
        e_rem = jnp.exp(tot[q] - cum[q])
        xa.append(bf(stack(-kk * jnp.exp(cum[q] - ld), r * jnp.exp(cum[q]))))
        xb.append(bf(stack(be * e_neg, k2 * e_neg)))
        bk.append(bf(stack(be * e_rem, k2 * e_rem)))
        ve.append(bf(expand(v)))
    g = [_dot_nt(xa[q], xb[q]) for q in pairs]
    ps = [_dot_nt(xa[q], bf(states[q])) for q in pairs]
    a_kk = [bf(jnp.concatenate([jnp.where(c["strict"], g[q][:n2, n2:], 0.0),
                                jnp.where(c["incl"], g[q][n2:, n2:], 0.0)], axis=0)) for q in pairs]
    av = [_dot(a_kk[q], ve[q]) for q in pairs]
    npow = [jnp.where(c["strict"], g[q][:n2, :n2], 0.0) for q in pairs]
    u = [ps[q][:n2] + av[q][:n2] for q in pairs]
    for k in range(c["squarings"]):
        out = [_dot(bf(npow[q]), bf(jnp.concatenate([npow[q], u[q]], axis=1))) for q in pairs]
        npow = [out[q][:, :n2] for q in pairs]
        u = [u[q] + out[q][:, n2:] for q in pairs]
    ub = [bf(u[q]) for q in pairs]
    u = [u[q] + _dot(bf(npow[q]), ub[q]) for q in pairs]
    ub = [bf(u[q]) for q in pairs]
    a_rb = [bf(jnp.where(c["incl"], g[q][n2:, :n2], 0.0)) for q in pairs]
    ye = [ps[q][n2:] + av[q][n2:] + _dot(a_rb[q], ub[q]) for q in pairs]
    ys = [ye[q][:_CHUNK] + ye[q][_CHUNK:] for q in pairs]
    new_states = [states[q] * jnp.exp(tot[q]) + _dot_tn(jnp.concatenate([ub[q], ve[q]], axis=0), bk[q])
                  for q in pairs]
    return ys, new_states


def _mm_exact_rhs_t(l_bf16, x):
    x_hi, x_lo = _split(x)
    return _dot(l_bf16, x_hi) + _dot(l_bf16, x_lo)


def _scan_consts():
    n = _LANES
    half = _CHUNK
    ri = lax.broadcasted_iota(jnp.int32, (n, n), 0)
    ci = lax.broadcasted_iota(jnp.int32, (n, n), 1)
    same = (ri // half) == (ci // half)
    lane = lax.broadcasted_iota(jnp.int32, (half, n), 1)
    tr = lax.broadcasted_iota(jnp.int32, (half, half), 0)
    tc = lax.broadcasted_iota(jnp.int32, (half, half), 1)
    squarings = 0
    while (2 << squarings) < half:
        squarings += 1
    return {
        "strict": same & ((ri % half) > (ci % half)),
        "incl": same & ((ri % half) >= (ci % half)),
        "m0": jnp.where(lane < half, 1.0, 0.0).astype(_F32),
        "m1": jnp.where(lane >= half, 1.0, 0.0).astype(_F32),
        "ltri": jnp.where(tr >= tc, 1.0, 0.0).astype(_BF16),
        "squarings": squarings,
    }


def _rwkv_scan_kernel(r_ref, ld_ref, k_ref, v_ref, kk_ref, be_ref, y_ref, s_ref, st_ref, *, tb, pp):
    t = pl.program_id(2)

    @pl.when(t == 0)
    def _():
        st_ref[...] = jnp.zeros_like(st_ref)

    c = _scan_consts()
    in_refs = (r_ref, ld_ref, k_ref, v_ref, kk_ref, be_ref)

    def body(ci, carry):
        rows = pl.ds(pl.multiple_of(ci * _CHUNK, _CHUNK), _CHUNK)
        cols = [slice(q * _LANES, (q + 1) * _LANES) for q in range(pp)]
        ys, new_states = _scan_chunks([tuple(ref[rows, cols[q]] for ref in in_refs) for q in range(pp)],
                                      [st_ref[q] for q in range(pp)], c)
        for q in range(pp):
            y_ref[rows, cols[q]] = ys[q]
            st_ref[q] = new_states[q]
        return carry

    lax.fori_loop(0, tb // _CHUNK, body, 0)

    @pl.when(t == pl.num_programs(2) - 1)
    def _():
        s_ref[0] = st_ref[...]


def _rwkv_scan(arrs, *, batch, seq, width, tb, pp):
    n_t = seq // tb
    n_p = width // _LANES
    blk = pl.BlockSpec((tb, pp * _LANES), lambda b, p, t: (b * n_t + t, p))
    kern = functools.partial(_rwkv_scan_kernel, tb=tb, pp=pp)
    return pl.pallas_call(
        kern,
        grid=(batch, n_p // pp, n_t),
        in_specs=[blk] * 6,
        out_specs=[blk, pl.BlockSpec((1, pp, _LANES, _LANES), lambda b, p, t: (b, p, 0, 0))],
        out_shape=[jax.ShapeDtypeStruct((batch * seq, width), _F32),
                   jax.ShapeDtypeStruct((batch, n_p, _LANES, _LANES), _F32)],
        scratch_shapes=[pltpu.VMEM((pp, _LANES, _LANES), _F32)],
        compiler_params=_params(("parallel", "parallel", "arbitrary")),
        name="rwkv_scan",
    )(*arrs)


def _rwkv_step_kernel(r_ref, ld_ref, k_ref, v_ref, kk_ref, be_ref, s_ref, y_ref, so_ref, *, hb, head_dim, vi):
    for h in range(hb):
        kk = kk_ref[h][None]
        dec = jnp.exp(ld_ref[h])[None]
        be = be_ref[h][None]
        k = k_ref[h][None]
        r = r_ref[h][None]

        def body(c, carry, h=h, kk=kk, dec=dec, be=be, k=k, r=r):
            rows = pl.ds(pl.multiple_of(c * vi, vi), vi)
            s = s_ref[h, rows]
            sa = -jnp.sum(s * kk, axis=1, keepdims=True)
            s_new = s * dec + sa * be + v_ref[h, rows] * k
            y_ref[h, rows] = jnp.sum(s_new * r, axis=1, keepdims=True)
            so_ref[h, rows] = s_new
            return carry

        lax.fori_loop(0, head_dim // vi, body, 0)


def _rwkv_step(arrs_t, state_t, *, hb):
    heads, hd, _, n = state_t.shape
    r, ld, k, v, kk, be = arrs_t
    per_key = lambda a: a.reshape(heads, hd, n)
    vec = pl.BlockSpec((hb, hd, n), lambda i: (i, 0, 0))
    col = pl.BlockSpec((hb, hd, 1, n), lambda i: (i, 0, 0, 0))
    mat = pl.BlockSpec((hb, hd, hd, n), lambda i: (i, 0, 0, 0))
    kern = functools.partial(_rwkv_step_kernel, hb=hb, head_dim=hd, vi=8)
    return pl.pallas_call(
        kern,
        grid=(heads // hb,),
        in_specs=[vec, vec, vec, col, vec, vec, mat],
        out_specs=[col, mat],
        out_shape=[jax.ShapeDtypeStruct((heads, hd, 1, n), _F32), jax.ShapeDtypeStruct(state_t.shape, _F32)],
        compiler_params=_params(("parallel",)),
        name="rwkv_step",
    )(per_key(r), per_key(ld), per_key(k), v.reshape(heads, hd, 1, n), per_key(kk), per_key(be), state_t)


def _rwkv_post_kernel(y_ref, r_ref, k_ref, v_ref, za_ref, zb_ref, rk_ref, lw_ref, lb_ref, o_ref, *, head_dim, delta):
    avg_bd = _head_block_matrix(1.0 / head_dim, head_dim)
    ones_bd = _head_block_matrix(1.0, head_dim)
    width = y_ref.shape[1]
    zwin = jnp.concatenate([za_ref[...], zb_ref[...]], axis=1)
    for p in range(width // _LANES):
        cols = slice(p * _LANES, (p + 1) * _LANES)
        y = y_ref[:, cols]
        d = y - _mm_exact_rhs(y, avg_bd)
        var = _mm_exact_rhs(d * d, avg_bd)
        yn = d * lax.rsqrt(var + _GN_EPS) * lw_ref[:, cols] + lb_ref[:, cols]
        bonus = _mm_exact_rhs(r_ref[:, cols] * k_ref[:, cols] * rk_ref[:, cols], ones_bd) * v_ref[:, cols]
        zg = zwin[:, delta + p * _LANES:delta + (p + 1) * _LANES]
        o_ref[:, cols] = ((yn + bonus) * _silu(zg)).astype(o_ref.dtype)


def _rwkv_post(y, r, k2, v, z, consts, *, row0, rows, width, tb, head_dim, col0):
    off = row0 // tb
    tail = 2 * _LANES
    cg, delta = col0 // width, col0 % width
    assert delta <= tail and width % tail == 0
    blk = pl.BlockSpec((tb, width), lambda i: (i, 0))
    vec = pl.BlockSpec((1, width), lambda i: (0, 0))
    kern = functools.partial(_rwkv_post_kernel, head_dim=head_dim, delta=delta)
    return pl.pallas_call(
        kern,
        grid=(rows // tb,),
        in_specs=[blk, blk, blk, blk,
                  pl.BlockSpec((tb, width), lambda i: (off + i, cg)),
                  pl.BlockSpec((tb, tail), lambda i: (off + i, (cg + 1) * (width // tail))),
                  vec, vec, vec],
        out_specs=blk,
        out_shape=jax.ShapeDtypeStruct((rows, width), _BF16),
        compiler_params=_params(("parallel",)),
        name="rwkv_post",
    )(y, r, k2, v, z, z, *consts)


def _mem_attn_prompt_kernel(za_ref, zb_ref, zc_ref, k_ref, v_ref, o_ref, *, heads, head_dim, delta):
    scale = head_dim ** -0.5
    mw = heads * head_dim
    zwin = jnp.concatenate([za_ref[...], zb_ref[...], zc_ref[...]], axis=1)
    for h in range(heads):
        cols = slice(h * head_dim, (h + 1) * head_dim)
        q = zwin[:, delta + cols.start:delta + cols.stop]
        zg = zwin[:, delta + mw + cols.start:delta + mw + cols.stop]
        s = _dot_nt(q.astype(_BF16), k_ref[:, cols].astype(_BF16)) * scale
        p = jnp.exp(s - jnp.max(s, axis=-1, keepdims=True))
        o = _dot(p.astype(_BF16), v_ref[:, cols].astype(_BF16)) / jnp.sum(p, axis=-1, keepdims=True)
        o_ref[:, cols] = (o * _silu(zg)).astype(o_ref.dtype)


def _mem_attn_prompt(z, mkv, *, batch, seq, n_mem, heads, head_dim, tq, col0):
    n_t = seq // tq
    mw = heads * head_dim
    tail = 2 * _LANES
    q_col, delta = col0 // mw, col0 % mw
    assert delta <= tail and mw % tail == 0
    kern = functools.partial(_mem_attn_prompt_kernel, heads=heads, head_dim=head_dim, delta=delta)
    return pl.pallas_call(
        kern,
        grid=(batch, n_t),
        in_specs=[
            pl.BlockSpec((tq, mw), lambda b, t: (b * n_t + t, q_col)),
            pl.BlockSpec((tq, mw), lambda b, t: (b * n_t + t, q_col + 1)),
            pl.BlockSpec((tq, tail), lambda b, t: (b * n_t + t, (q_col + 2) * (mw // tail))),
            pl.BlockSpec((n_mem, mw), lambda b, t: (b, 0)),
            pl.BlockSpec((n_mem, mw), lambda b, t: (b, 1)),
        ],
        out_specs=pl.BlockSpec((tq, mw), lambda b, t: (b * n_t + t, 0)),
        out_shape=jax.ShapeDtypeStruct((batch * seq, mw), _BF16),
        compiler_params=_params(("parallel", "parallel")),
        name="mem_attn_prompt",
    )(z, z, z, mkv, mkv)


def _mem_attn_sample_kernel(q_ref, zg_ref, k_ref, v_ref, o_ref, *, bb, head_dim):
    scale = head_dim ** -0.5
    for b in range(bb):
        s = jnp.sum(k_ref[b] * q_ref[b][None], axis=-1, keepdims=True) * scale
        p = jnp.exp(s - jnp.max(s, axis=0, keepdims=True))
        o = jnp.sum(p * v_ref[b], axis=0) / jnp.sum(p, axis=0)
        o_ref[b] = o * _silu(zg_ref[b])


def _mem_attn_sample(q, zg, mem_k, mem_v, *, bb):
    n, n_mem, heads, head_dim = mem_k.shape
    vec = pl.BlockSpec((bb, heads, head_dim), lambda i: (i, 0, 0))
    mat = pl.BlockSpec((bb, n_mem, heads, head_dim), lambda i: (i, 0, 0, 0))
    kern = functools.partial(_mem_attn_sample_kernel, bb=bb, head_dim=head_dim)
    return pl.pallas_call(
        kern,
        grid=(n // bb,),
        in_specs=[vec, vec, mat, mat],
        out_specs=vec,
        out_shape=jax.ShapeDtypeStruct((n, heads, head_dim), _F32),
        compiler_params=_params(("parallel",)),
        name="mem_attn_sample",
    )(q, zg, mem_k, mem_v)


def _branch_kernel(po_ref, ro_ref, mo_ref, wp_ref, wr_ref, wm_ref, gp_ref, gr_ref, gm_ref, tp_ref, tr_ref, tm_ref,
                   bp_ref, br_ref, bm_ref, o_ref, wpb_ref, wrb_ref, wmb_ref, *, delta):
    @pl.when(pl.program_id(1) == 0)
    def _():
        wpb_ref[...] = wp_ref[...].astype(_BF16)
        wrb_ref[...] = wr_ref[...].astype(_BF16)
        wmb_ref[...] = wm_ref[...].astype(_BF16)

    tn = o_ref.shape[1]

    def gate(g_ref, t_ref, b_ref):
        win = jnp.concatenate([g_ref[...], t_ref[...]], axis=1)
        return jax.nn.sigmoid(win[:, delta:delta + tn] + b_ref[...])

    h = gate(gp_ref, tp_ref, bp_ref) * _dot(po_ref[...], wpb_ref[...])
    h = h + gate(gr_ref, tr_ref, br_ref) * _dot(ro_ref[...], wrb_ref[...])
    h = h + gate(gm_ref, tm_ref, bm_ref) * _dot(mo_ref[...], wmb_ref[...])
    o_ref[...] = h.astype(o_ref.dtype)


def _branch(po, ro, mo, wp, wr, wm, z, b_gate, *, row0, tm, tn, gate_col0):
    rows = po.shape[0]
    d = wp.shape[1]
    off = row0 // tm
    tail = 2 * _LANES
    delta = gate_col0 % tn
    g0 = gate_col0 // tn
    assert delta <= tail and tn % tail == 0
    nb = d // tn
    act = lambda a: pl.BlockSpec((tm, a.shape[1]), lambda j, i: (i, 0))
    wgt = lambda w: pl.BlockSpec((w.shape[0], tn), lambda j, i: (0, j))
    gat = lambda k: pl.BlockSpec((tm, tn), lambda j, i, k=k: (off + i, g0 + k * nb + j))
    tai = lambda k: pl.BlockSpec((tm, tail), lambda j, i, k=k: (off + i, (g0 + k * nb + j + 1) * (tn // tail)))
    bia = lambda k: pl.BlockSpec((1, tn), lambda j, i, k=k: (0, k * nb + j))
    return pl.pallas_call(
        functools.partial(_branch_kernel, delta=delta),
        grid=(nb, rows // tm),
        in_specs=[act(po), act(ro), act(mo), wgt(wp), wgt(wr), wgt(wm), gat(0), gat(1), gat(2),
                  tai(0), tai(1), tai(2), bia(0), bia(1), bia(2)],
        out_specs=pl.BlockSpec((tm, tn), lambda j, i: (i, j)),
        out_shape=jax.ShapeDtypeStruct((rows, d), _BF16),
        scratch_shapes=[pltpu.VMEM((w.shape[0], tn), _BF16) for w in (wp, wr, wm)],
        compiler_params=_params(("parallel", "arbitrary")),
        name="branch_proj",
    )(po, ro, mo, wp, wr, wm, z, z, z, z, z, z, b_gate, b_gate, b_gate)


def _out_kernel(h_ref, w_ref, x_ref, g_ref, b_ref, o_ref, *, alpha):
    xf = alpha * x_ref[...] + _dot(h_ref[...], w_ref[...])
    mu = jnp.mean(xf, axis=-1, keepdims=True)
    d = xf - mu
    var = jnp.mean(d * d, axis=-1, keepdims=True)
    o_ref[...] = d * lax.rsqrt(var + _LN_EPS) * g_ref[...] + b_ref[...]


def _out_proj(h, w_out, x, ln_g, ln_b, *, tm, alpha):
    rows, d = x.shape
    kern = functools.partial(_out_kernel, alpha=alpha)
    return pl.pallas_call(
        kern,
        grid=(rows // tm,),
        in_specs=[
            pl.BlockSpec((tm, d), lambda i: (i, 0)),
            pl.BlockSpec((d, d), lambda i: (0, 0), pipeline_mode=pl.Buffered(1)),
            pl.BlockSpec((tm, d), lambda i: (i, 0)),
            pl.BlockSpec((1, d), lambda i: (0, 0)),
            pl.BlockSpec((1, d), lambda i: (0, 0)),
        ],
        out_specs=pl.BlockSpec((tm, d), lambda i: (i, 0)),
        out_shape=jax.ShapeDtypeStruct((rows, d), _F32),
        compiler_params=_params(("parallel",), vmem_limit=_VMEM_LIMIT_RESIDENT_WEIGHT),
        name="out_proj_ln",
    )(h, w_out, x, ln_g, ln_b)


def _tiles(batch, seq, n_sample):
    n_prompt = batch * seq
    return {
        "in_tm": min(1024, n_prompt), "in_tn": 640,
        "row_tb": min(256, seq), "prep_tb": min(128, seq),
        "scan_tb": min(256, seq), "scan_pp": 8,
        "attn_tq": min(512, seq),
        "proj_tm": min(512, n_prompt), "proj_tn": 512, "out_tm": min(256, n_prompt),
        "pool_bb": min(32, n_sample), "step_hb": 2, "attn_bb": 2,
    }


def _layer(xp, xs, mem_p, cache_k, cache_v, st_pool, st_shift, st_rwkv, lp):
    batch, seq, d = xp.shape
    ns = xs.shape[0]
    n_p = batch * seq
    m_all = n_p + ns
    pw = lp["pool_scale"].shape[-1]
    rw = lp["rwkv_w0"].shape[-1]
    heads, hd = lp["rwkv_r_k"].shape
    lora = lp["rwkv_w2"].shape[0]
    sw = lp["rwkv_mu"].shape[-1]
    n_mem, mh, md = cache_k.shape[1:]
    mw = mh * md
    tl = _tiles(batch, seq, ns)

    lo0, lo1 = 2 * pw + 3 * rw, 2 * pw + sw
    lw = 2 * _LANES
    assert 2 * lora <= lw
    q_col0 = lo1 + rw
    gate_col0 = q_col0 + 2 * mw

    z, z_s = _in_proj(xp.reshape(n_p, d).astype(_BF16), xs.reshape(ns, d).astype(_BF16),
                      jnp.swapaxes(lp["w_in"], 0, 1), tm=tl["in_tm"], tn=tl["in_tn"])

    pool_w = lp["pool_w"].astype(_BF16)
    pool_scale = lp["pool_scale"].reshape(1, pw)
    po_p = _pool_prompt(z, pool_w, pool_scale, batch=batch, seq=seq, width=pw, tb=tl["row_tb"])
    po_s = _pool_sample(z_s, jnp.swapaxes(st_pool, 0, 1), pool_w, pool_scale, row0=0, bb=tl["pool_bb"])
    nbuf = st_pool.shape[1]
    new_pool_p = jnp.stack([z[(b + 1) * seq - nbuf:(b + 1) * seq, :pw] for b in range(batch)])
    new_pool_s = jnp.concatenate([st_pool[:, 1:, :], z_s[:, :pw].reshape(ns, 1, pw)], axis=1)

    mu = lp["rwkv_mu"]
    pad_l = lambda a: jnp.pad(a, ((0, 0), (0, lw - 2 * lora)))
    w2p = jnp.zeros((lw, rw), _F32).at[:lora].set(lp["rwkv_w2"]).astype(_BF16)
    a2p = jnp.zeros((lw, rw), _F32).at[lora:2 * lora].set(lp["rwkv_a2"]).astype(_BF16)
    row = lambda a: a.reshape(1, -1)
    mix_consts = (row(mu[:rw]), row(mu[rw:2 * rw]), row(mu[2 * rw:3 * rw]), pad_l(row(mu[3 * rw:])),
                  row(lp["rwkv_w0"]), row(lp["rwkv_a0"]), row(lp["rwkv_k_k"]), row(lp["rwkv_k_a"]), w2p, a2p)
    post_consts = (row(lp["rwkv_r_k"]), row(lp["rwkv_ln_w"]), row(lp["rwkv_ln_b"]))

    r_p, ld_p, k_p, v_p, kk_p, be_p = _rwkv_prep_prompt(z, mix_consts, batch=batch, seq=seq, width=rw,
                                                        tb=tl["prep_tb"], head_dim=hd, col0=2 * pw, lw=lw)
    y_p, st_pairs = _rwkv_scan((r_p, ld_p, k_p, v_p, kk_p, be_p), batch=batch, seq=seq, width=rw,
                               tb=tl["scan_tb"], pp=tl["scan_pp"])
    ro_p = _rwkv_post(y_p, r_p, k_p, v_p, z, post_consts, row0=0, rows=n_p, width=rw, tb=tl["row_tb"],
                      head_dim=hd, col0=lo1)
    st5 = st_pairs.reshape(batch, heads // 2, 2, hd, 2, hd)
    new_state_p = jnp.stack([st5[:, :, 0, :, 0, :], st5[:, :, 1, :, 1, :]], axis=2).reshape(batch, heads, hd, hd)

    shift_s = st_shift.reshape(ns, sw)
    prep_s = _rwkv_prep_sample(z_s, shift_s[:, :3 * rw], pad_l(shift_s[:, 3 * rw:]), mix_consts, row0=0, rows=ns,
                               width=rw, head_dim=hd, col0=2 * pw, lw=lw)
    r_s, _, k_s, v_s, _, _ = prep_s[:6]
    y_t, state_t = _rwkv_step(prep_s[6:], jnp.transpose(st_rwkv, (1, 2, 3, 0)), hb=tl["step_hb"])
    new_state_s = jnp.transpose(state_t, (3, 0, 1, 2))
    ro_s = _rwkv_post(y_t.reshape(rw, ns).T, r_s, k_s, v_s, z_s, post_consts, row0=0, rows=ns, width=rw, tb=ns,
                      head_dim=hd, col0=lo1)
    new_shift_p = jnp.stack([z[(b + 1) * seq - 1:(b + 1) * seq, 2 * pw:lo1] for b in range(batch)])
    new_shift_s = z_s[:, 2 * pw:lo1].reshape(ns, 1, sw)

    mkv = _matmul(mem_p.reshape(batch * n_mem, d).astype(_BF16), lp["w_mem_kv"].astype(_BF16),
                  tm=min(512, batch * n_mem), tn=min(1024, 2 * mw), out_dtype=_F32, name="mem_kv_proj")
    mo_p = _mem_attn_prompt(z, mkv, batch=batch, seq=seq, n_mem=n_mem, heads=mh, head_dim=md, tq=tl["attn_tq"],
                            col0=q_col0)
    mo_s = _mem_attn_sample(z_s[:, q_col0:q_col0 + mw].reshape(ns, mh, md),
                            z_s[:, q_col0 + mw:q_col0 + 2 * mw].reshape(ns, mh, md),
                            cache_k, cache_v, bb=tl["attn_bb"]).reshape(ns, mw).astype(_BF16)
    mk_p = mkv[:, :mw].reshape(batch, n_mem, mh, md)
    mv_p = mkv[:, mw:].reshape(batch, n_mem, mh, md)

    wp, wr, wm = lp["w_branch_pool"], lp["w_branch_rwkv"], lp["w_branch_mem"]
    w_out = lp["w_out"].astype(_BF16)
    b_gate = row(lp["b_gate"])
    ln_g, ln_b = row(lp["ln_g"]), row(lp["ln_b"])
    alpha = lp["alpha"]
    outs = []
    for po, ro, mo, x2d, zz in ((po_p, ro_p, mo_p, xp.reshape(n_p, d), z), (po_s, ro_s, mo_s, xs.reshape(ns, d), z_s)):
        rows = x2d.shape[0]
        h = _branch(po, ro, mo, wp, wr, wm, zz, b_gate, row0=0, tm=min(tl["proj_tm"], rows), tn=tl["proj_tn"],
                    gate_col0=gate_col0)
        outs.append(_out_proj(h, w_out, x2d, ln_g, ln_b, tm=min(tl["out_tm"], rows), alpha=alpha))
    y_prompt = outs[0].reshape(batch, seq, d)
    y_sample = outs[1].reshape(ns, 1, d)
    return (y_prompt, y_sample, mk_p, mv_p, new_pool_p, new_shift_p, new_state_p, new_pool_s, new_shift_s,
            new_state_s)


def kernel(x_prompt, x_sample, cache_mem_k, cache_mem_v, state_pool, state_shift, state_rwkv, mem_prompt, w_in,
           b_gate, pool_w, pool_scale, rwkv_mu, rwkv_w0, rwkv_w2, rwkv_a0, rwkv_a2, rwkv_k_k, rwkv_k_a, rwkv_r_k,
           rwkv_ln_w, rwkv_ln_b, w_mem_kv, w_branch_pool, w_branch_rwkv, w_branch_mem, w_out, ln_g, ln_b):
    depth = w_in.shape[0]
    alpha = (2.0 * depth) ** 0.25
    weights = dict(w_in=w_in, b_gate=b_gate, pool_w=pool_w, pool_scale=pool_scale, rwkv_mu=rwkv_mu, rwkv_w0=rwkv_w0,
                   rwkv_w2=rwkv_w2, rwkv_a0=rwkv_a0, rwkv_a2=rwkv_a2, rwkv_k_k=rwkv_k_k, rwkv_k_a=rwkv_k_a,
                   rwkv_r_k=rwkv_r_k, rwkv_ln_w=rwkv_ln_w, rwkv_ln_b=rwkv_ln_b, w_mem_kv=w_mem_kv,
                   w_branch_pool=w_branch_pool, w_branch_rwkv=w_branch_rwkv, w_branch_mem=w_branch_mem,
                   w_out=w_out, ln_g=ln_g, ln_b=ln_b)
    yp, ys = x_prompt, x_sample
    per_layer = []
    for l in range(depth):
        lp = {k: v[l] for k, v in weights.items()}
        lp["alpha"] = alpha
        res = _layer(yp, ys, mem_prompt, cache_mem_k[l], cache_mem_v[l], state_pool[l], state_shift[l],
                     state_rwkv[l], lp)
        yp, ys = res[0], res[1]
        per_layer.append(res[2:])
    if depth == 1:
        stacked = tuple(a[None] for a in per_layer[0])
    else:
        stacked = tuple(jnp.stack([res[i] for res in per_layer]) for i in range(len(per_layer[0])))
    return (yp, ys) + stacked
```

```python
import functools

import jax
import jax.numpy as jnp
from jax import lax
from jax.experimental import pallas as pl
from jax.experimental.pallas import tpu as pltpu

_F32 = jnp.float32
_BF16 = jnp.bfloat16

_POOL_WINDOWS = (2, 4, 8, 16)
_PAST_LEN = 16384
_GN_EPS = 64e-5
_LN_EPS = 1e-5
_L2_EPS = 1e-12
_EXP_MINUS_HALF = 0.6065306597126334

_LANES = 128
_MIB = 1024 * 1024
_VMEM_LIMIT = 56 * _MIB
_VMEM_LIMIT_RESIDENT_WEIGHT = 62 * _MIB

_HEAD_PAIR = _LANES
_CHUNK = 64


def _params(sem, vmem_limit=_VMEM_LIMIT):
    return pltpu.CompilerParams(dimension_semantics=sem, vmem_limit_bytes=vmem_limit)


def _dot(a, b):
    return jnp.dot(a, b, preferred_element_type=_F32)


def _dot_nt(a, b):
    return lax.dot_general(a, b, (((1,), (1,)), ((), ())), preferred_element_type=_F32)


def _dot_tn(a, b):
    return lax.dot_general(a, b, (((0,), (0,)), ((), ())), preferred_element_type=_F32)


def _split(x):
    hi = x.astype(_BF16)
    lo = (x - hi.astype(_F32)).astype(_BF16)
    return hi, lo


def _mm1(dot, a, b):
    return dot(a.astype(_BF16), b.astype(_BF16))


def _mm_exact_rhs(a, b_bf16):
    a_hi, a_lo = _split(a)
    return _dot(a_hi, b_bf16) + _dot(a_lo, b_bf16)


def _silu(x):
    return x * jax.nn.sigmoid(x)


def _head_block_matrix(value, head_dim):
    r = lax.broadcasted_iota(jnp.int32, (_LANES, _LANES), 0) // head_dim
    c = lax.broadcasted_iota(jnp.int32, (_LANES, _LANES), 1) // head_dim
    return jnp.where(r == c, value, 0.0).astype(_BF16)


def _matmul_kernel(x_ref, w_ref, o_ref):
    o_ref[...] = _dot(x_ref[...], w_ref[...]).astype(o_ref.dtype)


def _matmul(x, w, *, tm, tn, out_dtype, name):
    m, k = x.shape
    n = w.shape[1]
    return pl.pallas_call(
        _matmul_kernel,
        grid=(m // tm, n // tn),
        in_specs=[pl.BlockSpec((tm, k), lambda i, j: (i, 0)), pl.BlockSpec((k, tn), lambda i, j: (0, j))],
        out_specs=pl.BlockSpec((tm, tn), lambda i, j: (i, j)),
        out_shape=jax.ShapeDtypeStruct((m, n), out_dtype),
        compiler_params=_params(("parallel", "parallel")),
        name=name,
    )(x, w)


def _in_proj_kernel(x_ref, wt_ref, o_ref, wb_ref):
    @pl.when(pl.program_id(1) == 0)
    def _():
        wb_ref[...] = wt_ref[...].astype(_BF16)

    o_ref[...] = _dot_nt(x_ref[...], wb_ref[...])


def _in_proj(x, wt, *, tm, tn):
    m, k = x.shape
    n = wt.shape[0]
    return pl.pallas_call(
        _in_proj_kernel,
        grid=(pl.cdiv(n, tn), m // tm),
        in_specs=[pl.BlockSpec((tm, k), lambda j, i: (i, 0)), pl.BlockSpec((tn, k), lambda j, i: (j, 0))],
        out_specs=pl.BlockSpec((tm, tn), lambda j, i: (i, j)),
        out_shape=jax.ShapeDtypeStruct((m, n), _F32),
        scratch_shapes=[pltpu.VMEM((tn, k), _BF16)],
        compiler_params=_params(("parallel", "arbitrary")),
        name="in_proj",
    )(x, wt)


def _pool_mix(pooled_fn, zp_ref, pw_ref, ps_ref, o_ref, group):
    for g in range(len(_POOL_WINDOWS)):
        cols = slice(g * group, (g + 1) * group)
        mixed = _dot(pooled_fn(g, cols).astype(_BF16), pw_ref[g])
        o_ref[:, cols] = (mixed * ps_ref[:, cols] * _silu(zp_ref[:, cols])).astype(o_ref.dtype)


def _pool_prompt_kernel(u_ref, halo_ref, zp_ref, pw_ref, ps_ref, o_ref, e_ref, *, tb, hist, group):
    t = pl.program_id(1)
    e_ref[0:hist, :] = jnp.where(t == 0, 0.0, halo_ref[...])
    e_ref[hist:hist + tb, :] = u_ref[...]
    pos = t * tb + lax.broadcasted_iota(jnp.int32, (tb, group), 0)

    def pooled(g, cols):
        win = _POOL_WINDOWS[g]
        x = e_ref[hist:hist + tb, cols]
        acc = x
        for d in range(1, win):
            acc = acc + e_ref[hist - d:hist - d + tb, cols]
        cnt = jnp.minimum(pos + 1, win).astype(_F32)
        return acc / cnt - x

    _pool_mix(pooled, zp_ref, pw_ref, ps_ref, o_ref, group)


def _pool_prompt(z, pool_w, pool_scale, *, batch, seq, width, tb):
    hist = 16
    n_t = seq // tb
    group = width // len(_POOL_WINDOWS)
    kern = functools.partial(_pool_prompt_kernel, tb=tb, hist=hist, group=group)
    return pl.pallas_call(
        kern,
        grid=(batch, n_t),
        in_specs=[
            pl.BlockSpec((tb, width), lambda b, t: (b * n_t + t, 0)),
            pl.BlockSpec((hist, width), lambda b, t: (jnp.maximum((b * seq + t * tb) // hist - 1, 0), 0)),
            pl.BlockSpec((tb, width), lambda b, t: (b * n_t + t, 1)),
            pl.BlockSpec(pool_w.shape, lambda b, t: (0, 0, 0)),
            pl.BlockSpec((1, width), lambda b, t: (0, 0)),
        ],
        out_specs=pl.BlockSpec((tb, width), lambda b, t: (b * n_t + t, 0)),
        out_shape=jax.ShapeDtypeStruct((batch * seq, width), _BF16),
        scratch_shapes=[pltpu.VMEM((hist + tb, width), _F32)],
        compiler_params=_params(("parallel", "parallel")),
        name="pool_prompt",
    )(z, z, z, pool_w, pool_scale)


def _pool_sample_kernel(u_ref, buf_ref, zp_ref, pw_ref, ps_ref, o_ref, *, nbuf, group):
    def pooled(g, cols):
        win = _POOL_WINDOWS[g]
        x = u_ref[:, cols]
        acc = x
        for d in range(1, win):
            acc = acc + buf_ref[nbuf - d, :, cols]
        cnt = float(min(_PAST_LEN + 1, win))
        return acc / cnt - x

    _pool_mix(pooled, zp_ref, pw_ref, ps_ref, o_ref, group)


def _pool_sample(z, buf, pool_w, pool_scale, *, row0, bb):
    nbuf, rows, width = buf.shape
    group = width // len(_POOL_WINDOWS)
    off = row0 // bb
    kern = functools.partial(_pool_sample_kernel, nbuf=nbuf, group=group)
    return pl.pallas_call(
        kern,
        grid=(rows // bb,),
        in_specs=[
            pl.BlockSpec((bb, width), lambda i: (off + i, 0)),
            pl.BlockSpec((nbuf, bb, width), lambda i: (0, i, 0)),
            pl.BlockSpec((bb, width), lambda i: (off + i, 1)),
            pl.BlockSpec(pool_w.shape, lambda i: (0, 0, 0)),
            pl.BlockSpec((1, width), lambda i: (0, 0)),
        ],
        out_specs=pl.BlockSpec((bb, width), lambda i: (i, 0)),
        out_shape=jax.ShapeDtypeStruct((rows, width), _BF16),
        compiler_params=_params(("parallel",)),
        name="pool_sample",
    )(z, buf, z, pool_w, pool_scale)


def _rwkv_mix(cur, prev, mu, w0, a0, kk_scale, ka, w2p, a2p, head_dim, outs):
    o_r, o_ld, o_k, o_v, o_kk, o_be = outs
    mix = lambda n: cur[n] + (prev[n] - cur[n]) * mu[n]
    xl = mix("l")
    wl = w0 + _mm1(_dot, jnp.tanh(xl), w2p)
    a = jax.nn.sigmoid(a0 + _mm1(_dot, xl, a2p))
    xk = mix("k")
    kkp = xk * kk_scale
    ones_bd = _head_block_matrix(1.0, head_dim)
    width = kkp.shape[1]
    o_r[...] = mix("r")
    o_v[...] = mix("v")
    o_ld[...] = -_EXP_MINUS_HALF * jax.nn.sigmoid(wl)
    o_k[...] = xk * (1.0 + (a - 1.0) * ka)
    for p in range(width // _LANES):
        cols = slice(p * _LANES, (p + 1) * _LANES)
        kp = kkp[:, cols]
        ss = _mm_exact_rhs(kp * kp, ones_bd)
        kk = kp * lax.rsqrt(jnp.maximum(ss, _L2_EPS * _L2_EPS))
        o_kk[:, cols] = kk
        o_be[:, cols] = kk * a[:, cols]


def _rwkv_prep_prompt_kernel(zr, zk, zv, zl, hr, hk, hv, hl, mur, muk, muv, mul, w0, a0, kks, ka, w2p, a2p,
                             o_r, o_ld, o_k, o_v, o_kk, o_be, er, ek, ev, el, *, tb, head_dim):
    t = pl.program_id(1)
    halo = 8
    cur, prev = {}, {}
    for name, z_ref, h_ref, e_ref in (("r", zr, hr, er), ("k", zk, hk, ek), ("v", zv, hv, ev), ("l", zl, hl, el)):
        e_ref[0:halo, :] = jnp.where(t == 0, 0.0, h_ref[...])
        e_ref[halo:halo + tb, :] = z_ref[...]
        cur[name] = z_ref[...]
        prev[name] = e_ref[halo - 1:halo - 1 + tb, :]
    mu = {"r": mur[...], "k": muk[...], "v": muv[...], "l": mul[...]}
    _rwkv_mix(cur, prev, mu, w0[...], a0[...], kks[...], ka[...], w2p[...], a2p[...], head_dim,
              (o_r, o_ld, o_k, o_v, o_kk, o_be))


def _rwkv_prep_prompt(z, consts, *, batch, seq, width, tb, head_dim, col0, lw):
    n_t = seq // tb
    c0 = col0 // width
    lcol = (col0 + 3 * width) // lw
    assert c0 * width == col0 and lcol * lw == col0 + 3 * width
    halo = 8
    row = lambda b, t: b * n_t + t
    hrow = lambda b, t: jnp.maximum((b * seq + t * tb) // halo - 1, 0)
    cblk = lambda c: pl.BlockSpec((tb, width), lambda b, t, c=c: (row(b, t), c))
    hblk = lambda c: pl.BlockSpec((halo, width), lambda b, t, c=c: (hrow(b, t), c))
    vec = lambda n: pl.BlockSpec((1, n), lambda b, t: (0, 0))
    mat = pl.BlockSpec((lw, width), lambda b, t: (0, 0))
    out_spec = pl.BlockSpec((tb, width), lambda b, t: (row(b, t), 0))
    out_shape = jax.ShapeDtypeStruct((batch * seq, width), _F32)
    kern = functools.partial(_rwkv_prep_prompt_kernel, tb=tb, head_dim=head_dim)
    return pl.pallas_call(
        kern,
        grid=(batch, n_t),
        in_specs=[cblk(c0), cblk(c0 + 1), cblk(c0 + 2), pl.BlockSpec((tb, lw), lambda b, t: (row(b, t), lcol)),
                  hblk(c0), hblk(c0 + 1), hblk(c0 + 2), pl.BlockSpec((halo, lw), lambda b, t: (hrow(b, t), lcol)),
                  vec(width), vec(width), vec(width), vec(lw),
                  vec(width), vec(width), vec(width), vec(width), mat, mat],
        out_specs=[out_spec] * 6,
        out_shape=[out_shape] * 6,
        scratch_shapes=[pltpu.VMEM((halo + tb, width), _F32)] * 3 + [pltpu.VMEM((halo + tb, lw), _F32)],
        compiler_params=_params(("parallel", "parallel")),
        name="rwkv_prep_prompt",
    )(z, z, z, z, z, z, z, z, *consts)


def _rwkv_prep_sample_kernel(zr, zk, zv, zl, pr, pk, pv, plr, mur, muk, muv, mul, w0, a0, kks, ka, w2p, a2p,
                             o_r, o_ld, o_k, o_v, o_kk, o_be, t_r, t_ld, t_k, t_v, t_kk, t_be, *, head_dim):
    cur = {"r": zr[...], "k": zk[...], "v": zv[...], "l": zl[...]}
    prev = {"r": pr[...], "k": pk[...], "v": pv[...], "l": plr[...]}
    mu = {"r": mur[...], "k": muk[...], "v": muv[...], "l": mul[...]}
    _rwkv_mix(cur, prev, mu, w0[...], a0[...], kks[...], ka[...], w2p[...], a2p[...], head_dim,
              (o_r, o_ld, o_k, o_v, o_kk, o_be))
    for o_ref, t_ref in ((o_r, t_r), (o_ld, t_ld), (o_k, t_k), (o_v, t_v), (o_kk, t_kk), (o_be, t_be)):
        t_ref[...] = o_ref[...].T


def _rwkv_prep_sample(z, prev_rkv, prev_l, consts, *, row0, rows, width, head_dim, col0, lw):
    c0 = col0 // width
    lcol = (col0 + 3 * width) // lw
    assert c0 * width == col0 and lcol * lw == col0 + 3 * width
    off = row0 // rows
    cblk = lambda c: pl.BlockSpec((rows, width), lambda i, c=c: (off, c))
    pblk = lambda c: pl.BlockSpec((rows, width), lambda i, c=c: (0, c))
    vec = lambda n: pl.BlockSpec((1, n), lambda i: (0, 0))
    mat = pl.BlockSpec((lw, width), lambda i: (0, 0))
    out_spec = pl.BlockSpec((rows, width), lambda i: (0, 0))
    out_shape = jax.ShapeDtypeStruct((rows, width), _F32)
    kern = functools.partial(_rwkv_prep_sample_kernel, head_dim=head_dim)
    return pl.pallas_call(
        kern,
        grid=(1,),
        in_specs=[cblk(c0), cblk(c0 + 1), cblk(c0 + 2), pl.BlockSpec((rows, lw), lambda i: (off, lcol)),
                  pblk(0), pblk(1), pblk(2), pl.BlockSpec((rows, lw), lambda i: (0, 0)),
                  vec(width), vec(width), vec(width), vec(lw),
                  vec(width), vec(width), vec(width), vec(width), mat, mat],
        out_specs=[out_spec] * 6 + [pl.BlockSpec((width, rows), lambda i: (0, 0))] * 6,
        out_shape=[out_shape] * 6 + [jax.ShapeDtypeStruct((width, rows), _F32)] * 6,
        compiler_params=_params(("arbitrary",)),
        name="rwkv_prep_sample",
    )(z, z, z, z, prev_rkv, prev_rkv, prev_rkv, prev_l, *consts)


def _scan_chunks(inputs, states, c):
    pairs = range(len(inputs))
    n2 = _LANES
    bf = lambda x: x.astype(_BF16)
    m0, m1 = c["m0"], c["m1"]
    expand = lambda x: jnp.concatenate([x * m0, x * m1], axis=0)
    stack = lambda a, b: jnp.concatenate([expand(a), expand(b)], axis=0)

    cum = [_mm_exact_rhs_t(c["ltri"], inputs[q][1]) for q in pairs]
    tot = [cum[q][_CHUNK - 1:_CHUNK, :] for q in pairs]
    xa, xb, bk, ve = [], [], [], []
    for q in pairs:
        r, ld, k2, v, kk, be = inputs[q]
        e_neg = jnp.exp(-cum[q])
        e_rem = jnp.exp(tot[q] - cum[q])
        xa.append(bf(stack(-kk * jnp.exp(cum[q] - ld), r * jnp.exp(cum[q]))))
        xb.append(bf(stack(be * e_neg, k2 * e_neg)))
        bk.append(bf(stack(be * e_rem, k2 * e_rem)))
        ve.append(bf(expand(v)))
    g = [_dot_nt(xa[q], xb[q]) for q in pairs]
    ps = [_dot_nt(xa[q], bf(states[q])) for q in pairs]
    a_kk = [bf(jnp.concatenate([jnp.where(c["strict"], g[q][:n2, n2:], 0.0),
                                jnp.where(c["incl"], g[q][n2:, n2:], 0.0)], axis=0)) for q in pairs]
    av = [_dot(a_kk[q], ve[q]) for q in pairs]
    npow = [jnp.where(c["strict"], g[q][:n2, :n2], 0.0) for q in pairs]
    u = [ps[q][:n2] + av[q][:n2] for q in pairs]
    for k in range(c["squarings"]):
        out = [_dot(bf(npow[q]), bf(jnp.concatenate([npow[q], u[q]], axis=1))) for q in pairs]
        npow = [out[q][:, :n2] for q in pairs]
        u = [u[q] + out[q][:, n2:] for q in pairs]
    ub = [bf(u[q]) for q in pairs]
    u = [u[q] + _dot(bf(npow[q]), ub[q]) for q in pairs]
    ub = [bf(u[q]) for q in pairs]
    a_rb = [bf(jnp.where(c["incl"], g[q][n2:, :n2], 0.0)) for q in pairs]
    ye = [ps[q][n2:] + av[q][n2:] + _dot(a_rb[q], ub[q]) for q in pairs]
    ys = [ye[q][:_CHUNK] + ye[q][_CHUNK:] for q in pairs]
    new_states = [states[q] * jnp.exp(tot[q]) + _dot_tn(jnp.concatenate([ub[q], ve[q]], axis=0), bk[q])
                  for q in pairs]
    return ys, new_states


def _mm_exact_rhs_t(l_bf16, x):
    x_hi, x_lo = _split(x)
    return _dot(l_bf16, x_hi) + _dot(l_bf16, x_lo)


def _scan_consts():
    n = _LANES
    half = _CHUNK
    ri = lax.broadcasted_iota(jnp.int32, (n, n), 0)
    ci = lax.broadcasted_iota(jnp.int32, (n, n), 1)
    same = (ri // half) == (ci // half)
    lane = lax.broadcasted_iota(jnp.int32, (half, n), 1)
    tr = lax.broadcasted_iota(jnp.int32, (half, half), 0)
    tc = lax.broadcasted_iota(jnp.int32, (half, half), 1)
    squarings = 0
    while (2 << squarings) < half:
        squarings += 1
    return {
        "strict": same & ((ri % half) > (ci % half)),
        "incl": same & ((ri % half) >= (ci % half)),
        "m0": jnp.where(lane < half, 1.0, 0.0).astype(_F32),
        "m1": jnp.where(lane >= half, 1.0, 0.0).astype(_F32),
        "ltri": jnp.where(tr >= tc, 1.0, 0.0).astype(_BF16),
        "squarings": squarings,
    }


def _rwkv_scan_kernel(r_ref, ld_ref, k_ref, v_ref, kk_ref, be_ref, y_ref, s_ref, st_ref, *, tb, pp):
    t = pl.program_id(2)

    @pl.when(t == 0)
    def _():
        st_ref[...] = jnp.zeros_like(st_ref)

    c = _scan_consts()
    in_refs = (r_ref, ld_ref, k_ref, v_ref, kk_ref, be_ref)

    def body(ci, carry):
        rows = pl.ds(pl.multiple_of(ci * _CHUNK, _CHUNK), _CHUNK)
        cols = [slice(q * _LANES, (q + 1) * _LANES) for q in range(pp)]
        ys, new_states = _scan_chunks([tuple(ref[rows, cols[q]] for ref in in_refs) for q in range(pp)],
                                      [st_ref[q] for q in range(pp)], c)
        for q in range(pp):
            y_ref[rows, cols[q]] = ys[q]
            st_ref[q] = new_states[q]
        return carry

    lax.fori_loop(0, tb // _CHUNK, body, 0)

    @pl.when(t == pl.num_programs(2) - 1)
    def _():
        s_ref[0] = st_ref[...]


def _rwkv_scan(arrs, *, batch, seq, width, tb, pp):
    n_t = seq // tb
    n_p = width // _LANES
    blk = pl.BlockSpec((tb, pp * _LANES), lambda b, p, t: (b * n_t + t, p))
    kern = functools.partial(_rwkv_scan_kernel, tb=tb, pp=pp)
    return pl.pallas_call(
        kern,
        grid=(batch, n_p // pp, n_t),
        in_specs=[blk] * 6,
        out_specs=[blk, pl.BlockSpec((1, pp, _LANES, _LANES), lambda b, p, t: (b, p, 0, 0))],
        out_shape=[jax.ShapeDtypeStruct((batch * seq, width), _F32),
                   jax.ShapeDtypeStruct((batch, n_p, _LANES, _LANES), _F32)],
        scratch_shapes=[pltpu.VMEM((pp, _LANES, _LANES), _F32)],
        compiler_params=_params(("parallel", "parallel", "arbitrary")),
        name="rwkv_scan",
    )(*arrs)


def _rwkv_step_kernel(r_ref, ld_ref, k_ref, v_ref, kk_ref, be_ref, s_ref, y_ref, so_ref, *, hb, head_dim, vi):
    for h in range(hb):
        kk = kk_ref[h][None]
        dec = jnp.exp(ld_ref[h])[None]
        be = be_ref[h][None]
        k = k_ref[h][None]
        r = r_ref[h][None]

        def body(c, carry, h=h, kk=kk, dec=dec, be=be, k=k, r=r):
            rows = pl.ds(pl.multiple_of(c * vi, vi), vi)
            s = s_ref[h, rows]
            sa = -jnp.sum(s * kk, axis=1, keepdims=True)
            s_new = s * dec + sa * be + v_ref[h, rows] * k
            y_ref[h, rows] = jnp.sum(s_new * r, axis=1, keepdims=True)
            so_ref[h, rows] = s_new
            return carry

        lax.fori_loop(0, head_dim // vi, body, 0)


def _rwkv_step(arrs_t, state_t, *, hb):
    heads, hd, _, n = state_t.shape
    r, ld, k, v, kk, be = arrs_t
    per_key = lambda a: a.reshape(heads, hd, n)
    vec = pl.BlockSpec((hb, hd, n), lambda i: (i, 0, 0))
    col = pl.BlockSpec((hb, hd, 1, n), lambda i: (i, 0, 0, 0))
    mat = pl.BlockSpec((hb, hd, hd, n), lambda i: (i, 0, 0, 0))
    kern = functools.partial(_rwkv_step_kernel, hb=hb, head_dim=hd, vi=8)
    return pl.pallas_call(
        kern,
        grid=(heads // hb,),
        in_specs=[vec, vec, vec, col, vec, vec, mat],
        out_specs=[col, mat],
        out_shape=[jax.ShapeDtypeStruct((heads, hd, 1, n), _F32), jax.ShapeDtypeStruct(state_t.shape, _F32)],
        compiler_params=_params(("parallel",)),
        name="rwkv_step",
    )(per_key(r), per_key(ld), per_key(k), v.reshape(heads, hd, 1, n), per_key(kk), per_key(be), state_t)


def _rwkv_post_kernel(y_ref, r_ref, k_ref, v_ref, za_ref, zb_ref, rk_ref, lw_ref, lb_ref, o_ref, *, head_dim, delta):
    avg_bd = _head_block_matrix(1.0 / head_dim, head_dim)
    ones_bd = _head_block_matrix(1.0, head_dim)
    width = y_ref.shape[1]
    zwin = jnp.concatenate([za_ref[...], zb_ref[...]], axis=1)
    for p in range(width // _LANES):
        cols = slice(p * _LANES, (p + 1) * _LANES)
        y = y_ref[:, cols]
        d = y - _mm_exact_rhs(y, avg_bd)
        var = _mm_exact_rhs(d * d, avg_bd)
        yn = d * lax.rsqrt(var + _GN_EPS) * lw_ref[:, cols] + lb_ref[:, cols]
        bonus = _mm_exact_rhs(r_ref[:, cols] * k_ref[:, cols] * rk_ref[:, cols], ones_bd) * v_ref[:, cols]
        zg = zwin[:, delta + p * _LANES:delta + (p + 1) * _LANES]
        o_ref[:, cols] = ((yn + bonus) * _silu(zg)).astype(o_ref.dtype)


def _rwkv_post(y, r, k2, v, z, consts, *, row0, rows, width, tb, head_dim, col0):
    off = row0 // tb
    tail = 2 * _LANES
    cg, delta = col0 // width, col0 % width
    assert delta <= tail and width % tail == 0
    blk = pl.BlockSpec((tb, width), lambda i: (i, 0))
    vec = pl.BlockSpec((1, width), lambda i: (0, 0))
    kern = functools.partial(_rwkv_post_kernel, head_dim=head_dim, delta=delta)
    return pl.pallas_call(
        kern,
        grid=(rows // tb,),
        in_specs=[blk, blk, blk, blk,
                  pl.BlockSpec((tb, width), lambda i: (off + i, cg)),
                  pl.BlockSpec((tb, tail), lambda i: (off + i, (cg + 1) * (width // tail))),
                  vec, vec, vec],
        out_specs=blk,
        out_shape=jax.ShapeDtypeStruct((rows, width), _BF16),
        compiler_params=_params(("parallel",)),
        name="rwkv_post",
    )(y, r, k2, v, z, z, *consts)


def _mem_attn_prompt_kernel(za_ref, zb_ref, zc_ref, k_ref, v_ref, o_ref, *, heads, head_dim, delta):
    scale = head_dim ** -0.5
    mw = heads * head_dim
    zwin = jnp.concatenate([za_ref[...], zb_ref[...], zc_ref[...]], axis=1)
    for h in range(heads):
        cols = slice(h * head_dim, (h + 1) * head_dim)
        q = zwin[:, delta + cols.start:delta + cols.stop]
        zg = zwin[:, delta + mw + cols.start:delta + mw + cols.stop]
        s = _dot_nt(q.astype(_BF16), k_ref[:, cols].astype(_BF16)) * scale
        p = jnp.exp(s - jnp.max(s, axis=-1, keepdims=True))
        o = _dot(p.astype(_BF16), v_ref[:, cols].astype(_BF16)) / jnp.sum(p, axis=-1, keepdims=True)
        o_ref[:, cols] = (o * _silu(zg)).astype(o_ref.dtype)


def _mem_attn_prompt(z, mkv, *, batch, seq, n_mem, heads, head_dim, tq, col0):
    n_t = seq // tq
    mw = heads * head_dim
    tail = 2 * _LANES
    q_col, delta = col0 // mw, col0 % mw
    assert delta <= tail and mw % tail == 0
    kern = functools.partial(_mem_attn_prompt_kernel, heads=heads, head_dim=head_dim, delta=delta)
    return pl.pallas_call(
        kern,
        grid=(batch, n_t),
        in_specs=[
            pl.BlockSpec((tq, mw), lambda b, t: (b * n_t + t, q_col)),
            pl.BlockSpec((tq, mw), lambda b, t: (b * n_t + t, q_col + 1)),
            pl.BlockSpec((tq, tail), lambda b, t: (b * n_t + t, (q_col + 2) * (mw // tail))),
            pl.BlockSpec((n_mem, mw), lambda b, t: (b, 0)),
            pl.BlockSpec((n_mem, mw), lambda b, t: (b, 1)),
        ],
        out_specs=pl.BlockSpec((tq, mw), lambda b, t: (b * n_t + t, 0)),
        out_shape=jax.ShapeDtypeStruct((batch * seq, mw), _BF16),
        compiler_params=_params(("parallel", "parallel")),
        name="mem_attn_prompt",
    )(z, z, z, mkv, mkv)


def _mem_attn_sample_kernel(q_ref, zg_ref, k_ref, v_ref, o_ref, *, bb, head_dim):
    scale = head_dim ** -0.5
    for b in range(bb):
        s = jnp.sum(k_ref[b] * q_ref[b][None], axis=-1, keepdims=True) * scale
        p = jnp.exp(s - jnp.max(s, axis=0, keepdims=True))
        o = jnp.sum(p * v_ref[b], axis=0) / jnp.sum(p, axis=0)
        o_ref[b] = o * _silu(zg_ref[b])


def _mem_attn_sample(q, zg, mem_k, mem_v, *, bb):
    n, n_mem, heads, head_dim = mem_k.shape
    vec = pl.BlockSpec((bb, heads, head_dim), lambda i: (i, 0, 0))
    mat = pl.BlockSpec((bb, n_mem, heads, head_dim), lambda i: (i, 0, 0, 0))
    kern = functools.partial(_mem_attn_sample_kernel, bb=bb, head_dim=head_dim)
    return pl.pallas_call(
        kern,
        grid=(n // bb,),
        in_specs=[vec, vec, mat, mat],
        out_specs=vec,
        out_shape=jax.ShapeDtypeStruct((n, heads, head_dim), _F32),
        compiler_params=_params(("parallel",)),
        name="mem_attn_sample",
    )(q, zg, mem_k, mem_v)


def _branch_kernel(po_ref, ro_ref, mo_ref, wp_ref, wr_ref, wm_ref, gp_ref, gr_ref, gm_ref, tp_ref, tr_ref, tm_ref,
                   bp_ref, br_ref, bm_ref, o_ref, wpb_ref, wrb_ref, wmb_ref, *, delta):
    @pl.when(pl.program_id(1) == 0)
    def _():
        wpb_ref[...] = wp_ref[...].astype(_BF16)
        wrb_ref[...] = wr_ref[...].astype(_BF16)
        wmb_ref[...] = wm_ref[...].astype(_BF16)

    tn = o_ref.shape[1]

    def gate(g_ref, t_ref, b_ref):
        win = jnp.concatenate([g_ref[...], t_ref[...]], axis=1)
        return jax.nn.sigmoid(win[:, delta:delta + tn] + b_ref[...])

    h = gate(gp_ref, tp_ref, bp_ref) * _dot(po_ref[...], wpb_ref[...])
    h = h + gate(gr_ref, tr_ref, br_ref) * _dot(ro_ref[...], wrb_ref[...])
    h = h + gate(gm_ref, tm_ref, bm_ref) * _dot(mo_ref[...], wmb_ref[...])
    o_ref[...] = h.astype(o_ref.dtype)


def _branch(po, ro, mo, wp, wr, wm, z, b_gate, *, row0, tm, tn, gate_col0):
    rows = po.shape[0]
    d = wp.shape[1]
    off = row0 // tm
    tail = 2 * _LANES
    delta = gate_col0 % tn
    g0 = gate_col0 // tn
    assert delta <= tail and tn % tail == 0
    nb = d // tn
    act = lambda a: pl.BlockSpec((tm, a.shape[1]), lambda j, i: (i, 0))
    wgt = lambda w: pl.BlockSpec((w.shape[0], tn), lambda j, i: (0, j))
    gat = lambda k: pl.BlockSpec((tm, tn), lambda j, i, k=k: (off + i, g0 + k * nb + j))
    tai = lambda k: pl.BlockSpec((tm, tail), lambda j, i, k=k: (off + i, (g0 + k * nb + j + 1) * (tn // tail)))
    bia = lambda k: pl.BlockSpec((1, tn), lambda j, i, k=k: (0, k * nb + j))
    return pl.pallas_call(
        functools.partial(_branch_kernel, delta=delta),
        grid=(nb, rows // tm),
        in_specs=[act(po), act(ro), act(mo), wgt(wp), wgt(wr), wgt(wm), gat(0), gat(1), gat(2),
                  tai(0), tai(1), tai(2), bia(0), bia(1), bia(2)],
        out_specs=pl.BlockSpec((tm, tn), lambda j, i: (i, j)),
        out_shape=jax.ShapeDtypeStruct((rows, d), _BF16),
        scratch_shapes=[pltpu.VMEM((w.shape[0], tn), _BF16) for w in (wp, wr, wm)],
        compiler_params=_params(("parallel", "arbitrary")),
        name="branch_proj",
    )(po, ro, mo, wp, wr, wm, z, z, z, z, z, z, b_gate, b_gate, b_gate)


def _out_kernel(h_ref, w_ref, x_ref, g_ref, b_ref, o_ref, *, alpha):
    xf = alpha * x_ref[...] + _dot(h_ref[...], w_ref[...])
    mu = jnp.mean(xf, axis=-1, keepdims=True)
    d = xf - mu
    var = jnp.mean(d * d, axis=-1, keepdims=True)
    o_ref[...] = d * lax.rsqrt(var + _LN_EPS) * g_ref[...] + b_ref[...]


def _out_proj(h, w_out, x, ln_g, ln_b, *, tm, alpha):
    rows, d = x.shape
    kern = functools.partial(_out_kernel, alpha=alpha)
    return pl.pallas_call(
        kern,
        grid=(rows // tm,),
        in_specs=[
            pl.BlockSpec((tm, d), lambda i: (i, 0)),
            pl.BlockSpec((d, d), lambda i: (0, 0), pipeline_mode=pl.Buffered(1)),
            pl.BlockSpec((tm, d), lambda i: (i, 0)),
            pl.BlockSpec((1, d), lambda i: (0, 0)),
            pl.BlockSpec((1, d), lambda i: (0, 0)),
        ],
        out_specs=pl.BlockSpec((tm, d), lambda i: (i, 0)),
        out_shape=jax.ShapeDtypeStruct((rows, d), _F32),
        compiler_params=_params(("parallel",), vmem_limit=_VMEM_LIMIT_RESIDENT_WEIGHT),
        name="out_proj_ln",
    )(h, w_out, x, ln_g, ln_b)


def _tiles(batch, seq, n_sample):
    n_prompt = batch * seq
    m_all = n_prompt + n_sample
    return {
        "in_tm": next(t for t in (832, 640, 512, 256, 128, 64, 32, 16) if m_all % t == 0), "in_tn": 768,
        "row_tb": min(256, seq), "prep_tb": min(128, seq),
        "scan_tb": min(256, seq), "scan_pp": 8,
        "attn_tq": min(512, seq),
        "proj_tm": min(512, n_prompt), "proj_tn": 512, "out_tm": min(256, n_prompt),
        "pool_bb": min(32, n_sample), "step_hb": 2, "attn_bb": 2,
    }


def _layer(xp, xs, mem_p, cache_k, cache_v, st_pool, st_shift, st_rwkv, lp):
    batch, seq, d = xp.shape
    ns = xs.shape[0]
    n_p = batch * seq
    m_all = n_p + ns
    pw = lp["pool_scale"].shape[-1]
    rw = lp["rwkv_w0"].shape[-1]
    heads, hd = lp["rwkv_r_k"].shape
    lora = lp["rwkv_w2"].shape[0]
    sw = lp["rwkv_mu"].shape[-1]
    n_mem, mh, md = cache_k.shape[1:]
    mw = mh * md
    tl = _tiles(batch, seq, ns)

    lo0, lo1 = 2 * pw + 3 * rw, 2 * pw + sw
    lw = 2 * _LANES
    assert 2 * lora <= lw
    q_col0 = lo1 + rw
    gate_col0 = q_col0 + 2 * mw

    x_all = jnp.concatenate([xp.reshape(n_p, d), xs.reshape(ns, d)], axis=0).astype(_BF16)
    z = _in_proj(x_all, jnp.swapaxes(lp["w_in"], 0, 1), tm=tl["in_tm"], tn=tl["in_tn"])
    z_s = z[n_p:]

    pool_w = lp["pool_w"].astype(_BF16)
    pool_scale = lp["pool_scale"].reshape(1, pw)
    po_p = _pool_prompt(z, pool_w, pool_scale, batch=batch, seq=seq, width=pw, tb=tl["row_tb"])
    po_s = _pool_sample(z_s, jnp.swapaxes(st_pool, 0, 1), pool_w, pool_scale, row0=0, bb=tl["pool_bb"])
    nbuf = st_pool.shape[1]
    new_pool_p = jnp.stack([z[(b + 1) * seq - nbuf:(b + 1) * seq, :pw] for b in range(batch)])
    new_pool_s = jnp.concatenate([st_pool[:, 1:, :], z_s[:, :pw].reshape(ns, 1, pw)], axis=1)

    mu = lp["rwkv_mu"]
    pad_l = lambda a: jnp.pad(a, ((0, 0), (0, lw - 2 * lora)))
    w2p = jnp.zeros((lw, rw), _F32).at[:lora].set(lp["rwkv_w2"]).astype(_BF16)
    a2p = jnp.zeros((lw, rw), _F32).at[lora:2 * lora].set(lp["rwkv_a2"]).astype(_BF16)
    row = lambda a: a.reshape(1, -1)
    mix_consts = (row(mu[:rw]), row(mu[rw:2 * rw]), row(mu[2 * rw:3 * rw]), pad_l(row(mu[3 * rw:])),
                  row(lp["rwkv_w0"]), row(lp["rwkv_a0"]), row(lp["rwkv_k_k"]), row(lp["rwkv_k_a"]), w2p, a2p)
    post_consts = (row(lp["rwkv_r_k"]), row(lp["rwkv_ln_w"]), row(lp["rwkv_ln_b"]))

    r_p, ld_p, k_p, v_p, kk_p, be_p = _rwkv_prep_prompt(z, mix_consts, batch=batch, seq=seq, width=rw,
                                                        tb=tl["prep_tb"], head_dim=hd, col0=2 * pw, lw=lw)
    y_p, st_pairs = _rwkv_scan((r_p, ld_p, k_p, v_p, kk_p, be_p), batch=batch, seq=seq, width=rw,
                               tb=tl["scan_tb"], pp=tl["scan_pp"])
    ro_p = _rwkv_post(y_p, r_p, k_p, v_p, z, post_consts, row0=0, rows=n_p, width=rw, tb=tl["row_tb"],
                      head_dim=hd, col0=lo1)
    st5 = st_pairs.reshape(batch, heads // 2, 2, hd, 2, hd)
    new_state_p = jnp.stack([st5[:, :, 0, :, 0, :], st5[:, :, 1, :, 1, :]], axis=2).reshape(batch, heads, hd, hd)

    shift_s = st_shift.reshape(ns, sw)
    prep_s = _rwkv_prep_sample(z_s, shift_s[:, :3 * rw], pad_l(shift_s[:, 3 * rw:]), mix_consts, row0=0, rows=ns,
                               width=rw, head_dim=hd, col0=2 * pw, lw=lw)
    r_s, _, k_s, v_s, _, _ = prep_s[:6]
    y_t, state_t = _rwkv_step(prep_s[6:], jnp.transpose(st_rwkv, (1, 2, 3, 0)), hb=tl["step_hb"])
    new_state_s = jnp.transpose(state_t, (3, 0, 1, 2))
    ro_s = _rwkv_post(y_t.reshape(rw, ns).T, r_s, k_s, v_s, z_s, post_consts, row0=0, rows=ns, width=rw, tb=ns,
                      head_dim=hd, col0=lo1)
    new_shift_p = jnp.stack([z[(b + 1) * seq - 1:(b + 1) * seq, 2 * pw:lo1] for b in range(batch)])
    new_shift_s = z_s[:, 2 * pw:lo1].reshape(ns, 1, sw)

    mkv = _matmul(mem_p.reshape(batch * n_mem, d).astype(_BF16), lp["w_mem_kv"].astype(_BF16),
                  tm=min(512, batch * n_mem), tn=min(1024, 2 * mw), out_dtype=_F32, name="mem_kv_proj")
    mo_p = _mem_attn_prompt(z, mkv, batch=batch, seq=seq, n_mem=n_mem, heads=mh, head_dim=md, tq=tl["attn_tq"],
                            col0=q_col0)
    mo_s = _mem_attn_sample(z_s[:, q_col0:q_col0 + mw].reshape(ns, mh, md),
                            z_s[:, q_col0 + mw:q_col0 + 2 * mw].reshape(ns, mh, md),
                            cache_k, cache_v, bb=tl["attn_bb"]).reshape(ns, mw).astype(_BF16)
    mk_p = mkv[:, :mw].reshape(batch, n_mem, mh, md)
    mv_p = mkv[:, mw:].reshape(batch, n_mem, mh, md)

    wp, wr, wm = lp["w_branch_pool"], lp["w_branch_rwkv"], lp["w_branch_mem"]
    w_out = lp["w_out"].astype(_BF16)
    b_gate = row(lp["b_gate"])
    ln_g, ln_b = row(lp["ln_g"]), row(lp["ln_b"])
    alpha = lp["alpha"]
    outs = []
    for po, ro, mo, x2d, zz in ((po_p, ro_p, mo_p, xp.reshape(n_p, d), z), (po_s, ro_s, mo_s, xs.reshape(ns, d), z_s)):
        rows = x2d.shape[0]
        h = _branch(po, ro, mo, wp, wr, wm, zz, b_gate, row0=0, tm=min(tl["proj_tm"], rows), tn=tl["proj_tn"],
                    gate_col0=gate_col0)
        outs.append(_out_proj(h, w_out, x2d, ln_g, ln_b, tm=min(tl["out_tm"], rows), alpha=alpha))
    y_prompt = outs[0].reshape(batch, seq, d)
    y_sample = outs[1].reshape(ns, 1, d)
    return (y_prompt, y_sample, mk_p, mv_p, new_pool_p, new_shift_p, new_state_p, new_pool_s, new_shift_s,
            new_state_s)


def kernel(x_prompt, x_sample, cache_mem_k, cache_mem_v, state_pool, state_shift, state_rwkv, mem_prompt, w_in,
           b_gate, pool_w, pool_scale, rwkv_mu, rwkv_w0, rwkv_w2, rwkv_a0, rwkv_a2, rwkv_k_k, rwkv_k_a, rwkv_r_k,
           rwkv_ln_w, rwkv_ln_b, w_mem_kv, w_branch_pool, w_branch_rwkv, w_branch_mem, w_out, ln_g, ln_b):
    depth = w_in.shape[0]
    alpha = (2.0 * depth) ** 0.25
    weights = dict(w_in=w_in, b_gate=b_gate, pool_w=pool_w, pool_scale=pool_scale, rwkv_mu=rwkv_mu, rwkv_w0=rwkv_w0,
                   rwkv_w2=rwkv_w2, rwkv_a0=rwkv_a0, rwkv_a2=rwkv_a2, rwkv_k_k=rwkv_k_k, rwkv_k_a=rwkv_k_a,
                   rwkv_r_k=rwkv_r_k, rwkv_ln_w=rwkv_ln_w, rwkv_ln_b=rwkv_ln_b, w_mem_kv=w_mem_kv,
                   w_branch_pool=w_branch_pool, w_branch_rwkv=w_branch_rwkv, w_branch_mem=w_branch_mem,
                   w_out=w_out, ln_g=ln_g, ln_b=ln_b)
    yp, ys = x_prompt, x_sample
    per_layer = []
    for l in range(depth):
        lp = {k: v[l] for k, v in weights.items()}
        lp["alpha"] = alpha
        res = _layer(yp, ys, mem_prompt, cache_mem_k[l], cache_mem_v[l], state_pool[l], state_shift[l],
                     state_rwkv[l], lp)
        yp, ys = res[0], res[1]
        per_layer.append(res[2:])
    if depth == 1:
        stacked = tuple(a[None] for a in per_layer[0])
    else:
        stacked = tuple(jnp.stack([res[i] for res in per_layer]) for i in range(len(per_layer[0])))
    return (yp, ys) + stacked
```

```python
import functools

import jax
import jax.numpy as jnp
from jax import lax
from jax.experimental import pallas as pl
from jax.experimental.pallas import tpu as pltpu

_F32 = jnp.float32
_BF16 = jnp.bfloat16

_POOL_WINDOWS = (2, 4, 8, 16)
_PAST_LEN = 16384
_GN_EPS = 64e-5
_LN_EPS = 1e-5
_L2_EPS = 1e-12
_EXP_MINUS_HALF = 0.6065306597126334

_LANES = 128
_MIB = 1024 * 1024
_VMEM_LIMIT = 56 * _MIB
_VMEM_LIMIT_RESIDENT_WEIGHT = 62 * _MIB

_HEAD_PAIR = _LANES
_CHUNK = 64


def _params(sem, vmem_limit=_VMEM_LIMIT):
    return pltpu.CompilerParams(dimension_semantics=sem, vmem_limit_bytes=vmem_limit)


def _dot(a, b):
    return jnp.dot(a, b, preferred_element_type=_F32)


def _dot_nt(a, b):
    return lax.dot_general(a, b, (((1,), (1,)), ((), ())), preferred_element_type=_F32)


def _dot_tn(a, b):
    return lax.dot_general(a, b, (((0,), (0,)), ((), ())), preferred_element_type=_F32)


def _split(x):
    hi = x.astype(_BF16)
    lo = (x - hi.astype(_F32)).astype(_BF16)
    return hi, lo


def _mm1(dot, a, b):
    return dot(a.astype(_BF16), b.astype(_BF16))


def _mm_exact_rhs(a, b_bf16):
    a_hi, a_lo = _split(a)
    return _dot(a_hi, b_bf16) + _dot(a_lo, b_bf16)


def _silu(x):
    return x * jax.nn.sigmoid(x)


def _head_block_matrix(value, head_dim):
    r = lax.broadcasted_iota(jnp.int32, (_LANES, _LANES), 0) // head_dim
    c = lax.broadcasted_iota(jnp.int32, (_LANES, _LANES), 1) // head_dim
    return jnp.where(r == c, value, 0.0).astype(_BF16)


def _matmul_kernel(x_ref, w_ref, o_ref):
    o_ref[...] = _dot(x_ref[...], w_ref[...]).astype(o_ref.dtype)


def _matmul(x, w, *, tm, tn, out_dtype, name):
    m, k = x.shape
    n = w.shape[1]
    return pl.pallas_call(
        _matmul_kernel,
        grid=(m // tm, n // tn),
        in_specs=[pl.BlockSpec((tm, k), lambda i, j: (i, 0)), pl.BlockSpec((k, tn), lambda i, j: (0, j))],
        out_specs=pl.BlockSpec((tm, tn), lambda i, j: (i, j)),
        out_shape=jax.ShapeDtypeStruct((m, n), out_dtype),
        compiler_params=_params(("parallel", "parallel")),
        name=name,
    )(x, w)


def _in_proj_kernel(x_ref, wt_ref, o_ref, wb_ref):
    @pl.when(pl.program_id(1) == 0)
    def _():
        wb_ref[...] = wt_ref[...].astype(_BF16)

    o_ref[...] = _dot_nt(x_ref[...], wb_ref[...])


def _in_proj(x, wt, *, tm, tn):
    m, k = x.shape
    n = wt.shape[0]
    return pl.pallas_call(
        _in_proj_kernel,
        grid=(pl.cdiv(n, tn), m // tm),
        in_specs=[pl.BlockSpec((tm, k), lambda j, i: (i, 0)), pl.BlockSpec((tn, k), lambda j, i: (j, 0))],
        out_specs=pl.BlockSpec((tm, tn), lambda j, i: (i, j)),
        out_shape=jax.ShapeDtypeStruct((m, n), _F32),
        scratch_shapes=[pltpu.VMEM((tn, k), _BF16)],
        compiler_params=_params(("parallel", "arbitrary")),
        name="in_proj",
    )(x, wt)


def _pool_mix(pooled_fn, zp_ref, pw_ref, ps_ref, o_ref, group):
    for g in range(len(_POOL_WINDOWS)):
        cols = slice(g * group, (g + 1) * group)
        mixed = _dot(pooled_fn(g, cols).astype(_BF16), pw_ref[g])
        o_ref[:, cols] = (mixed * ps_ref[:, cols] * _silu(zp_ref[:, cols])).astype(o_ref.dtype)


def _pool_prompt_kernel(u_ref, halo_ref, zp_ref, pw_ref, ps_ref, o_ref, e_ref, *, tb, hist, group):
    t = pl.program_id(1)
    e_ref[0:hist, :] = jnp.where(t == 0, 0.0, halo_ref[...])
    e_ref[hist:hist + tb, :] = u_ref[...]
    pos = t * tb + lax.broadcasted_iota(jnp.int32, (tb, group), 0)

    def pooled(g, cols):
        win = _POOL_WINDOWS[g]
        x = e_ref[hist:hist + tb, cols]
        acc = x
        for d in range(1, win):
            acc = acc + e_ref[hist - d:hist - d + tb, cols]
        cnt = jnp.minimum(pos + 1, win).astype(_F32)
        return acc / cnt - x

    _pool_mix(pooled, zp_ref, pw_ref, ps_ref, o_ref, group)


def _pool_prompt(z, pool_w, pool_scale, *, batch, seq, width, tb):
    hist = 16
    n_t = seq // tb
    group = width // len(_POOL_WINDOWS)
    kern = functools.partial(_pool_prompt_kernel, tb=tb, hist=hist, group=group)
    return pl.pallas_call(
        kern,
        grid=(batch, n_t),
        in_specs=[
            pl.BlockSpec((tb, width), lambda b, t: (b * n_t + t, 0)),
            pl.BlockSpec((hist, width), lambda b, t: (jnp.maximum((b * seq + t * tb) // hist - 1, 0), 0)),
            pl.BlockSpec((tb, width), lambda b, t: (b * n_t + t, 1)),
            pl.BlockSpec(pool_w.shape, lambda b, t: (0, 0, 0)),
            pl.BlockSpec((1, width), lambda b, t: (0, 0)),
        ],
        out_specs=pl.BlockSpec((tb, width), lambda b, t: (b * n_t + t, 0)),
        out_shape=jax.ShapeDtypeStruct((batch * seq, width), _BF16),
        scratch_shapes=[pltpu.VMEM((hist + tb, width), _F32)],
        compiler_params=_params(("parallel", "parallel")),
        name="pool_prompt",
    )(z, z, z, pool_w, pool_scale)


def _pool_sample_kernel(u_ref, buf_ref, zp_ref, pw_ref, ps_ref, o_ref, *, nbuf, group):
    def pooled(g, cols):
        win = _POOL_WINDOWS[g]
        x = u_ref[:, cols]
        acc = x
        for d in range(1, win):
            acc = acc + buf_ref[nbuf - d, :, cols]
        cnt = float(min(_PAST_LEN + 1, win))
        return acc / cnt - x

    _pool_mix(pooled, zp_ref, pw_ref, ps_ref, o_ref, group)


def _pool_sample(z, buf, pool_w, pool_scale, *, row0, bb):
    nbuf, rows, width = buf.shape
    group = width // len(_POOL_WINDOWS)
    off = row0 // bb
    kern = functools.partial(_pool_sample_kernel, nbuf=nbuf, group=group)
    return pl.pallas_call(
        kern,
        grid=(rows // bb,),
        in_specs=[
            pl.BlockSpec((bb, width), lambda i: (off + i, 0)),
            pl.BlockSpec((nbuf, bb, width), lambda i: (0, i, 0)),
            pl.BlockSpec((bb, width), lambda i: (off + i, 1)),
            pl.BlockSpec(pool_w.shape, lambda i: (0, 0, 0)),
            pl.BlockSpec((1, width), lambda i: (0, 0)),
        ],
        out_specs=pl.BlockSpec((bb, width), lambda i: (i, 0)),
        out_shape=jax.ShapeDtypeStruct((rows, width), _BF16),
        compiler_params=_params(("parallel",)),
        name="pool_sample",
    )(z, buf, z, pool_w, pool_scale)


def _rwkv_mix(cur, prev, mu, w0, a0, kk_scale, ka, w2p, a2p, head_dim, outs):
    o_r, o_ld, o_k, o_v, o_kk, o_be = outs
    mix = lambda n: cur[n] + (prev[n] - cur[n]) * mu[n]
    xl = mix("l")
    wl = w0 + _mm1(_dot, jnp.tanh(xl), w2p)
    a = jax.nn.sigmoid(a0 + _mm1(_dot, xl, a2p))
    xk = mix("k")
    kkp = xk * kk_scale
    ones_bd = _head_block_matrix(1.0, head_dim)
    width = kkp.shape[1]
    o_r[...] = mix("r")
    o_v[...] = mix("v")
    o_ld[...] = -_EXP_MINUS_HALF * jax.nn.sigmoid(wl)
    o_k[...] = xk * (1.0 + (a - 1.0) * ka)
    for p in range(width // _LANES):
        cols = slice(p * _LANES, (p + 1) * _LANES)
        kp = kkp[:, cols]
        ss = _mm_exact_rhs(kp * kp, ones_bd)
        kk = kp * lax.rsqrt(jnp.maximum(ss, _L2_EPS * _L2_EPS))
        o_kk[:, cols] = kk
        o_be[:, cols] = kk * a[:, cols]


def _rwkv_prep_sample_kernel(zr, zk, zv, zl, pr, pk, pv, plr, mur, muk, muv, mul, w0, a0, kks, ka, w2p, a2p,
                             o_r, o_ld, o_k, o_v, o_kk, o_be, t_r, t_ld, t_k, t_v, t_kk, t_be, *, head_dim):
    cur = {"r": zr[...], "k": zk[...], "v": zv[...], "l": zl[...]}
    prev = {"r": pr[...], "k": pk[...], "v": pv[...], "l": plr[...]}
    mu = {"r": mur[...], "k": muk[...], "v": muv[...], "l": mul[...]}
    _rwkv_mix(cur, prev, mu, w0[...], a0[...], kks[...], ka[...], w2p[...], a2p[...], head_dim,
              (o_r, o_ld, o_k, o_v, o_kk, o_be))
    for o_ref, t_ref in ((o_r, t_r), (o_ld, t_ld), (o_k, t_k), (o_v, t_v), (o_kk, t_kk), (o_be, t_be)):
        t_ref[...] = o_ref[...].T


def _rwkv_prep_sample(z, prev_rkv, prev_l, consts, *, row0, rows, width, head_dim, col0, lw):
    c0 = col0 // width
    lcol = (col0 + 3 * width) // lw
    assert c0 * width == col0 and lcol * lw == col0 + 3 * width
    off = row0 // rows
    cblk = lambda c: pl.BlockSpec((rows, width), lambda i, c=c: (off, c))
    pblk = lambda c: pl.BlockSpec((rows, width), lambda i, c=c: (0, c))
    vec = lambda n: pl.BlockSpec((1, n), lambda i: (0, 0))
    mat = pl.BlockSpec((lw, width), lambda i: (0, 0))
    out_spec = pl.BlockSpec((rows, width), lambda i: (0, 0))
    out_shape = jax.ShapeDtypeStruct((rows, width), _F32)
    kern = functools.partial(_rwkv_prep_sample_kernel, head_dim=head_dim)
    return pl.pallas_call(
        kern,
        grid=(1,),
        in_specs=[cblk(c0), cblk(c0 + 1), cblk(c0 + 2), pl.BlockSpec((rows, lw), lambda i: (off, lcol)),
                  pblk(0), pblk(1), pblk(2), pl.BlockSpec((rows, lw), lambda i: (0, 0)),
                  vec(width), vec(width), vec(width), vec(lw),
                  vec(width), vec(width), vec(width), vec(width), mat, mat],
        out_specs=[out_spec] * 6 + [pl.BlockSpec((width, rows), lambda i: (0, 0))] * 6,
        out_shape=[out_shape] * 6 + [jax.ShapeDtypeStruct((width, rows), _F32)] * 6,
        compiler_params=_params(("arbitrary",)),
        name="rwkv_prep_sample",
    )(z, z, z, z, prev_rkv, prev_rkv, prev_rkv, prev_l, *consts)


def _scan_chunks(inputs, states, c):
    pairs = range(len(inputs))
    n2 = _LANES
    bf = lambda x: x.astype(_BF16)
    m0, m1 = c["m0"], c["m1"]
    expand = lambda x: jnp.concatenate([x * m0, x * m1], axis=0)
    stack = lambda a, b: jnp.concatenate([expand(a), expand(b)], axis=0)

    cum = [_mm_exact_rhs_t(c["ltri"], inputs[q][1]) for q in pairs]
    tot = [cum[q][_CHUNK - 1:_CHUNK, :] for q in pairs]
    xa, xb, bk, ve = [], [], [], []
    for q in pairs:
        r, ld, k2, v, kk, be = inputs[q]
        e_neg = jnp.exp(-cum[q])
        e_rem = jnp.exp(tot[q] - cum[q])
        xa.append(bf(stack(-kk * jnp.exp(cum[q] - ld), r * jnp.exp(cum[q]))))
        xb.append(bf(stack(be * e_neg, k2 * e_neg)))
        bk.append(bf(stack(be * e_rem, k2 * e_rem)))
        ve.append(bf(expand(v)))
    g = [_dot_nt(xa[q], xb[q]) for q in pairs]
    ps = [_dot_nt(xa[q], bf(states[q])) for q in pairs]
    a_kk = [bf(jnp.concatenate([jnp.where(c["strict"], g[q][:n2, n2:], 0.0),
                                jnp.where(c["incl"], g[q][n2:, n2:], 0.0)], axis=0)) for q in pairs]
    av = [_dot(a_kk[q], ve[q]) for q in pairs]
    npow = [jnp.where(c["strict"], g[q][:n2, :n2], 0.0) for q in pairs]
    u = [ps[q][:n2] + av[q][:n2] for q in pairs]
    for k in range(c["squarings"]):
        out = [_dot(bf(npow[q]), bf(jnp.concatenate([npow[q], u[q]], axis=1))) for q in pairs]
        npow = [out[q][:, :n2] for q in pairs]
        u = [u[q] + out[q][:, n2:] for q in pairs]
    ub = [bf(u[q]) for q in pairs]
    u = [u[q] + _dot(bf(npow[q]), ub[q]) for q in pairs]
    ub = [bf(u[q]) for q in pairs]
    a_rb = [bf(jnp.where(c["incl"], g[q][n2:, :n2], 0.0)) for q in pairs]
    ye = [ps[q][n2:] + av[q][n2:] + _dot(a_rb[q], ub[q]) for q in pairs]
    ys = [ye[q][:_CHUNK] + ye[q][_CHUNK:] for q in pairs]
    new_states = [states[q] * jnp.exp(tot[q]) + _dot_tn(jnp.concatenate([ub[q], ve[q]], axis=0), bk[q])
                  for q in pairs]
    return ys, new_states


def _mm_exact_rhs_t(l_bf16, x):
    x_hi, x_lo = _split(x)
    return _dot(l_bf16, x_hi) + _dot(l_bf16, x_lo)


def _scan_consts():
    n = _LANES
    half = _CHUNK
    ri = lax.broadcasted_iota(jnp.int32, (n, n), 0)
    ci = lax.broadcasted_iota(jnp.int32, (n, n), 1)
    same = (ri // half) == (ci // half)
    lane = lax.broadcasted_iota(jnp.int32, (half, n), 1)
    tr = lax.broadcasted_iota(jnp.int32, (half, half), 0)
    tc = lax.broadcasted_iota(jnp.int32, (half, half), 1)
    squarings = 0
    while (2 << squarings) < half:
        squarings += 1
    return {
        "strict": same & ((ri % half) > (ci % half)),
        "incl": same & ((ri % half) >= (ci % half)),
        "m0": jnp.where(lane < half, 1.0, 0.0).astype(_F32),
        "m1": jnp.where(lane >= half, 1.0, 0.0).astype(_F32),
        "ltri": jnp.where(tr >= tc, 1.0, 0.0).astype(_BF16),
        "squarings": squarings,
    }


def _rwkv_prompt_kernel(zr, zk, zv, zl, hr, hk, hv, hl, za, zb, mur, muk, muv, mul, w0, a0, kks, ka, w2p, a2p,
                        rk, lnw, lnb, o_ref, s_ref, er, ek, ev, el, r_s, ld_s, k_s, v_s, kk_s, be_s, y_s, xl_s, st_ref,
                        *, tb, pp, head_dim, delta):
    t = pl.program_id(2)
    halo = 8
    n_chunks = tb // _CHUNK
    pairs = range(pp)
    cols = [slice(q * _LANES, (q + 1) * _LANES) for q in pairs]

    @pl.when(t == 0)
    def _():
        st_ref[...] = jnp.zeros_like(st_ref)

    for z_ref, h_ref, e_ref, mu_ref, dst in ((zr, hr, er, mur, r_s), (zk, hk, ek, muk, k_s), (zv, hv, ev, muv, v_s),
                                             (zl, hl, el, mul, xl_s)):
        e_ref[0:halo, :] = jnp.where(t == 0, 0.0, h_ref[...])
        e_ref[halo:halo + tb, :] = z_ref[...]
        cur = z_ref[...]
        dst[...] = cur + (e_ref[halo - 1:halo - 1 + tb, :] - cur) * mu_ref[...]

    c = _scan_consts()
    ones_bd = _head_block_matrix(1.0, head_dim)
    avg_bd = _head_block_matrix(1.0 / head_dim, head_dim)

    xl = xl_s[...]
    wl = w0[...] + _mm1(_dot, jnp.tanh(xl), w2p[...])
    a = jax.nn.sigmoid(a0[...] + _mm1(_dot, xl, a2p[...]))
    ld_s[...] = -_EXP_MINUS_HALF * jax.nn.sigmoid(wl)
    xk = k_s[...]
    k_s[...] = xk * (1.0 + (a - 1.0) * ka[...])
    kp = [xk[:, cols[q]] * kks[:, cols[q]] for q in pairs]
    ss = [_mm_exact_rhs(kp[q] * kp[q], ones_bd) for q in pairs]
    for q in pairs:
        kk = kp[q] * lax.rsqrt(jnp.maximum(ss[q], _L2_EPS * _L2_EPS))
        kk_s[:, cols[q]] = kk
        be_s[:, cols[q]] = kk * a[:, cols[q]]

    def body(ci, carry):
        rows = pl.ds(pl.multiple_of(ci * _CHUNK, _CHUNK), _CHUNK)
        in_refs = (r_s, ld_s, k_s, v_s, kk_s, be_s)
        ys, new_states = _scan_chunks([tuple(ref[rows, cols[q]] for ref in in_refs) for q in pairs],
                                      [st_ref[q] for q in pairs], c)
        for q in pairs:
            y_s[rows, cols[q]] = ys[q]
            st_ref[q] = new_states[q]
        return carry

    lax.fori_loop(0, n_chunks, body, 0)

    q0, off = delta // _LANES, delta % _LANES
    n_a = za.shape[1] // _LANES
    zblk = lambda i: (za[:, i * _LANES:(i + 1) * _LANES] if i < n_a
                      else zb[:, (i - n_a) * _LANES:(i - n_a + 1) * _LANES])
    y = [y_s[:, cols[q]] for q in pairs]
    d = [y[q] - _mm_exact_rhs(y[q], avg_bd) for q in pairs]
    var = [_mm_exact_rhs(d[q] * d[q], avg_bd) for q in pairs]
    bsum = [_mm_exact_rhs(r_s[:, cols[q]] * k_s[:, cols[q]] * rk[:, cols[q]], ones_bd) for q in pairs]
    for q in pairs:
        yn = d[q] * lax.rsqrt(var[q] + _GN_EPS) * lnw[:, cols[q]] + lnb[:, cols[q]]
        zg = jnp.concatenate([zblk(q0 + q), zblk(q0 + q + 1)], axis=1)[:, off:off + _LANES]
        o_ref[:, cols[q]] = ((yn + bsum[q] * v_s[:, cols[q]]) * _silu(zg)).astype(o_ref.dtype)

    @pl.when(t == pl.num_programs(2) - 1)
    def _():
        s_ref[0] = st_ref[...]


def _rwkv_prompt(z, mix_consts, post_consts, *, batch, seq, width, tb, pp, head_dim, col0, gate_col0, lw):
    n_t = seq // tb
    gw = pp * _LANES
    halo = 8
    tail = 2 * _LANES
    c0, lcol, cg = col0 // gw, (col0 + 3 * width) // lw, gate_col0 // gw
    delta = gate_col0 % gw
    assert c0 * gw == col0 and width % gw == 0 and lcol * lw == col0 + 3 * width
    assert delta <= tail and delta % _LANES + _LANES <= tail and gw % tail == 0
    row = lambda b, g, t: b * n_t + t
    hrow = lambda b, g, t: jnp.maximum((b * seq + t * tb) // halo - 1, 0)
    step = width // gw
    zblk = lambda i: pl.BlockSpec((tb, gw), lambda b, g, t, i=i: (row(b, g, t), c0 + i * step + g))
    hblk = lambda i: pl.BlockSpec((halo, gw), lambda b, g, t, i=i: (hrow(b, g, t), c0 + i * step + g))
    vec = pl.BlockSpec((1, gw), lambda b, g, t: (0, g))
    lvec = pl.BlockSpec((1, lw), lambda b, g, t: (0, 0))
    mat = pl.BlockSpec((lw, gw), lambda b, g, t: (0, g))
    mur, muk, muv, mul, w0, a0, kks, ka, w2p, a2p = mix_consts
    big = lambda: pltpu.VMEM((tb, gw), _F32)
    kern = functools.partial(_rwkv_prompt_kernel, tb=tb, pp=pp, head_dim=head_dim, delta=delta)
    return pl.pallas_call(
        kern,
        grid=(batch, width // gw, n_t),
        in_specs=[zblk(0), zblk(1), zblk(2), pl.BlockSpec((tb, lw), lambda b, g, t: (row(b, g, t), lcol)),
                  hblk(0), hblk(1), hblk(2), pl.BlockSpec((halo, lw), lambda b, g, t: (hrow(b, g, t), lcol)),
                  pl.BlockSpec((tb, gw), lambda b, g, t: (row(b, g, t), cg + g)),
                  pl.BlockSpec((tb, tail), lambda b, g, t: (row(b, g, t), (cg + g + 1) * (gw // tail))),
                  vec, vec, vec, lvec, vec, vec, vec, vec, mat, mat, vec, vec, vec],
        out_specs=[pl.BlockSpec((tb, gw), lambda b, g, t: (row(b, g, t), g)),
                   pl.BlockSpec((1, pp, _LANES, _LANES), lambda b, g, t: (b, g, 0, 0))],
        out_shape=[jax.ShapeDtypeStruct((batch * seq, width), _BF16),
                   jax.ShapeDtypeStruct((batch, width // _LANES, _LANES, _LANES), _F32)],
        scratch_shapes=[pltpu.VMEM((halo + tb, gw), _F32)] * 3 + [pltpu.VMEM((halo + tb, lw), _F32)]
                       + [big() for _ in range(7)] + [pltpu.VMEM((tb, lw), _F32),
                                                      pltpu.VMEM((pp, _LANES, _LANES), _F32)],
        compiler_params=_params(("parallel", "parallel", "arbitrary")),
        name="rwkv_prompt",
    )(z, z, z, z, z, z, z, z, z, z, mur, muk, muv, mul, w0, a0, kks, ka, w2p, a2p, *post_consts)


def _rwkv_step_kernel(r_ref, ld_ref, k_ref, v_ref, kk_ref, be_ref, s_ref, y_ref, so_ref, *, hb, head_dim, vi):
    for h in range(hb):
        kk = kk_ref[h][None]
        dec = jnp.exp(ld_ref[h])[None]
        be = be_ref[h][None]
        k = k_ref[h][None]
        r = r_ref[h][None]

        def body(c, carry, h=h, kk=kk, dec=dec, be=be, k=k, r=r):
            rows = pl.ds(pl.multiple_of(c * vi, vi), vi)
            s = s_ref[h, rows]
            sa = -jnp.sum(s * kk, axis=1, keepdims=True)
            s_new = s * dec + sa * be + v_ref[h, rows] * k
            y_ref[h, rows] = jnp.sum(s_new * r, axis=1, keepdims=True)
            so_ref[h, rows] = s_new
            return carry

        lax.fori_loop(0, head_dim // vi, body, 0)


def _rwkv_step(arrs_t, state_t, *, hb):
    heads, hd, _, n = state_t.shape
    r, ld, k, v, kk, be = arrs_t
    per_key = lambda a: a.reshape(heads, hd, n)
    vec = pl.BlockSpec((hb, hd, n), lambda i: (i, 0, 0))
    col = pl.BlockSpec((hb, hd, 1, n), lambda i: (i, 0, 0, 0))
    mat = pl.BlockSpec((hb, hd, hd, n), lambda i: (i, 0, 0, 0))
    kern = functools.partial(_rwkv_step_kernel, hb=hb, head_dim=hd, vi=8)
    return pl.pallas_call(
        kern,
        grid=(heads // hb,),
        in_specs=[vec, vec, vec, col, vec, vec, mat],
        out_specs=[col, mat],
        out_shape=[jax.ShapeDtypeStruct((heads, hd, 1, n), _F32), jax.ShapeDtypeStruct(state_t.shape, _F32)],
        compiler_params=_params(("parallel",)),
        name="rwkv_step",
    )(per_key(r), per_key(ld), per_key(k), v.reshape(heads, hd, 1, n), per_key(kk), per_key(be), state_t)


def _rwkv_post_kernel(y_ref, r_ref, k_ref, v_ref, za_ref, zb_ref, rk_ref, lw_ref, lb_ref, o_ref, *, head_dim, delta):
    avg_bd = _head_block_matrix(1.0 / head_dim, head_dim)
    ones_bd = _head_block_matrix(1.0, head_dim)
    width = y_ref.shape[1]
    zwin = jnp.concatenate([za_ref[...], zb_ref[...]], axis=1)
    for p in range(width // _LANES):
        cols = slice(p * _LANES, (p + 1) * _LANES)
        y = y_ref[:, cols]
        d = y - _mm_exact_rhs(y, avg_bd)
        var = _mm_exact_rhs(d * d, avg_bd)
        yn = d * lax.rsqrt(var + _GN_EPS) * lw_ref[:, cols] + lb_ref[:, cols]
        bonus = _mm_exact_rhs(r_ref[:, cols] * k_ref[:, cols] * rk_ref[:, cols], ones_bd) * v_ref[:, cols]
        zg = zwin[:, delta + p * _LANES:delta + (p + 1) * _LANES]
        o_ref[:, cols] = ((yn + bonus) * _silu(zg)).astype(o_ref.dtype)


def _rwkv_post(y, r, k2, v, z, consts, *, row0, rows, width, tb, head_dim, col0):
    off = row0 // tb
    tail = 2 * _LANES
    cg, delta = col0 // width, col0 % width
    assert delta <= tail and width % tail == 0
    blk = pl.BlockSpec((tb, width), lambda i: (i, 0))
    vec = pl.BlockSpec((1, width), lambda i: (0, 0))
    kern = functools.partial(_rwkv_post_kernel, head_dim=head_dim, delta=delta)
    return pl.pallas_call(
        kern,
        grid=(rows // tb,),
        in_specs=[blk, blk, blk, blk,
                  pl.BlockSpec((tb, width), lambda i: (off + i, cg)),
                  pl.BlockSpec((tb, tail), lambda i: (off + i, (cg + 1) * (width // tail))),
                  vec, vec, vec],
        out_specs=blk,
        out_shape=jax.ShapeDtypeStruct((rows, width), _BF16),
        compiler_params=_params(("parallel",)),
        name="rwkv_post",
    )(y, r, k2, v, z, z, *consts)


def _mem_attn_prompt_kernel(za_ref, zb_ref, zc_ref, k_ref, v_ref, o_ref, *, heads, head_dim, delta):
    scale = head_dim ** -0.5
    mw = heads * head_dim
    zwin = jnp.concatenate([za_ref[...], zb_ref[...], zc_ref[...]], axis=1)
    for h in range(heads):
        cols = slice(h * head_dim, (h + 1) * head_dim)
        q = zwin[:, delta + cols.start:delta + cols.stop]
        zg = zwin[:, delta + mw + cols.start:delta + mw + cols.stop]
        s = _dot_nt(q.astype(_BF16), k_ref[:, cols].astype(_BF16)) * scale
        p = jnp.exp(s - jnp.max(s, axis=-1, keepdims=True))
        o = _dot(p.astype(_BF16), v_ref[:, cols].astype(_BF16)) / jnp.sum(p, axis=-1, keepdims=True)
        o_ref[:, cols] = (o * _silu(zg)).astype(o_ref.dtype)


def _mem_attn_prompt(z, mkv, *, batch, seq, n_mem, heads, head_dim, tq, col0):
    n_t = seq // tq
    mw = heads * head_dim
    tail = 2 * _LANES
    q_col, delta = col0 // mw, col0 % mw
    assert delta <= tail and mw % tail == 0
    kern = functools.partial(_mem_attn_prompt_kernel, heads=heads, head_dim=head_dim, delta=delta)
    return pl.pallas_call(
        kern,
        grid=(batch, n_t),
        in_specs=[
            pl.BlockSpec((tq, mw), lambda b, t: (b * n_t + t, q_col)),
            pl.BlockSpec((tq, mw), lambda b, t: (b * n_t + t, q_col + 1)),
            pl.BlockSpec((tq, tail), lambda b, t: (b * n_t + t, (q_col + 2) * (mw // tail))),
            pl.BlockSpec((n_mem, mw), lambda b, t: (b, 0)),
            pl.BlockSpec((n_mem, mw), lambda b, t: (b, 1)),
        ],
        out_specs=pl.BlockSpec((tq, mw), lambda b, t: (b * n_t + t, 0)),
        out_shape=jax.ShapeDtypeStruct((batch * seq, mw), _BF16),
        compiler_params=_params(("parallel", "parallel")),
        name="mem_attn_prompt",
    )(z, z, z, mkv, mkv)


def _mem_attn_sample_kernel(q_ref, zg_ref, k_ref, v_ref, o_ref, *, bb, head_dim):
    scale = head_dim ** -0.5
    for b in range(bb):
        s = jnp.sum(k_ref[b] * q_ref[b][None], axis=-1, keepdims=True) * scale
        p = jnp.exp(s - jnp.max(s, axis=0, keepdims=True))
        o = jnp.sum(p * v_ref[b], axis=0) / jnp.sum(p, axis=0)
        o_ref[b] = o * _silu(zg_ref[b])


def _mem_attn_sample(q, zg, mem_k, mem_v, *, bb):
    n, n_mem, heads, head_dim = mem_k.shape
    vec = pl.BlockSpec((bb, heads, head_dim), lambda i: (i, 0, 0))
    mat = pl.BlockSpec((bb, n_mem, heads, head_dim), lambda i: (i, 0, 0, 0))
    kern = functools.partial(_mem_attn_sample_kernel, bb=bb, head_dim=head_dim)
    return pl.pallas_call(
        kern,
        grid=(n // bb,),
        in_specs=[vec, vec, mat, mat],
        out_specs=vec,
        out_shape=jax.ShapeDtypeStruct((n, heads, head_dim), _F32),
        compiler_params=_params(("parallel",)),
        name="mem_attn_sample",
    )(q, zg, mem_k, mem_v)


def _branch_kernel(po_ref, ro_ref, mo_ref, wp_ref, wr_ref, wm_ref, gp_ref, gr_ref, gm_ref, tp_ref, tr_ref, tm_ref,
                   bp_ref, br_ref, bm_ref, o_ref, wpb_ref, wrb_ref, wmb_ref, *, delta):
    @pl.when(pl.program_id(1) == 0)
    def _():
        wpb_ref[...] = wp_ref[...].astype(_BF16)
        wrb_ref[...] = wr_ref[...].astype(_BF16)
        wmb_ref[...] = wm_ref[...].astype(_BF16)

    tn = o_ref.shape[1]

    def gate(g_ref, t_ref, b_ref):
        win = jnp.concatenate([g_ref[...], t_ref[...]], axis=1)
        return jax.nn.sigmoid(win[:, delta:delta + tn] + b_ref[...])

    h = gate(gp_ref, tp_ref, bp_ref) * _dot(po_ref[...], wpb_ref[...])
    h = h + gate(gr_ref, tr_ref, br_ref) * _dot(ro_ref[...], wrb_ref[...])
    h = h + gate(gm_ref, tm_ref, bm_ref) * _dot(mo_ref[...], wmb_ref[...])
    o_ref[...] = h.astype(o_ref.dtype)


def _branch(po, ro, mo, wp, wr, wm, z, b_gate, *, row0, tm, tn, gate_col0):
    rows = po.shape[0]
    d = wp.shape[1]
    off = row0 // tm
    tail = 2 * _LANES
    delta = gate_col0 % tn
    g0 = gate_col0 // tn
    assert delta <= tail and tn % tail == 0
    nb = d // tn
    act = lambda a: pl.BlockSpec((tm, a.shape[1]), lambda j, i: (i, 0))
    wgt = lambda w: pl.BlockSpec((w.shape[0], tn), lambda j, i: (0, j), pipeline_mode=pl.Buffered(1))
    gat = lambda k: pl.BlockSpec((tm, tn), lambda j, i, k=k: (off + i, g0 + k * nb + j))
    tai = lambda k: pl.BlockSpec((tm, tail), lambda j, i, k=k: (off + i, (g0 + k * nb + j + 1) * (tn // tail)))
    bia = lambda k: pl.BlockSpec((1, tn), lambda j, i, k=k: (0, k * nb + j))
    return pl.pallas_call(
        functools.partial(_branch_kernel, delta=delta),
        grid=(nb, rows // tm),
        in_specs=[act(po), act(ro), act(mo), wgt(wp), wgt(wr), wgt(wm), gat(0), gat(1), gat(2),
                  tai(0), tai(1), tai(2), bia(0), bia(1), bia(2)],
        out_specs=pl.BlockSpec((tm, tn), lambda j, i: (i, j)),
        out_shape=jax.ShapeDtypeStruct((rows, d), _BF16),
        scratch_shapes=[pltpu.VMEM((w.shape[0], tn), _BF16) for w in (wp, wr, wm)],
        compiler_params=_params(("parallel", "arbitrary")),
        name="branch_proj",
    )(po, ro, mo, wp, wr, wm, z, z, z, z, z, z, b_gate, b_gate, b_gate)


def _out_kernel(h_ref, w_ref, x_ref, g_ref, b_ref, o_ref, *, alpha):
    xf = alpha * x_ref[...] + _dot(h_ref[...], w_ref[...])
    mu = jnp.mean(xf, axis=-1, keepdims=True)
    d = xf - mu
    var = jnp.mean(d * d, axis=-1, keepdims=True)
    o_ref[...] = d * lax.rsqrt(var + _LN_EPS) * g_ref[...] + b_ref[...]


def _out_proj(h, w_out, x, ln_g, ln_b, *, tm, alpha):
    rows, d = x.shape
    kern = functools.partial(_out_kernel, alpha=alpha)
    return pl.pallas_call(
        kern,
        grid=(rows // tm,),
        in_specs=[
            pl.BlockSpec((tm, d), lambda i: (i, 0)),
            pl.BlockSpec((d, d), lambda i: (0, 0), pipeline_mode=pl.Buffered(1)),
            pl.BlockSpec((tm, d), lambda i: (i, 0)),
            pl.BlockSpec((1, d), lambda i: (0, 0)),
            pl.BlockSpec((1, d), lambda i: (0, 0)),
        ],
        out_specs=pl.BlockSpec((tm, d), lambda i: (i, 0)),
        out_shape=jax.ShapeDtypeStruct((rows, d), _F32),
        compiler_params=_params(("parallel",), vmem_limit=_VMEM_LIMIT_RESIDENT_WEIGHT),
        name="out_proj_ln",
    )(h, w_out, x, ln_g, ln_b)


def _tiles(batch, seq, n_sample):
    n_prompt = batch * seq
    m_all = n_prompt + n_sample
    return {
        "in_tm": next(t for t in (832, 640, 512, 256, 128, 64, 32, 16) if m_all % t == 0), "in_tn": 768,
        "row_tb": min(256, seq),
        "scan_tb": min(256, seq), "scan_pp": 8,
        "attn_tq": min(512, seq),
        "proj_tm": min(256, n_prompt), "proj_tn": 1024, "out_tm": min(256, n_prompt),
        "pool_bb": min(32, n_sample), "step_hb": 2, "attn_bb": 2,
    }


def _layer(xp, xs, mem_p, cache_k, cache_v, st_pool, st_shift, st_rwkv, lp):
    batch, seq, d = xp.shape
    ns = xs.shape[0]
    n_p = batch * seq
    m_all = n_p + ns
    pw = lp["pool_scale"].shape[-1]
    rw = lp["rwkv_w0"].shape[-1]
    heads, hd = lp["rwkv_r_k"].shape
    lora = lp["rwkv_w2"].shape[0]
    sw = lp["rwkv_mu"].shape[-1]
    n_mem, mh, md = cache_k.shape[1:]
    mw = mh * md
    tl = _tiles(batch, seq, ns)

    lo0, lo1 = 2 * pw + 3 * rw, 2 * pw + sw
    lw = 2 * _LANES
    assert 2 * lora <= lw
    q_col0 = lo1 + rw
    gate_col0 = q_col0 + 2 * mw

    x_all = jnp.concatenate([xp.reshape(n_p, d), xs.reshape(ns, d)], axis=0).astype(_BF16)
    z = _in_proj(x_all, jnp.swapaxes(lp["w_in"], 0, 1), tm=tl["in_tm"], tn=tl["in_tn"])
    z_s = z[n_p:]

    pool_w = lp["pool_w"].astype(_BF16)
    pool_scale = lp["pool_scale"].reshape(1, pw)
    po_p = _pool_prompt(z, pool_w, pool_scale, batch=batch, seq=seq, width=pw, tb=tl["row_tb"])
    po_s = _pool_sample(z_s, jnp.swapaxes(st_pool, 0, 1), pool_w, pool_scale, row0=0, bb=tl["pool_bb"])
    nbuf = st_pool.shape[1]
    new_pool_p = jnp.stack([z[(b + 1) * seq - nbuf:(b + 1) * seq, :pw] for b in range(batch)])
    new_pool_s = jnp.concatenate([st_pool[:, 1:, :], z_s[:, :pw].reshape(ns, 1, pw)], axis=1)

    mu = lp["rwkv_mu"]
    pad_l = lambda a: jnp.pad(a, ((0, 0), (0, lw - 2 * lora)))
    w2p = jnp.zeros((lw, rw), _F32).at[:lora].set(lp["rwkv_w2"]).astype(_BF16)
    a2p = jnp.zeros((lw, rw), _F32).at[lora:2 * lora].set(lp["rwkv_a2"]).astype(_BF16)
    row = lambda a: a.reshape(1, -1)
    mix_consts = (row(mu[:rw]), row(mu[rw:2 * rw]), row(mu[2 * rw:3 * rw]), pad_l(row(mu[3 * rw:])),
                  row(lp["rwkv_w0"]), row(lp["rwkv_a0"]), row(lp["rwkv_k_k"]), row(lp["rwkv_k_a"]), w2p, a2p)
    post_consts = (row(lp["rwkv_r_k"]), row(lp["rwkv_ln_w"]), row(lp["rwkv_ln_b"]))

    ro_p, st_pairs = _rwkv_prompt(z, mix_consts, post_consts, batch=batch, seq=seq, width=rw, tb=tl["scan_tb"],
                                  pp=tl["scan_pp"], head_dim=hd, col0=2 * pw, gate_col0=lo1, lw=lw)
    st5 = st_pairs.reshape(batch, heads // 2, 2, hd, 2, hd)
    new_state_p = jnp.stack([st5[:, :, 0, :, 0, :], st5[:, :, 1, :, 1, :]], axis=2).reshape(batch, heads, hd, hd)

    shift_s = st_shift.reshape(ns, sw)
    prep_s = _rwkv_prep_sample(z_s, shift_s[:, :3 * rw], pad_l(shift_s[:, 3 * rw:]), mix_consts, row0=0, rows=ns,
                               width=rw, head_dim=hd, col0=2 * pw, lw=lw)
    r_s, _, k_s, v_s, _, _ = prep_s[:6]
    y_t, state_t = _rwkv_step(prep_s[6:], jnp.transpose(st_rwkv, (1, 2, 3, 0)), hb=tl["step_hb"])
    new_state_s = jnp.transpose(state_t, (3, 0, 1, 2))
    ro_s = _rwkv_post(y_t.reshape(rw, ns).T, r_s, k_s, v_s, z_s, post_consts, row0=0, rows=ns, width=rw, tb=ns,
                      head_dim=hd, col0=lo1)
    new_shift_p = jnp.stack([z[(b + 1) * seq - 1:(b + 1) * seq, 2 * pw:lo1] for b in range(batch)])
    new_shift_s = z_s[:, 2 * pw:lo1].reshape(ns, 1, sw)

    mkv = _matmul(mem_p.reshape(batch * n_mem, d).astype(_BF16), lp["w_mem_kv"].astype(_BF16),
                  tm=min(512, batch * n_mem), tn=min(1024, 2 * mw), out_dtype=_F32, name="mem_kv_proj")
    mo_p = _mem_attn_prompt(z, mkv, batch=batch, seq=seq, n_mem=n_mem, heads=mh, head_dim=md, tq=tl["attn_tq"],
                            col0=q_col0)
    mo_s = _mem_attn_sample(z_s[:, q_col0:q_col0 + mw].reshape(ns, mh, md),
                            z_s[:, q_col0 + mw:q_col0 + 2 * mw].reshape(ns, mh, md),
                            cache_k, cache_v, bb=tl["attn_bb"]).reshape(ns, mw).astype(_BF16)
    mk_p = mkv[:, :mw].reshape(batch, n_mem, mh, md)
    mv_p = mkv[:, mw:].reshape(batch, n_mem, mh, md)

    wp, wr, wm = lp["w_branch_pool"], lp["w_branch_rwkv"], lp["w_branch_mem"]
    w_out = lp["w_out"].astype(_BF16)
    b_gate = row(lp["b_gate"])
    ln_g, ln_b = row(lp["ln_g"]), row(lp["ln_b"])
    alpha = lp["alpha"]
    outs = []
    for po, ro, mo, x2d, zz in ((po_p, ro_p, mo_p, xp.reshape(n_p, d), z), (po_s, ro_s, mo_s, xs.reshape(ns, d), z_s)):
        rows = x2d.shape[0]
        h = _branch(po, ro, mo, wp, wr, wm, zz, b_gate, row0=0, tm=min(tl["proj_tm"], rows), tn=tl["proj_tn"],
                    gate_col0=gate_col0)
        outs.append(_out_proj(h, w_out, x2d, ln_g, ln_b, tm=min(tl["out_tm"], rows), alpha=alpha))
    y_prompt = outs[0].reshape(batch, seq, d)
    y_sample = outs[1].reshape(ns, 1, d)
    return (y_prompt, y_sample, mk_p, mv_p, new_pool_p, new_shift_p, new_state_p, new_pool_s, new_shift_s,
            new_state_s)


def kernel(x_prompt, x_sample, cache_mem_k, cache_mem_v, state_pool, state_shift, state_rwkv, mem_prompt, w_in,
           b_gate, pool_w, pool_scale, rwkv_mu, rwkv_w0, rwkv_w2, rwkv_a0, rwkv_a2, rwkv_k_k, rwkv_k_a, rwkv_r_k,
           rwkv_ln_w, rwkv_ln_b, w_mem_kv, w_branch_pool, w_branch_rwkv, w_branch_mem, w_out, ln_g, ln_b):
    depth = w_in.shape[0]
    alpha = (2.0 * depth) ** 0.25
    weights = dict(w_in=w_in, b_gate=b_gate, pool_w=pool_w, pool_scale=pool_scale, rwkv_mu=rwkv_mu, rwkv_w0=rwkv_w0,
                   rwkv_w2=rwkv_w2, rwkv_a0=rwkv_a0, rwkv_a2=rwkv_a2, rwkv_k_k=rwkv_k_k, rwkv_k_a=rwkv_k_a,
                   rwkv_r_k=rwkv_r_k, rwkv_ln_w=rwkv_ln_w, rwkv_ln_b=rwkv_ln_b, w_mem_kv=w_mem_kv,
                   w_branch_pool=w_branch_pool, w_branch_rwkv=w_branch_rwkv, w_branch_mem=w_branch_mem,
                   w_out=w_out, ln_g=ln_g, ln_b=ln_b)
    yp, ys = x_prompt, x_sample
    per_layer = []
    for l in range(depth):
        lp = {k: v[l] for k, v in weights.items()}
        lp["alpha"] = alpha
        res = _layer(yp, ys, mem_prompt, cache_mem_k[l], cache_mem_v[l], state_pool[l], state_shift[l],
                     state_rwkv[l], lp)
        yp, ys = res[0], res[1]
        per_layer.append(res[2:])
    if depth == 1:
        stacked = tuple(a[None] for a in per_layer[0])
    else:
        stacked = tuple(jnp.stack([res[i] for res in per_layer]) for i in range(len(per_layer[0])))
    return (yp, ys) + stacked
```

```python
import functools

import jax
import jax.numpy as jnp
from jax import lax
from jax.experimental import pallas as pl
from jax.experimental.pallas import tpu as pltpu

_F32 = jnp.float32
_BF16 = jnp.bfloat16

_POOL_WINDOWS = (2, 4, 8, 16)
_PAST_LEN = 16384
_GN_EPS = 64e-5
_LN_EPS = 1e-5
_L2_EPS = 1e-12
_EXP_MINUS_HALF = 0.6065306597126334

_LANES = 128
_MIB = 1024 * 1024
_VMEM_LIMIT = 56 * _MIB
_VMEM_LIMIT_RESIDENT_WEIGHT = 62 * _MIB

_HEAD_PAIR = _LANES
_CHUNK = 64


def _params(sem, vmem_limit=_VMEM_LIMIT):
    return pltpu.CompilerParams(dimension_semantics=sem, vmem_limit_bytes=vmem_limit)


def _dot(a, b):
    return jnp.dot(a, b, preferred_element_type=_F32)


def _dot_nt(a, b):
    return lax.dot_general(a, b, (((1,), (1,)), ((), ())), preferred_element_type=_F32)


def _dot_tn(a, b):
    return lax.dot_general(a, b, (((0,), (0,)), ((), ())), preferred_element_type=_F32)


def _split(x):
    hi = x.astype(_BF16)
    lo = (x - hi.astype(_F32)).astype(_BF16)
    return hi, lo


def _mm1(dot, a, b):
    return dot(a.astype(_BF16), b.astype(_BF16))


def _mm_exact_rhs(a, b_bf16):
    a_hi, a_lo = _split(a)
    return _dot(a_hi, b_bf16) + _dot(a_lo, b_bf16)


def _silu(x):
    return x * jax.nn.sigmoid(x)


def _head_block_matrix(value, head_dim):
    r = lax.broadcasted_iota(jnp.int32, (_LANES, _LANES), 0) // head_dim
    c = lax.broadcasted_iota(jnp.int32, (_LANES, _LANES), 1) // head_dim
    return jnp.where(r == c, value, 0.0).astype(_BF16)


def _matmul_kernel(x_ref, w_ref, o_ref):
    o_ref[...] = _dot(x_ref[...], w_ref[...]).astype(o_ref.dtype)


def _matmul(x, w, *, tm, tn, out_dtype, name):
    m, k = x.shape
    n = w.shape[1]
    return pl.pallas_call(
        _matmul_kernel,
        grid=(m // tm, n // tn),
        in_specs=[pl.BlockSpec((tm, k), lambda i, j: (i, 0)), pl.BlockSpec((k, tn), lambda i, j: (0, j))],
        out_specs=pl.BlockSpec((tm, tn), lambda i, j: (i, j)),
        out_shape=jax.ShapeDtypeStruct((m, n), out_dtype),
        compiler_params=_params(("parallel", "parallel")),
        name=name,
    )(x, w)


def _in_proj_kernel(x_ref, wt_ref, o_ref, wb_ref):
    @pl.when(pl.program_id(1) == 0)
    def _():
        wb_ref[...] = wt_ref[...].astype(_BF16)

    o_ref[...] = _dot_nt(x_ref[...], wb_ref[...])


def _in_proj(x, wt, *, tm, tn):
    m, k = x.shape
    n = wt.shape[0]
    return pl.pallas_call(
        _in_proj_kernel,
        grid=(pl.cdiv(n, tn), m // tm),
        in_specs=[pl.BlockSpec((tm, k), lambda j, i: (i, 0)), pl.BlockSpec((tn, k), lambda j, i: (j, 0))],
        out_specs=pl.BlockSpec((tm, tn), lambda j, i: (i, j)),
        out_shape=jax.ShapeDtypeStruct((m, n), _F32),
        scratch_shapes=[pltpu.VMEM((tn, k), _BF16)],
        compiler_params=_params(("parallel", "arbitrary"), vmem_limit=_VMEM_LIMIT_RESIDENT_WEIGHT),
        name="in_proj",
    )(x, wt)


def _pool_mix(pooled_fn, zp_ref, pw_ref, ps_ref, o_ref, group):
    for g in range(len(_POOL_WINDOWS)):
        cols = slice(g * group, (g + 1) * group)
        mixed = _dot(pooled_fn(g, cols).astype(_BF16), pw_ref[g])
        o_ref[:, cols] = (mixed * ps_ref[:, cols] * _silu(zp_ref[:, cols])).astype(o_ref.dtype)


def _pool_prompt_kernel(u_ref, halo_ref, zp_ref, pw_ref, ps_ref, o_ref, e_ref, *, tb, hist, group):
    t = pl.program_id(1)
    e_ref[0:hist, :] = jnp.where(t == 0, 0.0, halo_ref[...])
    e_ref[hist:hist + tb, :] = u_ref[...]
    pos = t * tb + lax.broadcasted_iota(jnp.int32, (tb, group), 0)

    def window_sum(cols, win):
        if win < 8:
            acc = e_ref[hist:hist + tb, cols]
            for d in range(1, win):
                acc = acc + e_ref[hist - d:hist - d + tb, cols]
            return acc
        assert win & (win - 1) == 0
        s, span = e_ref[:, cols], 1
        while span < win:
            s = s[span:] + s[:-span]
            span *= 2
        first = hist - (win - 1)
        return s[first:first + tb]

    def pooled(g, cols):
        win = _POOL_WINDOWS[g]
        x = e_ref[hist:hist + tb, cols]
        cnt = jnp.minimum(pos + 1, win).astype(_F32)
        return window_sum(cols, win) / cnt - x

    _pool_mix(pooled, zp_ref, pw_ref, ps_ref, o_ref, group)


def _pool_prompt(z, pool_w, pool_scale, *, batch, seq, width, tb):
    hist = 16
    n_t = seq // tb
    group = width // len(_POOL_WINDOWS)
    kern = functools.partial(_pool_prompt_kernel, tb=tb, hist=hist, group=group)
    return pl.pallas_call(
        kern,
        grid=(batch, n_t),
        in_specs=[
            pl.BlockSpec((tb, width), lambda b, t: (b * n_t + t, 0)),
            pl.BlockSpec((hist, width), lambda b, t: (jnp.maximum((b * seq + t * tb) // hist - 1, 0), 0)),
            pl.BlockSpec((tb, width), lambda b, t: (b * n_t + t, 1)),
            pl.BlockSpec(pool_w.shape, lambda b, t: (0, 0, 0)),
            pl.BlockSpec((1, width), lambda b, t: (0, 0)),
        ],
        out_specs=pl.BlockSpec((tb, width), lambda b, t: (b * n_t + t, 0)),
        out_shape=jax.ShapeDtypeStruct((batch * seq, width), _BF16),
        scratch_shapes=[pltpu.VMEM((hist + tb, width), _F32)],
        compiler_params=_params(("parallel", "parallel")),
        name="pool_prompt",
    )(z, z, z, pool_w, pool_scale)


def _pool_sample_kernel(u_ref, buf_ref, zp_ref, pw_ref, ps_ref, o_ref, *, nbuf, group):
    def pooled(g, cols):
        win = _POOL_WINDOWS[g]
        x = u_ref[:, cols]
        acc = x
        for d in range(1, win):
            acc = acc + buf_ref[nbuf - d, :, cols]
        cnt = float(min(_PAST_LEN + 1, win))
        return acc / cnt - x

    _pool_mix(pooled, zp_ref, pw_ref, ps_ref, o_ref, group)


def _pool_sample(z, buf, pool_w, pool_scale, *, row0, bb):
    nbuf, rows, width = buf.shape
    group = width // len(_POOL_WINDOWS)
    off = row0 // bb
    kern = functools.partial(_pool_sample_kernel, nbuf=nbuf, group=group)
    return pl.pallas_call(
        kern,
        grid=(rows // bb,),
        in_specs=[
            pl.BlockSpec((bb, width), lambda i: (off + i, 0)),
            pl.BlockSpec((nbuf, bb, width), lambda i: (0, i, 0)),
            pl.BlockSpec((bb, width), lambda i: (off + i, 1)),
            pl.BlockSpec(pool_w.shape, lambda i: (0, 0, 0)),
            pl.BlockSpec((1, width), lambda i: (0, 0)),
        ],
        out_specs=pl.BlockSpec((bb, width), lambda i: (i, 0)),
        out_shape=jax.ShapeDtypeStruct((rows, width), _BF16),
        compiler_params=_params(("parallel",)),
        name="pool_sample",
    )(z, buf, z, pool_w, pool_scale)


def _rwkv_mix(cur, prev, mu, w0, a0, kk_scale, ka, w2p, a2p, head_dim, outs):
    o_r, o_ld, o_k, o_v, o_kk, o_be = outs
    mix = lambda n: cur[n] + (prev[n] - cur[n]) * mu[n]
    xl = mix("l")
    wl = w0 + _mm1(_dot, jnp.tanh(xl), w2p)
    a = jax.nn.sigmoid(a0 + _mm1(_dot, xl, a2p))
    xk = mix("k")
    kkp = xk * kk_scale
    ones_bd = _head_block_matrix(1.0, head_dim)
    width = kkp.shape[1]
    o_r[...] = mix("r")
    o_v[...] = mix("v")
    o_ld[...] = -_EXP_MINUS_HALF * jax.nn.sigmoid(wl)
    o_k[...] = xk * (1.0 + (a - 1.0) * ka)
    for p in range(width // _LANES):
        cols = slice(p * _LANES, (p + 1) * _LANES)
        kp = kkp[:, cols]
        ss = _mm_exact_rhs(kp * kp, ones_bd)
        kk = kp * lax.rsqrt(jnp.maximum(ss, _L2_EPS * _L2_EPS))
        o_kk[:, cols] = kk
        o_be[:, cols] = kk * a[:, cols]


def _rwkv_prep_sample_kernel(zr, zk, zv, zl, pr, pk, pv, plr, mur, muk, muv, mul, w0, a0, kks, ka, w2p, a2p,
                             o_r, o_ld, o_k, o_v, o_kk, o_be, t_r, t_ld, t_k, t_v, t_kk, t_be, *, head_dim):
    cur = {"r": zr[...], "k": zk[...], "v": zv[...], "l": zl[...]}
    prev = {"r": pr[...], "k": pk[...], "v": pv[...], "l": plr[...]}
    mu = {"r": mur[...], "k": muk[...], "v": muv[...], "l": mul[...]}
    _rwkv_mix(cur, prev, mu, w0[...], a0[...], kks[...], ka[...], w2p[...], a2p[...], head_dim,
              (o_r, o_ld, o_k, o_v, o_kk, o_be))
    for o_ref, t_ref in ((o_r, t_r), (o_ld, t_ld), (o_k, t_k), (o_v, t_v), (o_kk, t_kk), (o_be, t_be)):
        t_ref[...] = o_ref[...].T


def _rwkv_prep_sample(z, prev_rkv, prev_l, consts, *, row0, rows, width, head_dim, col0, lw):
    c0 = col0 // width
    lcol = (col0 + 3 * width) // lw
    assert c0 * width == col0 and lcol * lw == col0 + 3 * width
    off = row0 // rows
    cblk = lambda c: pl.BlockSpec((rows, width), lambda i, c=c: (off, c))
    pblk = lambda c: pl.BlockSpec((rows, width), lambda i, c=c: (0, c))
    vec = lambda n: pl.BlockSpec((1, n), lambda i: (0, 0))
    mat = pl.BlockSpec((lw, width), lambda i: (0, 0))
    out_spec = pl.BlockSpec((rows, width), lambda i: (0, 0))
    out_shape = jax.ShapeDtypeStruct((rows, width), _F32)
    kern = functools.partial(_rwkv_prep_sample_kernel, head_dim=head_dim)
    return pl.pallas_call(
        kern,
        grid=(1,),
        in_specs=[cblk(c0), cblk(c0 + 1), cblk(c0 + 2), pl.BlockSpec((rows, lw), lambda i: (off, lcol)),
                  pblk(0), pblk(1), pblk(2), pl.BlockSpec((rows, lw), lambda i: (0, 0)),
                  vec(width), vec(width), vec(width), vec(lw),
                  vec(width), vec(width), vec(width), vec(width), mat, mat],
        out_specs=[out_spec] * 6 + [pl.BlockSpec((width, rows), lambda i: (0, 0))] * 6,
        out_shape=[out_shape] * 6 + [jax.ShapeDtypeStruct((width, rows), _F32)] * 6,
        compiler_params=_params(("arbitrary",)),
        name="rwkv_prep_sample",
    )(z, z, z, z, prev_rkv, prev_rkv, prev_rkv, prev_l, *consts)


def _scan_chunks(inputs, states, c):
    pairs = range(len(inputs))
    n2 = _LANES
    bf = lambda x: x.astype(_BF16)
    m0, m1 = c["m0"], c["m1"]
    expand = lambda x: jnp.concatenate([x * m0, x * m1], axis=0)
    stack = lambda a, b: jnp.concatenate([expand(a), expand(b)], axis=0)

    cum = [_mm_exact_rhs_t(c["ltri"], inputs[q][1]) for q in pairs]
    tot = [cum[q][_CHUNK - 1:_CHUNK, :] for q in pairs]
    xa, xb, bk, ve = [], [], [], []
    for q in pairs:
        r, ld, k2, v, kk, be = inputs[q]
        e_neg = jnp.exp(-cum[q])
        e_rem = jnp.exp(tot[q] - cum[q])
        xa.append(bf(stack(-kk * jnp.exp(cum[q] - ld), r * jnp.exp(cum[q]))))
        xb.append(bf(stack(be * e_neg, k2 * e_neg)))
        bk.append(bf(stack(be * e_rem, k2 * e_rem)))
        ve.append(bf(expand(v)))
    g = [_dot_nt(xa[q], xb[q]) for q in pairs]
    ps = [_dot_nt(xa[q], bf(states[q])) for q in pairs]
    a_kk = [bf(jnp.concatenate([jnp.where(c["strict"], g[q][:n2, n2:], 0.0),
                                jnp.where(c["incl"], g[q][n2:, n2:], 0.0)], axis=0)) for q in pairs]
    av = [_dot(a_kk[q], ve[q]) for q in pairs]
    npow = [jnp.where(c["strict"], g[q][:n2, :n2], 0.0) for q in pairs]
    u = [ps[q][:n2] + av[q][:n2] for q in pairs]
    for k in range(c["squarings"]):
        out = [_dot(bf(npow[q]), bf(jnp.concatenate([npow[q], u[q]], axis=1))) for q in pairs]
        npow = [out[q][:, :n2] for q in pairs]
        u = [u[q] + out[q][:, n2:] for q in pairs]
    ub = [bf(u[q]) for q in pairs]
    u = [u[q] + _dot(bf(npow[q]), ub[q]) for q in pairs]
    ub = [bf(u[q]) for q in pairs]
    a_rb = [bf(jnp.where(c["incl"], g[q][n2:, :n2], 0.0)) for q in pairs]
    ye = [ps[q][n2:] + av[q][n2:] + _dot(a_rb[q], ub[q]) for q in pairs]
    ys = [ye[q][:_CHUNK] + ye[q][_CHUNK:] for q in pairs]
    new_states = [states[q] * jnp.exp(tot[q]) + _dot_tn(jnp.concatenate([ub[q], ve[q]], axis=0), bk[q])
                  for q in pairs]
    return ys, new_states


def _mm_exact_rhs_t(l_bf16, x):
    x_hi, x_lo = _split(x)
    return _dot(l_bf16, x_hi) + _dot(l_bf16, x_lo)


def _scan_consts():
    n = _LANES
    half = _CHUNK
    ri = lax.broadcasted_iota(jnp.int32, (n, n), 0)
    ci = lax.broadcasted_iota(jnp.int32, (n, n), 1)
    same = (ri // half) == (ci // half)
    lane = lax.broadcasted_iota(jnp.int32, (half, n), 1)
    tr = lax.broadcasted_iota(jnp.int32, (half, half), 0)
    tc = lax.broadcasted_iota(jnp.int32, (half, half), 1)
    squarings = 0
    while (2 << squarings) < half:
        squarings += 1
    return {
        "strict": same & ((ri % half) > (ci % half)),
        "incl": same & ((ri % half) >= (ci % half)),
        "m0": jnp.where(lane < half, 1.0, 0.0).astype(_F32),
        "m1": jnp.where(lane >= half, 1.0, 0.0).astype(_F32),
        "ltri": jnp.where(tr >= tc, 1.0, 0.0).astype(_BF16),
        "squarings": squarings,
    }


def _rwkv_prompt_kernel(zr, zk, zv, zl, hr, hk, hv, hl, za, zb, mur, muk, muv, mul, w0, a0, kks, ka, w2p, a2p,
                        rk, lnw, lnb, o_ref, s_ref, er, ek, ev, el, r_s, ld_s, k_s, v_s, kk_s, be_s, y_s, xl_s, st_ref,
                        *, tb, pp, head_dim, delta):
    t = pl.program_id(2)
    halo = 8
    n_chunks = tb // _CHUNK
    pairs = range(pp)
    cols = [slice(q * _LANES, (q + 1) * _LANES) for q in pairs]

    @pl.when(t == 0)
    def _():
        st_ref[...] = jnp.zeros_like(st_ref)

    for z_ref, h_ref, e_ref, mu_ref, dst in ((zr, hr, er, mur, r_s), (zk, hk, ek, muk, k_s), (zv, hv, ev, muv, v_s),
                                             (zl, hl, el, mul, xl_s)):
        e_ref[0:halo, :] = jnp.where(t == 0, 0.0, h_ref[...])
        e_ref[halo:halo + tb, :] = z_ref[...]
        cur = z_ref[...]
        dst[...] = cur + (e_ref[halo - 1:halo - 1 + tb, :] - cur) * mu_ref[...]

    c = _scan_consts()
    ones_bd = _head_block_matrix(1.0, head_dim)
    avg_bd = _head_block_matrix(1.0 / head_dim, head_dim)

    xl = xl_s[...]
    wl = w0[...] + _mm1(_dot, jnp.tanh(xl), w2p[...])
    a = jax.nn.sigmoid(a0[...] + _mm1(_dot, xl, a2p[...]))
    ld_s[...] = -_EXP_MINUS_HALF * jax.nn.sigmoid(wl)
    xk = k_s[...]
    k_s[...] = xk * (1.0 + (a - 1.0) * ka[...])
    kp = [xk[:, cols[q]] * kks[:, cols[q]] for q in pairs]
    ss = [_mm_exact_rhs(kp[q] * kp[q], ones_bd) for q in pairs]
    for q in pairs:
        kk = kp[q] * lax.rsqrt(jnp.maximum(ss[q], _L2_EPS * _L2_EPS))
        kk_s[:, cols[q]] = kk
        be_s[:, cols[q]] = kk * a[:, cols[q]]

    def body(ci, carry):
        rows = pl.ds(pl.multiple_of(ci * _CHUNK, _CHUNK), _CHUNK)
        in_refs = (r_s, ld_s, k_s, v_s, kk_s, be_s)
        ys, new_states = _scan_chunks([tuple(ref[rows, cols[q]] for ref in in_refs) for q in pairs],
                                      [st_ref[q] for q in pairs], c)
        for q in pairs:
            y_s[rows, cols[q]] = ys[q]
            st_ref[q] = new_states[q]
        return carry

    lax.fori_loop(0, n_chunks, body, 0)

    q0, off = delta // _LANES, delta % _LANES
    n_a = za.shape[1] // _LANES
    zblk = lambda i: (za[:, i * _LANES:(i + 1) * _LANES] if i < n_a
                      else zb[:, (i - n_a) * _LANES:(i - n_a + 1) * _LANES])
    y = [y_s[:, cols[q]] for q in pairs]
    d = [y[q] - _mm_exact_rhs(y[q], avg_bd) for q in pairs]
    var = [_mm_exact_rhs(d[q] * d[q], avg_bd) for q in pairs]
    bsum = [_mm_exact_rhs(r_s[:, cols[q]] * k_s[:, cols[q]] * rk[:, cols[q]], ones_bd) for q in pairs]
    for q in pairs:
        yn = d[q] * lax.rsqrt(var[q] + _GN_EPS) * lnw[:, cols[q]] + lnb[:, cols[q]]
        zg = jnp.concatenate([zblk(q0 + q), zblk(q0 + q + 1)], axis=1)[:, off:off + _LANES]
        o_ref[:, cols[q]] = ((yn + bsum[q] * v_s[:, cols[q]]) * _silu(zg)).astype(o_ref.dtype)

    @pl.when(t == pl.num_programs(2) - 1)
    def _():
        s_ref[0] = st_ref[...]


def _rwkv_prompt(z, mix_consts, post_consts, *, batch, seq, width, tb, pp, head_dim, col0, gate_col0, lw):
    n_t = seq // tb
    gw = pp * _LANES
    halo = 8
    tail = 2 * _LANES
    c0, lcol, cg = col0 // gw, (col0 + 3 * width) // lw, gate_col0 // gw
    delta = gate_col0 % gw
    assert c0 * gw == col0 and width % gw == 0 and lcol * lw == col0 + 3 * width
    assert delta <= tail and delta % _LANES + _LANES <= tail and gw % tail == 0
    row = lambda b, g, t: b * n_t + t
    hrow = lambda b, g, t: jnp.maximum((b * seq + t * tb) // halo - 1, 0)
    step = width // gw
    zblk = lambda i: pl.BlockSpec((tb, gw), lambda b, g, t, i=i: (row(b, g, t), c0 + i * step + g))
    hblk = lambda i: pl.BlockSpec((halo, gw), lambda b, g, t, i=i: (hrow(b, g, t), c0 + i * step + g))
    vec = pl.BlockSpec((1, gw), lambda b, g, t: (0, g))
    lvec = pl.BlockSpec((1, lw), lambda b, g, t: (0, 0))
    mat = pl.BlockSpec((lw, gw), lambda b, g, t: (0, g))
    mur, muk, muv, mul, w0, a0, kks, ka, w2p, a2p = mix_consts
    big = lambda: pltpu.VMEM((tb, gw), _F32)
    kern = functools.partial(_rwkv_prompt_kernel, tb=tb, pp=pp, head_dim=head_dim, delta=delta)
    return pl.pallas_call(
        kern,
        grid=(batch, width // gw, n_t),
        in_specs=[zblk(0), zblk(1), zblk(2), pl.BlockSpec((tb, lw), lambda b, g, t: (row(b, g, t), lcol)),
                  hblk(0), hblk(1), hblk(2), pl.BlockSpec((halo, lw), lambda b, g, t: (hrow(b, g, t), lcol)),
                  pl.BlockSpec((tb, gw), lambda b, g, t: (row(b, g, t), cg + g)),
                  pl.BlockSpec((tb, tail), lambda b, g, t: (row(b, g, t), (cg + g + 1) * (gw // tail))),
                  vec, vec, vec, lvec, vec, vec, vec, vec, mat, mat, vec, vec, vec],
        out_specs=[pl.BlockSpec((tb, gw), lambda b, g, t: (row(b, g, t), g)),
                   pl.BlockSpec((1, pp, _LANES, _LANES), lambda b, g, t: (b, g, 0, 0))],
        out_shape=[jax.ShapeDtypeStruct((batch * seq, width), _BF16),
                   jax.ShapeDtypeStruct((batch, width // _LANES, _LANES, _LANES), _F32)],
        scratch_shapes=[pltpu.VMEM((halo + tb, gw), _F32)] * 3 + [pltpu.VMEM((halo + tb, lw), _F32)]
                       + [big() for _ in range(7)] + [pltpu.VMEM((tb, lw), _F32),
                                                      pltpu.VMEM((pp, _LANES, _LANES), _F32)],
        compiler_params=_params(("parallel", "parallel", "arbitrary")),
        name="rwkv_prompt",
    )(z, z, z, z, z, z, z, z, z, z, mur, muk, muv, mul, w0, a0, kks, ka, w2p, a2p, *post_consts)


def _rwkv_step_kernel(r_ref, ld_ref, k_ref, v_ref, kk_ref, be_ref, s_ref, y_ref, so_ref, *, hb, head_dim, vi):
    for h in range(hb):
        kk = kk_ref[h][None]
        dec = jnp.exp(ld_ref[h])[None]
        be = be_ref[h][None]
        k = k_ref[h][None]
        r = r_ref[h][None]

        def body(c, carry, h=h, kk=kk, dec=dec, be=be, k=k, r=r):
            rows = pl.ds(pl.multiple_of(c * vi, vi), vi)
            s = s_ref[h, rows]
            sa = -jnp.sum(s * kk, axis=1, keepdims=True)
            s_new = s * dec + sa * be + v_ref[h, rows] * k
            y_ref[h, rows] = jnp.sum(s_new * r, axis=1, keepdims=True)
            so_ref[h, rows] = s_new
            return carry

        lax.fori_loop(0, head_dim // vi, body, 0)


def _rwkv_step(arrs_t, state_t, *, hb):
    heads, hd, _, n = state_t.shape
    r, ld, k, v, kk, be = arrs_t
    per_key = lambda a: a.reshape(heads, hd, n)
    vec = pl.BlockSpec((hb, hd, n), lambda i: (i, 0, 0))
    col = pl.BlockSpec((hb, hd, 1, n), lambda i: (i, 0, 0, 0))
    mat = pl.BlockSpec((hb, hd, hd, n), lambda i: (i, 0, 0, 0))
    kern = functools.partial(_rwkv_step_kernel, hb=hb, head_dim=hd, vi=8)
    return pl.pallas_call(
        kern,
        grid=(heads // hb,),
        in_specs=[vec, vec, vec, col, vec, vec, mat],
        out_specs=[col, mat],
        out_shape=[jax.ShapeDtypeStruct((heads, hd, 1, n), _F32), jax.ShapeDtypeStruct(state_t.shape, _F32)],
        compiler_params=_params(("parallel",)),
        name="rwkv_step",
    )(per_key(r), per_key(ld), per_key(k), v.reshape(heads, hd, 1, n), per_key(kk), per_key(be), state_t)


def _rwkv_post_kernel(y_ref, r_ref, k_ref, v_ref, za_ref, zb_ref, rk_ref, lw_ref, lb_ref, o_ref, *, head_dim, delta):
    avg_bd = _head_block_matrix(1.0 / head_dim, head_dim)
    ones_bd = _head_block_matrix(1.0, head_dim)
    width = y_ref.shape[1]
    zwin = jnp.concatenate([za_ref[...], zb_ref[...]], axis=1)
    for p in range(width // _LANES):
        cols = slice(p * _LANES, (p + 1) * _LANES)
        y = y_ref[:, cols]
        d = y - _mm_exact_rhs(y, avg_bd)
        var = _mm_exact_rhs(d * d, avg_bd)
        yn = d * lax.rsqrt(var + _GN_EPS) * lw_ref[:, cols] + lb_ref[:, cols]
        bonus = _mm_exact_rhs(r_ref[:, cols] * k_ref[:, cols] * rk_ref[:, cols], ones_bd) * v_ref[:, cols]
        zg = zwin[:, delta + p * _LANES:delta + (p + 1) * _LANES]
        o_ref[:, cols] = ((yn + bonus) * _silu(zg)).astype(o_ref.dtype)


def _rwkv_post(y, r, k2, v, z, consts, *, row0, rows, width, tb, head_dim, col0):
    off = row0 // tb
    tail = 2 * _LANES
    cg, delta = col0 // width, col0 % width
    assert delta <= tail and width % tail == 0
    blk = pl.BlockSpec((tb, width), lambda i: (i, 0))
    vec = pl.BlockSpec((1, width), lambda i: (0, 0))
    kern = functools.partial(_rwkv_post_kernel, head_dim=head_dim, delta=delta)
    return pl.pallas_call(
        kern,
        grid=(rows // tb,),
        in_specs=[blk, blk, blk, blk,
                  pl.BlockSpec((tb, width), lambda i: (off + i, cg)),
                  pl.BlockSpec((tb, tail), lambda i: (off + i, (cg + 1) * (width // tail))),
                  vec, vec, vec],
        out_specs=blk,
        out_shape=jax.ShapeDtypeStruct((rows, width), _BF16),
        compiler_params=_params(("parallel",)),
        name="rwkv_post",
    )(y, r, k2, v, z, z, *consts)


def _mem_attn_prompt_kernel(za_ref, zb_ref, zc_ref, k_ref, v_ref, o_ref, *, heads, head_dim, delta):
    scale = head_dim ** -0.5
    mw = heads * head_dim
    zwin = jnp.concatenate([za_ref[...], zb_ref[...], zc_ref[...]], axis=1)
    for h in range(heads):
        cols = slice(h * head_dim, (h + 1) * head_dim)
        q = zwin[:, delta + cols.start:delta + cols.stop]
        zg = zwin[:, delta + mw + cols.start:delta + mw + cols.stop]
        s = _dot_nt(q.astype(_BF16), k_ref[:, cols].astype(_BF16)) * scale
        p = jnp.exp(s - jnp.max(s, axis=-1, keepdims=True))
        o = _dot(p.astype(_BF16), v_ref[:, cols].astype(_BF16)) / jnp.sum(p, axis=-1, keepdims=True)
        o_ref[:, cols] = (o * _silu(zg)).astype(o_ref.dtype)


def _mem_attn_prompt(z, mkv, *, batch, seq, n_mem, heads, head_dim, tq, col0):
    n_t = seq // tq
    mw = heads * head_dim
    tail = 2 * _LANES
    q_col, delta = col0 // mw, col0 % mw
    assert delta <= tail and mw % tail == 0
    kern = functools.partial(_mem_attn_prompt_kernel, heads=heads, head_dim=head_dim, delta=delta)
    return pl.pallas_call(
        kern,
        grid=(batch, n_t),
        in_specs=[
            pl.BlockSpec((tq, mw), lambda b, t: (b * n_t + t, q_col)),
            pl.BlockSpec((tq, mw), lambda b, t: (b * n_t + t, q_col + 1)),
            pl.BlockSpec((tq, tail), lambda b, t: (b * n_t + t, (q_col + 2) * (mw // tail))),
            pl.BlockSpec((n_mem, mw), lambda b, t: (b, 0)),
            pl.BlockSpec((n_mem, mw), lambda b, t: (b, 1)),
        ],
        out_specs=pl.BlockSpec((tq, mw), lambda b, t: (b * n_t + t, 0)),
        out_shape=jax.ShapeDtypeStruct((batch * seq, mw), _BF16),
        compiler_params=_params(("parallel", "parallel")),
        name="mem_attn_prompt",
    )(z, z, z, mkv, mkv)


def _mem_attn_sample_kernel(q_ref, zg_ref, k_ref, v_ref, o_ref, *, bb, head_dim):
    scale = head_dim ** -0.5
    for b in range(bb):
        s = jnp.sum(k_ref[b] * q_ref[b][None], axis=-1, keepdims=True) * scale
        p = jnp.exp(s - jnp.max(s, axis=0, keepdims=True))
        o = jnp.sum(p * v_ref[b], axis=0) / jnp.sum(p, axis=0)
        o_ref[b] = o * _silu(zg_ref[b])


def _mem_attn_sample(q, zg, mem_k, mem_v, *, bb):
    n, n_mem, heads, head_dim = mem_k.shape
    vec = pl.BlockSpec((bb, heads, head_dim), lambda i: (i, 0, 0))
    mat = pl.BlockSpec((bb, n_mem, heads, head_dim), lambda i: (i, 0, 0, 0))
    kern = functools.partial(_mem_attn_sample_kernel, bb=bb, head_dim=head_dim)
    return pl.pallas_call(
        kern,
        grid=(n // bb,),
        in_specs=[vec, vec, mat, mat],
        out_specs=vec,
        out_shape=jax.ShapeDtypeStruct((n, heads, head_dim), _F32),
        compiler_params=_params(("parallel",)),
        name="mem_attn_sample",
    )(q, zg, mem_k, mem_v)


def _branch_kernel(po_ref, ro_ref, mo_ref, wp_ref, wr_ref, wm_ref, gp_ref, gr_ref, gm_ref, tp_ref, tr_ref, tm_ref,
                   bp_ref, br_ref, bm_ref, o_ref, wpb_ref, wrb_ref, wmb_ref, *, delta):
    @pl.when(pl.program_id(1) == 0)
    def _():
        wpb_ref[...] = wp_ref[...].astype(_BF16)
        wrb_ref[...] = wr_ref[...].astype(_BF16)
        wmb_ref[...] = wm_ref[...].astype(_BF16)

    tn = o_ref.shape[1]

    def gate(g_ref, t_ref, b_ref):
        win = jnp.concatenate([g_ref[...], t_ref[...]], axis=1)
        return jax.nn.sigmoid(win[:, delta:delta + tn] + b_ref[...])

    h = gate(gp_ref, tp_ref, bp_ref) * _dot(po_ref[...], wpb_ref[...])
    h = h + gate(gr_ref, tr_ref, br_ref) * _dot(ro_ref[...], wrb_ref[...])
    h = h + gate(gm_ref, tm_ref, bm_ref) * _dot(mo_ref[...], wmb_ref[...])
    o_ref[...] = h.astype(o_ref.dtype)


def _branch(po, ro, mo, wp, wr, wm, z, b_gate, *, row0, tm, tn, gate_col0):
    rows = po.shape[0]
    d = wp.shape[1]
    off = row0 // tm
    tail = 2 * _LANES
    delta = gate_col0 % tn
    g0 = gate_col0 // tn
    assert delta <= tail and tn % tail == 0
    nb = d // tn
    act = lambda a: pl.BlockSpec((tm, a.shape[1]), lambda j, i: (i, 0))
    wgt = lambda w: pl.BlockSpec((w.shape[0], tn), lambda j, i: (0, j), pipeline_mode=pl.Buffered(1))
    gat = lambda k: pl.BlockSpec((tm, tn), lambda j, i, k=k: (off + i, g0 + k * nb + j))
    tai = lambda k: pl.BlockSpec((tm, tail), lambda j, i, k=k: (off + i, (g0 + k * nb + j + 1) * (tn // tail)))
    bia = lambda k: pl.BlockSpec((1, tn), lambda j, i, k=k: (0, k * nb + j))
    return pl.pallas_call(
        functools.partial(_branch_kernel, delta=delta),
        grid=(nb, rows // tm),
        in_specs=[act(po), act(ro), act(mo), wgt(wp), wgt(wr), wgt(wm), gat(0), gat(1), gat(2),
                  tai(0), tai(1), tai(2), bia(0), bia(1), bia(2)],
        out_specs=pl.BlockSpec((tm, tn), lambda j, i: (i, j)),
        out_shape=jax.ShapeDtypeStruct((rows, d), _BF16),
        scratch_shapes=[pltpu.VMEM((w.shape[0], tn), _BF16) for w in (wp, wr, wm)],
        compiler_params=_params(("parallel", "arbitrary")),
        name="branch_proj",
    )(po, ro, mo, wp, wr, wm, z, z, z, z, z, z, b_gate, b_gate, b_gate)


def _out_kernel(h_ref, w_ref, x_ref, g_ref, b_ref, o_ref, *, alpha):
    xf = alpha * x_ref[...] + _dot(h_ref[...], w_ref[...])
    mu = jnp.mean(xf, axis=-1, keepdims=True)
    d = xf - mu
    var = jnp.mean(d * d, axis=-1, keepdims=True)
    o_ref[...] = d * lax.rsqrt(var + _LN_EPS) * g_ref[...] + b_ref[...]


def _out_proj(h, w_out, x, ln_g, ln_b, *, tm, alpha):
    rows, d = x.shape
    kern = functools.partial(_out_kernel, alpha=alpha)
    return pl.pallas_call(
        kern,
        grid=(rows // tm,),
        in_specs=[
            pl.BlockSpec((tm, d), lambda i: (i, 0)),
            pl.BlockSpec((d, d), lambda i: (0, 0), pipeline_mode=pl.Buffered(1)),
            pl.BlockSpec((tm, d), lambda i: (i, 0)),
            pl.BlockSpec((1, d), lambda i: (0, 0)),
            pl.BlockSpec((1, d), lambda i: (0, 0)),
        ],
        out_specs=pl.BlockSpec((tm, d), lambda i: (i, 0)),
        out_shape=jax.ShapeDtypeStruct((rows, d), _F32),
        compiler_params=_params(("parallel",), vmem_limit=_VMEM_LIMIT_RESIDENT_WEIGHT),
        name="out_proj_ln",
    )(h, w_out, x, ln_g, ln_b)


def _tiles(batch, seq, n_sample):
    n_prompt = batch * seq
    m_all = n_prompt + n_sample
    return {
        "in_tm": next(t for t in (1040, 832, 640, 512, 256, 128, 64, 32, 16) if m_all % t == 0), "in_tn": 768,
        "row_tb": min(256, seq),
        "scan_tb": min(256, seq), "scan_pp": 8,
        "attn_tq": min(512, seq),
        "proj_tm": min(256, n_prompt), "proj_tn": 1024, "out_tm": min(256, n_prompt),
        "pool_bb": min(32, n_sample), "step_hb": 2, "attn_bb": 2,
    }


def _layer(xp, xs, mem_p, cache_k, cache_v, st_pool, st_shift, st_rwkv, lp):
    batch, seq, d = xp.shape
    ns = xs.shape[0]
    n_p = batch * seq
    m_all = n_p + ns
    pw = lp["pool_scale"].shape[-1]
    rw = lp["rwkv_w0"].shape[-1]
    heads, hd = lp["rwkv_r_k"].shape
    lora = lp["rwkv_w2"].shape[0]
    sw = lp["rwkv_mu"].shape[-1]
    n_mem, mh, md = cache_k.shape[1:]
    mw = mh * md
    tl = _tiles(batch, seq, ns)

    lo0, lo1 = 2 * pw + 3 * rw, 2 * pw + sw
    lw = 2 * _LANES
    assert 2 * lora <= lw
    q_col0 = lo1 + rw
    gate_col0 = q_col0 + 2 * mw

    x_all = jnp.concatenate([xp.reshape(n_p, d), xs.reshape(ns, d)], axis=0).astype(_BF16)
    z = _in_proj(x_all, jnp.swapaxes(lp["w_in"], 0, 1), tm=tl["in_tm"], tn=tl["in_tn"])
    z_s = z[n_p:]

    pool_w = lp["pool_w"].astype(_BF16)
    pool_scale = lp["pool_scale"].reshape(1, pw)
    po_p = _pool_prompt(z, pool_w, pool_scale, batch=batch, seq=seq, width=pw, tb=tl["row_tb"])
    po_s = _pool_sample(z_s, jnp.swapaxes(st_pool, 0, 1), pool_w, pool_scale, row0=0, bb=tl["pool_bb"])
    nbuf = st_pool.shape[1]
    new_pool_p = jnp.stack([z[(b + 1) * seq - nbuf:(b + 1) * seq, :pw] for b in range(batch)])
    new_pool_s = jnp.concatenate([st_pool[:, 1:, :], z_s[:, :pw].reshape(ns, 1, pw)], axis=1)

    mu = lp["rwkv_mu"]
    pad_l = lambda a: jnp.pad(a, ((0, 0), (0, lw - 2 * lora)))
    w2p = jnp.zeros((lw, rw), _F32).at[:lora].set(lp["rwkv_w2"]).astype(_BF16)
    a2p = jnp.zeros((lw, rw), _F32).at[lora:2 * lora].set(lp["rwkv_a2"]).astype(_BF16)
    row = lambda a: a.reshape(1, -1)
    mix_consts = (row(mu[:rw]), row(mu[rw:2 * rw]), row(mu[2 * rw:3 * rw]), pad_l(row(mu[3 * rw:])),
                  row(lp["rwkv_w0"]), row(lp["rwkv_a0"]), row(lp["rwkv_k_k"]), row(lp["rwkv_k_a"]), w2p, a2p)
    post_consts = (row(lp["rwkv_r_k"]), row(lp["rwkv_ln_w"]), row(lp["rwkv_ln_b"]))

    ro_p, st_pairs = _rwkv_prompt(z, mix_consts, post_consts, batch=batch, seq=seq, width=rw, tb=tl["scan_tb"],
                                  pp=tl["scan_pp"], head_dim=hd, col0=2 * pw, gate_col0=lo1, lw=lw)
    st5 = st_pairs.reshape(batch, heads // 2, 2, hd, 2, hd)
    new_state_p = jnp.stack([st5[:, :, 0, :, 0, :], st5[:, :, 1, :, 1, :]], axis=2).reshape(batch, heads, hd, hd)

    shift_s = st_shift.reshape(ns, sw)
    prep_s = _rwkv_prep_sample(z_s, shift_s[:, :3 * rw], pad_l(shift_s[:, 3 * rw:]), mix_consts, row0=0, rows=ns,
                               width=rw, head_dim=hd, col0=2 * pw, lw=lw)
    r_s, _, k_s, v_s, _, _ = prep_s[:6]
    y_t, state_t = _rwkv_step(prep_s[6:], jnp.transpose(st_rwkv, (1, 2, 3, 0)), hb=tl["step_hb"])
    new_state_s = jnp.transpose(state_t, (3, 0, 1, 2))
    ro_s = _rwkv_post(y_t.reshape(rw, ns).T, r_s, k_s, v_s, z_s, post_consts, row0=0, rows=ns, width=rw, tb=ns,
                      head_dim=hd, col0=lo1)
    new_shift_p = jnp.stack([z[(b + 1) * seq - 1:(b + 1) * seq, 2 * pw:lo1] for b in range(batch)])
    new_shift_s = z_s[:, 2 * pw:lo1].reshape(ns, 1, sw)

    mkv = _matmul(mem_p.reshape(batch * n_mem, d).astype(_BF16), lp["w_mem_kv"].astype(_BF16),
                  tm=min(512, batch * n_mem), tn=min(1024, 2 * mw), out_dtype=_F32, name="mem_kv_proj")
    mo_p = _mem_attn_prompt(z, mkv, batch=batch, seq=seq, n_mem=n_mem, heads=mh, head_dim=md, tq=tl["attn_tq"],
                            col0=q_col0)
    mo_s = _mem_attn_sample(z_s[:, q_col0:q_col0 + mw].reshape(ns, mh, md),
                            z_s[:, q_col0 + mw:q_col0 + 2 * mw].reshape(ns, mh, md),
                            cache_k, cache_v, bb=tl["attn_bb"]).reshape(ns, mw).astype(_BF16)
    mk_p = mkv[:, :mw].reshape(batch, n_mem, mh, md)
    mv_p = mkv[:, mw:].reshape(batch, n_mem, mh, md)

    wp, wr, wm = lp["w_branch_pool"], lp["w_branch_rwkv"], lp["w_branch_mem"]
    w_out = lp["w_out"].astype(_BF16)
    b_gate = row(lp["b_gate"])
    ln_g, ln_b = row(lp["ln_g"]), row(lp["ln_b"])
    alpha = lp["alpha"]
    outs = []
    for po, ro, mo, x2d, zz in ((po_p, ro_p, mo_p, xp.reshape(n_p, d), z), (po_s, ro_s, mo_s, xs.reshape(ns, d), z_s)):
        rows = x2d.shape[0]
        h = _branch(po, ro, mo, wp, wr, wm, zz, b_gate, row0=0, tm=min(tl["proj_tm"], rows), tn=tl["proj_tn"],
                    gate_col0=gate_col0)
        outs.append(_out_proj(h, w_out, x2d, ln_g, ln_b, tm=min(tl["out_tm"], rows), alpha=alpha))
    y_prompt = outs[0].reshape(batch, seq, d)
    y_sample = outs[1].reshape(ns, 1, d)
    return (y_prompt, y_sample, mk_p, mv_p, new_pool_p, new_shift_p, new_state_p, new_pool_s, new_shift_s,
            new_state_s)


def kernel(x_prompt, x_sample, cache_mem_k, cache_mem_v, state_pool, state_shift, state_rwkv, mem_prompt, w_in,
           b_gate, pool_w, pool_scale, rwkv_mu, rwkv_w0, rwkv_w2, rwkv_a0, rwkv_a2, rwkv_k_k, rwkv_k_a, rwkv_r_k,
           rwkv_ln_w, rwkv_ln_b, w_mem_kv, w_branch_pool, w_branch_rwkv, w_branch_mem, w_out, ln_g, ln_b):
    depth = w_in.shape[0]
    alpha = (2.0 * depth) ** 0.25
    weights = dict(w_in=w_in, b_gate=b_gate, pool_w=pool_w, pool_scale=pool_scale, rwkv_mu=rwkv_mu, rwkv_w0=rwkv_w0,
                   rwkv_w2=rwkv_w2, rwkv_a0=rwkv_a0, rwkv_a2=rwkv_a2, rwkv_k_k=rwkv_k_k, rwkv_k_a=rwkv_k_a,
                   rwkv_r_k=rwkv_r_k, rwkv_ln_w=rwkv_ln_w, rwkv_ln_b=rwkv_ln_b, w_mem_kv=w_mem_kv,
                   w_branch_pool=w_branch_pool, w_branch_rwkv=w_branch_rwkv, w_branch_mem=w_branch_mem,
                   w_out=w_out, ln_g=ln_g, ln_b=ln_b)
    yp, ys = x_prompt, x_sample
    per_layer = []
    for l in range(depth):
        lp = {k: v[l] for k, v in weights.items()}
        lp["alpha"] = alpha
        res = _layer(yp, ys, mem_prompt, cache_mem_k[l], cache_mem_v[l], state_pool[l], state_shift[l],
                     state_rwkv[l], lp)
        yp, ys = res[0], res[1]
        per_layer.append(res[2:])
    if depth == 1:
        stacked = tuple(a[None] for a in per_layer[0])
    else:
        stacked = tuple(jnp.stack([res[i] for res in per_layer]) for i in range(len(per_layer[0])))
    return (yp, ys) + stacked
```

```python
import functools

import jax
import jax.numpy as jnp
from jax import lax
from jax.experimental import pallas as pl
from jax.experimental.pallas import tpu as pltpu

_F32 = jnp.float32
_BF16 = jnp.bfloat16

_POOL_WINDOWS = (2, 4, 8, 16)
_PAST_LEN = 16384
_GN_EPS = 64e-5
_LN_EPS = 1e-5
_L2_EPS = 1e-12
_EXP_MINUS_HALF = 0.6065306597126334

_LANES = 128
_MIB = 1024 * 1024
_VMEM_LIMIT = 56 * _MIB
_VMEM_LIMIT_RESIDENT_WEIGHT = 62 * _MIB

_HEAD_PAIR = _LANES
_CHUNK = 64


def _params(sem, vmem_limit=_VMEM_LIMIT):
    return pltpu.CompilerParams(dimension_semantics=sem, vmem_limit_bytes=vmem_limit)


def _dot(a, b):
    return jnp.dot(a, b, preferred_element_type=_F32)


def _dot_nt(a, b):
    return lax.dot_general(a, b, (((1,), (1,)), ((), ())), preferred_element_type=_F32)


def _dot_tn(a, b):
    return lax.dot_general(a, b, (((0,), (0,)), ((), ())), preferred_element_type=_F32)


def _split(x):
    hi = x.astype(_BF16)
    lo = (x - hi.astype(_F32)).astype(_BF16)
    return hi, lo


def _mm1(dot, a, b):
    return dot(a.astype(_BF16), b.astype(_BF16))


def _mm_exact_rhs(a, b_bf16):
    a_hi, a_lo = _split(a)
    return _dot(a_hi, b_bf16) + _dot(a_lo, b_bf16)


def _silu(x):
    return x * jax.nn.sigmoid(x)


def _head_block_matrix(value, head_dim):
    r = lax.broadcasted_iota(jnp.int32, (_LANES, _LANES), 0) // head_dim
    c = lax.broadcasted_iota(jnp.int32, (_LANES, _LANES), 1) // head_dim
    return jnp.where(r == c, value, 0.0).astype(_BF16)


def _matmul_kernel(x_ref, w_ref, o_ref):
    o_ref[...] = _dot(x_ref[...], w_ref[...]).astype(o_ref.dtype)


def _matmul(x, w, *, tm, tn, out_dtype, name):
    m, k = x.shape
    n = w.shape[1]
    return pl.pallas_call(
        _matmul_kernel,
        grid=(m // tm, n // tn),
        in_specs=[pl.BlockSpec((tm, k), lambda i, j: (i, 0)), pl.BlockSpec((k, tn), lambda i, j: (0, j))],
        out_specs=pl.BlockSpec((tm, tn), lambda i, j: (i, j)),
        out_shape=jax.ShapeDtypeStruct((m, n), out_dtype),
        compiler_params=_params(("parallel", "parallel")),
        name=name,
    )(x, w)


def _in_proj_kernel(x_ref, wt_ref, o_ref, wb_ref):
    @pl.when(pl.program_id(1) == 0)
    def _():
        wb_ref[...] = wt_ref[...].astype(_BF16)

    o_ref[...] = _dot_nt(x_ref[...], wb_ref[...])


def _in_proj(x, wt, *, tm, tn):
    m, k = x.shape
    n = wt.shape[0]
    return pl.pallas_call(
        _in_proj_kernel,
        grid=(pl.cdiv(n, tn), m // tm),
        in_specs=[pl.BlockSpec((tm, k), lambda j, i: (i, 0)), pl.BlockSpec((tn, k), lambda j, i: (j, 0))],
        out_specs=pl.BlockSpec((tm, tn), lambda j, i: (i, j)),
        out_shape=jax.ShapeDtypeStruct((m, n), _F32),
        scratch_shapes=[pltpu.VMEM((tn, k), _BF16)],
        compiler_params=_params(("parallel", "arbitrary"), vmem_limit=_VMEM_LIMIT_RESIDENT_WEIGHT),
        name="in_proj",
    )(x, wt)


def _pool_mix(pooled_fn, zp_ref, pw_ref, ps_ref, o_ref, group):
    for g in range(len(_POOL_WINDOWS)):
        cols = slice(g * group, (g + 1) * group)
        mixed = _dot(pooled_fn(g, cols).astype(_BF16), pw_ref[g])
        o_ref[:, cols] = (mixed * ps_ref[:, cols] * _silu(zp_ref[:, cols])).astype(o_ref.dtype)


def _pool_prompt_kernel(u_ref, halo_ref, zp_ref, pw_ref, ps_ref, o_ref, e_ref, *, tb, hist, group):
    t = pl.program_id(1)
    e_ref[0:hist, :] = jnp.where(t == 0, 0.0, halo_ref[...])
    e_ref[hist:hist + tb, :] = u_ref[...]
    pos = t * tb + lax.broadcasted_iota(jnp.int32, (tb, group), 0)

    def window_sum(cols, win):
        if win < 8:
            acc = e_ref[hist:hist + tb, cols]
            for d in range(1, win):
                acc = acc + e_ref[hist - d:hist - d + tb, cols]
            return acc
        assert win & (win - 1) == 0
        s, span = e_ref[:, cols], 1
        while span < win:
            s = s[span:] + s[:-span]
            span *= 2
        first = hist - (win - 1)
        return s[first:first + tb]

    def pooled(g, cols):
        win = _POOL_WINDOWS[g]
        x = e_ref[hist:hist + tb, cols]
        cnt = jnp.minimum(pos + 1, win).astype(_F32)
        return window_sum(cols, win) / cnt - x

    _pool_mix(pooled, zp_ref, pw_ref, ps_ref, o_ref, group)


def _pool_prompt(z, pool_w, pool_scale, *, batch, seq, width, tb):
    hist = 16
    n_t = seq // tb
    group = width // len(_POOL_WINDOWS)
    kern = functools.partial(_pool_prompt_kernel, tb=tb, hist=hist, group=group)
    return pl.pallas_call(
        kern,
        grid=(batch, n_t),
        in_specs=[
            pl.BlockSpec((tb, width), lambda b, t: (b * n_t + t, 0)),
            pl.BlockSpec((hist, width), lambda b, t: (jnp.maximum((b * seq + t * tb) // hist - 1, 0), 0)),
            pl.BlockSpec((tb, width), lambda b, t: (b * n_t + t, 1)),
            pl.BlockSpec(pool_w.shape, lambda b, t: (0, 0, 0)),
            pl.BlockSpec((1, width), lambda b, t: (0, 0)),
        ],
        out_specs=pl.BlockSpec((tb, width), lambda b, t: (b * n_t + t, 0)),
        out_shape=jax.ShapeDtypeStruct((batch * seq, width), _BF16),
        scratch_shapes=[pltpu.VMEM((hist + tb, width), _F32)],
        compiler_params=_params(("parallel", "parallel")),
        name="pool_prompt",
    )(z, z, z, pool_w, pool_scale)


def _pool_sample_kernel(u_ref, buf_ref, zp_ref, pw_ref, ps_ref, o_ref, *, nbuf, group):
    def pooled(g, cols):
        win = _POOL_WINDOWS[g]
        x = u_ref[:, cols]
        acc = x
        for d in range(1, win):
            acc = acc + buf_ref[nbuf - d, :, cols]
        cnt = float(min(_PAST_LEN + 1, win))
        return acc / cnt - x

    _pool_mix(pooled, zp_ref, pw_ref, ps_ref, o_ref, group)


def _pool_sample(z, buf, pool_w, pool_scale, *, row0, bb):
    nbuf, rows, width = buf.shape
    group = width // len(_POOL_WINDOWS)
    off = row0 // bb
    kern = functools.partial(_pool_sample_kernel, nbuf=nbuf, group=group)
    return pl.pallas_call(
        kern,
        grid=(rows // bb,),
        in_specs=[
            pl.BlockSpec((bb, width), lambda i: (off + i, 0)),
            pl.BlockSpec((nbuf, bb, width), lambda i: (0, i, 0)),
            pl.BlockSpec((bb, width), lambda i: (off + i, 1)),
            pl.BlockSpec(pool_w.shape, lambda i: (0, 0, 0)),
            pl.BlockSpec((1, width), lambda i: (0, 0)),
        ],
        out_specs=pl.BlockSpec((bb, width), lambda i: (i, 0)),
        out_shape=jax.ShapeDtypeStruct((rows, width), _BF16),
        compiler_params=_params(("parallel",)),
        name="pool_sample",
    )(z, buf, z, pool_w, pool_scale)


def _rwkv_mix(cur, prev, mu, w0, a0, kk_scale, ka, w2p, a2p, head_dim, outs):
    o_r, o_ld, o_k, o_v, o_kk, o_be = outs
    mix = lambda n: cur[n] + (prev[n] - cur[n]) * mu[n]
    xl = mix("l")
    wl = w0 + _mm1(_dot, jnp.tanh(xl), w2p)
    a = jax.nn.sigmoid(a0 + _mm1(_dot, xl, a2p))
    xk = mix("k")
    kkp = xk * kk_scale
    ones_bd = _head_block_matrix(1.0, head_dim)
    width = kkp.shape[1]
    o_r[...] = mix("r")
    o_v[...] = mix("v")
    o_ld[...] = -_EXP_MINUS_HALF * jax.nn.sigmoid(wl)
    o_k[...] = xk * (1.0 + (a - 1.0) * ka)
    for p in range(width // _LANES):
        cols = slice(p * _LANES, (p + 1) * _LANES)
        kp = kkp[:, cols]
        ss = _mm_exact_rhs(kp * kp, ones_bd)
        kk = kp * lax.rsqrt(jnp.maximum(ss, _L2_EPS * _L2_EPS))
        o_kk[:, cols] = kk
        o_be[:, cols] = kk * a[:, cols]


def _rwkv_prep_sample_kernel(zr, zk, zv, zl, pr, pk, pv, plr, mur, muk, muv, mul, w0, a0, kks, ka, w2p, a2p,
                             o_r, o_ld, o_k, o_v, o_kk, o_be, t_r, t_ld, t_k, t_v, t_kk, t_be, *, head_dim):
    cur = {"r": zr[...], "k": zk[...], "v": zv[...], "l": zl[...]}
    prev = {"r": pr[...], "k": pk[...], "v": pv[...], "l": plr[...]}
    mu = {"r": mur[...], "k": muk[...], "v": muv[...], "l": mul[...]}
    _rwkv_mix(cur, prev, mu, w0[...], a0[...], kks[...], ka[...], w2p[...], a2p[...], head_dim,
              (o_r, o_ld, o_k, o_v, o_kk, o_be))
    for o_ref, t_ref in ((o_r, t_r), (o_ld, t_ld), (o_k, t_k), (o_v, t_v), (o_kk, t_kk), (o_be, t_be)):
        t_ref[...] = o_ref[...].T


def _rwkv_prep_sample(z, prev_rkv, prev_l, consts, *, row0, rows, width, head_dim, col0, lw):
    c0 = col0 // width
    lcol = (col0 + 3 * width) // lw
    assert c0 * width == col0 and lcol * lw == col0 + 3 * width
    off = row0 // rows
    cblk = lambda c: pl.BlockSpec((rows, width), lambda i, c=c: (off, c))
    pblk = lambda c: pl.BlockSpec((rows, width), lambda i, c=c: (0, c))
    vec = lambda n: pl.BlockSpec((1, n), lambda i: (0, 0))
    mat = pl.BlockSpec((lw, width), lambda i: (0, 0))
    out_spec = pl.BlockSpec((rows, width), lambda i: (0, 0))
    out_shape = jax.ShapeDtypeStruct((rows, width), _F32)
    kern = functools.partial(_rwkv_prep_sample_kernel, head_dim=head_dim)
    return pl.pallas_call(
        kern,
        grid=(1,),
        in_specs=[cblk(c0), cblk(c0 + 1), cblk(c0 + 2), pl.BlockSpec((rows, lw), lambda i: (off, lcol)),
                  pblk(0), pblk(1), pblk(2), pl.BlockSpec((rows, lw), lambda i: (0, 0)),
                  vec(width), vec(width), vec(width), vec(lw),
                  vec(width), vec(width), vec(width), vec(width), mat, mat],
        out_specs=[out_spec] * 6 + [pl.BlockSpec((width, rows), lambda i: (0, 0))] * 6,
        out_shape=[out_shape] * 6 + [jax.ShapeDtypeStruct((width, rows), _F32)] * 6,
        compiler_params=_params(("arbitrary",)),
        name="rwkv_prep_sample",
    )(z, z, z, z, prev_rkv, prev_rkv, prev_rkv, prev_l, *consts)


def _scan_chunks(inputs, states, c):
    pairs = range(len(inputs))
    n2 = _LANES
    bf = lambda x: x.astype(_BF16)
    m0, m1 = c["m0"], c["m1"]
    expand = lambda x: jnp.concatenate([x * m0, x * m1], axis=0)
    stack = lambda a, b: jnp.concatenate([expand(a), expand(b)], axis=0)

    cum = [_mm_exact_rhs_t(c["ltri"], inputs[q][1]) for q in pairs]
    tot = [cum[q][_CHUNK - 1:_CHUNK, :] for q in pairs]
    xa, xb, bk, ve = [], [], [], []
    for q in pairs:
        r, ld, k2, v, kk, be = inputs[q]
        e_neg = jnp.exp(-cum[q])
        e_rem = jnp.exp(tot[q] - cum[q])
        xa.append(bf(stack(-kk * jnp.exp(cum[q] - ld), r * jnp.exp(cum[q]))))
        xb.append(bf(stack(be * e_neg, k2 * e_neg)))
        bk.append(bf(stack(be * e_rem, k2 * e_rem)))
        ve.append(bf(expand(v)))
    g = [_dot_nt(xa[q], xb[q]) for q in pairs]
    ps = [_dot_nt(xa[q], bf(states[q])) for q in pairs]
    a_kk = [bf(jnp.concatenate([jnp.where(c["strict"], g[q][:n2, n2:], 0.0),
                                jnp.where(c["incl"], g[q][n2:, n2:], 0.0)], axis=0)) for q in pairs]
    av = [_dot(a_kk[q], ve[q]) for q in pairs]
    npow = [jnp.where(c["strict"], g[q][:n2, :n2], 0.0) for q in pairs]
    u = [ps[q][:n2] + av[q][:n2] for q in pairs]
    for k in range(c["squarings"]):
        out = [_dot(bf(npow[q]), bf(jnp.concatenate([npow[q], u[q]], axis=1))) for q in pairs]
        npow = [out[q][:, :n2] for q in pairs]
        u = [u[q] + out[q][:, n2:] for q in pairs]
    ub = [bf(u[q]) for q in pairs]
    u = [u[q] + _dot(bf(npow[q]), ub[q]) for q in pairs]
    ub = [bf(u[q]) for q in pairs]
    a_rb = [bf(jnp.where(c["incl"], g[q][n2:, :n2], 0.0)) for q in pairs]
    ye = [ps[q][n2:] + av[q][n2:] + _dot(a_rb[q], ub[q]) for q in pairs]
    ys = [ye[q][:_CHUNK] + ye[q][_CHUNK:] for q in pairs]
    new_states = [states[q] * jnp.exp(tot[q]) + _dot_tn(jnp.concatenate([ub[q], ve[q]], axis=0), bk[q])
                  for q in pairs]
    return ys, new_states


def _mm_exact_rhs_t(l_bf16, x):
    x_hi, x_lo = _split(x)
    return _dot(l_bf16, x_hi) + _dot(l_bf16, x_lo)


def _scan_consts():
    n = _LANES
    half = _CHUNK
    ri = lax.broadcasted_iota(jnp.int32, (n, n), 0)
    ci = lax.broadcasted_iota(jnp.int32, (n, n), 1)
    same = (ri // half) == (ci // half)
    lane = lax.broadcasted_iota(jnp.int32, (half, n), 1)
    tr = lax.broadcasted_iota(jnp.int32, (half, half), 0)
    tc = lax.broadcasted_iota(jnp.int32, (half, half), 1)
    squarings = 0
    while (2 << squarings) < half:
        squarings += 1
    return {
        "strict": same & ((ri % half) > (ci % half)),
        "incl": same & ((ri % half) >= (ci % half)),
        "m0": jnp.where(lane < half, 1.0, 0.0).astype(_F32),
        "m1": jnp.where(lane >= half, 1.0, 0.0).astype(_F32),
        "ltri": jnp.where(tr >= tc, 1.0, 0.0).astype(_BF16),
        "squarings": squarings,
    }


def _rwkv_prompt_kernel(zr, zk, zv, zl, hr, hk, hv, hl, za, zb, mur, muk, muv, mul, w0, a0, kks, ka, w2p, a2p,
                        rk, lnw, lnb, o_ref, s_ref, er, ek, ev, el, r_s, ld_s, k_s, v_s, kk_s, be_s, y_s, xl_s, st_ref,
                        *, tb, pp, head_dim, delta):
    t = pl.program_id(2)
    halo = 8
    n_chunks = tb // _CHUNK
    pairs = range(pp)
    cols = [slice(q * _LANES, (q + 1) * _LANES) for q in pairs]

    @pl.when(t == 0)
    def _():
        st_ref[...] = jnp.zeros_like(st_ref)

    for z_ref, h_ref, e_ref, mu_ref, dst in ((zr, hr, er, mur, r_s), (zk, hk, ek, muk, k_s), (zv, hv, ev, muv, v_s),
                                             (zl, hl, el, mul, xl_s)):
        e_ref[0:halo, :] = jnp.where(t == 0, 0.0, h_ref[...])
        e_ref[halo:halo + tb, :] = z_ref[...]
        cur = z_ref[...]
        dst[...] = cur + (e_ref[halo - 1:halo - 1 + tb, :] - cur) * mu_ref[...]

    c = _scan_consts()
    ones_bd = _head_block_matrix(1.0, head_dim)
    avg_bd = _head_block_matrix(1.0 / head_dim, head_dim)

    xl = xl_s[...]
    wl = w0[...] + _mm1(_dot, jnp.tanh(xl), w2p[...])
    a = jax.nn.sigmoid(a0[...] + _mm1(_dot, xl, a2p[...]))
    ld_s[...] = -_EXP_MINUS_HALF * jax.nn.sigmoid(wl)
    xk = k_s[...]
    k_s[...] = xk * (1.0 + (a - 1.0) * ka[...])
    kp = [xk[:, cols[q]] * kks[:, cols[q]] for q in pairs]
    ss = [_mm_exact_rhs(kp[q] * kp[q], ones_bd) for q in pairs]
    for q in pairs:
        kk = kp[q] * lax.rsqrt(jnp.maximum(ss[q], _L2_EPS * _L2_EPS))
        kk_s[:, cols[q]] = kk
        be_s[:, cols[q]] = kk * a[:, cols[q]]

    def body(ci, carry):
        rows = pl.ds(pl.multiple_of(ci * _CHUNK, _CHUNK), _CHUNK)
        in_refs = (r_s, ld_s, k_s, v_s, kk_s, be_s)
        ys, new_states = _scan_chunks([tuple(ref[rows, cols[q]] for ref in in_refs) for q in pairs],
                                      [st_ref[q] for q in pairs], c)
        for q in pairs:
            y_s[rows, cols[q]] = ys[q]
            st_ref[q] = new_states[q]
        return carry

    lax.fori_loop(0, n_chunks, body, 0)

    q0, off = delta // _LANES, delta % _LANES
    n_a = za.shape[1] // _LANES
    zblk = lambda i: (za[:, i * _LANES:(i + 1) * _LANES] if i < n_a
                      else zb[:, (i - n_a) * _LANES:(i - n_a + 1) * _LANES])
    y = [y_s[:, cols[q]] for q in pairs]
    d = [y[q] - _mm_exact_rhs(y[q], avg_bd) for q in pairs]
    var = [_mm_exact_rhs(d[q] * d[q], avg_bd) for q in pairs]
    bsum = [_mm_exact_rhs(r_s[:, cols[q]] * k_s[:, cols[q]] * rk[:, cols[q]], ones_bd) for q in pairs]
    for q in pairs:
        yn = d[q] * lax.rsqrt(var[q] + _GN_EPS) * lnw[:, cols[q]] + lnb[:, cols[q]]
        zg = jnp.concatenate([zblk(q0 + q), zblk(q0 + q + 1)], axis=1)[:, off:off + _LANES]
        o_ref[:, cols[q]] = ((yn + bsum[q] * v_s[:, cols[q]]) * _silu(zg)).astype(o_ref.dtype)

    @pl.when(t == pl.num_programs(2) - 1)
    def _():
        s_ref[0] = st_ref[...]


def _rwkv_prompt(z, mix_consts, post_consts, *, batch, seq, width, tb, pp, head_dim, col0, gate_col0, lw):
    n_t = seq // tb
    gw = pp * _LANES
    halo = 8
    tail = 2 * _LANES
    c0, lcol, cg = col0 // gw, (col0 + 3 * width) // lw, gate_col0 // gw
    delta = gate_col0 % gw
    assert c0 * gw == col0 and width % gw == 0 and lcol * lw == col0 + 3 * width
    assert delta <= tail and delta % _LANES + _LANES <= tail and gw % tail == 0
    row = lambda b, g, t: b * n_t + t
    hrow = lambda b, g, t: jnp.maximum((b * seq + t * tb) // halo - 1, 0)
    step = width // gw
    zblk = lambda i: pl.BlockSpec((tb, gw), lambda b, g, t, i=i: (row(b, g, t), c0 + i * step + g))
    hblk = lambda i: pl.BlockSpec((halo, gw), lambda b, g, t, i=i: (hrow(b, g, t), c0 + i * step + g))
    vec = pl.BlockSpec((1, gw), lambda b, g, t: (0, g))
    lvec = pl.BlockSpec((1, lw), lambda b, g, t: (0, 0))
    mat = pl.BlockSpec((lw, gw), lambda b, g, t: (0, g))
    mur, muk, muv, mul, w0, a0, kks, ka, w2p, a2p = mix_consts
    big = lambda: pltpu.VMEM((tb, gw), _F32)
    kern = functools.partial(_rwkv_prompt_kernel, tb=tb, pp=pp, head_dim=head_dim, delta=delta)
    return pl.pallas_call(
        kern,
        grid=(batch, width // gw, n_t),
        in_specs=[zblk(0), zblk(1), zblk(2), pl.BlockSpec((tb, lw), lambda b, g, t: (row(b, g, t), lcol)),
                  hblk(0), hblk(1), hblk(2), pl.BlockSpec((halo, lw), lambda b, g, t: (hrow(b, g, t), lcol)),
                  pl.BlockSpec((tb, gw), lambda b, g, t: (row(b, g, t), cg + g)),
                  pl.BlockSpec((tb, tail), lambda b, g, t: (row(b, g, t), (cg + g + 1) * (gw // tail))),
                  vec, vec, vec, lvec, vec, vec, vec, vec, mat, mat, vec, vec, vec],
        out_specs=[pl.BlockSpec((tb, gw), lambda b, g, t: (row(b, g, t), g)),
                   pl.BlockSpec((1, pp, _LANES, _LANES), lambda b, g, t: (b, g, 0, 0))],
        out_shape=[jax.ShapeDtypeStruct((batch * seq, width), _BF16),
                   jax.ShapeDtypeStruct((batch, width // _LANES, _LANES, _LANES), _F32)],
        scratch_shapes=[pltpu.VMEM((halo + tb, gw), _F32)] * 3 + [pltpu.VMEM((halo + tb, lw), _F32)]
                       + [big() for _ in range(7)] + [pltpu.VMEM((tb, lw), _F32),
                                                      pltpu.VMEM((pp, _LANES, _LANES), _F32)],
        compiler_params=_params(("parallel", "parallel", "arbitrary")),
        name="rwkv_prompt",
    )(z, z, z, z, z, z, z, z, z, z, mur, muk, muv, mul, w0, a0, kks, ka, w2p, a2p, *post_consts)


def _rwkv_step_kernel(r_ref, ld_ref, k_ref, v_ref, kk_ref, be_ref, s_ref, y_ref, so_ref, *, hb, head_dim, vi):
    for h in range(hb):
        kk = kk_ref[h][None]
        dec = jnp.exp(ld_ref[h])[None]
        be = be_ref[h][None]
        k = k_ref[h][None]
        r = r_ref[h][None]

        def body(c, carry, h=h, kk=kk, dec=dec, be=be, k=k, r=r):
            rows = pl.ds(pl.multiple_of(c * vi, vi), vi)
            s = s_ref[h, rows]
            sa = -jnp.sum(s * kk, axis=1, keepdims=True)
            s_new = s * dec + sa * be + v_ref[h, rows] * k
            y_ref[h, rows] = jnp.sum(s_new * r, axis=1, keepdims=True)
            so_ref[h, rows] = s_new
            return carry

        lax.fori_loop(0, head_dim // vi, body, 0)


def _rwkv_step(arrs_t, state_t, *, hb):
    heads, hd, _, n = state_t.shape
    r, ld, k, v, kk, be = arrs_t
    per_key = lambda a: a.reshape(heads, hd, n)
    vec = pl.BlockSpec((hb, hd, n), lambda i: (i, 0, 0))
    col = pl.BlockSpec((hb, hd, 1, n), lambda i: (i, 0, 0, 0))
    mat = pl.BlockSpec((hb, hd, hd, n), lambda i: (i, 0, 0, 0))
    kern = functools.partial(_rwkv_step_kernel, hb=hb, head_dim=hd, vi=8)
    return pl.pallas_call(
        kern,
        grid=(heads // hb,),
        in_specs=[vec, vec, vec, col, vec, vec, mat],
        out_specs=[col, mat],
        out_shape=[jax.ShapeDtypeStruct((heads, hd, 1, n), _F32), jax.ShapeDtypeStruct(state_t.shape, _F32)],
        compiler_params=_params(("parallel",)),
        name="rwkv_step",
    )(per_key(r), per_key(ld), per_key(k), v.reshape(heads, hd, 1, n), per_key(kk), per_key(be), state_t)


def _rwkv_post_kernel(y_ref, r_ref, k_ref, v_ref, za_ref, zb_ref, rk_ref, lw_ref, lb_ref, o_ref, *, head_dim, delta):
    avg_bd = _head_block_matrix(1.0 / head_dim, head_dim)
    ones_bd = _head_block_matrix(1.0, head_dim)
    width = y_ref.shape[1]
    zwin = jnp.concatenate([za_ref[...], zb_ref[...]], axis=1)
    for p in range(width // _LANES):
        cols = slice(p * _LANES, (p + 1) * _LANES)
        y = y_ref[:, cols]
        d = y - _mm_exact_rhs(y, avg_bd)
        var = _mm_exact_rhs(d * d, avg_bd)
        yn = d * lax.rsqrt(var + _GN_EPS) * lw_ref[:, cols] + lb_ref[:, cols]
        bonus = _mm_exact_rhs(r_ref[:, cols] * k_ref[:, cols] * rk_ref[:, cols], ones_bd) * v_ref[:, cols]
        zg = zwin[:, delta + p * _LANES:delta + (p + 1) * _LANES]
        o_ref[:, cols] = ((yn + bonus) * _silu(zg)).astype(o_ref.dtype)


def _rwkv_post(y, r, k2, v, z, consts, *, row0, rows, width, tb, head_dim, col0):
    off = row0 // tb
    tail = 2 * _LANES
    cg, delta = col0 // width, col0 % width
    assert delta <= tail and width % tail == 0
    blk = pl.BlockSpec((tb, width), lambda i: (i, 0))
    vec = pl.BlockSpec((1, width), lambda i: (0, 0))
    kern = functools.partial(_rwkv_post_kernel, head_dim=head_dim, delta=delta)
    return pl.pallas_call(
        kern,
        grid=(rows // tb,),
        in_specs=[blk, blk, blk, blk,
                  pl.BlockSpec((tb, width), lambda i: (off + i, cg)),
                  pl.BlockSpec((tb, tail), lambda i: (off + i, (cg + 1) * (width // tail))),
                  vec, vec, vec],
        out_specs=blk,
        out_shape=jax.ShapeDtypeStruct((rows, width), _BF16),
        compiler_params=_params(("parallel",)),
        name="rwkv_post",
    )(y, r, k2, v, z, z, *consts)


def _mem_attn_prompt_kernel(za_ref, zb_ref, zc_ref, k_ref, v_ref, o_ref, *, heads, head_dim, delta):
    scale = head_dim ** -0.5
    mw = heads * head_dim
    zwin = jnp.concatenate([za_ref[...], zb_ref[...], zc_ref[...]], axis=1)
    for h in range(heads):
        cols = slice(h * head_dim, (h + 1) * head_dim)
        q = zwin[:, delta + cols.start:delta + cols.stop]
        zg = zwin[:, delta + mw + cols.start:delta + mw + cols.stop]
        s = _dot_nt(q.astype(_BF16), k_ref[:, cols].astype(_BF16)) * scale
        p = jnp.exp(s - jnp.max(s, axis=-1, keepdims=True))
        o = _dot(p.astype(_BF16), v_ref[:, cols].astype(_BF16)) / jnp.sum(p, axis=-1, keepdims=True)
        o_ref[:, cols] = (o * _silu(zg)).astype(o_ref.dtype)


def _mem_attn_prompt(z, mkv, *, batch, seq, n_mem, heads, head_dim, tq, col0):
    n_t = seq // tq
    mw = heads * head_dim
    tail = 2 * _LANES
    q_col, delta = col0 // mw, col0 % mw
    assert delta <= tail and mw % tail == 0
    kern = functools.partial(_mem_attn_prompt_kernel, heads=heads, head_dim=head_dim, delta=delta)
    return pl.pallas_call(
        kern,
        grid=(batch, n_t),
        in_specs=[
            pl.BlockSpec((tq, mw), lambda b, t: (b * n_t + t, q_col)),
            pl.BlockSpec((tq, mw), lambda b, t: (b * n_t + t, q_col + 1)),
            pl.BlockSpec((tq, tail), lambda b, t: (b * n_t + t, (q_col + 2) * (mw // tail))),
            pl.BlockSpec((n_mem, mw), lambda b, t: (b, 0)),
            pl.BlockSpec((n_mem, mw), lambda b, t: (b, 1)),
        ],
        out_specs=pl.BlockSpec((tq, mw), lambda b, t: (b * n_t + t, 0)),
        out_shape=jax.ShapeDtypeStruct((batch * seq, mw), _BF16),
        compiler_params=_params(("parallel", "parallel")),
        name="mem_attn_prompt",
    )(z, z, z, mkv, mkv)


def _mem_attn_sample_kernel(q_ref, zg_ref, k_ref, v_ref, o_ref, *, bb, head_dim):
    scale = head_dim ** -0.5
    for b in range(bb):
        s = jnp.sum(k_ref[b] * q_ref[b][None], axis=-1, keepdims=True) * scale
        p = jnp.exp(s - jnp.max(s, axis=0, keepdims=True))
        o = jnp.sum(p * v_ref[b], axis=0) / jnp.sum(p, axis=0)
        o_ref[b] = o * _silu(zg_ref[b])


def _mem_attn_sample(q, zg, mem_k, mem_v, *, bb):
    n, n_mem, heads, head_dim = mem_k.shape
    vec = pl.BlockSpec((bb, heads, head_dim), lambda i: (i, 0, 0))
    mat = pl.BlockSpec((bb, n_mem, heads, head_dim), lambda i: (i, 0, 0, 0))
    kern = functools.partial(_mem_attn_sample_kernel, bb=bb, head_dim=head_dim)
    return pl.pallas_call(
        kern,
        grid=(n // bb,),
        in_specs=[vec, vec, mat, mat],
        out_specs=vec,
        out_shape=jax.ShapeDtypeStruct((n, heads, head_dim), _F32),
        compiler_params=_params(("parallel",)),
        name="mem_attn_sample",
    )(q, zg, mem_k, mem_v)


def _branch_kernel(po_ref, ro_ref, mo_ref, wp_ref, wr_ref, wm_ref, gp_ref, gr_ref, gm_ref, tp_ref, tr_ref, tm_ref,
                   bp_ref, br_ref, bm_ref, o_ref, wpb_ref, wrb_ref, wmb_ref, *, delta):
    @pl.when(pl.program_id(1) == 0)
    def _():
        wpb_ref[...] = wp_ref[...].astype(_BF16)
        wrb_ref[...] = wr_ref[...].astype(_BF16)
        wmb_ref[...] = wm_ref[...].astype(_BF16)

    tn = o_ref.shape[1]

    def gate(g_ref, t_ref, b_ref):
        win = jnp.concatenate([g_ref[...], t_ref[...]], axis=1)
        return jax.nn.sigmoid(win[:, delta:delta + tn] + b_ref[...])

    h = gate(gp_ref, tp_ref, bp_ref) * _dot(po_ref[...], wpb_ref[...])
    h = h + gate(gr_ref, tr_ref, br_ref) * _dot(ro_ref[...], wrb_ref[...])
    h = h + gate(gm_ref, tm_ref, bm_ref) * _dot(mo_ref[...], wmb_ref[...])
    o_ref[...] = h.astype(o_ref.dtype)


def _branch(po, ro, mo, wp, wr, wm, z, b_gate, *, row0, tm, tn, gate_col0):
    rows = po.shape[0]
    d = wp.shape[1]
    off = row0 // tm
    tail = 2 * _LANES
    delta = gate_col0 % tn
    g0 = gate_col0 // tn
    assert delta <= tail and tn % tail == 0
    nb = d // tn
    act = lambda a: pl.BlockSpec((tm, a.shape[1]), lambda j, i: (i, 0))
    wgt = lambda w: pl.BlockSpec((w.shape[0], tn), lambda j, i: (0, j), pipeline_mode=pl.Buffered(1))
    gat = lambda k: pl.BlockSpec((tm, tn), lambda j, i, k=k: (off + i, g0 + k * nb + j))
    tai = lambda k: pl.BlockSpec((tm, tail), lambda j, i, k=k: (off + i, (g0 + k * nb + j + 1) * (tn // tail)))
    bia = lambda k: pl.BlockSpec((1, tn), lambda j, i, k=k: (0, k * nb + j))
    return pl.pallas_call(
        functools.partial(_branch_kernel, delta=delta),
        grid=(nb, rows // tm),
        in_specs=[act(po), act(ro), act(mo), wgt(wp), wgt(wr), wgt(wm), gat(0), gat(1), gat(2),
                  tai(0), tai(1), tai(2), bia(0), bia(1), bia(2)],
        out_specs=pl.BlockSpec((tm, tn), lambda j, i: (i, j)),
        out_shape=jax.ShapeDtypeStruct((rows, d), _BF16),
        scratch_shapes=[pltpu.VMEM((w.shape[0], tn), _BF16) for w in (wp, wr, wm)],
        compiler_params=_params(("parallel", "arbitrary")),
        name="branch_proj",
    )(po, ro, mo, wp, wr, wm, z, z, z, z, z, z, b_gate, b_gate, b_gate)


def _out_kernel(h_ref, w_ref, x_ref, g_ref, b_ref, o_ref, *, alpha):
    xf = alpha * x_ref[...] + _dot(h_ref[...], w_ref[...])
    mu = jnp.mean(xf, axis=-1, keepdims=True)
    d = xf - mu
    var = jnp.mean(d * d, axis=-1, keepdims=True)
    o_ref[...] = d * lax.rsqrt(var + _LN_EPS) * g_ref[...] + b_ref[...]


def _out_proj(h, w_out, x, ln_g, ln_b, *, tm, alpha):
    rows, d = x.shape
    kern = functools.partial(_out_kernel, alpha=alpha)
    return pl.pallas_call(
        kern,
        grid=(rows // tm,),
        in_specs=[
            pl.BlockSpec((tm, d), lambda i: (i, 0)),
            pl.BlockSpec((d, d), lambda i: (0, 0), pipeline_mode=pl.Buffered(1)),
            pl.BlockSpec((tm, d), lambda i: (i, 0)),
            pl.BlockSpec((1, d), lambda i: (0, 0)),
            pl.BlockSpec((1, d), lambda i: (0, 0)),
        ],
        out_specs=pl.BlockSpec((tm, d), lambda i: (i, 0)),
        out_shape=jax.ShapeDtypeStruct((rows, d), _F32),
        compiler_params=_params(("parallel",), vmem_limit=_VMEM_LIMIT_RESIDENT_WEIGHT),
        name="out_proj_ln",
    )(h, w_out, x, ln_g, ln_b)


def _tiles(batch, seq, n_sample):
    n_prompt = batch * seq
    m_all = n_prompt + n_sample
    return {
        "in_tm": next(t for t in (1040, 832, 640, 512, 256, 128, 64, 32, 16) if m_all % t == 0), "in_tn": 768,
        "row_tb": min(512, seq),
        "scan_tb": min(512, seq), "scan_pp": 8,
        "attn_tq": min(1024, seq),
        "proj_tm": min(256, n_prompt), "proj_tn": 1024, "out_tm": min(256, n_prompt),
        "pool_bb": min(32, n_sample), "step_hb": 4, "attn_bb": 4,
    }


def _layer(xp, xs, mem_p, cache_k, cache_v, st_pool, st_shift, st_rwkv, lp):
    batch, seq, d = xp.shape
    ns = xs.shape[0]
    n_p = batch * seq
    m_all = n_p + ns
    pw = lp["pool_scale"].shape[-1]
    rw = lp["rwkv_w0"].shape[-1]
    heads, hd = lp["rwkv_r_k"].shape
    lora = lp["rwkv_w2"].shape[0]
    sw = lp["rwkv_mu"].shape[-1]
    n_mem, mh, md = cache_k.shape[1:]
    mw = mh * md
    tl = _tiles(batch, seq, ns)

    lo0, lo1 = 2 * pw + 3 * rw, 2 * pw + sw
    lw = 2 * _LANES
    assert 2 * lora <= lw
    q_col0 = lo1 + rw
    gate_col0 = q_col0 + 2 * mw

    x_all = jnp.concatenate([xp.reshape(n_p, d), xs.reshape(ns, d)], axis=0).astype(_BF16)
    z = _in_proj(x_all, jnp.swapaxes(lp["w_in"], 0, 1), tm=tl["in_tm"], tn=tl["in_tn"])
    z_s = z[n_p:]

    pool_w = lp["pool_w"].astype(_BF16)
    pool_scale = lp["pool_scale"].reshape(1, pw)
    po_p = _pool_prompt(z, pool_w, pool_scale, batch=batch, seq=seq, width=pw, tb=tl["row_tb"])
    po_s = _pool_sample(z_s, jnp.swapaxes(st_pool, 0, 1), pool_w, pool_scale, row0=0, bb=tl["pool_bb"])
    nbuf = st_pool.shape[1]
    new_pool_p = jnp.stack([z[(b + 1) * seq - nbuf:(b + 1) * seq, :pw] for b in range(batch)])
    new_pool_s = jnp.concatenate([st_pool[:, 1:, :], z_s[:, :pw].reshape(ns, 1, pw)], axis=1)

    mu = lp["rwkv_mu"]
    pad_l = lambda a: jnp.pad(a, ((0, 0), (0, lw - 2 * lora)))
    w2p = jnp.zeros((lw, rw), _F32).at[:lora].set(lp["rwkv_w2"]).astype(_BF16)
    a2p = jnp.zeros((lw, rw), _F32).at[lora:2 * lora].set(lp["rwkv_a2"]).astype(_BF16)
    row = lambda a: a.reshape(1, -1)
    mix_consts = (row(mu[:rw]), row(mu[rw:2 * rw]), row(mu[2 * rw:3 * rw]), pad_l(row(mu[3 * rw:])),
                  row(lp["rwkv_w0"]), row(lp["rwkv_a0"]), row(lp["rwkv_k_k"]), row(lp["rwkv_k_a"]), w2p, a2p)
    post_consts = (row(lp["rwkv_r_k"]), row(lp["rwkv_ln_w"]), row(lp["rwkv_ln_b"]))

    ro_p, st_pairs = _rwkv_prompt(z, mix_consts, post_consts, batch=batch, seq=seq, width=rw, tb=tl["scan_tb"],
                                  pp=tl["scan_pp"], head_dim=hd, col0=2 * pw, gate_col0=lo1, lw=lw)
    st5 = st_pairs.reshape(batch, heads // 2, 2, hd, 2, hd)
    new_state_p = jnp.stack([st5[:, :, 0, :, 0, :], st5[:, :, 1, :, 1, :]], axis=2).reshape(batch, heads, hd, hd)

    shift_s = st_shift.reshape(ns, sw)
    prep_s = _rwkv_prep_sample(z_s, shift_s[:, :3 * rw], pad_l(shift_s[:, 3 * rw:]), mix_consts, row0=0, rows=ns,
                               width=rw, head_dim=hd, col0=2 * pw, lw=lw)
    r_s, _, k_s, v_s, _, _ = prep_s[:6]
    y_t, state_t = _rwkv_step(prep_s[6:], jnp.transpose(st_rwkv, (1, 2, 3, 0)), hb=tl["step_hb"])
    new_state_s = jnp.transpose(state_t, (3, 0, 1, 2))
    ro_s = _rwkv_post(y_t.reshape(rw, ns).T, r_s, k_s, v_s, z_s, post_consts, row0=0, rows=ns, width=rw, tb=ns,
                      head_dim=hd, col0=lo1)
    new_shift_p = jnp.stack([z[(b + 1) * seq - 1:(b + 1) * seq, 2 * pw:lo1] for b in range(batch)])
    new_shift_s = z_s[:, 2 * pw:lo1].reshape(ns, 1, sw)

    mkv = _matmul(mem_p.reshape(batch * n_mem, d).astype(_BF16), lp["w_mem_kv"].astype(_BF16),
                  tm=min(512, batch * n_mem), tn=min(1024, 2 * mw), out_dtype=_F32, name="mem_kv_proj")
    mo_p = _mem_attn_prompt(z, mkv, batch=batch, seq=seq, n_mem=n_mem, heads=mh, head_dim=md, tq=tl["attn_tq"],
                            col0=q_col0)
    mo_s = _mem_attn_sample(z_s[:, q_col0:q_col0 + mw].reshape(ns, mh, md),
                            z_s[:, q_col0 + mw:q_col0 + 2 * mw].reshape(ns, mh, md),
                            cache_k, cache_v, bb=tl["attn_bb"]).reshape(ns, mw).astype(_BF16)
    mk_p = mkv[:, :mw].reshape(batch, n_mem, mh, md)
    mv_p = mkv[:, mw:].reshape(batch, n_mem, mh, md)

    wp, wr, wm = lp["w_branch_pool"], lp["w_branch_rwkv"], lp["w_branch_mem"]
    w_out = lp["w_out"].astype(_BF16)
    b_gate = row(lp["b_gate"])
    ln_g, ln_b = row(lp["ln_g"]), row(lp["ln_b"])
    alpha = lp["alpha"]
    outs = []
    for po, ro, mo, x2d, zz in ((po_p, ro_p, mo_p, xp.reshape(n_p, d), z), (po_s, ro_s, mo_s, xs.reshape(ns, d), z_s)):
        rows = x2d.shape[0]
        h = _branch(po, ro, mo, wp, wr, wm, zz, b_gate, row0=0, tm=min(tl["proj_tm"], rows), tn=tl["proj_tn"],
                    gate_col0=gate_col0)
        outs.append(_out_proj(h, w_out, x2d, ln_g, ln_b, tm=min(tl["out_tm"], rows), alpha=alpha))
    y_prompt = outs[0].reshape(batch, seq, d)
    y_sample = outs[1].reshape(ns, 1, d)
    return (y_prompt, y_sample, mk_p, mv_p, new_pool_p, new_shift_p, new_state_p, new_pool_s, new_shift_s,
            new_state_s)


def kernel(x_prompt, x_sample, cache_mem_k, cache_mem_v, state_pool, state_shift, state_rwkv, mem_prompt, w_in,
           b_gate, pool_w, pool_scale, rwkv_mu, rwkv_w0, rwkv_w2, rwkv_a0, rwkv_a2, rwkv_k_k, rwkv_k_a, rwkv_r_k,
           rwkv_ln_w, rwkv_ln_b, w_mem_kv, w_branch_pool, w_branch_rwkv, w_branch_mem, w_out, ln_g, ln_b):
    depth = w_in.shape[0]
    alpha = (2.0 * depth) ** 0.25
    weights = dict(w_in=w_in, b_gate=b_gate, pool_w=pool_w, pool_scale=pool_scale, rwkv_mu=rwkv_mu, rwkv_w0=rwkv_w0,
                   rwkv_w2=rwkv_w2, rwkv_a0=rwkv_a0, rwkv_a2=rwkv_a2, rwkv_k_k=rwkv_k_k, rwkv_k_a=rwkv_k_a,
                   rwkv_r_k=rwkv_r_k, rwkv_ln_w=rwkv_ln_w, rwkv_ln_b=rwkv_ln_b, w_mem_kv=w_mem_kv,
                   w_branch_pool=w_branch_pool, w_branch_rwkv=w_branch_rwkv, w_branch_mem=w_branch_mem,
                   w_out=w_out, ln_g=ln_g, ln_b=ln_b)
    yp, ys = x_prompt, x_sample
    per_layer = []
    for l in range(depth):
        lp = {k: v[l] for k, v in weights.items()}
        lp["alpha"] = alpha
        res = _layer(yp, ys, mem_prompt, cache_mem_k[l], cache_mem_v[l], state_pool[l], state_shift[l],
                     state_rwkv[l], lp)
        yp, ys = res[0], res[1]
        per_layer.append(res[2:])
    if depth == 1:
        stacked = tuple(a[None] for a in per_layer[0])
    else:
        stacked = tuple(jnp.stack([res[i] for res in per_layer]) for i in range(len(per_layer[0])))
    return (yp, ys) + stacked
```

```python
import functools

import jax
import jax.numpy as jnp
from jax import lax
from jax.experimental import pallas as pl
from jax.experimental.pallas import tpu as pltpu

_F32 = jnp.float32
_BF16 = jnp.bfloat16

_POOL_WINDOWS = (2, 4, 8, 16)
_PAST_LEN = 16384
_GN_EPS = 64e-5
_LN_EPS = 1e-5
_L2_EPS = 1e-12
_EXP_MINUS_HALF = 0.6065306597126334

_LANES = 128
_MIB = 1024 * 1024
_VMEM_LIMIT = 56 * _MIB
_VMEM_LIMIT_RESIDENT_WEIGHT = 62 * _MIB

_HEAD_PAIR = _LANES
_CHUNK = 64


def _params(sem, vmem_limit=_VMEM_LIMIT):
    return pltpu.CompilerParams(dimension_semantics=sem, vmem_limit_bytes=vmem_limit)


def _dot(a, b):
    return jnp.dot(a, b, preferred_element_type=_F32)


def _dot_nt(a, b):
    return lax.dot_general(a, b, (((1,), (1,)), ((), ())), preferred_element_type=_F32)


def _dot_tn(a, b):
    return lax.dot_general(a, b, (((0,), (0,)), ((), ())), preferred_element_type=_F32)


def _split(x):
    hi = x.astype(_BF16)
    lo = (x - hi.astype(_F32)).astype(_BF16)
    return hi, lo


def _mm1(dot, a, b):
    return dot(a.astype(_BF16), b.astype(_BF16))


def _mm_exact_rhs(a, b_bf16):
    a_hi, a_lo = _split(a)
    return _dot(a_hi, b_bf16) + _dot(a_lo, b_bf16)


def _silu(x):
    return x * jax.nn.sigmoid(x)


def _head_block_matrix(value, head_dim):
    r = lax.broadcasted_iota(jnp.int32, (_LANES, _LANES), 0) // head_dim
    c = lax.broadcasted_iota(jnp.int32, (_LANES, _LANES), 1) // head_dim
    return jnp.where(r == c, value, 0.0).astype(_BF16)


def _matmul_f32w_kernel(x_ref, w_ref, o_ref, wb_ref):
    @pl.when(pl.program_id(1) == 0)
    def _():
        wb_ref[...] = w_ref[...].astype(_BF16)

    o_ref[...] = _dot(x_ref[...], wb_ref[...])


def _matmul_f32w(x, w, *, tm, tn, name):
    m, k = x.shape
    n = w.shape[1]
    return pl.pallas_call(
        _matmul_f32w_kernel,
        grid=(n // tn, m // tm),
        in_specs=[pl.BlockSpec((tm, k), lambda j, i: (i, 0)), pl.BlockSpec((k, tn), lambda j, i: (0, j))],
        out_specs=pl.BlockSpec((tm, tn), lambda j, i: (i, j)),
        out_shape=jax.ShapeDtypeStruct((m, n), _F32),
        scratch_shapes=[pltpu.VMEM((k, tn), _BF16)],
        compiler_params=_params(("parallel", "arbitrary")),
        name=name,
    )(x, w)


def _cast_rows_kernel(xp_ref, xs_ref, o_ref, *, tail):
    i = pl.program_id(0)
    last = pl.num_programs(0) - 1
    tm = o_ref.shape[0]

    @pl.when(i < last)
    def _():
        o_ref[...] = xp_ref[...].astype(_BF16)

    @pl.when(i == last)
    def _():
        o_ref[0:tm - tail, :] = xp_ref[tail:tm, :].astype(_BF16)
        o_ref[tm - tail:tm, :] = xs_ref[...].astype(_BF16)


def _cast_rows(xp, xs, *, tm):
    n_p, d = xp.shape
    tail = xs.shape[0]
    n_tiles = (n_p + tail) // tm
    assert n_tiles * tm == n_p + tail and 0 < tail < tm <= n_p and (n_p - tm) % 8 == 0
    return pl.pallas_call(
        functools.partial(_cast_rows_kernel, tail=tail),
        grid=(n_tiles,),
        in_specs=[pl.BlockSpec((pl.Element(tm), pl.Element(d)),
                               lambda i: (pl.multiple_of(jnp.minimum(i * tm, n_p - tm), 8), 0)),
                  pl.BlockSpec((tail, d), lambda i: (0, 0))],
        out_specs=pl.BlockSpec((tm, d), lambda i: (i, 0)),
        out_shape=jax.ShapeDtypeStruct((n_p + tail, d), _BF16),
        compiler_params=_params(("parallel",), vmem_limit=_VMEM_LIMIT_RESIDENT_WEIGHT),
        name="cast_rows",
    )(xp, xs)


def _in_proj_kernel(x_ref, wt_ref, o_ref, wb_ref):
    @pl.when(pl.program_id(1) == 0)
    def _():
        wb_ref[...] = wt_ref[...].astype(_BF16)

    o_ref[...] = _dot_nt(x_ref[...], wb_ref[...])


def _in_proj(x, wt, *, tm, tn):
    m, k = x.shape
    n = wt.shape[0]
    return pl.pallas_call(
        _in_proj_kernel,
        grid=(pl.cdiv(n, tn), m // tm),
        in_specs=[pl.BlockSpec((tm, k), lambda j, i: (i, 0)), pl.BlockSpec((tn, k), lambda j, i: (j, 0))],
        out_specs=pl.BlockSpec((tm, tn), lambda j, i: (i, j)),
        out_shape=jax.ShapeDtypeStruct((m, n), _F32),
        scratch_shapes=[pltpu.VMEM((tn, k), _BF16)],
        compiler_params=_params(("parallel", "arbitrary"), vmem_limit=_VMEM_LIMIT_RESIDENT_WEIGHT),
        name="in_proj",
    )(x, wt)


def _pool_mix(pooled_fn, zp_ref, pw_ref, ps_ref, o_ref, group):
    for g in range(len(_POOL_WINDOWS)):
        cols = slice(g * group, (g + 1) * group)
        mixed = _dot(pooled_fn(g, cols).astype(_BF16), pw_ref[g])
        o_ref[:, cols] = (mixed * ps_ref[:, cols] * _silu(zp_ref[:, cols])).astype(o_ref.dtype)


def _pool_prompt_kernel(u_ref, halo_ref, zp_ref, pw_ref, ps_ref, o_ref, e_ref, *, tb, hist, group):
    t = pl.program_id(1)
    e_ref[0:hist, :] = jnp.where(t == 0, 0.0, halo_ref[...])
    e_ref[hist:hist + tb, :] = u_ref[...]
    pos = t * tb + lax.broadcasted_iota(jnp.int32, (tb, group), 0)

    def window_sum(cols, win):
        if win < 8:
            acc = e_ref[hist:hist + tb, cols]
            for d in range(1, win):
                acc = acc + e_ref[hist - d:hist - d + tb, cols]
            return acc
        assert win & (win - 1) == 0
        s, span = e_ref[:, cols], 1
        while span < win:
            s = s[span:] + s[:-span]
            span *= 2
        first = hist - (win - 1)
        return s[first:first + tb]

    def pooled(g, cols):
        win = _POOL_WINDOWS[g]
        x = e_ref[hist:hist + tb, cols]
        cnt = jnp.minimum(pos + 1, win).astype(_F32)
        return window_sum(cols, win) / cnt - x

    _pool_mix(pooled, zp_ref, pw_ref, ps_ref, o_ref, group)


def _pool_prompt(z, pool_w, pool_scale, *, batch, seq, width, tb):
    hist = 16
    n_t = seq // tb
    group = width // len(_POOL_WINDOWS)
    kern = functools.partial(_pool_prompt_kernel, tb=tb, hist=hist, group=group)
    return pl.pallas_call(
        kern,
        grid=(batch, n_t),
        in_specs=[
            pl.BlockSpec((tb, width), lambda b, t: (b * n_t + t, 0)),
            pl.BlockSpec((hist, width), lambda b, t: (jnp.maximum((b * seq + t * tb) // hist - 1, 0), 0)),
            pl.BlockSpec((tb, width), lambda b, t: (b * n_t + t, 1)),
            pl.BlockSpec(pool_w.shape, lambda b, t: (0, 0, 0)),
            pl.BlockSpec((1, width), lambda b, t: (0, 0)),
        ],
        out_specs=pl.BlockSpec((tb, width), lambda b, t: (b * n_t + t, 0)),
        out_shape=jax.ShapeDtypeStruct((batch * seq, width), _BF16),
        scratch_shapes=[pltpu.VMEM((hist + tb, width), _F32)],
        compiler_params=_params(("parallel", "parallel")),
        name="pool_prompt",
    )(z, z, z, pool_w, pool_scale)


def _pool_sample_kernel(u_ref, buf_ref, zp_ref, pw_ref, ps_ref, o_ref, *, nbuf, group):
    def pooled(g, cols):
        win = _POOL_WINDOWS[g]
        x = u_ref[:, cols]
        acc = x
        for d in range(1, win):
            acc = acc + buf_ref[nbuf - d, :, cols]
        cnt = float(min(_PAST_LEN + 1, win))
        return acc / cnt - x

    _pool_mix(pooled, zp_ref, pw_ref, ps_ref, o_ref, group)


def _pool_sample(z, buf, pool_w, pool_scale, *, row0, bb):
    nbuf, rows, width = buf.shape
    group = width // len(_POOL_WINDOWS)
    off = row0 // bb
    kern = functools.partial(_pool_sample_kernel, nbuf=nbuf, group=group)
    return pl.pallas_call(
        kern,
        grid=(rows // bb,),
        in_specs=[
            pl.BlockSpec((bb, width), lambda i: (off + i, 0)),
            pl.BlockSpec((nbuf, bb, width), lambda i: (0, i, 0)),
            pl.BlockSpec((bb, width), lambda i: (off + i, 1)),
            pl.BlockSpec(pool_w.shape, lambda i: (0, 0, 0)),
            pl.BlockSpec((1, width), lambda i: (0, 0)),
        ],
        out_specs=pl.BlockSpec((bb, width), lambda i: (i, 0)),
        out_shape=jax.ShapeDtypeStruct((rows, width), _BF16),
        compiler_params=_params(("parallel",)),
        name="pool_sample",
    )(z, buf, z, pool_w, pool_scale)


def _rwkv_mix(cur, prev, mu, w0, a0, kk_scale, ka, w2p, a2p, head_dim, outs):
    o_r, o_ld, o_k, o_v, o_kk, o_be = outs
    mix = lambda n: cur[n] + (prev[n] - cur[n]) * mu[n]
    xl = mix("l")
    wl = w0 + _mm1(_dot, jnp.tanh(xl), w2p)
    a = jax.nn.sigmoid(a0 + _mm1(_dot, xl, a2p))
    xk = mix("k")
    kkp = xk * kk_scale
    ones_bd = _head_block_matrix(1.0, head_dim)
    width = kkp.shape[1]
    o_r[...] = mix("r")
    o_v[...] = mix("v")
    o_ld[...] = -_EXP_MINUS_HALF * jax.nn.sigmoid(wl)
    o_k[...] = xk * (1.0 + (a - 1.0) * ka)
    for p in range(width // _LANES):
        cols = slice(p * _LANES, (p + 1) * _LANES)
        kp = kkp[:, cols]
        ss = _mm_exact_rhs(kp * kp, ones_bd)
        kk = kp * lax.rsqrt(jnp.maximum(ss, _L2_EPS * _L2_EPS))
        o_kk[:, cols] = kk
        o_be[:, cols] = kk * a[:, cols]


def _rwkv_prep_sample_kernel(zr, zk, zv, zl, pr, pk, pv, plr, mur, muk, muv, mul, w0, a0, kks, ka, w2p, a2p,
                             o_r, o_ld, o_k, o_v, o_kk, o_be, t_r, t_ld, t_k, t_v, t_kk, t_be, *, head_dim):
    cur = {"r": zr[...], "k": zk[...], "v": zv[...], "l": zl[...]}
    prev = {"r": pr[...], "k": pk[...], "v": pv[...], "l": plr[...]}
    mu = {"r": mur[...], "k": muk[...], "v": muv[...], "l": mul[...]}
    _rwkv_mix(cur, prev, mu, w0[...], a0[...], kks[...], ka[...], w2p[...], a2p[...], head_dim,
              (o_r, o_ld, o_k, o_v, o_kk, o_be))
    for o_ref, t_ref in ((o_r, t_r), (o_ld, t_ld), (o_k, t_k), (o_v, t_v), (o_kk, t_kk), (o_be, t_be)):
        t_ref[...] = o_ref[...].T


def _rwkv_prep_sample(z, prev_rkv, prev_l, consts, *, row0, rows, width, head_dim, col0, lw):
    c0 = col0 // width
    lcol = (col0 + 3 * width) // lw
    assert c0 * width == col0 and lcol * lw == col0 + 3 * width
    off = row0 // rows
    cblk = lambda c: pl.BlockSpec((rows, width), lambda i, c=c: (off, c))
    pblk = lambda c: pl.BlockSpec((rows, width), lambda i, c=c: (0, c))
    vec = lambda n: pl.BlockSpec((1, n), lambda i: (0, 0))
    mat = pl.BlockSpec((lw, width), lambda i: (0, 0))
    out_spec = pl.BlockSpec((rows, width), lambda i: (0, 0))
    out_shape = jax.ShapeDtypeStruct((rows, width), _F32)
    kern = functools.partial(_rwkv_prep_sample_kernel, head_dim=head_dim)
    return pl.pallas_call(
        kern,
        grid=(1,),
        in_specs=[cblk(c0), cblk(c0 + 1), cblk(c0 + 2), pl.BlockSpec((rows, lw), lambda i: (off, lcol)),
                  pblk(0), pblk(1), pblk(2), pl.BlockSpec((rows, lw), lambda i: (0, 0)),
                  vec(width), vec(width), vec(width), vec(lw),
                  vec(width), vec(width), vec(width), vec(width), mat, mat],
        out_specs=[out_spec] * 6 + [pl.BlockSpec((width, rows), lambda i: (0, 0))] * 6,
        out_shape=[out_shape] * 6 + [jax.ShapeDtypeStruct((width, rows), _F32)] * 6,
        compiler_params=_params(("arbitrary",)),
        name="rwkv_prep_sample",
    )(z, z, z, z, prev_rkv, prev_rkv, prev_rkv, prev_l, *consts)


def _scan_chunks(inputs, states, c):
    pairs = range(len(inputs))
    n2 = _LANES
    bf = lambda x: x.astype(_BF16)
    m0, m1 = c["m0"], c["m1"]
    expand = lambda x: jnp.concatenate([x * m0, x * m1], axis=0)
    stack = lambda a, b: jnp.concatenate([expand(a), expand(b)], axis=0)

    cum = [_mm_exact_rhs_t(c["ltri"], inputs[q][1]) for q in pairs]
    tot = [cum[q][_CHUNK - 1:_CHUNK, :] for q in pairs]
    xa, xb, bk, ve = [], [], [], []
    for q in pairs:
        r, ld, k2, v, kk, be = inputs[q]
        e_neg = jnp.exp(-cum[q])
        e_rem = jnp.exp(tot[q] - cum[q])
        xa.append(bf(stack(-kk * jnp.exp(cum[q] - ld), r * jnp.exp(cum[q]))))
        xb.append(bf(stack(be * e_neg, k2 * e_neg)))
        bk.append(bf(stack(be * e_rem, k2 * e_rem)))
        ve.append(bf(expand(v)))
    g = [_dot_nt(xa[q], xb[q]) for q in pairs]
    ps = [_dot_nt(xa[q], bf(states[q])) for q in pairs]
    a_kk = [bf(jnp.concatenate([jnp.where(c["strict"], g[q][:n2, n2:], 0.0),
                                jnp.where(c["incl"], g[q][n2:, n2:], 0.0)], axis=0)) for q in pairs]
    av = [_dot(a_kk[q], ve[q]) for q in pairs]
    npow = [jnp.where(c["strict"], g[q][:n2, :n2], 0.0) for q in pairs]
    u = [ps[q][:n2] + av[q][:n2] for q in pairs]
    for k in range(c["squarings"]):
        out = [_dot(bf(npow[q]), bf(jnp.concatenate([npow[q], u[q]], axis=1))) for q in pairs]
        npow = [out[q][:, :n2] for q in pairs]
        u = [u[q] + out[q][:, n2:] for q in pairs]
    ub = [bf(u[q]) for q in pairs]
    u = [u[q] + _dot(bf(npow[q]), ub[q]) for q in pairs]
    ub = [bf(u[q]) for q in pairs]
    a_rb = [bf(jnp.where(c["incl"], g[q][n2:, :n2], 0.0)) for q in pairs]
    ye = [ps[q][n2:] + av[q][n2:] + _dot(a_rb[q], ub[q]) for q in pairs]
    ys = [ye[q][:_CHUNK] + ye[q][_CHUNK:] for q in pairs]
    new_states = [states[q] * jnp.exp(tot[q]) + _dot_tn(jnp.concatenate([ub[q], ve[q]], axis=0), bk[q])
                  for q in pairs]
    return ys, new_states


def _mm_exact_rhs_t(l_bf16, x):
    x_hi, x_lo = _split(x)
    return _dot(l_bf16, x_hi) + _dot(l_bf16, x_lo)


def _scan_consts():
    n = _LANES
    half = _CHUNK
    ri = lax.broadcasted_iota(jnp.int32, (n, n), 0)
    ci = lax.broadcasted_iota(jnp.int32, (n, n), 1)
    same = (ri // half) == (ci // half)
    lane = lax.broadcasted_iota(jnp.int32, (half, n), 1)
    tr = lax.broadcasted_iota(jnp.int32, (half, half), 0)
    tc = lax.broadcasted_iota(jnp.int32, (half, half), 1)
    squarings = 0
    while (2 << squarings) < half:
        squarings += 1
    return {
        "strict": same & ((ri % half) > (ci % half)),
        "incl": same & ((ri % half) >= (ci % half)),
        "m0": jnp.where(lane < half, 1.0, 0.0).astype(_F32),
        "m1": jnp.where(lane >= half, 1.0, 0.0).astype(_F32),
        "ltri": jnp.where(tr >= tc, 1.0, 0.0).astype(_BF16),
        "squarings": squarings,
    }


def _rwkv_prompt_kernel(zr, zk, zv, zl, hr, hk, hv, hl, za, zb, mur, muk, muv, mul, w0, a0, kks, ka, w2p, a2p,
                        rk, lnw, lnb, o_ref, s_ref, er, ek, ev, el, r_s, ld_s, k_s, v_s, kk_s, be_s, y_s, xl_s, st_ref,
                        *, tb, pp, head_dim, delta):
    t = pl.program_id(2)
    halo = 8
    n_chunks = tb // _CHUNK
    pairs = range(pp)
    cols = [slice(q * _LANES, (q + 1) * _LANES) for q in pairs]

    @pl.when(t == 0)
    def _():
        st_ref[...] = jnp.zeros_like(st_ref)

    for z_ref, h_ref, e_ref, mu_ref, dst in ((zr, hr, er, mur, r_s), (zk, hk, ek, muk, k_s), (zv, hv, ev, muv, v_s),
                                             (zl, hl, el, mul, xl_s)):
        e_ref[0:halo, :] = jnp.where(t == 0, 0.0, h_ref[...])
        e_ref[halo:halo + tb, :] = z_ref[...]
        cur = z_ref[...]
        dst[...] = cur + (e_ref[halo - 1:halo - 1 + tb, :] - cur) * mu_ref[...]

    c = _scan_consts()
    ones_bd = _head_block_matrix(1.0, head_dim)
    avg_bd = _head_block_matrix(1.0 / head_dim, head_dim)

    xl = xl_s[...]
    wl = w0[...] + _mm1(_dot, jnp.tanh(xl), w2p[...])
    a = jax.nn.sigmoid(a0[...] + _mm1(_dot, xl, a2p[...]))
    ld_s[...] = -_EXP_MINUS_HALF * jax.nn.sigmoid(wl)
    xk = k_s[...]
    k_s[...] = xk * (1.0 + (a - 1.0) * ka[...])
    kp = [xk[:, cols[q]] * kks[:, cols[q]] for q in pairs]
    ss = [_mm_exact_rhs(kp[q] * kp[q], ones_bd) for q in pairs]
    for q in pairs:
        kk = kp[q] * lax.rsqrt(jnp.maximum(ss[q], _L2_EPS * _L2_EPS))
        kk_s[:, cols[q]] = kk
        be_s[:, cols[q]] = kk * a[:, cols[q]]

    def body(ci, carry):
        rows = pl.ds(pl.multiple_of(ci * _CHUNK, _CHUNK), _CHUNK)
        in_refs = (r_s, ld_s, k_s, v_s, kk_s, be_s)
        ys, new_states = _scan_chunks([tuple(ref[rows, cols[q]] for ref in in_refs) for q in pairs],
                                      [st_ref[q] for q in pairs], c)
        for q in pairs:
            y_s[rows, cols[q]] = ys[q]
            st_ref[q] = new_states[q]
        return carry

    lax.fori_loop(0, n_chunks, body, 0)

    q0, off = delta // _LANES, delta % _LANES
    n_a = za.shape[1] // _LANES
    zblk = lambda i: (za[:, i * _LANES:(i + 1) * _LANES] if i < n_a
                      else zb[:, (i - n_a) * _LANES:(i - n_a + 1) * _LANES])
    y = [y_s[:, cols[q]] for q in pairs]
    d = [y[q] - _mm_exact_rhs(y[q], avg_bd) for q in pairs]
    var = [_mm_exact_rhs(d[q] * d[q], avg_bd) for q in pairs]
    bsum = [_mm_exact_rhs(r_s[:, cols[q]] * k_s[:, cols[q]] * rk[:, cols[q]], ones_bd) for q in pairs]
    for q in pairs:
        yn = d[q] * lax.rsqrt(var[q] + _GN_EPS) * lnw[:, cols[q]] + lnb[:, cols[q]]
        zg = jnp.concatenate([zblk(q0 + q), zblk(q0 + q + 1)], axis=1)[:, off:off + _LANES]
        o_ref[:, cols[q]] = ((yn + bsum[q] * v_s[:, cols[q]]) * _silu(zg)).astype(o_ref.dtype)

    @pl.when(t == pl.num_programs(2) - 1)
    def _():
        s_ref[0] = st_ref[...]


def _rwkv_prompt(z, mix_consts, post_consts, *, batch, seq, width, tb, pp, head_dim, col0, gate_col0, lw):
    n_t = seq // tb
    gw = pp * _LANES
    halo = 8
    tail = 2 * _LANES
    c0, lcol, cg = col0 // gw, (col0 + 3 * width) // lw, gate_col0 // gw
    delta = gate_col0 % gw
    assert c0 * gw == col0 and width % gw == 0 and lcol * lw == col0 + 3 * width
    assert delta <= tail and delta % _LANES + _LANES <= tail and gw % tail == 0
    row = lambda b, g, t: b * n_t + t
    hrow = lambda b, g, t: jnp.maximum((b * seq + t * tb) // halo - 1, 0)
    step = width // gw
    zblk = lambda i: pl.BlockSpec((tb, gw), lambda b, g, t, i=i: (row(b, g, t), c0 + i * step + g))
    hblk = lambda i: pl.BlockSpec((halo, gw), lambda b, g, t, i=i: (hrow(b, g, t), c0 + i * step + g))
    vec = pl.BlockSpec((1, gw), lambda b, g, t: (0, g))
    lvec = pl.BlockSpec((1, lw), lambda b, g, t: (0, 0))
    mat = pl.BlockSpec((lw, gw), lambda b, g, t: (0, g))
    mur, muk, muv, mul, w0, a0, kks, ka, w2p, a2p = mix_consts
    big = lambda: pltpu.VMEM((tb, gw), _F32)
    kern = functools.partial(_rwkv_prompt_kernel, tb=tb, pp=pp, head_dim=head_dim, delta=delta)
    return pl.pallas_call(
        kern,
        grid=(batch, width // gw, n_t),
        in_specs=[zblk(0), zblk(1), zblk(2), pl.BlockSpec((tb, lw), lambda b, g, t: (row(b, g, t), lcol)),
                  hblk(0), hblk(1), hblk(2), pl.BlockSpec((halo, lw), lambda b, g, t: (hrow(b, g, t), lcol)),
                  pl.BlockSpec((tb, gw), lambda b, g, t: (row(b, g, t), cg + g)),
                  pl.BlockSpec((tb, tail), lambda b, g, t: (row(b, g, t), (cg + g + 1) * (gw // tail))),
                  vec, vec, vec, lvec, vec, vec, vec, vec, mat, mat, vec, vec, vec],
        out_specs=[pl.BlockSpec((tb, gw), lambda b, g, t: (row(b, g, t), g)),
                   pl.BlockSpec((1, pp, _LANES, _LANES), lambda b, g, t: (b, g, 0, 0))],
        out_shape=[jax.ShapeDtypeStruct((batch * seq, width), _BF16),
                   jax.ShapeDtypeStruct((batch, width // _LANES, _LANES, _LANES), _F32)],
        scratch_shapes=[pltpu.VMEM((halo + tb, gw), _F32)] * 3 + [pltpu.VMEM((halo + tb, lw), _F32)]
                       + [big() for _ in range(7)] + [pltpu.VMEM((tb, lw), _F32),
                                                      pltpu.VMEM((pp, _LANES, _LANES), _F32)],
        compiler_params=_params(("parallel", "parallel", "arbitrary")),
        name="rwkv_prompt",
    )(z, z, z, z, z, z, z, z, z, z, mur, muk, muv, mul, w0, a0, kks, ka, w2p, a2p, *post_consts)


def _rwkv_step_kernel(r_ref, ld_ref, k_ref, v_ref, kk_ref, be_ref, s_ref, y_ref, so_ref, *, hb, head_dim, vi):
    for h in range(hb):
        kk = kk_ref[h][None]
        dec = jnp.exp(ld_ref[h])[None]
        be = be_ref[h][None]
        k = k_ref[h][None]
        r = r_ref[h][None]

        def body(c, carry, h=h, kk=kk, dec=dec, be=be, k=k, r=r):
            rows = pl.ds(pl.multiple_of(c * vi, vi), vi)
            s = s_ref[h, rows]
            sa = -jnp.sum(s * kk, axis=1, keepdims=True)
            s_new = s * dec + sa * be + v_ref[h, rows] * k
            y_ref[h, rows] = jnp.sum(s_new * r, axis=1, keepdims=True)
            so_ref[h, rows] = s_new
            return carry

        lax.fori_loop(0, head_dim // vi, body, 0)


def _rwkv_step(arrs_t, state_t, *, hb):
    heads, hd, _, n = state_t.shape
    r, ld, k, v, kk, be = arrs_t
    per_key = lambda a: a.reshape(heads, hd, n)
    vec = pl.BlockSpec((hb, hd, n), lambda i: (i, 0, 0))
    col = pl.BlockSpec((hb, hd, 1, n), lambda i: (i, 0, 0, 0))
    mat = pl.BlockSpec((hb, hd, hd, n), lambda i: (i, 0, 0, 0))
    kern = functools.partial(_rwkv_step_kernel, hb=hb, head_dim=hd, vi=8)
    return pl.pallas_call(
        kern,
        grid=(heads // hb,),
        in_specs=[vec, vec, vec, col, vec, vec, mat],
        out_specs=[col, mat],
        out_shape=[jax.ShapeDtypeStruct((heads, hd, 1, n), _F32), jax.ShapeDtypeStruct(state_t.shape, _F32)],
        compiler_params=_params(("parallel",)),
        name="rwkv_step",
    )(per_key(r), per_key(ld), per_key(k), v.reshape(heads, hd, 1, n), per_key(kk), per_key(be), state_t)


def _rwkv_post_kernel(y_ref, r_ref, k_ref, v_ref, za_ref, zb_ref, rk_ref, lw_ref, lb_ref, o_ref, *, head_dim, delta):
    avg_bd = _head_block_matrix(1.0 / head_dim, head_dim)
    ones_bd = _head_block_matrix(1.0, head_dim)
    width = y_ref.shape[1]
    zwin = jnp.concatenate([za_ref[...], zb_ref[...]], axis=1)
    for p in range(width // _LANES):
        cols = slice(p * _LANES, (p + 1) * _LANES)
        y = y_ref[:, cols]
        d = y - _mm_exact_rhs(y, avg_bd)
        var = _mm_exact_rhs(d * d, avg_bd)
        yn = d * lax.rsqrt(var + _GN_EPS) * lw_ref[:, cols] + lb_ref[:, cols]
        bonus = _mm_exact_rhs(r_ref[:, cols] * k_ref[:, cols] * rk_ref[:, cols], ones_bd) * v_ref[:, cols]
        zg = zwin[:, delta + p * _LANES:delta + (p + 1) * _LANES]
        o_ref[:, cols] = ((yn + bonus) * _silu(zg)).astype(o_ref.dtype)


def _rwkv_post(y, r, k2, v, z, consts, *, row0, rows, width, tb, head_dim, col0):
    off = row0 // tb
    tail = 2 * _LANES
    cg, delta = col0 // width, col0 % width
    assert delta <= tail and width % tail == 0
    blk = pl.BlockSpec((tb, width), lambda i: (i, 0))
    vec = pl.BlockSpec((1, width), lambda i: (0, 0))
    kern = functools.partial(_rwkv_post_kernel, head_dim=head_dim, delta=delta)
    return pl.pallas_call(
        kern,
        grid=(rows // tb,),
        in_specs=[blk, blk, blk, blk,
                  pl.BlockSpec((tb, width), lambda i: (off + i, cg)),
                  pl.BlockSpec((tb, tail), lambda i: (off + i, (cg + 1) * (width // tail))),
                  vec, vec, vec],
        out_specs=blk,
        out_shape=jax.ShapeDtypeStruct((rows, width), _BF16),
        compiler_params=_params(("parallel",)),
        name="rwkv_post",
    )(y, r, k2, v, z, z, *consts)


def _mem_attn_prompt_kernel(za_ref, zb_ref, zc_ref, k_ref, v_ref, o_ref, *, heads, head_dim, delta):
    scale = head_dim ** -0.5
    mw = heads * head_dim
    zwin = jnp.concatenate([za_ref[...], zb_ref[...], zc_ref[...]], axis=1)
    for h in range(heads):
        cols = slice(h * head_dim, (h + 1) * head_dim)
        q = zwin[:, delta + cols.start:delta + cols.stop]
        zg = zwin[:, delta + mw + cols.start:delta + mw + cols.stop]
        s = _dot_nt(q.astype(_BF16), k_ref[:, cols].astype(_BF16)) * scale
        p = jnp.exp(s - jnp.max(s, axis=-1, keepdims=True))
        o = _dot(p.astype(_BF16), v_ref[:, cols].astype(_BF16)) / jnp.sum(p, axis=-1, keepdims=True)
        o_ref[:, cols] = (o * _silu(zg)).astype(o_ref.dtype)


def _mem_attn_prompt(z, mkv, *, batch, seq, n_mem, heads, head_dim, tq, col0):
    n_t = seq // tq
    mw = heads * head_dim
    tail = 2 * _LANES
    q_col, delta = col0 // mw, col0 % mw
    assert delta <= tail and mw % tail == 0
    kern = functools.partial(_mem_attn_prompt_kernel, heads=heads, head_dim=head_dim, delta=delta)
    return pl.pallas_call(
        kern,
        grid=(batch, n_t),
        in_specs=[
            pl.BlockSpec((tq, mw), lambda b, t: (b * n_t + t, q_col)),
            pl.BlockSpec((tq, mw), lambda b, t: (b * n_t + t, q_col + 1)),
            pl.BlockSpec((tq, tail), lambda b, t: (b * n_t + t, (q_col + 2) * (mw // tail))),
            pl.BlockSpec((n_mem, mw), lambda b, t: (b, 0)),
            pl.BlockSpec((n_mem, mw), lambda b, t: (b, 1)),
        ],
        out_specs=pl.BlockSpec((tq, mw), lambda b, t: (b * n_t + t, 0)),
        out_shape=jax.ShapeDtypeStruct((batch * seq, mw), _BF16),
        compiler_params=_params(("parallel", "parallel")),
        name="mem_attn_prompt",
    )(z, z, z, mkv, mkv)


def _mem_attn_sample_kernel(q_ref, zg_ref, k_ref, v_ref, o_ref, *, bb, head_dim):
    scale = head_dim ** -0.5
    for b in range(bb):
        s = jnp.sum(k_ref[b] * q_ref[b][None], axis=-1, keepdims=True) * scale
        p = jnp.exp(s - jnp.max(s, axis=0, keepdims=True))
        o = jnp.sum(p * v_ref[b], axis=0) / jnp.sum(p, axis=0)
        o_ref[b] = o * _silu(zg_ref[b])


def _mem_attn_sample(q, zg, mem_k, mem_v, *, bb):
    n, n_mem, heads, head_dim = mem_k.shape
    vec = pl.BlockSpec((bb, heads, head_dim), lambda i: (i, 0, 0))
    mat = pl.BlockSpec((bb, n_mem, heads, head_dim), lambda i: (i, 0, 0, 0))
    kern = functools.partial(_mem_attn_sample_kernel, bb=bb, head_dim=head_dim)
    return pl.pallas_call(
        kern,
        grid=(n // bb,),
        in_specs=[vec, vec, mat, mat],
        out_specs=vec,
        out_shape=jax.ShapeDtypeStruct((n, heads, head_dim), _F32),
        compiler_params=_params(("parallel",)),
        name="mem_attn_sample",
    )(q, zg, mem_k, mem_v)


def _branch_kernel(po_ref, ro_ref, mo_ref, wp_ref, wr_ref, wm_ref, gp_ref, gr_ref, gm_ref, tp_ref, tr_ref, tm_ref,
                   bp_ref, br_ref, bm_ref, o_ref, wpb_ref, wrb_ref, wmb_ref, *, delta):
    @pl.when(pl.program_id(1) == 0)
    def _():
        wpb_ref[...] = wp_ref[...].astype(_BF16)
        wrb_ref[...] = wr_ref[...].astype(_BF16)
        wmb_ref[...] = wm_ref[...].astype(_BF16)

    tn = o_ref.shape[1]

    def gate(g_ref, t_ref, b_ref):
        win = jnp.concatenate([g_ref[...], t_ref[...]], axis=1)
        return jax.nn.sigmoid(win[:, delta:delta + tn] + b_ref[...])

    h = gate(gp_ref, tp_ref, bp_ref) * _dot(po_ref[...], wpb_ref[...])
    h = h + gate(gr_ref, tr_ref, br_ref) * _dot(ro_ref[...], wrb_ref[...])
    h = h + gate(gm_ref, tm_ref, bm_ref) * _dot(mo_ref[...], wmb_ref[...])
    o_ref[...] = h.astype(o_ref.dtype)


def _branch(po, ro, mo, wp, wr, wm, z, b_gate, *, row0, tm, tn, gate_col0):
    rows = po.shape[0]
    d = wp.shape[1]
    off = row0 // tm
    tail = 2 * _LANES
    delta = gate_col0 % tn
    g0 = gate_col0 // tn
    assert delta <= tail and tn % tail == 0
    nb = d // tn
    act = lambda a: pl.BlockSpec((tm, a.shape[1]), lambda j, i: (i, 0))
    wgt = lambda w: pl.BlockSpec((w.shape[0], tn), lambda j, i: (0, j), pipeline_mode=pl.Buffered(1))
    gat = lambda k: pl.BlockSpec((tm, tn), lambda j, i, k=k: (off + i, g0 + k * nb + j))
    tai = lambda k: pl.BlockSpec((tm, tail), lambda j, i, k=k: (off + i, (g0 + k * nb + j + 1) * (tn // tail)))
    bia = lambda k: pl.BlockSpec((1, tn), lambda j, i, k=k: (0, k * nb + j))
    return pl.pallas_call(
        functools.partial(_branch_kernel, delta=delta),
        grid=(nb, rows // tm),
        in_specs=[act(po), act(ro), act(mo), wgt(wp), wgt(wr), wgt(wm), gat(0), gat(1), gat(2),
                  tai(0), tai(1), tai(2), bia(0), bia(1), bia(2)],
        out_specs=pl.BlockSpec((tm, tn), lambda j, i: (i, j)),
        out_shape=jax.ShapeDtypeStruct((rows, d), _BF16),
        scratch_shapes=[pltpu.VMEM((w.shape[0], tn), _BF16) for w in (wp, wr, wm)],
        compiler_params=_params(("parallel", "arbitrary")),
        name="branch_proj",
    )(po, ro, mo, wp, wr, wm, z, z, z, z, z, z, b_gate, b_gate, b_gate)


def _out_kernel(h_ref, w_ref, x_ref, g_ref, b_ref, o_ref, *, alpha):
    xf = alpha * x_ref[...] + _dot(h_ref[...], w_ref[...])
    mu = jnp.mean(xf, axis=-1, keepdims=True)
    d = xf - mu
    var = jnp.mean(d * d, axis=-1, keepdims=True)
    o_ref[...] = d * lax.rsqrt(var + _LN_EPS) * g_ref[...] + b_ref[...]


def _out_proj(h, w_out, x, ln_g, ln_b, *, tm, alpha):
    rows, d = x.shape
    kern = functools.partial(_out_kernel, alpha=alpha)
    return pl.pallas_call(
        kern,
        grid=(rows // tm,),
        in_specs=[
            pl.BlockSpec((tm, d), lambda i: (i, 0)),
            pl.BlockSpec((d, d), lambda i: (0, 0), pipeline_mode=pl.Buffered(1)),
            pl.BlockSpec((tm, d), lambda i: (i, 0)),
            pl.BlockSpec((1, d), lambda i: (0, 0)),
            pl.BlockSpec((1, d), lambda i: (0, 0)),
        ],
        out_specs=pl.BlockSpec((tm, d), lambda i: (i, 0)),
        out_shape=jax.ShapeDtypeStruct((rows, d), _F32),
        compiler_params=_params(("parallel",), vmem_limit=_VMEM_LIMIT_RESIDENT_WEIGHT),
        name="out_proj_ln",
    )(h, w_out, x, ln_g, ln_b)


def _tiles(batch, seq, n_sample):
    n_prompt = batch * seq
    m_all = n_prompt + n_sample
    return {
        "in_tm": next(t for t in (1040, 832, 640, 512, 256, 128, 64, 32, 16) if m_all % t == 0), "in_tn": 768,
        "row_tb": min(512, seq),
        "scan_tb": min(512, seq), "scan_pp": 8,
        "attn_tq": min(1024, seq),
        "proj_tm": min(256, n_prompt), "proj_tn": 1024, "out_tm": min(256, n_prompt),
        "pool_bb": min(32, n_sample), "step_hb": 4, "attn_bb": 4,
    }


def _layer(xp, xs, mem_p, cache_k, cache_v, st_pool, st_shift, st_rwkv, lp):
    batch, seq, d = xp.shape
    ns = xs.shape[0]
    n_p = batch * seq
    m_all = n_p + ns
    pw = lp["pool_scale"].shape[-1]
    rw = lp["rwkv_w0"].shape[-1]
    heads, hd = lp["rwkv_r_k"].shape
    lora = lp["rwkv_w2"].shape[0]
    sw = lp["rwkv_mu"].shape[-1]
    n_mem, mh, md = cache_k.shape[1:]
    mw = mh * md
    tl = _tiles(batch, seq, ns)

    lo0, lo1 = 2 * pw + 3 * rw, 2 * pw + sw
    lw = 2 * _LANES
    assert 2 * lora <= lw
    q_col0 = lo1 + rw
    gate_col0 = q_col0 + 2 * mw

    if ns < tl["in_tm"] <= n_p:
        x_all = _cast_rows(xp.reshape(n_p, d), xs.reshape(ns, d), tm=tl["in_tm"])
    else:
        x_all = jnp.concatenate([xp.reshape(n_p, d), xs.reshape(ns, d)], axis=0).astype(_BF16)
    z = _in_proj(x_all, jnp.swapaxes(lp["w_in"], 0, 1), tm=tl["in_tm"], tn=tl["in_tn"])
    z_s = z[n_p:]

    pool_w = lp["pool_w"].astype(_BF16)
    pool_scale = lp["pool_scale"].reshape(1, pw)
    po_p = _pool_prompt(z, pool_w, pool_scale, batch=batch, seq=seq, width=pw, tb=tl["row_tb"])
    po_s = _pool_sample(z_s, jnp.swapaxes(st_pool, 0, 1), pool_w, pool_scale, row0=0, bb=tl["pool_bb"])
    nbuf = st_pool.shape[1]
    new_pool_p = jnp.stack([z[(b + 1) * seq - nbuf:(b + 1) * seq, :pw] for b in range(batch)])
    new_pool_s = jnp.concatenate([st_pool[:, 1:, :], z_s[:, :pw].reshape(ns, 1, pw)], axis=1)

    mu = lp["rwkv_mu"]
    pad_l = lambda a: jnp.pad(a, ((0, 0), (0, lw - 2 * lora)))
    w2p = jnp.zeros((lw, rw), _F32).at[:lora].set(lp["rwkv_w2"]).astype(_BF16)
    a2p = jnp.zeros((lw, rw), _F32).at[lora:2 * lora].set(lp["rwkv_a2"]).astype(_BF16)
    row = lambda a: a.reshape(1, -1)
    mix_consts = (row(mu[:rw]), row(mu[rw:2 * rw]), row(mu[2 * rw:3 * rw]), pad_l(row(mu[3 * rw:])),
                  row(lp["rwkv_w0"]), row(lp["rwkv_a0"]), row(lp["rwkv_k_k"]), row(lp["rwkv_k_a"]), w2p, a2p)
    post_consts = (row(lp["rwkv_r_k"]), row(lp["rwkv_ln_w"]), row(lp["rwkv_ln_b"]))

    ro_p, st_pairs = _rwkv_prompt(z, mix_consts, post_consts, batch=batch, seq=seq, width=rw, tb=tl["scan_tb"],
                                  pp=tl["scan_pp"], head_dim=hd, col0=2 * pw, gate_col0=lo1, lw=lw)
    st5 = st_pairs.reshape(batch, heads // 2, 2, hd, 2, hd)
    new_state_p = jnp.stack([st5[:, :, 0, :, 0, :], st5[:, :, 1, :, 1, :]], axis=2).reshape(batch, heads, hd, hd)

    shift_s = st_shift.reshape(ns, sw)
    prep_s = _rwkv_prep_sample(z_s, shift_s[:, :3 * rw], pad_l(shift_s[:, 3 * rw:]), mix_consts, row0=0, rows=ns,
                               width=rw, head_dim=hd, col0=2 * pw, lw=lw)
    r_s, _, k_s, v_s, _, _ = prep_s[:6]
    y_t, state_t = _rwkv_step(prep_s[6:], jnp.transpose(st_rwkv, (1, 2, 3, 0)), hb=tl["step_hb"])
    new_state_s = jnp.transpose(state_t, (3, 0, 1, 2))
    ro_s = _rwkv_post(y_t.reshape(rw, ns).T, r_s, k_s, v_s, z_s, post_consts, row0=0, rows=ns, width=rw, tb=ns,
                      head_dim=hd, col0=lo1)
    new_shift_p = jnp.stack([z[(b + 1) * seq - 1:(b + 1) * seq, 2 * pw:lo1] for b in range(batch)])
    new_shift_s = z_s[:, 2 * pw:lo1].reshape(ns, 1, sw)

    mkv = _matmul_f32w(mem_p.reshape(batch * n_mem, d).astype(_BF16), lp["w_mem_kv"],
                       tm=next(t for t in (512, 256, 128, 64, 32, 16, 8) if (batch * n_mem) % t == 0),
                       tn=min(768, 2 * mw), name="mem_kv_proj")
    mo_p = _mem_attn_prompt(z, mkv, batch=batch, seq=seq, n_mem=n_mem, heads=mh, head_dim=md, tq=tl["attn_tq"],
                            col0=q_col0)
    mo_s = _mem_attn_sample(z_s[:, q_col0:q_col0 + mw].reshape(ns, mh, md),
                            z_s[:, q_col0 + mw:q_col0 + 2 * mw].reshape(ns, mh, md),
                            cache_k, cache_v, bb=tl["attn_bb"]).reshape(ns, mw).astype(_BF16)
    mk_p = mkv[:, :mw].reshape(batch, n_mem, mh, md)
    mv_p = mkv[:, mw:].reshape(batch, n_mem, mh, md)

    wp, wr, wm = lp["w_branch_pool"], lp["w_branch_rwkv"], lp["w_branch_mem"]
    w_out = lp["w_out"].astype(_BF16)
    b_gate = row(lp["b_gate"])
    ln_g, ln_b = row(lp["ln_g"]), row(lp["ln_b"])
    alpha = lp["alpha"]
    outs = []
    for po, ro, mo, x2d, zz in ((po_p, ro_p, mo_p, xp.reshape(n_p, d), z), (po_s, ro_s, mo_s, xs.reshape(ns, d), z_s)):
        rows = x2d.shape[0]
        h = _branch(po, ro, mo, wp, wr, wm, zz, b_gate, row0=0, tm=min(tl["proj_tm"], rows), tn=tl["proj_tn"],
                    gate_col0=gate_col0)
        outs.append(_out_proj(h, w_out, x2d, ln_g, ln_b, tm=min(tl["out_tm"], rows), alpha=alpha))
    y_prompt = outs[0].reshape(batch, seq, d)
    y_sample = outs[1].reshape(ns, 1, d)
    return (y_prompt, y_sample, mk_p, mv_p, new_pool_p, new_shift_p, new_state_p, new_pool_s, new_shift_s,
            new_state_s)


def kernel(x_prompt, x_sample, cache_mem_k, cache_mem_v, state_pool, state_shift, state_rwkv, mem_prompt, w_in,
           b_gate, pool_w, pool_scale, rwkv_mu, rwkv_w0, rwkv_w2, rwkv_a0, rwkv_a2, rwkv_k_k, rwkv_k_a, rwkv_r_k,
           rwkv_ln_w, rwkv_ln_b, w_mem_kv, w_branch_pool, w_branch_rwkv, w_branch_mem, w_out, ln_g, ln_b):
    depth = w_in.shape[0]
    alpha = (2.0 * depth) ** 0.25
    weights = dict(w_in=w_in, b_gate=b_gate, pool_w=pool_w, pool_scale=pool_scale, rwkv_mu=rwkv_mu, rwkv_w0=rwkv_w0,
                   rwkv_w2=rwkv_w2, rwkv_a0=rwkv_a0, rwkv_a2=rwkv_a2, rwkv_k_k=rwkv_k_k, rwkv_k_a=rwkv_k_a,
                   rwkv_r_k=rwkv_r_k, rwkv_ln_w=rwkv_ln_w, rwkv_ln_b=rwkv_ln_b, w_mem_kv=w_mem_kv,
                   w_branch_pool=w_branch_pool, w_branch_rwkv=w_branch_rwkv, w_branch_mem=w_branch_mem,
                   w_out=w_out, ln_g=ln_g, ln_b=ln_b)
    yp, ys = x_prompt, x_sample
    per_layer = []
    for l in range(depth):
        lp = {k: v[l] for k, v in weights.items()}
        lp["alpha"] = alpha
        res = _layer(yp, ys, mem_prompt, cache_mem_k[l], cache_mem_v[l], state_pool[l], state_shift[l],
                     state_rwkv[l], lp)
        yp, ys = res[0], res[1]
        per_layer.append(res[2:])
    if depth == 1:
        stacked = tuple(a[None] for a in per_layer[0])
    else:
        stacked = tuple(jnp.stack([res[i] for res in per_layer]) for i in range(len(per_layer[0])))
    return (yp, ys) + stacked
```

```python
import functools

import jax
import jax.numpy as jnp
from jax import lax
from jax.experimental import pallas as pl
from jax.experimental.pallas import tpu as pltpu

_F32 = jnp.float32
_BF16 = jnp.bfloat16

_POOL_WINDOWS = (2, 4, 8, 16)
_PAST_LEN = 16384
_GN_EPS = 64e-5
_LN_EPS = 1e-5
_L2_EPS = 1e-12
_EXP_MINUS_HALF = 0.6065306597126334

_LANES = 128
_MIB = 1024 * 1024
_VMEM_LIMIT = 56 * _MIB
_VMEM_LIMIT_RESIDENT_WEIGHT = 62 * _MIB

_HEAD_PAIR = _LANES
_CHUNK = 64


def _params(sem, vmem_limit=_VMEM_LIMIT):
    return pltpu.CompilerParams(dimension_semantics=sem, vmem_limit_bytes=vmem_limit)


def _dot(a, b):
    return jnp.dot(a, b, preferred_element_type=_F32)


def _dot_nt(a, b):
    return lax.dot_general(a, b, (((1,), (1,)), ((), ())), preferred_element_type=_F32)


def _dot_tn(a, b):
    return lax.dot_general(a, b, (((0,), (0,)), ((), ())), preferred_element_type=_F32)


def _split(x):
    hi = x.astype(_BF16)
    lo = (x - hi.astype(_F32)).astype(_BF16)
    return hi, lo


def _mm1(dot, a, b):
    return dot(a.astype(_BF16), b.astype(_BF16))


def _mm_exact_rhs(a, b_bf16):
    a_hi, a_lo = _split(a)
    return _dot(a_hi, b_bf16) + _dot(a_lo, b_bf16)


def _silu(x):
    return x * jax.nn.sigmoid(x)


def _head_block_matrix(value, head_dim):
    r = lax.broadcasted_iota(jnp.int32, (_LANES, _LANES), 0) // head_dim
    c = lax.broadcasted_iota(jnp.int32, (_LANES, _LANES), 1) // head_dim
    return jnp.where(r == c, value, 0.0).astype(_BF16)


def _matmul_f32w_kernel(x_ref, w_ref, o_ref, wb_ref):
    @pl.when(pl.program_id(1) == 0)
    def _():
        wb_ref[...] = w_ref[...].astype(_BF16)

    o_ref[...] = _dot(x_ref[...], wb_ref[...])


def _matmul_f32w(x, w, *, tm, tn, name):
    m, k = x.shape
    n = w.shape[1]
    return pl.pallas_call(
        _matmul_f32w_kernel,
        grid=(n // tn, m // tm),
        in_specs=[pl.BlockSpec((tm, k), lambda j, i: (i, 0)), pl.BlockSpec((k, tn), lambda j, i: (0, j))],
        out_specs=pl.BlockSpec((tm, tn), lambda j, i: (i, j)),
        out_shape=jax.ShapeDtypeStruct((m, n), _F32),
        scratch_shapes=[pltpu.VMEM((k, tn), _BF16)],
        compiler_params=_params(("parallel", "arbitrary")),
        name=name,
    )(x, w)


def _cast_rows_kernel(xp_ref, xs_ref, o_ref, *, tail):
    i = pl.program_id(0)
    last = pl.num_programs(0) - 1
    tm = o_ref.shape[0]

    @pl.when(i < last)
    def _():
        o_ref[...] = xp_ref[...].astype(_BF16)

    @pl.when(i == last)
    def _():
        o_ref[0:tm - tail, :] = xp_ref[tail:tm, :].astype(_BF16)
        o_ref[tm - tail:tm, :] = xs_ref[...].astype(_BF16)


def _cast_rows(xp, xs, *, tm):
    n_p, d = xp.shape
    tail = xs.shape[0]
    n_tiles = (n_p + tail) // tm
    assert n_tiles * tm == n_p + tail and 0 < tail < tm <= n_p and (n_p - tm) % 8 == 0
    return pl.pallas_call(
        functools.partial(_cast_rows_kernel, tail=tail),
        grid=(n_tiles,),
        in_specs=[pl.BlockSpec((pl.Element(tm), pl.Element(d)),
                               lambda i: (pl.multiple_of(jnp.minimum(i * tm, n_p - tm), 8), 0)),
                  pl.BlockSpec((tail, d), lambda i: (0, 0))],
        out_specs=pl.BlockSpec((tm, d), lambda i: (i, 0)),
        out_shape=jax.ShapeDtypeStruct((n_p + tail, d), _BF16),
        compiler_params=_params(("parallel",), vmem_limit=_VMEM_LIMIT_RESIDENT_WEIGHT),
        name="cast_rows",
    )(xp, xs)


def _in_proj_kernel(x_ref, wt_ref, o_ref, wb_ref):
    @pl.when(pl.program_id(1) == 0)
    def _():
        wb_ref[...] = wt_ref[...].astype(_BF16)

    o_ref[...] = _dot_nt(x_ref[...], wb_ref[...])


def _in_proj(x, wt, *, tm, tn):
    m, k = x.shape
    n = wt.shape[0]
    return pl.pallas_call(
        _in_proj_kernel,
        grid=(pl.cdiv(n, tn), m // tm),
        in_specs=[pl.BlockSpec((tm, k), lambda j, i: (i, 0)), pl.BlockSpec((tn, k), lambda j, i: (j, 0))],
        out_specs=pl.BlockSpec((tm, tn), lambda j, i: (i, j)),
        out_shape=jax.ShapeDtypeStruct((m, n), _F32),
        scratch_shapes=[pltpu.VMEM((tn, k), _BF16)],
        compiler_params=_params(("parallel", "arbitrary"), vmem_limit=_VMEM_LIMIT_RESIDENT_WEIGHT),
        name="in_proj",
    )(x, wt)


def _pool_mix(pooled_fn, zp_ref, pw_ref, ps_ref, o_ref, group):
    for g in range(len(_POOL_WINDOWS)):
        cols = slice(g * group, (g + 1) * group)
        mixed = _dot(pooled_fn(g, cols).astype(_BF16), pw_ref[g])
        o_ref[:, cols] = (mixed * ps_ref[:, cols] * _silu(zp_ref[:, cols])).astype(o_ref.dtype)


def _pool_prompt_kernel(u_ref, halo_ref, zp_ref, pw_ref, ps_ref, o_ref, e_ref, *, tb, hist, group):
    t = pl.program_id(1)
    e_ref[0:hist, :] = jnp.where(t == 0, 0.0, halo_ref[...])
    e_ref[hist:hist + tb, :] = u_ref[...]
    pos = t * tb + lax.broadcasted_iota(jnp.int32, (tb, group), 0)

    def window_sum(cols, win):
        if win < 8:
            acc = e_ref[hist:hist + tb, cols]
            for d in range(1, win):
                acc = acc + e_ref[hist - d:hist - d + tb, cols]
            return acc
        assert win & (win - 1) == 0
        s, span = e_ref[:, cols], 1
        while span < win:
            s = s[span:] + s[:-span]
            span *= 2
        first = hist - (win - 1)
        return s[first:first + tb]

    def pooled(g, cols):
        win = _POOL_WINDOWS[g]
        x = e_ref[hist:hist + tb, cols]
        cnt = jnp.minimum(pos + 1, win).astype(_F32)
        return window_sum(cols, win) / cnt - x

    _pool_mix(pooled, zp_ref, pw_ref, ps_ref, o_ref, group)


def _pool_prompt(z, pool_w, pool_scale, *, batch, seq, width, tb):
    hist = 16
    n_t = seq // tb
    group = width // len(_POOL_WINDOWS)
    kern = functools.partial(_pool_prompt_kernel, tb=tb, hist=hist, group=group)
    return pl.pallas_call(
        kern,
        grid=(batch, n_t),
        in_specs=[
            pl.BlockSpec((tb, width), lambda b, t: (b * n_t + t, 0)),
            pl.BlockSpec((hist, width), lambda b, t: (jnp.maximum((b * seq + t * tb) // hist - 1, 0), 0)),
            pl.BlockSpec((tb, width), lambda b, t: (b * n_t + t, 1)),
            pl.BlockSpec(pool_w.shape, lambda b, t: (0, 0, 0)),
            pl.BlockSpec((1, width), lambda b, t: (0, 0)),
        ],
        out_specs=pl.BlockSpec((tb, width), lambda b, t: (b * n_t + t, 0)),
        out_shape=jax.ShapeDtypeStruct((batch * seq, width), _BF16),
        scratch_shapes=[pltpu.VMEM((hist + tb, width), _F32)],
        compiler_params=_params(("parallel", "parallel")),
        name="pool_prompt",
    )(z, z, z, pool_w, pool_scale)


def _pool_sample_kernel(u_ref, buf_ref, zp_ref, pw_ref, ps_ref, o_ref, *, nbuf, group):
    def pooled(g, cols):
        win = _POOL_WINDOWS[g]
        x = u_ref[:, cols]
        acc = x
        for d in range(1, win):
            acc = acc + buf_ref[nbuf - d, :, cols]
        cnt = float(min(_PAST_LEN + 1, win))
        return acc / cnt - x

    _pool_mix(pooled, zp_ref, pw_ref, ps_ref, o_ref, group)


def _pool_sample(z, buf, pool_w, pool_scale, *, row0, bb):
    nbuf, rows, width = buf.shape
    group = width // len(_POOL_WINDOWS)
    off = row0 // bb
    kern = functools.partial(_pool_sample_kernel, nbuf=nbuf, group=group)
    return pl.pallas_call(
        kern,
        grid=(rows // bb,),
        in_specs=[
            pl.BlockSpec((bb, width), lambda i: (off + i, 0)),
            pl.BlockSpec((nbuf, bb, width), lambda i: (0, i, 0)),
            pl.BlockSpec((bb, width), lambda i: (off + i, 1)),
            pl.BlockSpec(pool_w.shape, lambda i: (0, 0, 0)),
            pl.BlockSpec((1, width), lambda i: (0, 0)),
        ],
        out_specs=pl.BlockSpec((bb, width), lambda i: (i, 0)),
        out_shape=jax.ShapeDtypeStruct((rows, width), _BF16),
        compiler_params=_params(("parallel",)),
        name="pool_sample",
    )(z, buf, z, pool_w, pool_scale)


def _rwkv_mix(cur, prev, mu, w0, a0, kk_scale, ka, w2p, a2p, head_dim, outs):
    o_r, o_ld, o_k, o_v, o_kk, o_be = outs
    mix = lambda n: cur[n] + (prev[n] - cur[n]) * mu[n]
    xl = mix("l")
    wl = w0 + _mm1(_dot, jnp.tanh(xl), w2p)
    a = jax.nn.sigmoid(a0 + _mm1(_dot, xl, a2p))
    xk = mix("k")
    kkp = xk * kk_scale
    ones_bd = _head_block_matrix(1.0, head_dim)
    width = kkp.shape[1]
    o_r[...] = mix("r")
    o_v[...] = mix("v")
    o_ld[...] = -_EXP_MINUS_HALF * jax.nn.sigmoid(wl)
    o_k[...] = xk * (1.0 + (a - 1.0) * ka)
    for p in range(width // _LANES):
        cols = slice(p * _LANES, (p + 1) * _LANES)
        kp = kkp[:, cols]
        ss = _mm_exact_rhs(kp * kp, ones_bd)
        kk = kp * lax.rsqrt(jnp.maximum(ss, _L2_EPS * _L2_EPS))
        o_kk[:, cols] = kk
        o_be[:, cols] = kk * a[:, cols]


def _rwkv_prep_sample_kernel(zr, zk, zv, zl, pr, pk, pv, plr, mur, muk, muv, mul, w0, a0, kks, ka, w2p, a2p,
                             o_r, o_ld, o_k, o_v, o_kk, o_be, t_r, t_ld, t_k, t_v, t_kk, t_be, *, head_dim):
    cur = {"r": zr[...], "k": zk[...], "v": zv[...], "l": zl[...]}
    prev = {"r": pr[...], "k": pk[...], "v": pv[...], "l": plr[...]}
    mu = {"r": mur[...], "k": muk[...], "v": muv[...], "l": mul[...]}
    _rwkv_mix(cur, prev, mu, w0[...], a0[...], kks[...], ka[...], w2p[...], a2p[...], head_dim,
              (o_r, o_ld, o_k, o_v, o_kk, o_be))
    for o_ref, t_ref in ((o_r, t_r), (o_ld, t_ld), (o_k, t_k), (o_v, t_v), (o_kk, t_kk), (o_be, t_be)):
        t_ref[...] = o_ref[...].T


def _rwkv_prep_sample(z, prev_rkv, prev_l, consts, *, row0, rows, width, head_dim, col0, lw):
    c0 = col0 // width
    lcol = (col0 + 3 * width) // lw
    assert c0 * width == col0 and lcol * lw == col0 + 3 * width
    off = row0 // rows
    cblk = lambda c: pl.BlockSpec((rows, width), lambda i, c=c: (off, c))
    pblk = lambda c: pl.BlockSpec((rows, width), lambda i, c=c: (0, c))
    vec = lambda n: pl.BlockSpec((1, n), lambda i: (0, 0))
    mat = pl.BlockSpec((lw, width), lambda i: (0, 0))
    out_spec = pl.BlockSpec((rows, width), lambda i: (0, 0))
    out_shape = jax.ShapeDtypeStruct((rows, width), _F32)
    kern = functools.partial(_rwkv_prep_sample_kernel, head_dim=head_dim)
    return pl.pallas_call(
        kern,
        grid=(1,),
        in_specs=[cblk(c0), cblk(c0 + 1), cblk(c0 + 2), pl.BlockSpec((rows, lw), lambda i: (off, lcol)),
                  pblk(0), pblk(1), pblk(2), pl.BlockSpec((rows, lw), lambda i: (0, 0)),
                  vec(width), vec(width), vec(width), vec(lw),
                  vec(width), vec(width), vec(width), vec(width), mat, mat],
        out_specs=[out_spec] * 6 + [pl.BlockSpec((width, rows), lambda i: (0, 0))] * 6,
        out_shape=[out_shape] * 6 + [jax.ShapeDtypeStruct((width, rows), _F32)] * 6,
        compiler_params=_params(("arbitrary",)),
        name="rwkv_prep_sample",
    )(z, z, z, z, prev_rkv, prev_rkv, prev_rkv, prev_l, *consts)


def _scan_chunks(inputs, states, c):
    pairs = range(len(inputs))
    n2 = _LANES
    bf = lambda x: x.astype(_BF16)
    m0, m1 = c["m0"], c["m1"]
    expand = lambda x: jnp.concatenate([x * m0, x * m1], axis=0)
    stack = lambda a, b: jnp.concatenate([expand(a), expand(b)], axis=0)

    cum = [_mm_exact_rhs_t(c["ltri"], inputs[q][1]) for q in pairs]
    tot = [cum[q][_CHUNK - 1:_CHUNK, :] for q in pairs]
    xa, xb, bk, ve = [], [], [], []
    for q in pairs:
        r, ld, k2, v, kk, be = inputs[q]
        e_neg = jnp.exp(-cum[q])
        e_rem = jnp.exp(tot[q] - cum[q])
        xa.append(bf(stack(-kk * jnp.exp(cum[q] - ld), r * jnp.exp(cum[q]))))
        xb.append(bf(stack(be * e_neg, k2 * e_neg)))
        bk.append(bf(stack(be * e_rem, k2 * e_rem)))
        ve.append(bf(expand(v)))
    g = [_dot_nt(xa[q], xb[q]) for q in pairs]
    ps = [_dot_nt(xa[q], bf(states[q])) for q in pairs]
    a_kk = [bf(jnp.concatenate([jnp.where(c["strict"], g[q][:n2, n2:], 0.0),
                                jnp.where(c["incl"], g[q][n2:, n2:], 0.0)], axis=0)) for q in pairs]
    av = [_dot(a_kk[q], ve[q]) for q in pairs]
    npow = [jnp.where(c["strict"], g[q][:n2, :n2], 0.0) for q in pairs]
    u = [ps[q][:n2] + av[q][:n2] for q in pairs]
    for k in range(c["squarings"]):
        out = [_dot(bf(npow[q]), bf(jnp.concatenate([npow[q], u[q]], axis=1))) for q in pairs]
        npow = [out[q][:, :n2] for q in pairs]
        u = [u[q] + out[q][:, n2:] for q in pairs]
    ub = [bf(u[q]) for q in pairs]
    u = [u[q] + _dot(bf(npow[q]), ub[q]) for q in pairs]
    ub = [bf(u[q]) for q in pairs]
    a_rb = [bf(jnp.where(c["incl"], g[q][n2:, :n2], 0.0)) for q in pairs]
    ye = [ps[q][n2:] + av[q][n2:] + _dot(a_rb[q], ub[q]) for q in pairs]
    ys = [ye[q][:_CHUNK] + ye[q][_CHUNK:] for q in pairs]
    new_states = [states[q] * jnp.exp(tot[q]) + _dot_tn(jnp.concatenate([ub[q], ve[q]], axis=0), bk[q])
                  for q in pairs]
    return ys, new_states


def _mm_exact_rhs_t(l_bf16, x):
    x_hi, x_lo = _split(x)
    return _dot(l_bf16, x_hi) + _dot(l_bf16, x_lo)


def _scan_consts():
    n = _LANES
    half = _CHUNK
    ri = lax.broadcasted_iota(jnp.int32, (n, n), 0)
    ci = lax.broadcasted_iota(jnp.int32, (n, n), 1)
    same = (ri // half) == (ci // half)
    lane = lax.broadcasted_iota(jnp.int32, (half, n), 1)
    tr = lax.broadcasted_iota(jnp.int32, (half, half), 0)
    tc = lax.broadcasted_iota(jnp.int32, (half, half), 1)
    squarings = 0
    while (2 << squarings) < half:
        squarings += 1
    return {
        "strict": same & ((ri % half) > (ci % half)),
        "incl": same & ((ri % half) >= (ci % half)),
        "m0": jnp.where(lane < half, 1.0, 0.0).astype(_F32),
        "m1": jnp.where(lane >= half, 1.0, 0.0).astype(_F32),
        "ltri": jnp.where(tr >= tc, 1.0, 0.0).astype(_BF16),
        "squarings": squarings,
    }


def _rwkv_prompt_kernel(zr, zk, zv, zl, hr, hk, hv, hl, za, zb, mur, muk, muv, mul, w0, a0, kks, ka, w2p, a2p,
                        rk, lnw, lnb, o_ref, s_ref, er, ek, ev, el, r_s, ld_s, k_s, v_s, kk_s, be_s, y_s, xl_s, st_ref,
                        *, tb, pp, head_dim, delta):
    t = pl.program_id(2)
    halo = 8
    n_chunks = tb // _CHUNK
    pairs = range(pp)
    cols = [slice(q * _LANES, (q + 1) * _LANES) for q in pairs]

    @pl.when(t == 0)
    def _():
        st_ref[...] = jnp.zeros_like(st_ref)

    for z_ref, h_ref, e_ref, mu_ref, dst in ((zr, hr, er, mur, r_s), (zk, hk, ek, muk, k_s), (zv, hv, ev, muv, v_s),
                                             (zl, hl, el, mul, xl_s)):
        e_ref[0:halo, :] = jnp.where(t == 0, 0.0, h_ref[...])
        e_ref[halo:halo + tb, :] = z_ref[...]
        cur = z_ref[...]
        dst[...] = cur + (e_ref[halo - 1:halo - 1 + tb, :] - cur) * mu_ref[...]

    c = _scan_consts()
    ones_bd = _head_block_matrix(1.0, head_dim)
    avg_bd = _head_block_matrix(1.0 / head_dim, head_dim)

    xl = xl_s[...]
    wl = w0[...] + _mm1(_dot, jnp.tanh(xl), w2p[...])
    a = jax.nn.sigmoid(a0[...] + _mm1(_dot, xl, a2p[...]))
    ld_s[...] = -_EXP_MINUS_HALF * jax.nn.sigmoid(wl)
    xk = k_s[...]
    k_s[...] = xk * (1.0 + (a - 1.0) * ka[...])
    kp = [xk[:, cols[q]] * kks[:, cols[q]] for q in pairs]
    ss = [_mm_exact_rhs(kp[q] * kp[q], ones_bd) for q in pairs]
    for q in pairs:
        kk = kp[q] * lax.rsqrt(jnp.maximum(ss[q], _L2_EPS * _L2_EPS))
        kk_s[:, cols[q]] = kk
        be_s[:, cols[q]] = kk * a[:, cols[q]]

    def body(ci, carry):
        rows = pl.ds(pl.multiple_of(ci * _CHUNK, _CHUNK), _CHUNK)
        in_refs = (r_s, ld_s, k_s, v_s, kk_s, be_s)
        ys, new_states = _scan_chunks([tuple(ref[rows, cols[q]] for ref in in_refs) for q in pairs],
                                      [st_ref[q] for q in pairs], c)
        for q in pairs:
            y_s[rows, cols[q]] = ys[q]
            st_ref[q] = new_states[q]
        return carry

    lax.fori_loop(0, n_chunks, body, 0)

    q0, off = delta // _LANES, delta % _LANES
    n_a = za.shape[1] // _LANES
    zblk = lambda i: (za[:, i * _LANES:(i + 1) * _LANES] if i < n_a
                      else zb[:, (i - n_a) * _LANES:(i - n_a + 1) * _LANES])
    y = [y_s[:, cols[q]] for q in pairs]
    d = [y[q] - _mm_exact_rhs(y[q], avg_bd) for q in pairs]
    var = [_mm_exact_rhs(d[q] * d[q], avg_bd) for q in pairs]
    bsum = [_mm_exact_rhs(r_s[:, cols[q]] * k_s[:, cols[q]] * rk[:, cols[q]], ones_bd) for q in pairs]
    for q in pairs:
        yn = d[q] * lax.rsqrt(var[q] + _GN_EPS) * lnw[:, cols[q]] + lnb[:, cols[q]]
        zg = jnp.concatenate([zblk(q0 + q), zblk(q0 + q + 1)], axis=1)[:, off:off + _LANES]
        o_ref[:, cols[q]] = ((yn + bsum[q] * v_s[:, cols[q]]) * _silu(zg)).astype(o_ref.dtype)

    @pl.when(t == pl.num_programs(2) - 1)
    def _():
        s_ref[0] = st_ref[...]


def _rwkv_prompt(z, mix_consts, post_consts, *, batch, seq, width, tb, pp, head_dim, col0, gate_col0, lw):
    n_t = seq // tb
    gw = pp * _LANES
    halo = 8
    tail = 2 * _LANES
    c0, lcol, cg = col0 // gw, (col0 + 3 * width) // lw, gate_col0 // gw
    delta = gate_col0 % gw
    assert c0 * gw == col0 and width % gw == 0 and lcol * lw == col0 + 3 * width
    assert delta <= tail and delta % _LANES + _LANES <= tail and gw % tail == 0
    row = lambda b, g, t: b * n_t + t
    hrow = lambda b, g, t: jnp.maximum((b * seq + t * tb) // halo - 1, 0)
    step = width // gw
    zblk = lambda i: pl.BlockSpec((tb, gw), lambda b, g, t, i=i: (row(b, g, t), c0 + i * step + g))
    hblk = lambda i: pl.BlockSpec((halo, gw), lambda b, g, t, i=i: (hrow(b, g, t), c0 + i * step + g))
    vec = pl.BlockSpec((1, gw), lambda b, g, t: (0, g))
    lvec = pl.BlockSpec((1, lw), lambda b, g, t: (0, 0))
    mat = pl.BlockSpec((lw, gw), lambda b, g, t: (0, g))
    mur, muk, muv, mul, w0, a0, kks, ka, w2p, a2p = mix_consts
    big = lambda: pltpu.VMEM((tb, gw), _F32)
    kern = functools.partial(_rwkv_prompt_kernel, tb=tb, pp=pp, head_dim=head_dim, delta=delta)
    return pl.pallas_call(
        kern,
        grid=(batch, width // gw, n_t),
        in_specs=[zblk(0), zblk(1), zblk(2), pl.BlockSpec((tb, lw), lambda b, g, t: (row(b, g, t), lcol)),
                  hblk(0), hblk(1), hblk(2), pl.BlockSpec((halo, lw), lambda b, g, t: (hrow(b, g, t), lcol)),
                  pl.BlockSpec((tb, gw), lambda b, g, t: (row(b, g, t), cg + g)),
                  pl.BlockSpec((tb, tail), lambda b, g, t: (row(b, g, t), (cg + g + 1) * (gw // tail))),
                  vec, vec, vec, lvec, vec, vec, vec, vec, mat, mat, vec, vec, vec],
        out_specs=[pl.BlockSpec((tb, gw), lambda b, g, t: (row(b, g, t), g)),
                   pl.BlockSpec((1, pp, _LANES, _LANES), lambda b, g, t: (b, g, 0, 0))],
        out_shape=[jax.ShapeDtypeStruct((batch * seq, width), _BF16),
                   jax.ShapeDtypeStruct((batch, width // _LANES, _LANES, _LANES), _F32)],
        scratch_shapes=[pltpu.VMEM((halo + tb, gw), _F32)] * 3 + [pltpu.VMEM((halo + tb, lw), _F32)]
                       + [big() for _ in range(7)] + [pltpu.VMEM((tb, lw), _F32),
                                                      pltpu.VMEM((pp, _LANES, _LANES), _F32)],
        compiler_params=_params(("parallel", "parallel", "arbitrary")),
        name="rwkv_prompt",
    )(z, z, z, z, z, z, z, z, z, z, mur, muk, muv, mul, w0, a0, kks, ka, w2p, a2p, *post_consts)


def _rwkv_step_kernel(r_ref, ld_ref, k_ref, v_ref, kk_ref, be_ref, s_ref, y_ref, so_ref, *, hb, head_dim, vi):
    for h in range(hb):
        kk = kk_ref[h][None]
        dec = jnp.exp(ld_ref[h])[None]
        be = be_ref[h][None]
        k = k_ref[h][None]
        r = r_ref[h][None]

        def body(c, carry, h=h, kk=kk, dec=dec, be=be, k=k, r=r):
            rows = pl.ds(pl.multiple_of(c * vi, vi), vi)
            s = s_ref[h, rows]
            sa = -jnp.sum(s * kk, axis=1, keepdims=True)
            s_new = s * dec + sa * be + v_ref[h, rows] * k
            y_ref[h, rows] = jnp.sum(s_new * r, axis=1, keepdims=True)
            so_ref[h, rows] = s_new
            return carry

        lax.fori_loop(0, head_dim // vi, body, 0)


def _rwkv_step(arrs_t, state_t, *, hb):
    heads, hd, _, n = state_t.shape
    r, ld, k, v, kk, be = arrs_t
    per_key = lambda a: a.reshape(heads, hd, n)
    vec = pl.BlockSpec((hb, hd, n), lambda i: (i, 0, 0))
    col = pl.BlockSpec((hb, hd, 1, n), lambda i: (i, 0, 0, 0))
    mat = pl.BlockSpec((hb, hd, hd, n), lambda i: (i, 0, 0, 0))
    kern = functools.partial(_rwkv_step_kernel, hb=hb, head_dim=hd, vi=8)
    return pl.pallas_call(
        kern,
        grid=(heads // hb,),
        in_specs=[vec, vec, vec, col, vec, vec, mat],
        out_specs=[col, mat],
        out_shape=[jax.ShapeDtypeStruct((heads, hd, 1, n), _F32), jax.ShapeDtypeStruct(state_t.shape, _F32)],
        compiler_params=_params(("parallel",)),
        name="rwkv_step",
    )(per_key(r), per_key(ld), per_key(k), v.reshape(heads, hd, 1, n), per_key(kk), per_key(be), state_t)


def _rwkv_post_kernel(y_ref, r_ref, k_ref, v_ref, za_ref, zb_ref, rk_ref, lw_ref, lb_ref, o_ref, *, head_dim, delta):
    avg_bd = _head_block_matrix(1.0 / head_dim, head_dim)
    ones_bd = _head_block_matrix(1.0, head_dim)
    width = y_ref.shape[1]
    zwin = jnp.concatenate([za_ref[...], zb_ref[...]], axis=1)
    for p in range(width // _LANES):
        cols = slice(p * _LANES, (p + 1) * _LANES)
        y = y_ref[:, cols]
        d = y - _mm_exact_rhs(y, avg_bd)
        var = _mm_exact_rhs(d * d, avg_bd)
        yn = d * lax.rsqrt(var + _GN_EPS) * lw_ref[:, cols] + lb_ref[:, cols]
        bonus = _mm_exact_rhs(r_ref[:, cols] * k_ref[:, cols] * rk_ref[:, cols], ones_bd) * v_ref[:, cols]
        zg = zwin[:, delta + p * _LANES:delta + (p + 1) * _LANES]
        o_ref[:, cols] = ((yn + bonus) * _silu(zg)).astype(o_ref.dtype)


def _rwkv_post(y, r, k2, v, z, consts, *, row0, rows, width, tb, head_dim, col0):
    off = row0 // tb
    tail = 2 * _LANES
    cg, delta = col0 // width, col0 % width
    assert delta <= tail and width % tail == 0
    blk = pl.BlockSpec((tb, width), lambda i: (i, 0))
    vec = pl.BlockSpec((1, width), lambda i: (0, 0))
    kern = functools.partial(_rwkv_post_kernel, head_dim=head_dim, delta=delta)
    return pl.pallas_call(
        kern,
        grid=(rows // tb,),
        in_specs=[blk, blk, blk, blk,
                  pl.BlockSpec((tb, width), lambda i: (off + i, cg)),
                  pl.BlockSpec((tb, tail), lambda i: (off + i, (cg + 1) * (width // tail))),
                  vec, vec, vec],
        out_specs=blk,
        out_shape=jax.ShapeDtypeStruct((rows, width), _BF16),
        compiler_params=_params(("parallel",)),
        name="rwkv_post",
    )(y, r, k2, v, z, z, *consts)


def _mem_attn_prompt_kernel(za_ref, zb_ref, zc_ref, k_ref, v_ref, o_ref, *, heads, head_dim, delta):
    scale = head_dim ** -0.5
    mw = heads * head_dim
    zwin = jnp.concatenate([za_ref[...], zb_ref[...], zc_ref[...]], axis=1)
    for h in range(heads):
        cols = slice(h * head_dim, (h + 1) * head_dim)
        q = zwin[:, delta + cols.start:delta + cols.stop]
        zg = zwin[:, delta + mw + cols.start:delta + mw + cols.stop]
        s = _dot_nt(q.astype(_BF16), k_ref[:, cols].astype(_BF16)) * scale
        p = jnp.exp(s - jnp.max(s, axis=-1, keepdims=True))
        o = _dot(p.astype(_BF16), v_ref[:, cols].astype(_BF16)) / jnp.sum(p, axis=-1, keepdims=True)
        o_ref[:, cols] = (o * _silu(zg)).astype(o_ref.dtype)


def _mem_attn_prompt(z, mkv, *, batch, seq, n_mem, heads, head_dim, tq, col0):
    n_t = seq // tq
    mw = heads * head_dim
    tail = 2 * _LANES
    q_col, delta = col0 // mw, col0 % mw
    assert delta <= tail and mw % tail == 0
    kern = functools.partial(_mem_attn_prompt_kernel, heads=heads, head_dim=head_dim, delta=delta)
    return pl.pallas_call(
        kern,
        grid=(batch, n_t),
        in_specs=[
            pl.BlockSpec((tq, mw), lambda b, t: (b * n_t + t, q_col)),
            pl.BlockSpec((tq, mw), lambda b, t: (b * n_t + t, q_col + 1)),
            pl.BlockSpec((tq, tail), lambda b, t: (b * n_t + t, (q_col + 2) * (mw // tail))),
            pl.BlockSpec((n_mem, mw), lambda b, t: (b, 0)),
            pl.BlockSpec((n_mem, mw), lambda b, t: (b, 1)),
        ],
        out_specs=pl.BlockSpec((tq, mw), lambda b, t: (b * n_t + t, 0)),
        out_shape=jax.ShapeDtypeStruct((batch * seq, mw), _BF16),
        compiler_params=_params(("parallel", "parallel")),
        name="mem_attn_prompt",
    )(z, z, z, mkv, mkv)


def _mem_attn_sample_kernel(q_ref, zg_ref, k_ref, v_ref, o_ref, *, bb, head_dim):
    scale = head_dim ** -0.5
    for b in range(bb):
        s = jnp.sum(k_ref[b] * q_ref[b][None], axis=-1, keepdims=True) * scale
        p = jnp.exp(s - jnp.max(s, axis=0, keepdims=True))
        o = jnp.sum(p * v_ref[b], axis=0) / jnp.sum(p, axis=0)
        o_ref[b] = o * _silu(zg_ref[b])


def _mem_attn_sample(q, zg, mem_k, mem_v, *, bb):
    n, n_mem, heads, head_dim = mem_k.shape
    vec = pl.BlockSpec((bb, heads, head_dim), lambda i: (i, 0, 0))
    mat = pl.BlockSpec((bb, n_mem, heads, head_dim), lambda i: (i, 0, 0, 0))
    kern = functools.partial(_mem_attn_sample_kernel, bb=bb, head_dim=head_dim)
    return pl.pallas_call(
        kern,
        grid=(n // bb,),
        in_specs=[vec, vec, mat, mat],
        out_specs=vec,
        out_shape=jax.ShapeDtypeStruct((n, heads, head_dim), _F32),
        compiler_params=_params(("parallel",)),
        name="mem_attn_sample",
    )(q, zg, mem_k, mem_v)


def _branch_kernel(po_ref, ro_ref, mo_ref, pos_ref, ros_ref, mos_ref, wp_ref, wr_ref, wm_ref, gp_ref, gr_ref, gm_ref,
                   tp_ref, tr_ref, tm_ref, bp_ref, br_ref, bm_ref, o_ref, os_ref, wpb_ref, wrb_ref, wmb_ref,
                   *, delta, row_tiles):
    i = pl.program_id(1)

    @pl.when(i == 0)
    def _():
        wpb_ref[...] = wp_ref[...].astype(_BF16)
        wrb_ref[...] = wr_ref[...].astype(_BF16)
        wmb_ref[...] = wm_ref[...].astype(_BF16)

    tn = o_ref.shape[1]

    def gated_sum(acts, rows):
        h = None
        for a_ref, wb_ref, g_ref, t_ref, b_ref in zip(acts, (wpb_ref, wrb_ref, wmb_ref), (gp_ref, gr_ref, gm_ref),
                                                      (tp_ref, tr_ref, tm_ref), (bp_ref, br_ref, bm_ref)):
            win = jnp.concatenate([g_ref[0:rows, :], t_ref[0:rows, :]], axis=1)
            term = jax.nn.sigmoid(win[:, delta:delta + tn] + b_ref[...]) * _dot(a_ref[...], wb_ref[...])
            h = term if h is None else h + term
        return h

    @pl.when(i < row_tiles)
    def _():
        o_ref[...] = gated_sum((po_ref, ro_ref, mo_ref), o_ref.shape[0]).astype(o_ref.dtype)

    @pl.when(i == row_tiles)
    def _():
        os_ref[...] = gated_sum((pos_ref, ros_ref, mos_ref), os_ref.shape[0]).astype(os_ref.dtype)


def _branch(acts_p, acts_s, wp, wr, wm, z, b_gate, *, tm, tn, gate_col0):
    rows, ns = acts_p[0].shape[0], acts_s[0].shape[0]
    d = wp.shape[1]
    tail = 2 * _LANES
    delta = gate_col0 % tn
    g0 = gate_col0 // tn
    nb = d // tn
    nt = rows // tm
    assert delta <= tail and tn % tail == 0 and nt * tm == rows and ns <= tm and z.shape[0] >= rows + ns
    act = lambda a: pl.BlockSpec((tm, a.shape[1]), lambda j, i: (jnp.minimum(i, nt - 1), 0))
    act_s = lambda a: pl.BlockSpec((ns, a.shape[1]), lambda j, i: (0, 0))
    wgt = lambda w: pl.BlockSpec((w.shape[0], tn), lambda j, i: (0, j), pipeline_mode=pl.Buffered(1))
    gat = lambda k: pl.BlockSpec((tm, tn), lambda j, i, k=k: (i, g0 + k * nb + j))
    tai = lambda k: pl.BlockSpec((tm, tail), lambda j, i, k=k: (i, (g0 + k * nb + j + 1) * (tn // tail)))
    bia = lambda k: pl.BlockSpec((1, tn), lambda j, i, k=k: (0, k * nb + j))
    return pl.pallas_call(
        functools.partial(_branch_kernel, delta=delta, row_tiles=nt),
        grid=(nb, nt + 1),
        in_specs=[act(acts_p[0]), act(acts_p[1]), act(acts_p[2]), act_s(acts_s[0]), act_s(acts_s[1]), act_s(acts_s[2]),
                  wgt(wp), wgt(wr), wgt(wm), gat(0), gat(1), gat(2), tai(0), tai(1), tai(2), bia(0), bia(1), bia(2)],
        out_specs=[pl.BlockSpec((tm, tn), lambda j, i: (jnp.minimum(i, nt - 1), j)),
                   pl.BlockSpec((ns, tn), lambda j, i: (0, j))],
        out_shape=[jax.ShapeDtypeStruct((rows, d), _BF16), jax.ShapeDtypeStruct((ns, d), _BF16)],
        scratch_shapes=[pltpu.VMEM((w.shape[0], tn), _BF16) for w in (wp, wr, wm)],
        compiler_params=_params(("parallel", "arbitrary")),
        name="branch_proj",
    )(*acts_p, *acts_s, wp, wr, wm, z, z, z, z, z, z, b_gate, b_gate, b_gate)


def _out_kernel(h_ref, w_ref, x_ref, g_ref, b_ref, o_ref, *, alpha):
    xf = alpha * x_ref[...] + _dot(h_ref[...], w_ref[...])
    mu = jnp.mean(xf, axis=-1, keepdims=True)
    d = xf - mu
    var = jnp.mean(d * d, axis=-1, keepdims=True)
    o_ref[...] = d * lax.rsqrt(var + _LN_EPS) * g_ref[...] + b_ref[...]


def _out_proj(h, w_out, x, ln_g, ln_b, *, tm, alpha):
    rows, d = x.shape
    kern = functools.partial(_out_kernel, alpha=alpha)
    return pl.pallas_call(
        kern,
        grid=(rows // tm,),
        in_specs=[
            pl.BlockSpec((tm, d), lambda i: (i, 0)),
            pl.BlockSpec((d, d), lambda i: (0, 0), pipeline_mode=pl.Buffered(1)),
            pl.BlockSpec((tm, d), lambda i: (i, 0)),
            pl.BlockSpec((1, d), lambda i: (0, 0)),
            pl.BlockSpec((1, d), lambda i: (0, 0)),
        ],
        out_specs=pl.BlockSpec((tm, d), lambda i: (i, 0)),
        out_shape=jax.ShapeDtypeStruct((rows, d), _F32),
        compiler_params=_params(("parallel",), vmem_limit=_VMEM_LIMIT_RESIDENT_WEIGHT),
        name="out_proj_ln",
    )(h, w_out, x, ln_g, ln_b)


def _tiles(batch, seq, n_sample):
    n_prompt = batch * seq
    m_all = n_prompt + n_sample
    return {
        "in_tm": next(t for t in (1040, 832, 640, 512, 256, 128, 64, 32, 16) if m_all % t == 0), "in_tn": 768,
        "row_tb": min(512, seq),
        "scan_tb": min(512, seq), "scan_pp": 8,
        "attn_tq": min(1024, seq),
        "proj_tm": min(256, n_prompt), "proj_tn": 1024, "out_tm": min(256, n_prompt),
        "pool_bb": min(32, n_sample), "step_hb": 4, "attn_bb": 4,
    }


def _layer(xp, xs, mem_p, cache_k, cache_v, st_pool, st_shift, st_rwkv, lp):
    batch, seq, d = xp.shape
    ns = xs.shape[0]
    n_p = batch * seq
    m_all = n_p + ns
    pw = lp["pool_scale"].shape[-1]
    rw = lp["rwkv_w0"].shape[-1]
    heads, hd = lp["rwkv_r_k"].shape
    lora = lp["rwkv_w2"].shape[0]
    sw = lp["rwkv_mu"].shape[-1]
    n_mem, mh, md = cache_k.shape[1:]
    mw = mh * md
    tl = _tiles(batch, seq, ns)

    lo0, lo1 = 2 * pw + 3 * rw, 2 * pw + sw
    lw = 2 * _LANES
    assert 2 * lora <= lw
    q_col0 = lo1 + rw
    gate_col0 = q_col0 + 2 * mw

    if ns < tl["in_tm"] <= n_p:
        x_all = _cast_rows(xp.reshape(n_p, d), xs.reshape(ns, d), tm=tl["in_tm"])
    else:
        x_all = jnp.concatenate([xp.reshape(n_p, d), xs.reshape(ns, d)], axis=0).astype(_BF16)
    z = _in_proj(x_all, jnp.swapaxes(lp["w_in"], 0, 1), tm=tl["in_tm"], tn=tl["in_tn"])
    z_s = z[n_p:]

    pool_w = lp["pool_w"].astype(_BF16)
    pool_scale = lp["pool_scale"].reshape(1, pw)
    po_p = _pool_prompt(z, pool_w, pool_scale, batch=batch, seq=seq, width=pw, tb=tl["row_tb"])
    po_s = _pool_sample(z_s, jnp.swapaxes(st_pool, 0, 1), pool_w, pool_scale, row0=0, bb=tl["pool_bb"])
    nbuf = st_pool.shape[1]
    new_pool_p = jnp.stack([z[(b + 1) * seq - nbuf:(b + 1) * seq, :pw] for b in range(batch)])
    new_pool_s = jnp.concatenate([st_pool[:, 1:, :], z_s[:, :pw].reshape(ns, 1, pw)], axis=1)

    mu = lp["rwkv_mu"]
    pad_l = lambda a: jnp.pad(a, ((0, 0), (0, lw - 2 * lora)))
    w2p = jnp.zeros((lw, rw), _F32).at[:lora].set(lp["rwkv_w2"]).astype(_BF16)
    a2p = jnp.zeros((lw, rw), _F32).at[lora:2 * lora].set(lp["rwkv_a2"]).astype(_BF16)
    row = lambda a: a.reshape(1, -1)
    mix_consts = (row(mu[:rw]), row(mu[rw:2 * rw]), row(mu[2 * rw:3 * rw]), pad_l(row(mu[3 * rw:])),
                  row(lp["rwkv_w0"]), row(lp["rwkv_a0"]), row(lp["rwkv_k_k"]), row(lp["rwkv_k_a"]), w2p, a2p)
    post_consts = (row(lp["rwkv_r_k"]), row(lp["rwkv_ln_w"]), row(lp["rwkv_ln_b"]))

    ro_p, st_pairs = _rwkv_prompt(z, mix_consts, post_consts, batch=batch, seq=seq, width=rw, tb=tl["scan_tb"],
                                  pp=tl["scan_pp"], head_dim=hd, col0=2 * pw, gate_col0=lo1, lw=lw)
    st5 = st_pairs.reshape(batch, heads // 2, 2, hd, 2, hd)
    new_state_p = jnp.stack([st5[:, :, 0, :, 0, :], st5[:, :, 1, :, 1, :]], axis=2).reshape(batch, heads, hd, hd)

    shift_s = st_shift.reshape(ns, sw)
    prep_s = _rwkv_prep_sample(z_s, shift_s[:, :3 * rw], pad_l(shift_s[:, 3 * rw:]), mix_consts, row0=0, rows=ns,
                               width=rw, head_dim=hd, col0=2 * pw, lw=lw)
    r_s, _, k_s, v_s, _, _ = prep_s[:6]
    y_t, state_t = _rwkv_step(prep_s[6:], jnp.transpose(st_rwkv, (1, 2, 3, 0)), hb=tl["step_hb"])
    new_state_s = jnp.transpose(state_t, (3, 0, 1, 2))
    ro_s = _rwkv_post(y_t.reshape(rw, ns).T, r_s, k_s, v_s, z_s, post_consts, row0=0, rows=ns, width=rw, tb=ns,
                      head_dim=hd, col0=lo1)
    new_shift_p = jnp.stack([z[(b + 1) * seq - 1:(b + 1) * seq, 2 * pw:lo1] for b in range(batch)])
    new_shift_s = z_s[:, 2 * pw:lo1].reshape(ns, 1, sw)

    mkv = _matmul_f32w(mem_p.reshape(batch * n_mem, d).astype(_BF16), lp["w_mem_kv"],
                       tm=next(t for t in (512, 256, 128, 64, 32, 16, 8) if (batch * n_mem) % t == 0),
                       tn=min(768, 2 * mw), name="mem_kv_proj")
    mo_p = _mem_attn_prompt(z, mkv, batch=batch, seq=seq, n_mem=n_mem, heads=mh, head_dim=md, tq=tl["attn_tq"],
                            col0=q_col0)
    mo_s = _mem_attn_sample(z_s[:, q_col0:q_col0 + mw].reshape(ns, mh, md),
                            z_s[:, q_col0 + mw:q_col0 + 2 * mw].reshape(ns, mh, md),
                            cache_k, cache_v, bb=tl["attn_bb"]).reshape(ns, mw).astype(_BF16)
    mk_p = mkv[:, :mw].reshape(batch, n_mem, mh, md)
    mv_p = mkv[:, mw:].reshape(batch, n_mem, mh, md)

    wp, wr, wm = lp["w_branch_pool"], lp["w_branch_rwkv"], lp["w_branch_mem"]
    w_out = lp["w_out"].astype(_BF16)
    b_gate = row(lp["b_gate"])
    ln_g, ln_b = row(lp["ln_g"]), row(lp["ln_b"])
    alpha = lp["alpha"]
    h_p, h_s = _branch((po_p, ro_p, mo_p), (po_s, ro_s, mo_s), wp, wr, wm, z, b_gate, tm=tl["proj_tm"],
                       tn=tl["proj_tn"], gate_col0=gate_col0)
    y_prompt = _out_proj(h_p, w_out, xp.reshape(n_p, d), ln_g, ln_b, tm=tl["out_tm"], alpha=alpha)
    y_sample = _out_proj(h_s, w_out, xs.reshape(ns, d), ln_g, ln_b, tm=min(tl["out_tm"], ns), alpha=alpha)
    y_prompt = y_prompt.reshape(batch, seq, d)
    y_sample = y_sample.reshape(ns, 1, d)
    return (y_prompt, y_sample, mk_p, mv_p, new_pool_p, new_shift_p, new_state_p, new_pool_s, new_shift_s,
            new_state_s)


def kernel(x_prompt, x_sample, cache_mem_k, cache_mem_v, state_pool, state_shift, state_rwkv, mem_prompt, w_in,
           b_gate, pool_w, pool_scale, rwkv_mu, rwkv_w0, rwkv_w2, rwkv_a0, rwkv_a2, rwkv_k_k, rwkv_k_a, rwkv_r_k,
           rwkv_ln_w, rwkv_ln_b, w_mem_kv, w_branch_pool, w_branch_rwkv, w_branch_mem, w_out, ln_g, ln_b):
    depth = w_in.shape[0]
    alpha = (2.0 * depth) ** 0.25
    weights = dict(w_in=w_in, b_gate=b_gate, pool_w=pool_w, pool_scale=pool_scale, rwkv_mu=rwkv_mu, rwkv_w0=rwkv_w0,
                   rwkv_w2=rwkv_w2, rwkv_a0=rwkv_a0, rwkv_a2=rwkv_a2, rwkv_k_k=rwkv_k_k, rwkv_k_a=rwkv_k_a,
                   rwkv_r_k=rwkv_r_k, rwkv_ln_w=rwkv_ln_w, rwkv_ln_b=rwkv_ln_b, w_mem_kv=w_mem_kv,
                   w_branch_pool=w_branch_pool, w_branch_rwkv=w_branch_rwkv, w_branch_mem=w_branch_mem,
                   w_out=w_out, ln_g=ln_g, ln_b=ln_b)
    yp, ys = x_prompt, x_sample
    per_layer = []
    for l in range(depth):
        lp = {k: v[l] for k, v in weights.items()}
        lp["alpha"] = alpha
        res = _layer(yp, ys, mem_prompt, cache_mem_k[l], cache_mem_v[l], state_pool[l], state_shift[l],
                     state_rwkv[l], lp)
        yp, ys = res[0], res[1]
        per_layer.append(res[2:])
    if depth == 1:
        stacked = tuple(a[None] for a in per_layer[0])
    else:
        stacked = tuple(jnp.stack([res[i] for res in per_layer]) for i in range(len(per_layer[0])))
    return (yp, ys) + stacked
```

```python
import functools

import jax
import jax.numpy as jnp
from jax import lax
from jax.experimental import pallas as pl
from jax.experimental.pallas import tpu as pltpu

_F32 = jnp.float32
_BF16 = jnp.bfloat16

_POOL_WINDOWS = (2, 4, 8, 16)
_PAST_LEN = 16384
_GN_EPS = 64e-5
_LN_EPS = 1e-5
_L2_EPS = 1e-12
_EXP_MINUS_HALF = 0.6065306597126334

_LANES = 128
_MIB = 1024 * 1024
_VMEM_LIMIT = 56 * _MIB
_VMEM_LIMIT_RESIDENT_WEIGHT = 62 * _MIB

_HEAD_PAIR = _LANES
_CHUNK = 64


def _params(sem, vmem_limit=_VMEM_LIMIT):
    return pltpu.CompilerParams(dimension_semantics=sem, vmem_limit_bytes=vmem_limit)


def _dot(a, b):
    return jnp.dot(a, b, preferred_element_type=_F32)


def _dot_nt(a, b):
    return lax.dot_general(a, b, (((1,), (1,)), ((), ())), preferred_element_type=_F32)


def _dot_tn(a, b):
    return lax.dot_general(a, b, (((0,), (0,)), ((), ())), preferred_element_type=_F32)


def _split(x):
    hi = x.astype(_BF16)
    lo = (x - hi.astype(_F32)).astype(_BF16)
    return hi, lo


def _mm1(dot, a, b):
    return dot(a.astype(_BF16), b.astype(_BF16))


def _mm_exact_rhs(a, b_bf16):
    a_hi, a_lo = _split(a)
    return _dot(a_hi, b_bf16) + _dot(a_lo, b_bf16)


def _silu(x):
    return x * jax.nn.sigmoid(x)


def _head_block_matrix(value, head_dim):
    r = lax.broadcasted_iota(jnp.int32, (_LANES, _LANES), 0) // head_dim
    c = lax.broadcasted_iota(jnp.int32, (_LANES, _LANES), 1) // head_dim
    return jnp.where(r == c, value, 0.0).astype(_BF16)


def _matmul_f32w_kernel(x_ref, w_ref, o_ref, wb_ref):
    @pl.when(pl.program_id(1) == 0)
    def _():
        wb_ref[...] = w_ref[...].astype(_BF16)

    o_ref[...] = _dot(x_ref[...], wb_ref[...])


def _matmul_f32w(x, w, *, tm, tn, name):
    m, k = x.shape
    n = w.shape[1]
    return pl.pallas_call(
        _matmul_f32w_kernel,
        grid=(n // tn, m // tm),
        in_specs=[pl.BlockSpec((tm, k), lambda j, i: (i, 0)), pl.BlockSpec((k, tn), lambda j, i: (0, j))],
        out_specs=pl.BlockSpec((tm, tn), lambda j, i: (i, j)),
        out_shape=jax.ShapeDtypeStruct((m, n), _F32),
        scratch_shapes=[pltpu.VMEM((k, tn), _BF16)],
        compiler_params=_params(("parallel", "arbitrary")),
        name=name,
    )(x, w)


def _cast_rows_kernel(xp_ref, xs_ref, o_ref, *, tail):
    i = pl.program_id(0)
    last = pl.num_programs(0) - 1
    tm = o_ref.shape[0]

    @pl.when(i < last)
    def _():
        o_ref[...] = xp_ref[...].astype(_BF16)

    @pl.when(i == last)
    def _():
        o_ref[0:tm - tail, :] = xp_ref[tail:tm, :].astype(_BF16)
        o_ref[tm - tail:tm, :] = xs_ref[...].astype(_BF16)


def _cast_rows(xp, xs, *, tm):
    n_p, d = xp.shape
    tail = xs.shape[0]
    n_tiles = (n_p + tail) // tm
    assert n_tiles * tm == n_p + tail and 0 < tail < tm <= n_p and (n_p - tm) % 8 == 0
    return pl.pallas_call(
        functools.partial(_cast_rows_kernel, tail=tail),
        grid=(n_tiles,),
        in_specs=[pl.BlockSpec((pl.Element(tm), pl.Element(d)),
                               lambda i: (pl.multiple_of(jnp.minimum(i * tm, n_p - tm), 8), 0)),
                  pl.BlockSpec((tail, d), lambda i: (0, 0))],
        out_specs=pl.BlockSpec((tm, d), lambda i: (i, 0)),
        out_shape=jax.ShapeDtypeStruct((n_p + tail, d), _BF16),
        compiler_params=_params(("parallel",), vmem_limit=_VMEM_LIMIT_RESIDENT_WEIGHT),
        name="cast_rows",
    )(xp, xs)


def _in_proj_kernel(x_ref, wt_ref, o_ref, wb_ref):
    @pl.when(pl.program_id(1) == 0)
    def _():
        wb_ref[...] = wt_ref[...].astype(_BF16)

    o_ref[...] = _dot_nt(x_ref[...], wb_ref[...])


def _in_proj(x, wt, *, tm, tn):
    m, k = x.shape
    n = wt.shape[0]
    return pl.pallas_call(
        _in_proj_kernel,
        grid=(pl.cdiv(n, tn), m // tm),
        in_specs=[pl.BlockSpec((tm, k), lambda j, i: (i, 0)), pl.BlockSpec((tn, k), lambda j, i: (j, 0))],
        out_specs=pl.BlockSpec((tm, tn), lambda j, i: (i, j)),
        out_shape=jax.ShapeDtypeStruct((m, n), _F32),
        scratch_shapes=[pltpu.VMEM((tn, k), _BF16)],
        compiler_params=_params(("parallel", "arbitrary"), vmem_limit=_VMEM_LIMIT_RESIDENT_WEIGHT),
        name="in_proj",
    )(x, wt)


def _pool_mix(pooled_fn, zp_ref, pw_ref, ps_ref, o_ref, group):
    for g in range(len(_POOL_WINDOWS)):
        cols = slice(g * group, (g + 1) * group)
        mixed = _dot(pooled_fn(g, cols).astype(_BF16), pw_ref[g])
        o_ref[:, cols] = (mixed * ps_ref[:, cols] * _silu(zp_ref[:, cols])).astype(o_ref.dtype)


def _pool_prompt_kernel(u_ref, halo_ref, zp_ref, pw_ref, ps_ref, o_ref, e_ref, *, tb, hist, group):
    t = pl.program_id(1)
    e_ref[0:hist, :] = jnp.where(t == 0, 0.0, halo_ref[...])
    e_ref[hist:hist + tb, :] = u_ref[...]
    pos = t * tb + lax.broadcasted_iota(jnp.int32, (tb, group), 0)

    def window_sum(cols, win):
        if win < 8:
            acc = e_ref[hist:hist + tb, cols]
            for d in range(1, win):
                acc = acc + e_ref[hist - d:hist - d + tb, cols]
            return acc
        assert win & (win - 1) == 0
        s, span = e_ref[:, cols], 1
        while span < win:
            s = s[span:] + s[:-span]
            span *= 2
        first = hist - (win - 1)
        return s[first:first + tb]

    def pooled(g, cols):
        win = _POOL_WINDOWS[g]
        x = e_ref[hist:hist + tb, cols]
        cnt = jnp.minimum(pos + 1, win).astype(_F32)
        return window_sum(cols, win) / cnt - x

    _pool_mix(pooled, zp_ref, pw_ref, ps_ref, o_ref, group)


def _pool_prompt(z, pool_w, pool_scale, *, batch, seq, width, tb):
    hist = 16
    n_t = seq // tb
    group = width // len(_POOL_WINDOWS)
    kern = functools.partial(_pool_prompt_kernel, tb=tb, hist=hist, group=group)
    return pl.pallas_call(
        kern,
        grid=(batch, n_t),
        in_specs=[
            pl.BlockSpec((tb, width), lambda b, t: (b * n_t + t, 0)),
            pl.BlockSpec((hist, width), lambda b, t: (jnp.maximum((b * seq + t * tb) // hist - 1, 0), 0)),
            pl.BlockSpec((tb, width), lambda b, t: (b * n_t + t, 1)),
            pl.BlockSpec(pool_w.shape, lambda b, t: (0, 0, 0)),
            pl.BlockSpec((1, width), lambda b, t: (0, 0)),
        ],
        out_specs=pl.BlockSpec((tb, width), lambda b, t: (b * n_t + t, 0)),
        out_shape=jax.ShapeDtypeStruct((batch * seq, width), _BF16),
        scratch_shapes=[pltpu.VMEM((hist + tb, width), _F32)],
        compiler_params=_params(("parallel", "parallel")),
        name="pool_prompt",
    )(z, z, z, pool_w, pool_scale)


def _pool_sample_kernel(u_ref, buf_ref, zp_ref, pw_ref, ps_ref, o_ref, *, nbuf, group):
    def pooled(g, cols):
        win = _POOL_WINDOWS[g]
        x = u_ref[:, cols]
        acc = x
        for d in range(1, win):
            acc = acc + buf_ref[nbuf - d, :, cols]
        cnt = float(min(_PAST_LEN + 1, win))
        return acc / cnt - x

    _pool_mix(pooled, zp_ref, pw_ref, ps_ref, o_ref, group)


def _pool_sample(z, buf, pool_w, pool_scale, *, row0, bb):
    nbuf, rows, width = buf.shape
    group = width // len(_POOL_WINDOWS)
    off = row0 // bb
    kern = functools.partial(_pool_sample_kernel, nbuf=nbuf, group=group)
    return pl.pallas_call(
        kern,
        grid=(rows // bb,),
        in_specs=[
            pl.BlockSpec((bb, width), lambda i: (off + i, 0)),
            pl.BlockSpec((nbuf, bb, width), lambda i: (0, i, 0)),
            pl.BlockSpec((bb, width), lambda i: (off + i, 1)),
            pl.BlockSpec(pool_w.shape, lambda i: (0, 0, 0)),
            pl.BlockSpec((1, width), lambda i: (0, 0)),
        ],
        out_specs=pl.BlockSpec((bb, width), lambda i: (i, 0)),
        out_shape=jax.ShapeDtypeStruct((rows, width), _BF16),
        compiler_params=_params(("parallel",)),
        name="pool_sample",
    )(z, buf, z, pool_w, pool_scale)


def _rwkv_mix(cur, prev, mu, w0, a0, kk_scale, ka, w2p, a2p, head_dim, outs):
    o_r, o_ld, o_k, o_v, o_kk, o_be = outs
    mix = lambda n: cur[n] + (prev[n] - cur[n]) * mu[n]
    xl = mix("l")
    wl = w0 + _mm1(_dot, jnp.tanh(xl), w2p)
    a = jax.nn.sigmoid(a0 + _mm1(_dot, xl, a2p))
    xk = mix("k")
    kkp = xk * kk_scale
    ones_bd = _head_block_matrix(1.0, head_dim)
    width = kkp.shape[1]
    o_r[...] = mix("r")
    o_v[...] = mix("v")
    o_ld[...] = -_EXP_MINUS_HALF * jax.nn.sigmoid(wl)
    o_k[...] = xk * (1.0 + (a - 1.0) * ka)
    for p in range(width // _LANES):
        cols = slice(p * _LANES, (p + 1) * _LANES)
        kp = kkp[:, cols]
        ss = _mm_exact_rhs(kp * kp, ones_bd)
        kk = kp * lax.rsqrt(jnp.maximum(ss, _L2_EPS * _L2_EPS))
        o_kk[:, cols] = kk
        o_be[:, cols] = kk * a[:, cols]


def _rwkv_prep_sample_kernel(zr, zk, zv, zl, pr, pk, pv, plr, mur, muk, muv, mul, w0, a0, kks, ka, w2p, a2p,
                             o_r, o_ld, o_k, o_v, o_kk, o_be, t_r, t_ld, t_k, t_v, t_kk, t_be, *, head_dim):
    cur = {"r": zr[...], "k": zk[...], "v": zv[...], "l": zl[...]}
    prev = {"r": pr[...], "k": pk[...], "v": pv[...], "l": plr[...]}
    mu = {"r": mur[...], "k": muk[...], "v": muv[...], "l": mul[...]}
    _rwkv_mix(cur, prev, mu, w0[...], a0[...], kks[...], ka[...], w2p[...], a2p[...], head_dim,
              (o_r, o_ld, o_k, o_v, o_kk, o_be))
    for o_ref, t_ref in ((o_r, t_r), (o_ld, t_ld), (o_k, t_k), (o_v, t_v), (o_kk, t_kk), (o_be, t_be)):
        t_ref[...] = o_ref[...].T


def _rwkv_prep_sample(z, prev_rkv, prev_l, consts, *, row0, rows, width, head_dim, col0, lw):
    c0 = col0 // width
    lcol = (col0 + 3 * width) // lw
    assert c0 * width == col0 and lcol * lw == col0 + 3 * width
    off = row0 // rows
    cblk = lambda c: pl.BlockSpec((rows, width), lambda i, c=c: (off, c))
    pblk = lambda c: pl.BlockSpec((rows, width), lambda i, c=c: (0, c))
    vec = lambda n: pl.BlockSpec((1, n), lambda i: (0, 0))
    mat = pl.BlockSpec((lw, width), lambda i: (0, 0))
    out_spec = pl.BlockSpec((rows, width), lambda i: (0, 0))
    out_shape = jax.ShapeDtypeStruct((rows, width), _F32)
    kern = functools.partial(_rwkv_prep_sample_kernel, head_dim=head_dim)
    return pl.pallas_call(
        kern,
        grid=(1,),
        in_specs=[cblk(c0), cblk(c0 + 1), cblk(c0 + 2), pl.BlockSpec((rows, lw), lambda i: (off, lcol)),
                  pblk(0), pblk(1), pblk(2), pl.BlockSpec((rows, lw), lambda i: (0, 0)),
                  vec(width), vec(width), vec(width), vec(lw),
                  vec(width), vec(width), vec(width), vec(width), mat, mat],
        out_specs=[out_spec] * 6 + [pl.BlockSpec((width, rows), lambda i: (0, 0))] * 6,
        out_shape=[out_shape] * 6 + [jax.ShapeDtypeStruct((width, rows), _F32)] * 6,
        compiler_params=_params(("arbitrary",)),
        name="rwkv_prep_sample",
    )(z, z, z, z, prev_rkv, prev_rkv, prev_rkv, prev_l, *consts)


def _scan_chunks(inputs, states, c):
    pairs = range(len(inputs))
    n2 = _LANES
    bf = lambda x: x.astype(_BF16)
    m0, m1 = c["m0"], c["m1"]
    expand = lambda x: jnp.concatenate([x * m0, x * m1], axis=0)
    stack = lambda a, b: jnp.concatenate([expand(a), expand(b)], axis=0)

    cum = [_mm_exact_rhs_t(c["ltri"], inputs[q][1]) for q in pairs]
    tot = [cum[q][_CHUNK - 1:_CHUNK, :] for q in pairs]
    xa, xb, bk, ve = [], [], [], []
    for q in pairs:
        r, ld, k2, v, kk, be = inputs[q]
        e_neg = jnp.exp(-cum[q])
        e_rem = jnp.exp(tot[q] - cum[q])
        xa.append(bf(stack(-kk * jnp.exp(cum[q] - ld), r * jnp.exp(cum[q]))))
        xb.append(bf(stack(be * e_neg, k2 * e_neg)))
        bk.append(bf(stack(be * e_rem, k2 * e_rem)))
        ve.append(bf(expand(v)))
    g = [_dot_nt(xa[q], xb[q]) for q in pairs]
    ps = [_dot_nt(xa[q], bf(states[q])) for q in pairs]
    a_kk = [bf(jnp.concatenate([jnp.where(c["strict"], g[q][:n2, n2:], 0.0),
                                jnp.where(c["incl"], g[q][n2:, n2:], 0.0)], axis=0)) for q in pairs]
    av = [_dot(a_kk[q], ve[q]) for q in pairs]
    npow = [jnp.where(c["strict"], g[q][:n2, :n2], 0.0) for q in pairs]
    u = [ps[q][:n2] + av[q][:n2] for q in pairs]
    for k in range(c["squarings"]):
        out = [_dot(bf(npow[q]), bf(jnp.concatenate([npow[q], u[q]], axis=1))) for q in pairs]
        npow = [out[q][:, :n2] for q in pairs]
        u = [u[q] + out[q][:, n2:] for q in pairs]
    ub = [bf(u[q]) for q in pairs]
    u = [u[q] + _dot(bf(npow[q]), ub[q]) for q in pairs]
    ub = [bf(u[q]) for q in pairs]
    a_rb = [bf(jnp.where(c["incl"], g[q][n2:, :n2], 0.0)) for q in pairs]
    ye = [ps[q][n2:] + av[q][n2:] + _dot(a_rb[q], ub[q]) for q in pairs]
    ys = [ye[q][:_CHUNK] + ye[q][_CHUNK:] for q in pairs]
    new_states = [states[q] * jnp.exp(tot[q]) + _dot_tn(jnp.concatenate([ub[q], ve[q]], axis=0), bk[q])
                  for q in pairs]
    return ys, new_states


def _mm_exact_rhs_t(l_bf16, x):
    x_hi, x_lo = _split(x)
    return _dot(l_bf16, x_hi) + _dot(l_bf16, x_lo)


def _scan_consts():
    n = _LANES
    half = _CHUNK
    ri = lax.broadcasted_iota(jnp.int32, (n, n), 0)
    ci = lax.broadcasted_iota(jnp.int32, (n, n), 1)
    same = (ri // half) == (ci // half)
    lane = lax.broadcasted_iota(jnp.int32, (half, n), 1)
    tr = lax.broadcasted_iota(jnp.int32, (half, half), 0)
    tc = lax.broadcasted_iota(jnp.int32, (half, half), 1)
    squarings = 0
    while (2 << squarings) < half:
        squarings += 1
    return {
        "strict": same & ((ri % half) > (ci % half)),
        "incl": same & ((ri % half) >= (ci % half)),
        "m0": jnp.where(lane < half, 1.0, 0.0).astype(_F32),
        "m1": jnp.where(lane >= half, 1.0, 0.0).astype(_F32),
        "ltri": jnp.where(tr >= tc, 1.0, 0.0).astype(_BF16),
        "squarings": squarings,
    }


def _rwkv_prompt_kernel(zr, zk, zv, zl, hr, hk, hv, hl, za, zb, mur, muk, muv, mul, w0, a0, kks, ka, w2p, a2p,
                        rk, lnw, lnb, o_ref, s_ref, er, ek, ev, el, r_s, ld_s, k_s, v_s, kk_s, be_s, y_s, xl_s, st_ref,
                        *, tb, pp, head_dim, delta):
    t = pl.program_id(2)
    halo = 8
    n_chunks = tb // _CHUNK
    pairs = range(pp)
    cols = [slice(q * _LANES, (q + 1) * _LANES) for q in pairs]

    @pl.when(t == 0)
    def _():
        st_ref[...] = jnp.zeros_like(st_ref)

    for z_ref, h_ref, e_ref, mu_ref, dst in ((zr, hr, er, mur, r_s), (zk, hk, ek, muk, k_s), (zv, hv, ev, muv, v_s),
                                             (zl, hl, el, mul, xl_s)):
        e_ref[0:halo, :] = jnp.where(t == 0, 0.0, h_ref[...])
        e_ref[halo:halo + tb, :] = z_ref[...]
        cur = z_ref[...]
        dst[...] = cur + (e_ref[halo - 1:halo - 1 + tb, :] - cur) * mu_ref[...]

    c = _scan_consts()
    ones_bd = _head_block_matrix(1.0, head_dim)
    avg_bd = _head_block_matrix(1.0 / head_dim, head_dim)

    xl = xl_s[...]
    wl = w0[...] + _mm1(_dot, jnp.tanh(xl), w2p[...])
    a = jax.nn.sigmoid(a0[...] + _mm1(_dot, xl, a2p[...]))
    ld_s[...] = -_EXP_MINUS_HALF * jax.nn.sigmoid(wl)
    xk = k_s[...]
    k_s[...] = xk * (1.0 + (a - 1.0) * ka[...])
    kp = [xk[:, cols[q]] * kks[:, cols[q]] for q in pairs]
    ss = [_mm_exact_rhs(kp[q] * kp[q], ones_bd) for q in pairs]
    for q in pairs:
        kk = kp[q] * lax.rsqrt(jnp.maximum(ss[q], _L2_EPS * _L2_EPS))
        kk_s[:, cols[q]] = kk
        be_s[:, cols[q]] = kk * a[:, cols[q]]

    def body(ci, carry):
        rows = pl.ds(pl.multiple_of(ci * _CHUNK, _CHUNK), _CHUNK)
        in_refs = (r_s, ld_s, k_s, v_s, kk_s, be_s)
        ys, new_states = _scan_chunks([tuple(ref[rows, cols[q]] for ref in in_refs) for q in pairs],
                                      [st_ref[q] for q in pairs], c)
        for q in pairs:
            y_s[rows, cols[q]] = ys[q]
            st_ref[q] = new_states[q]
        return carry

    lax.fori_loop(0, n_chunks, body, 0)

    q0, off = delta // _LANES, delta % _LANES
    n_a = za.shape[1] // _LANES
    zblk = lambda i: (za[:, i * _LANES:(i + 1) * _LANES] if i < n_a
                      else zb[:, (i - n_a) * _LANES:(i - n_a + 1) * _LANES])
    y = [y_s[:, cols[q]] for q in pairs]
    d = [y[q] - _mm_exact_rhs(y[q], avg_bd) for q in pairs]
    var = [_mm_exact_rhs(d[q] * d[q], avg_bd) for q in pairs]
    bsum = [_mm_exact_rhs(r_s[:, cols[q]] * k_s[:, cols[q]] * rk[:, cols[q]], ones_bd) for q in pairs]
    for q in pairs:
        yn = d[q] * lax.rsqrt(var[q] + _GN_EPS) * lnw[:, cols[q]] + lnb[:, cols[q]]
        zg = jnp.concatenate([zblk(q0 + q), zblk(q0 + q + 1)], axis=1)[:, off:off + _LANES]
        o_ref[:, cols[q]] = ((yn + bsum[q] * v_s[:, cols[q]]) * _silu(zg)).astype(o_ref.dtype)

    @pl.when(t == pl.num_programs(2) - 1)
    def _():
        s_ref[0] = st_ref[...]


def _rwkv_prompt(z, mix_consts, post_consts, *, batch, seq, width, tb, pp, head_dim, col0, gate_col0, lw):
    n_t = seq // tb
    gw = pp * _LANES
    halo = 8
    tail = 2 * _LANES
    c0, lcol, cg = col0 // gw, (col0 + 3 * width) // lw, gate_col0 // gw
    delta = gate_col0 % gw
    assert c0 * gw == col0 and width % gw == 0 and lcol * lw == col0 + 3 * width
    assert delta <= tail and delta % _LANES + _LANES <= tail and gw % tail == 0
    row = lambda b, g, t: b * n_t + t
    hrow = lambda b, g, t: jnp.maximum((b * seq + t * tb) // halo - 1, 0)
    step = width // gw
    zblk = lambda i: pl.BlockSpec((tb, gw), lambda b, g, t, i=i: (row(b, g, t), c0 + i * step + g))
    hblk = lambda i: pl.BlockSpec((halo, gw), lambda b, g, t, i=i: (hrow(b, g, t), c0 + i * step + g))
    vec = pl.BlockSpec((1, gw), lambda b, g, t: (0, g))
    lvec = pl.BlockSpec((1, lw), lambda b, g, t: (0, 0))
    mat = pl.BlockSpec((lw, gw), lambda b, g, t: (0, g))
    mur, muk, muv, mul, w0, a0, kks, ka, w2p, a2p = mix_consts
    big = lambda: pltpu.VMEM((tb, gw), _F32)
    kern = functools.partial(_rwkv_prompt_kernel, tb=tb, pp=pp, head_dim=head_dim, delta=delta)
    return pl.pallas_call(
        kern,
        grid=(batch, width // gw, n_t),
        in_specs=[zblk(0), zblk(1), zblk(2), pl.BlockSpec((tb, lw), lambda b, g, t: (row(b, g, t), lcol)),
                  hblk(0), hblk(1), hblk(2), pl.BlockSpec((halo, lw), lambda b, g, t: (hrow(b, g, t), lcol)),
                  pl.BlockSpec((tb, gw), lambda b, g, t: (row(b, g, t), cg + g)),
                  pl.BlockSpec((tb, tail), lambda b, g, t: (row(b, g, t), (cg + g + 1) * (gw // tail))),
                  vec, vec, vec, lvec, vec, vec, vec, vec, mat, mat, vec, vec, vec],
        out_specs=[pl.BlockSpec((tb, gw), lambda b, g, t: (row(b, g, t), g)),
                   pl.BlockSpec((1, pp, _LANES, _LANES), lambda b, g, t: (b, g, 0, 0))],
        out_shape=[jax.ShapeDtypeStruct((batch * seq, width), _BF16),
                   jax.ShapeDtypeStruct((batch, width // _LANES, _LANES, _LANES), _F32)],
        scratch_shapes=[pltpu.VMEM((halo + tb, gw), _F32)] * 3 + [pltpu.VMEM((halo + tb, lw), _F32)]
                       + [big() for _ in range(7)] + [pltpu.VMEM((tb, lw), _F32),
                                                      pltpu.VMEM((pp, _LANES, _LANES), _F32)],
        compiler_params=_params(("parallel", "parallel", "arbitrary"), vmem_limit=_VMEM_LIMIT_RESIDENT_WEIGHT),
        name="rwkv_prompt",
    )(z, z, z, z, z, z, z, z, z, z, mur, muk, muv, mul, w0, a0, kks, ka, w2p, a2p, *post_consts)


def _rwkv_step_kernel(r_ref, ld_ref, k_ref, v_ref, kk_ref, be_ref, s_ref, y_ref, so_ref, *, hb, head_dim, vi):
    for h in range(hb):
        kk = kk_ref[h][None]
        dec = jnp.exp(ld_ref[h])[None]
        be = be_ref[h][None]
        k = k_ref[h][None]
        r = r_ref[h][None]

        def body(c, carry, h=h, kk=kk, dec=dec, be=be, k=k, r=r):
            rows = pl.ds(pl.multiple_of(c * vi, vi), vi)
            s = s_ref[h, rows]
            sa = -jnp.sum(s * kk, axis=1, keepdims=True)
            s_new = s * dec + sa * be + v_ref[h, rows] * k
            y_ref[h, rows] = jnp.sum(s_new * r, axis=1, keepdims=True)
            so_ref[h, rows] = s_new
            return carry

        lax.fori_loop(0, head_dim // vi, body, 0)


def _rwkv_step(arrs_t, state_t, *, hb):
    heads, hd, _, n = state_t.shape
    r, ld, k, v, kk, be = arrs_t
    per_key = lambda a: a.reshape(heads, hd, n)
    vec = pl.BlockSpec((hb, hd, n), lambda i: (i, 0, 0))
    col = pl.BlockSpec((hb, hd, 1, n), lambda i: (i, 0, 0, 0))
    mat = pl.BlockSpec((hb, hd, hd, n), lambda i: (i, 0, 0, 0))
    kern = functools.partial(_rwkv_step_kernel, hb=hb, head_dim=hd, vi=8)
    return pl.pallas_call(
        kern,
        grid=(heads // hb,),
        in_specs=[vec, vec, vec, col, vec, vec, mat],
        out_specs=[col, mat],
        out_shape=[jax.ShapeDtypeStruct((heads, hd, 1, n), _F32), jax.ShapeDtypeStruct(state_t.shape, _F32)],
        compiler_params=_params(("parallel",)),
        name="rwkv_step",
    )(per_key(r), per_key(ld), per_key(k), v.reshape(heads, hd, 1, n), per_key(kk), per_key(be), state_t)


def _rwkv_post_kernel(y_ref, r_ref, k_ref, v_ref, za_ref, zb_ref, rk_ref, lw_ref, lb_ref, o_ref, *, head_dim, delta):
    avg_bd = _head_block_matrix(1.0 / head_dim, head_dim)
    ones_bd = _head_block_matrix(1.0, head_dim)
    width = y_ref.shape[1]
    zwin = jnp.concatenate([za_ref[...], zb_ref[...]], axis=1)
    for p in range(width // _LANES):
        cols = slice(p * _LANES, (p + 1) * _LANES)
        y = y_ref[:, cols]
        d = y - _mm_exact_rhs(y, avg_bd)
        var = _mm_exact_rhs(d * d, avg_bd)
        yn = d * lax.rsqrt(var + _GN_EPS) * lw_ref[:, cols] + lb_ref[:, cols]
        bonus = _mm_exact_rhs(r_ref[:, cols] * k_ref[:, cols] * rk_ref[:, cols], ones_bd) * v_ref[:, cols]
        zg = zwin[:, delta + p * _LANES:delta + (p + 1) * _LANES]
        o_ref[:, cols] = ((yn + bonus) * _silu(zg)).astype(o_ref.dtype)


def _rwkv_post(y, r, k2, v, z, consts, *, row0, rows, width, tb, head_dim, col0):
    off = row0 // tb
    tail = 2 * _LANES
    cg, delta = col0 // width, col0 % width
    assert delta <= tail and width % tail == 0
    blk = pl.BlockSpec((tb, width), lambda i: (i, 0))
    vec = pl.BlockSpec((1, width), lambda i: (0, 0))
    kern = functools.partial(_rwkv_post_kernel, head_dim=head_dim, delta=delta)
    return pl.pallas_call(
        kern,
        grid=(rows // tb,),
        in_specs=[blk, blk, blk, blk,
                  pl.BlockSpec((tb, width), lambda i: (off + i, cg)),
                  pl.BlockSpec((tb, tail), lambda i: (off + i, (cg + 1) * (width // tail))),
                  vec, vec, vec],
        out_specs=blk,
        out_shape=jax.ShapeDtypeStruct((rows, width), _BF16),
        compiler_params=_params(("parallel",)),
        name="rwkv_post",
    )(y, r, k2, v, z, z, *consts)


def _mem_attn_prompt_kernel(za_ref, zb_ref, zc_ref, k_ref, v_ref, o_ref, *, heads, head_dim, delta):
    scale = head_dim ** -0.5
    mw = heads * head_dim
    zwin = jnp.concatenate([za_ref[...], zb_ref[...], zc_ref[...]], axis=1)
    for h in range(heads):
        cols = slice(h * head_dim, (h + 1) * head_dim)
        q = zwin[:, delta + cols.start:delta + cols.stop]
        zg = zwin[:, delta + mw + cols.start:delta + mw + cols.stop]
        s = _dot_nt(q.astype(_BF16), k_ref[:, cols].astype(_BF16)) * scale
        p = jnp.exp(s - jnp.max(s, axis=-1, keepdims=True))
        o = _dot(p.astype(_BF16), v_ref[:, cols].astype(_BF16)) / jnp.sum(p, axis=-1, keepdims=True)
        o_ref[:, cols] = (o * _silu(zg)).astype(o_ref.dtype)


def _mem_attn_prompt(z, mkv, *, batch, seq, n_mem, heads, head_dim, tq, col0):
    n_t = seq // tq
    mw = heads * head_dim
    tail = 2 * _LANES
    q_col, delta = col0 // mw, col0 % mw
    assert delta <= tail and mw % tail == 0
    kern = functools.partial(_mem_attn_prompt_kernel, heads=heads, head_dim=head_dim, delta=delta)
    return pl.pallas_call(
        kern,
        grid=(batch, n_t),
        in_specs=[
            pl.BlockSpec((tq, mw), lambda b, t: (b * n_t + t, q_col)),
            pl.BlockSpec((tq, mw), lambda b, t: (b * n_t + t, q_col + 1)),
            pl.BlockSpec((tq, tail), lambda b, t: (b * n_t + t, (q_col + 2) * (mw // tail))),
            pl.BlockSpec((n_mem, mw), lambda b, t: (b, 0)),
            pl.BlockSpec((n_mem, mw), lambda b, t: (b, 1)),
        ],
        out_specs=pl.BlockSpec((tq, mw), lambda b, t: (b * n_t + t, 0)),
        out_shape=jax.ShapeDtypeStruct((batch * seq, mw), _BF16),
        compiler_params=_params(("parallel", "parallel")),
        name="mem_attn_prompt",
    )(z, z, z, mkv, mkv)


def _mem_attn_sample_kernel(q_ref, zg_ref, k_ref, v_ref, o_ref, *, bb, head_dim):
    scale = head_dim ** -0.5
    for b in range(bb):
        s = jnp.sum(k_ref[b] * q_ref[b][None], axis=-1, keepdims=True) * scale
        p = jnp.exp(s - jnp.max(s, axis=0, keepdims=True))
        o = jnp.sum(p * v_ref[b], axis=0) / jnp.sum(p, axis=0)
        o_ref[b] = o * _silu(zg_ref[b])


def _mem_attn_sample(q, zg, mem_k, mem_v, *, bb):
    n, n_mem, heads, head_dim = mem_k.shape
    vec = pl.BlockSpec((bb, heads, head_dim), lambda i: (i, 0, 0))
    mat = pl.BlockSpec((bb, n_mem, heads, head_dim), lambda i: (i, 0, 0, 0))
    kern = functools.partial(_mem_attn_sample_kernel, bb=bb, head_dim=head_dim)
    return pl.pallas_call(
        kern,
        grid=(n // bb,),
        in_specs=[vec, vec, mat, mat],
        out_specs=vec,
        out_shape=jax.ShapeDtypeStruct((n, heads, head_dim), _F32),
        compiler_params=_params(("parallel",)),
        name="mem_attn_sample",
    )(q, zg, mem_k, mem_v)


def _branch_kernel(po_ref, ro_ref, mo_ref, pos_ref, ros_ref, mos_ref, wp_ref, wr_ref, wm_ref, gp_ref, gr_ref, gm_ref,
                   tp_ref, tr_ref, tm_ref, bp_ref, br_ref, bm_ref, o_ref, os_ref, wpb_ref, wrb_ref, wmb_ref,
                   *, delta, row_tiles):
    i = pl.program_id(1)

    @pl.when(i == 0)
    def _():
        wpb_ref[...] = wp_ref[...].astype(_BF16)
        wrb_ref[...] = wr_ref[...].astype(_BF16)
        wmb_ref[...] = wm_ref[...].astype(_BF16)

    tn = o_ref.shape[1]

    def gated_sum(acts, rows):
        h = None
        for a_ref, wb_ref, g_ref, t_ref, b_ref in zip(acts, (wpb_ref, wrb_ref, wmb_ref), (gp_ref, gr_ref, gm_ref),
                                                      (tp_ref, tr_ref, tm_ref), (bp_ref, br_ref, bm_ref)):
            win = jnp.concatenate([g_ref[0:rows, :], t_ref[0:rows, :]], axis=1)
            term = jax.nn.sigmoid(win[:, delta:delta + tn] + b_ref[...]) * _dot(a_ref[...], wb_ref[...])
            h = term if h is None else h + term
        return h

    @pl.when(i < row_tiles)
    def _():
        o_ref[...] = gated_sum((po_ref, ro_ref, mo_ref), o_ref.shape[0]).astype(o_ref.dtype)

    @pl.when(i == row_tiles)
    def _():
        os_ref[...] = gated_sum((pos_ref, ros_ref, mos_ref), os_ref.shape[0]).astype(os_ref.dtype)


def _branch(acts_p, acts_s, wp, wr, wm, z, b_gate, *, tm, tn, gate_col0):
    rows, ns = acts_p[0].shape[0], acts_s[0].shape[0]
    d = wp.shape[1]
    tail = 2 * _LANES
    delta = gate_col0 % tn
    g0 = gate_col0 // tn
    nb = d // tn
    nt = rows // tm
    assert delta <= tail and tn % tail == 0 and nt * tm == rows and ns <= tm and z.shape[0] >= rows + ns
    act = lambda a: pl.BlockSpec((tm, a.shape[1]), lambda j, i: (jnp.minimum(i, nt - 1), 0))
    act_s = lambda a: pl.BlockSpec((ns, a.shape[1]), lambda j, i: (0, 0))
    wgt = lambda w: pl.BlockSpec((w.shape[0], tn), lambda j, i: (0, j), pipeline_mode=pl.Buffered(1))
    gat = lambda k: pl.BlockSpec((tm, tn), lambda j, i, k=k: (i, g0 + k * nb + j))
    tai = lambda k: pl.BlockSpec((tm, tail), lambda j, i, k=k: (i, (g0 + k * nb + j + 1) * (tn // tail)))
    bia = lambda k: pl.BlockSpec((1, tn), lambda j, i, k=k: (0, k * nb + j))
    return pl.pallas_call(
        functools.partial(_branch_kernel, delta=delta, row_tiles=nt),
        grid=(nb, nt + 1),
        in_specs=[act(acts_p[0]), act(acts_p[1]), act(acts_p[2]), act_s(acts_s[0]), act_s(acts_s[1]), act_s(acts_s[2]),
                  wgt(wp), wgt(wr), wgt(wm), gat(0), gat(1), gat(2), tai(0), tai(1), tai(2), bia(0), bia(1), bia(2)],
        out_specs=[pl.BlockSpec((tm, tn), lambda j, i: (jnp.minimum(i, nt - 1), j)),
                   pl.BlockSpec((ns, tn), lambda j, i: (0, j))],
        out_shape=[jax.ShapeDtypeStruct((rows, d), _BF16), jax.ShapeDtypeStruct((ns, d), _BF16)],
        scratch_shapes=[pltpu.VMEM((w.shape[0], tn), _BF16) for w in (wp, wr, wm)],
        compiler_params=_params(("parallel", "arbitrary")),
        name="branch_proj",
    )(*acts_p, *acts_s, wp, wr, wm, z, z, z, z, z, z, b_gate, b_gate, b_gate)


def _out_kernel(h_ref, w_ref, x_ref, g_ref, b_ref, o_ref, *, alpha):
    xf = alpha * x_ref[...] + _dot(h_ref[...], w_ref[...])
    mu = jnp.mean(xf, axis=-1, keepdims=True)
    d = xf - mu
    var = jnp.mean(d * d, axis=-1, keepdims=True)
    o_ref[...] = d * lax.rsqrt(var + _LN_EPS) * g_ref[...] + b_ref[...]


def _out_proj(h, w_out, x, ln_g, ln_b, *, tm, alpha):
    rows, d = x.shape
    kern = functools.partial(_out_kernel, alpha=alpha)
    return pl.pallas_call(
        kern,
        grid=(rows // tm,),
        in_specs=[
            pl.BlockSpec((tm, d), lambda i: (i, 0)),
            pl.BlockSpec((d, d), lambda i: (0, 0), pipeline_mode=pl.Buffered(1)),
            pl.BlockSpec((tm, d), lambda i: (i, 0)),
            pl.BlockSpec((1, d), lambda i: (0, 0)),
            pl.BlockSpec((1, d), lambda i: (0, 0)),
        ],
        out_specs=pl.BlockSpec((tm, d), lambda i: (i, 0)),
        out_shape=jax.ShapeDtypeStruct((rows, d), _F32),
        compiler_params=_params(("parallel",), vmem_limit=_VMEM_LIMIT_RESIDENT_WEIGHT),
        name="out_proj_ln",
    )(h, w_out, x, ln_g, ln_b)


def _tiles(batch, seq, n_sample):
    n_prompt = batch * seq
    m_all = n_prompt + n_sample
    return {
        "in_tm": next(t for t in (1040, 832, 640, 512, 256, 128, 64, 32, 16) if m_all % t == 0), "in_tn": 768,
        "row_tb": min(512, seq),
        "scan_tb": min(256, seq), "scan_pp": 16,
        "attn_tq": min(1024, seq),
        "proj_tm": min(256, n_prompt), "proj_tn": 1024, "out_tm": min(256, n_prompt),
        "pool_bb": min(32, n_sample), "step_hb": 4, "attn_bb": 4,
    }


def _layer(xp, xs, mem_p, cache_k, cache_v, st_pool, st_shift, st_rwkv, lp):
    batch, seq, d = xp.shape
    ns = xs.shape[0]
    n_p = batch * seq
    m_all = n_p + ns
    pw = lp["pool_scale"].shape[-1]
    rw = lp["rwkv_w0"].shape[-1]
    heads, hd = lp["rwkv_r_k"].shape
    lora = lp["rwkv_w2"].shape[0]
    sw = lp["rwkv_mu"].shape[-1]
    n_mem, mh, md = cache_k.shape[1:]
    mw = mh * md
    tl = _tiles(batch, seq, ns)

    lo0, lo1 = 2 * pw + 3 * rw, 2 * pw + sw
    lw = 2 * _LANES
    assert 2 * lora <= lw
    q_col0 = lo1 + rw
    gate_col0 = q_col0 + 2 * mw

    if ns < tl["in_tm"] <= n_p:
        x_all = _cast_rows(xp.reshape(n_p, d), xs.reshape(ns, d), tm=tl["in_tm"])
    else:
        x_all = jnp.concatenate([xp.reshape(n_p, d), xs.reshape(ns, d)], axis=0).astype(_BF16)
    z = _in_proj(x_all, jnp.swapaxes(lp["w_in"], 0, 1), tm=tl["in_tm"], tn=tl["in_tn"])
    z_s = z[n_p:]

    pool_w = lp["pool_w"].astype(_BF16)
    pool_scale = lp["pool_scale"].reshape(1, pw)
    po_p = _pool_prompt(z, pool_w, pool_scale, batch=batch, seq=seq, width=pw, tb=tl["row_tb"])
    po_s = _pool_sample(z_s, jnp.swapaxes(st_pool, 0, 1), pool_w, pool_scale, row0=0, bb=tl["pool_bb"])
    nbuf = st_pool.shape[1]
    new_pool_p = jnp.stack([z[(b + 1) * seq - nbuf:(b + 1) * seq, :pw] for b in range(batch)])
    new_pool_s = jnp.concatenate([st_pool[:, 1:, :], z_s[:, :pw].reshape(ns, 1, pw)], axis=1)

    mu = lp["rwkv_mu"]
    pad_l = lambda a: jnp.pad(a, ((0, 0), (0, lw - 2 * lora)))
    w2p = jnp.zeros((lw, rw), _F32).at[:lora].set(lp["rwkv_w2"]).astype(_BF16)
    a2p = jnp.zeros((lw, rw), _F32).at[lora:2 * lora].set(lp["rwkv_a2"]).astype(_BF16)
    row = lambda a: a.reshape(1, -1)
    mix_consts = (row(mu[:rw]), row(mu[rw:2 * rw]), row(mu[2 * rw:3 * rw]), pad_l(row(mu[3 * rw:])),
                  row(lp["rwkv_w0"]), row(lp["rwkv_a0"]), row(lp["rwkv_k_k"]), row(lp["rwkv_k_a"]), w2p, a2p)
    post_consts = (row(lp["rwkv_r_k"]), row(lp["rwkv_ln_w"]), row(lp["rwkv_ln_b"]))

    ro_p, st_pairs = _rwkv_prompt(z, mix_consts, post_consts, batch=batch, seq=seq, width=rw, tb=tl["scan_tb"],
                                  pp=tl["scan_pp"], head_dim=hd, col0=2 * pw, gate_col0=lo1, lw=lw)
    st5 = st_pairs.reshape(batch, heads // 2, 2, hd, 2, hd)
    new_state_p = jnp.stack([st5[:, :, 0, :, 0, :], st5[:, :, 1, :, 1, :]], axis=2).reshape(batch, heads, hd, hd)

    shift_s = st_shift.reshape(ns, sw)
    prep_s = _rwkv_prep_sample(z_s, shift_s[:, :3 * rw], pad_l(shift_s[:, 3 * rw:]), mix_consts, row0=0, rows=ns,
                               width=rw, head_dim=hd, col0=2 * pw, lw=lw)
    r_s, _, k_s, v_s, _, _ = prep_s[:6]
    y_t, state_t = _rwkv_step(prep_s[6:], jnp.transpose(st_rwkv, (1, 2, 3, 0)), hb=tl["step_hb"])
    new_state_s = jnp.transpose(state_t, (3, 0, 1, 2))
    ro_s = _rwkv_post(y_t.reshape(rw, ns).T, r_s, k_s, v_s, z_s, post_consts, row0=0, rows=ns, width=rw, tb=ns,
                      head_dim=hd, col0=lo1)
    new_shift_p = jnp.stack([z[(b + 1) * seq - 1:(b + 1) * seq, 2 * pw:lo1] for b in range(batch)])
    new_shift_s = z_s[:, 2 * pw:lo1].reshape(ns, 1, sw)

    mkv = _matmul_f32w(mem_p.reshape(batch * n_mem, d).astype(_BF16), lp["w_mem_kv"],
                       tm=next(t for t in (512, 256, 128, 64, 32, 16, 8) if (batch * n_mem) % t == 0),
                       tn=min(768, 2 * mw), name="mem_kv_proj")
    mo_p = _mem_attn_prompt(z, mkv, batch=batch, seq=seq, n_mem=n_mem, heads=mh, head_dim=md, tq=tl["attn_tq"],
                            col0=q_col0)
    mo_s = _mem_attn_sample(z_s[:, q_col0:q_col0 + mw].reshape(ns, mh, md),
                            z_s[:, q_col0 + mw:q_col0 + 2 * mw].reshape(ns, mh, md),
                            cache_k, cache_v, bb=tl["attn_bb"]).reshape(ns, mw).astype(_BF16)
    mk_p = mkv[:, :mw].reshape(batch, n_mem, mh, md)
    mv_p = mkv[:, mw:].reshape(batch, n_mem, mh, md)

    wp, wr, wm = lp["w_branch_pool"], lp["w_branch_rwkv"], lp["w_branch_mem"]
    w_out = lp["w_out"].astype(_BF16)
    b_gate = row(lp["b_gate"])
    ln_g, ln_b = row(lp["ln_g"]), row(lp["ln_b"])
    alpha = lp["alpha"]
    h_p, h_s = _branch((po_p, ro_p, mo_p), (po_s, ro_s, mo_s), wp, wr, wm, z, b_gate, tm=tl["proj_tm"],
                       tn=tl["proj_tn"], gate_col0=gate_col0)
    y_prompt = _out_proj(h_p, w_out, xp.reshape(n_p, d), ln_g, ln_b, tm=tl["out_tm"], alpha=alpha)
    y_sample = _out_proj(h_s, w_out, xs.reshape(ns, d), ln_g, ln_b, tm=min(tl["out_tm"], ns), alpha=alpha)
    y_prompt = y_prompt.reshape(batch, seq, d)
    y_sample = y_sample.reshape(ns, 1, d)
    return (y_prompt, y_sample, mk_p, mv_p, new_pool_p, new_shift_p, new_state_p, new_pool_s, new_shift_s,
            new_state_s)


def kernel(x_prompt, x_sample, cache_mem_k, cache_mem_v, state_pool, state_shift, state_rwkv, mem_prompt, w_in,
           b_gate, pool_w, pool_scale, rwkv_mu, rwkv_w0, rwkv_w2, rwkv_a0, rwkv_a2, rwkv_k_k, rwkv_k_a, rwkv_r_k,
           rwkv_ln_w, rwkv_ln_b, w_mem_kv, w_branch_pool, w_branch_rwkv, w_branch_mem, w_out, ln_g, ln_b):
    depth = w_in.shape[0]
    alpha = (2.0 * depth) ** 0.25
    weights = dict(w_in=w_in, b_gate=b_gate, pool_w=pool_w, pool_scale=pool_scale, rwkv_mu=rwkv_mu, rwkv_w0=rwkv_w0,
                   rwkv_w2=rwkv_w2, rwkv_a0=rwkv_a0, rwkv_a2=rwkv_a2, rwkv_k_k=rwkv_k_k, rwkv_k_a=rwkv_k_a,
                   rwkv_r_k=rwkv_r_k, rwkv_ln_w=rwkv_ln_w, rwkv_ln_b=rwkv_ln_b, w_mem_kv=w_mem_kv,
                   w_branch_pool=w_branch_pool, w_branch_rwkv=w_branch_rwkv, w_branch_mem=w_branch_mem,
                   w_out=w_out, ln_g=ln_g, ln_b=ln_b)
    yp, ys = x_prompt, x_sample
    per_layer = []
    for l in range(depth):
        lp = {k: v[l] for k, v in weights.items()}
        lp["alpha"] = alpha
        res = _layer(yp, ys, mem_prompt, cache_mem_k[l], cache_mem_v[l], state_pool[l], state_shift[l],
                     state_rwkv[l], lp)
        yp, ys = res[0], res[1]
        per_layer.append(res[2:])
    if depth == 1:
        stacked = tuple(a[None] for a in per_layer[0])
    else:
        stacked = tuple(jnp.stack([res[i] for res in per_layer]) for i in range(len(per_layer[0])))
    return (yp, ys) + stacked
```

```python
import functools

import jax
import jax.numpy as jnp
from jax import lax
from jax.experimental import pallas as pl
from jax.experimental.pallas import tpu as pltpu

_F32 = jnp.float32
_BF16 = jnp.bfloat16

_POOL_WINDOWS = (2, 4, 8, 16)
_PAST_LEN = 16384
_GN_EPS = 64e-5
_LN_EPS = 1e-5
_L2_EPS = 1e-12
_EXP_MINUS_HALF = 0.6065306597126334

_LANES = 128
_MIB = 1024 * 1024
_VMEM_LIMIT = 56 * _MIB
_VMEM_LIMIT_RESIDENT_WEIGHT = 62 * _MIB

_HEAD_PAIR = _LANES
_CHUNK = 64


def _params(sem, vmem_limit=_VMEM_LIMIT):
    return pltpu.CompilerParams(dimension_semantics=sem, vmem_limit_bytes=vmem_limit)


def _dot(a, b):
    return jnp.dot(a, b, preferred_element_type=_F32)


def _dot_nt(a, b):
    return lax.dot_general(a, b, (((1,), (1,)), ((), ())), preferred_element_type=_F32)


def _dot_tn(a, b):
    return lax.dot_general(a, b, (((0,), (0,)), ((), ())), preferred_element_type=_F32)


def _split(x):
    hi = x.astype(_BF16)
    lo = (x - hi.astype(_F32)).astype(_BF16)
    return hi, lo


def _mm1(dot, a, b):
    return dot(a.astype(_BF16), b.astype(_BF16))


def _mm_exact_rhs(a, b_bf16):
    a_hi, a_lo = _split(a)
    return _dot(a_hi, b_bf16) + _dot(a_lo, b_bf16)


def _silu(x):
    return x * jax.nn.sigmoid(x)


def _head_block_matrix(value, head_dim):
    r = lax.broadcasted_iota(jnp.int32, (_LANES, _LANES), 0) // head_dim
    c = lax.broadcasted_iota(jnp.int32, (_LANES, _LANES), 1) // head_dim
    return jnp.where(r == c, value, 0.0).astype(_BF16)


def _matmul_f32w_kernel(x_ref, w_ref, o_ref, wb_ref):
    @pl.when(pl.program_id(1) == 0)
    def _():
        wb_ref[...] = w_ref[...].astype(_BF16)

    o_ref[...] = _dot(x_ref[...], wb_ref[...])


def _matmul_f32w(x, w, *, tm, tn, name):
    m, k = x.shape
    n = w.shape[1]
    return pl.pallas_call(
        _matmul_f32w_kernel,
        grid=(n // tn, m // tm),
        in_specs=[pl.BlockSpec((tm, k), lambda j, i: (i, 0)), pl.BlockSpec((k, tn), lambda j, i: (0, j))],
        out_specs=pl.BlockSpec((tm, tn), lambda j, i: (i, j)),
        out_shape=jax.ShapeDtypeStruct((m, n), _F32),
        scratch_shapes=[pltpu.VMEM((k, tn), _BF16)],
        compiler_params=_params(("parallel", "arbitrary")),
        name=name,
    )(x, w)


def _cast_rows_kernel(xp_ref, xs_ref, o_ref, *, tail):
    i = pl.program_id(0)
    last = pl.num_programs(0) - 1
    tm = o_ref.shape[0]

    @pl.when(i < last)
    def _():
        o_ref[...] = xp_ref[...].astype(_BF16)

    @pl.when(i == last)
    def _():
        o_ref[0:tm - tail, :] = xp_ref[tail:tm, :].astype(_BF16)
        o_ref[tm - tail:tm, :] = xs_ref[...].astype(_BF16)


def _cast_rows(xp, xs, *, tm):
    n_p, d = xp.shape
    tail = xs.shape[0]
    n_tiles = (n_p + tail) // tm
    assert n_tiles * tm == n_p + tail and 0 < tail < tm <= n_p and (n_p - tm) % 8 == 0
    return pl.pallas_call(
        functools.partial(_cast_rows_kernel, tail=tail),
        grid=(n_tiles,),
        in_specs=[pl.BlockSpec((pl.Element(tm), pl.Element(d)),
                               lambda i: (pl.multiple_of(jnp.minimum(i * tm, n_p - tm), 8), 0)),
                  pl.BlockSpec((tail, d), lambda i: (0, 0))],
        out_specs=pl.BlockSpec((tm, d), lambda i: (i, 0)),
        out_shape=jax.ShapeDtypeStruct((n_p + tail, d), _BF16),
        compiler_params=_params(("parallel",), vmem_limit=_VMEM_LIMIT_RESIDENT_WEIGHT),
        name="cast_rows",
    )(xp, xs)


def _in_proj_kernel(x_ref, wt_ref, o_ref, wb_ref):
    @pl.when(pl.program_id(1) == 0)
    def _():
        wb_ref[...] = wt_ref[...].astype(_BF16)

    o_ref[...] = _dot_nt(x_ref[...], wb_ref[...])


def _in_proj(x, wt, *, tm, tn):
    m, k = x.shape
    n = wt.shape[0]
    return pl.pallas_call(
        _in_proj_kernel,
        grid=(pl.cdiv(n, tn), m // tm),
        in_specs=[pl.BlockSpec((tm, k), lambda j, i: (i, 0)), pl.BlockSpec((tn, k), lambda j, i: (j, 0))],
        out_specs=pl.BlockSpec((tm, tn), lambda j, i: (i, j)),
        out_shape=jax.ShapeDtypeStruct((m, n), _F32),
        scratch_shapes=[pltpu.VMEM((tn, k), _BF16)],
        compiler_params=_params(("parallel", "arbitrary"), vmem_limit=_VMEM_LIMIT_RESIDENT_WEIGHT),
        name="in_proj",
    )(x, wt)


def _pool_mix(pooled_fn, zp_ref, pw_ref, ps_ref, o_ref, group):
    for g in range(len(_POOL_WINDOWS)):
        cols = slice(g * group, (g + 1) * group)
        mixed = _dot(pooled_fn(g, cols).astype(_BF16), pw_ref[g])
        o_ref[:, cols] = (mixed * ps_ref[:, cols] * _silu(zp_ref[:, cols])).astype(o_ref.dtype)


def _pool_prompt_kernel(u_ref, halo_ref, zp_ref, pw_ref, ps_ref, o_ref, e_ref, *, tb, hist, group):
    t = pl.program_id(1)
    e_ref[0:hist, :] = jnp.where(t == 0, 0.0, halo_ref[...])
    e_ref[hist:hist + tb, :] = u_ref[...]
    pos = t * tb + lax.broadcasted_iota(jnp.int32, (tb, group), 0)

    def window_sum(cols, win):
        if win < 8:
            acc = e_ref[hist:hist + tb, cols]
            for d in range(1, win):
                acc = acc + e_ref[hist - d:hist - d + tb, cols]
            return acc
        assert win & (win - 1) == 0
        s, span = e_ref[:, cols], 1
        while span < win:
            s = s[span:] + s[:-span]
            span *= 2
        first = hist - (win - 1)
        return s[first:first + tb]

    def pooled(g, cols):
        win = _POOL_WINDOWS[g]
        x = e_ref[hist:hist + tb, cols]
        cnt = jnp.minimum(pos + 1, win).astype(_F32)
        return window_sum(cols, win) / cnt - x

    _pool_mix(pooled, zp_ref, pw_ref, ps_ref, o_ref, group)


def _pool_prompt(z, pool_w, pool_scale, *, batch, seq, width, tb):
    hist = 16
    n_t = seq // tb
    group = width // len(_POOL_WINDOWS)
    kern = functools.partial(_pool_prompt_kernel, tb=tb, hist=hist, group=group)
    return pl.pallas_call(
        kern,
        grid=(batch, n_t),
        in_specs=[
            pl.BlockSpec((tb, width), lambda b, t: (b * n_t + t, 0)),
            pl.BlockSpec((hist, width), lambda b, t: (jnp.maximum((b * seq + t * tb) // hist - 1, 0), 0)),
            pl.BlockSpec((tb, width), lambda b, t: (b * n_t + t, 1)),
            pl.BlockSpec(pool_w.shape, lambda b, t: (0, 0, 0)),
            pl.BlockSpec((1, width), lambda b, t: (0, 0)),
        ],
        out_specs=pl.BlockSpec((tb, width), lambda b, t: (b * n_t + t, 0)),
        out_shape=jax.ShapeDtypeStruct((batch * seq, width), _BF16),
        scratch_shapes=[pltpu.VMEM((hist + tb, width), _F32)],
        compiler_params=_params(("parallel", "parallel")),
        name="pool_prompt",
    )(z, z, z, pool_w, pool_scale)


def _pool_sample_kernel(u_ref, buf_ref, zp_ref, pw_ref, ps_ref, o_ref, *, nbuf, group):
    def pooled(g, cols):
        win = _POOL_WINDOWS[g]
        x = u_ref[:, cols]
        acc = x
        for d in range(1, win):
            acc = acc + buf_ref[nbuf - d, :, cols]
        cnt = float(min(_PAST_LEN + 1, win))
        return acc / cnt - x

    _pool_mix(pooled, zp_ref, pw_ref, ps_ref, o_ref, group)


def _pool_sample(z, buf, pool_w, pool_scale, *, row0, bb):
    nbuf, rows, width = buf.shape
    group = width // len(_POOL_WINDOWS)
    off = row0 // bb
    kern = functools.partial(_pool_sample_kernel, nbuf=nbuf, group=group)
    return pl.pallas_call(
        kern,
        grid=(rows // bb,),
        in_specs=[
            pl.BlockSpec((bb, width), lambda i: (off + i, 0)),
            pl.BlockSpec((nbuf, bb, width), lambda i: (0, i, 0)),
            pl.BlockSpec((bb, width), lambda i: (off + i, 1)),
            pl.BlockSpec(pool_w.shape, lambda i: (0, 0, 0)),
            pl.BlockSpec((1, width), lambda i: (0, 0)),
        ],
        out_specs=pl.BlockSpec((bb, width), lambda i: (i, 0)),
        out_shape=jax.ShapeDtypeStruct((rows, width), _BF16),
        compiler_params=_params(("parallel",)),
        name="pool_sample",
    )(z, buf, z, pool_w, pool_scale)


def _rwkv_mix(cur, prev, mu, w0, a0, kk_scale, ka, w2p, a2p, head_dim, outs):
    o_r, o_ld, o_k, o_v, o_kk, o_be = outs
    mix = lambda n: cur[n] + (prev[n] - cur[n]) * mu[n]
    xl = mix("l")
    wl = w0 + _mm1(_dot, jnp.tanh(xl), w2p)
    a = jax.nn.sigmoid(a0 + _mm1(_dot, xl, a2p))
    xk = mix("k")
    kkp = xk * kk_scale
    ones_bd = _head_block_matrix(1.0, head_dim)
    width = kkp.shape[1]
    o_r[...] = mix("r")
    o_v[...] = mix("v")
    o_ld[...] = -_EXP_MINUS_HALF * jax.nn.sigmoid(wl)
    o_k[...] = xk * (1.0 + (a - 1.0) * ka)
    for p in range(width // _LANES):
        cols = slice(p * _LANES, (p + 1) * _LANES)
        kp = kkp[:, cols]
        ss = _mm_exact_rhs(kp * kp, ones_bd)
        kk = kp * lax.rsqrt(jnp.maximum(ss, _L2_EPS * _L2_EPS))
        o_kk[:, cols] = kk
        o_be[:, cols] = kk * a[:, cols]


def _rwkv_prep_sample_kernel(zr, zk, zv, zl, pr, pk, pv, plr, mur, muk, muv, mul, w0, a0, kks, ka, w2p, a2p,
                             o_r, o_ld, o_k, o_v, o_kk, o_be, t_r, t_ld, t_k, t_v, t_kk, t_be, *, head_dim):
    cur = {"r": zr[...], "k": zk[...], "v": zv[...], "l": zl[...]}
    prev = {"r": pr[...], "k": pk[...], "v": pv[...], "l": plr[...]}
    mu = {"r": mur[...], "k": muk[...], "v": muv[...], "l": mul[...]}
    _rwkv_mix(cur, prev, mu, w0[...], a0[...], kks[...], ka[...], w2p[...], a2p[...], head_dim,
              (o_r, o_ld, o_k, o_v, o_kk, o_be))
    for o_ref, t_ref in ((o_r, t_r), (o_ld, t_ld), (o_k, t_k), (o_v, t_v), (o_kk, t_kk), (o_be, t_be)):
        t_ref[...] = o_ref[...].T


def _rwkv_prep_sample(z, prev_rkv, prev_l, consts, *, row0, rows, width, head_dim, col0, lw):
    c0 = col0 // width
    lcol = (col0 + 3 * width) // lw
    assert c0 * width == col0 and lcol * lw == col0 + 3 * width
    off = row0 // rows
    cblk = lambda c: pl.BlockSpec((rows, width), lambda i, c=c: (off, c))
    pblk = lambda c: pl.BlockSpec((rows, width), lambda i, c=c: (0, c))
    vec = lambda n: pl.BlockSpec((1, n), lambda i: (0, 0))
    mat = pl.BlockSpec((lw, width), lambda i: (0, 0))
    out_spec = pl.BlockSpec((rows, width), lambda i: (0, 0))
    out_shape = jax.ShapeDtypeStruct((rows, width), _F32)
    kern = functools.partial(_rwkv_prep_sample_kernel, head_dim=head_dim)
    return pl.pallas_call(
        kern,
        grid=(1,),
        in_specs=[cblk(c0), cblk(c0 + 1), cblk(c0 + 2), pl.BlockSpec((rows, lw), lambda i: (off, lcol)),
                  pblk(0), pblk(1), pblk(2), pl.BlockSpec((rows, lw), lambda i: (0, 0)),
                  vec(width), vec(width), vec(width), vec(lw),
                  vec(width), vec(width), vec(width), vec(width), mat, mat],
        out_specs=[out_spec] * 6 + [pl.BlockSpec((width, rows), lambda i: (0, 0))] * 6,
        out_shape=[out_shape] * 6 + [jax.ShapeDtypeStruct((width, rows), _F32)] * 6,
        compiler_params=_params(("arbitrary",)),
        name="rwkv_prep_sample",
    )(z, z, z, z, prev_rkv, prev_rkv, prev_rkv, prev_l, *consts)


def _scan_chunks(inputs, states, c):
    pairs = range(len(inputs))
    n2 = _LANES
    bf = lambda x: x.astype(_BF16)
    m0, m1 = c["m0"], c["m1"]
    expand = lambda x: jnp.concatenate([x * m0, x * m1], axis=0)
    stack = lambda a, b: jnp.concatenate([expand(a), expand(b)], axis=0)

    cum = [_mm_exact_rhs_t(c["ltri"], inputs[q][1]) for q in pairs]
    tot = [cum[q][_CHUNK - 1:_CHUNK, :] for q in pairs]
    xa, xb, bk, ve = [], [], [], []
    for q in pairs:
        r, ld, k2, v, kk, be = inputs[q]
        e_neg = jnp.exp(-cum[q])
        e_rem = jnp.exp(tot[q] - cum[q])
        xa.append(bf(stack(-kk * jnp.exp(cum[q] - ld), r * jnp.exp(cum[q]))))
        xb.append(bf(stack(be * e_neg, k2 * e_neg)))
        bk.append(bf(stack(be * e_rem, k2 * e_rem)))
        ve.append(bf(expand(v)))
    g = [_dot_nt(xa[q], xb[q]) for q in pairs]
    ps = [_dot_nt(xa[q], bf(states[q])) for q in pairs]
    a_kk = [bf(jnp.concatenate([jnp.where(c["strict"], g[q][:n2, n2:], 0.0),
                                jnp.where(c["incl"], g[q][n2:, n2:], 0.0)], axis=0)) for q in pairs]
    av = [_dot(a_kk[q], ve[q]) for q in pairs]
    npow = [jnp.where(c["strict"], g[q][:n2, :n2], 0.0) for q in pairs]
    u = [ps[q][:n2] + av[q][:n2] for q in pairs]
    for k in range(c["squarings"]):
        out = [_dot(bf(npow[q]), bf(jnp.concatenate([npow[q], u[q]], axis=1))) for q in pairs]
        npow = [out[q][:, :n2] for q in pairs]
        u = [u[q] + out[q][:, n2:] for q in pairs]
    ub = [bf(u[q]) for q in pairs]
    u = [u[q] + _dot(bf(npow[q]), ub[q]) for q in pairs]
    ub = [bf(u[q]) for q in pairs]
    a_rb = [bf(jnp.where(c["incl"], g[q][n2:, :n2], 0.0)) for q in pairs]
    ye = [ps[q][n2:] + av[q][n2:] + _dot(a_rb[q], ub[q]) for q in pairs]
    ys = [ye[q][:_CHUNK] + ye[q][_CHUNK:] for q in pairs]
    new_states = [states[q] * jnp.exp(tot[q]) + _dot_tn(jnp.concatenate([ub[q], ve[q]], axis=0), bk[q])
                  for q in pairs]
    return ys, new_states


def _mm_exact_rhs_t(l_bf16, x):
    x_hi, x_lo = _split(x)
    return _dot(l_bf16, x_hi) + _dot(l_bf16, x_lo)


def _scan_consts():
    n = _LANES
    half = _CHUNK
    ri = lax.broadcasted_iota(jnp.int32, (n, n), 0)
    ci = lax.broadcasted_iota(jnp.int32, (n, n), 1)
    same = (ri // half) == (ci // half)
    lane = lax.broadcasted_iota(jnp.int32, (half, n), 1)
    tr = lax.broadcasted_iota(jnp.int32, (half, half), 0)
    tc = lax.broadcasted_iota(jnp.int32, (half, half), 1)
    squarings = 0
    while (2 << squarings) < half:
        squarings += 1
    return {
        "strict": same & ((ri % half) > (ci % half)),
        "incl": same & ((ri % half) >= (ci % half)),
        "m0": jnp.where(lane < half, 1.0, 0.0).astype(_F32),
        "m1": jnp.where(lane >= half, 1.0, 0.0).astype(_F32),
        "ltri": jnp.where(tr >= tc, 1.0, 0.0).astype(_BF16),
        "squarings": squarings,
    }


def _rwkv_prompt_kernel(zr, zk, zv, zl, hr, hk, hv, hl, za, zb, mur, muk, muv, mul, w0, a0, kks, ka, w2p, a2p,
                        rk, lnw, lnb, o_ref, s_ref, er, ek, ev, el, r_s, ld_s, k_s, v_s, kk_s, be_s, y_s, xl_s, st_ref,
                        *, tb, pp, head_dim, delta):
    t = pl.program_id(2)
    halo = 8
    n_chunks = tb // _CHUNK
    pairs = range(pp)
    cols = [slice(q * _LANES, (q + 1) * _LANES) for q in pairs]

    @pl.when(t == 0)
    def _():
        st_ref[...] = jnp.zeros_like(st_ref)

    for z_ref, h_ref, e_ref, mu_ref, dst in ((zr, hr, er, mur, r_s), (zk, hk, ek, muk, k_s), (zv, hv, ev, muv, v_s),
                                             (zl, hl, el, mul, xl_s)):
        e_ref[0:halo, :] = jnp.where(t == 0, 0.0, h_ref[...])
        e_ref[halo:halo + tb, :] = z_ref[...]
        cur = z_ref[...]
        dst[...] = cur + (e_ref[halo - 1:halo - 1 + tb, :] - cur) * mu_ref[...]

    c = _scan_consts()
    ones_bd = _head_block_matrix(1.0, head_dim)
    avg_bd = _head_block_matrix(1.0 / head_dim, head_dim)

    xl = xl_s[...]
    wl = w0[...] + _mm1(_dot, jnp.tanh(xl), w2p[...])
    a = jax.nn.sigmoid(a0[...] + _mm1(_dot, xl, a2p[...]))
    ld_s[...] = -_EXP_MINUS_HALF * jax.nn.sigmoid(wl)
    xk = k_s[...]
    k_s[...] = xk * (1.0 + (a - 1.0) * ka[...])
    kp = [xk[:, cols[q]] * kks[:, cols[q]] for q in pairs]
    ss = [_mm_exact_rhs(kp[q] * kp[q], ones_bd) for q in pairs]
    for q in pairs:
        kk = kp[q] * lax.rsqrt(jnp.maximum(ss[q], _L2_EPS * _L2_EPS))
        kk_s[:, cols[q]] = kk
        be_s[:, cols[q]] = kk * a[:, cols[q]]

    def body(ci, carry):
        rows = pl.ds(pl.multiple_of(ci * _CHUNK, _CHUNK), _CHUNK)
        in_refs = (r_s, ld_s, k_s, v_s, kk_s, be_s)
        ys, new_states = _scan_chunks([tuple(ref[rows, cols[q]] for ref in in_refs) for q in pairs],
                                      [st_ref[q] for q in pairs], c)
        for q in pairs:
            y_s[rows, cols[q]] = ys[q]
            st_ref[q] = new_states[q]
        return carry

    lax.fori_loop(0, n_chunks, body, 0)

    q0, off = delta // _LANES, delta % _LANES
    n_a = za.shape[1] // _LANES
    zblk = lambda i: (za[:, i * _LANES:(i + 1) * _LANES] if i < n_a
                      else zb[:, (i - n_a) * _LANES:(i - n_a + 1) * _LANES])
    y = [y_s[:, cols[q]] for q in pairs]
    d = [y[q] - _mm_exact_rhs(y[q], avg_bd) for q in pairs]
    var = [_mm_exact_rhs(d[q] * d[q], avg_bd) for q in pairs]
    bsum = [_mm_exact_rhs(r_s[:, cols[q]] * k_s[:, cols[q]] * rk[:, cols[q]], ones_bd) for q in pairs]
    for q in pairs:
        yn = d[q] * lax.rsqrt(var[q] + _GN_EPS) * lnw[:, cols[q]] + lnb[:, cols[q]]
        zg = jnp.concatenate([zblk(q0 + q), zblk(q0 + q + 1)], axis=1)[:, off:off + _LANES]
        o_ref[:, cols[q]] = ((yn + bsum[q] * v_s[:, cols[q]]) * _silu(zg)).astype(o_ref.dtype)

    @pl.when(t == pl.num_programs(2) - 1)
    def _():
        s_ref[0] = st_ref[...]


def _rwkv_prompt(z, mix_consts, post_consts, *, batch, seq, width, tb, pp, head_dim, col0, gate_col0, lw):
    n_t = seq // tb
    gw = pp * _LANES
    halo = 8
    tail = 2 * _LANES
    c0, lcol, cg = col0 // gw, (col0 + 3 * width) // lw, gate_col0 // gw
    delta = gate_col0 % gw
    assert c0 * gw == col0 and width % gw == 0 and lcol * lw == col0 + 3 * width
    assert delta <= tail and delta % _LANES + _LANES <= tail and gw % tail == 0
    row = lambda b, g, t: b * n_t + t
    hrow = lambda b, g, t: jnp.maximum((b * seq + t * tb) // halo - 1, 0)
    step = width // gw
    zblk = lambda i: pl.BlockSpec((tb, gw), lambda b, g, t, i=i: (row(b, g, t), c0 + i * step + g))
    hblk = lambda i: pl.BlockSpec((halo, gw), lambda b, g, t, i=i: (hrow(b, g, t), c0 + i * step + g))
    vec = pl.BlockSpec((1, gw), lambda b, g, t: (0, g))
    lvec = pl.BlockSpec((1, lw), lambda b, g, t: (0, 0))
    mat = pl.BlockSpec((lw, gw), lambda b, g, t: (0, g))
    mur, muk, muv, mul, w0, a0, kks, ka, w2p, a2p = mix_consts
    big = lambda: pltpu.VMEM((tb, gw), _F32)
    kern = functools.partial(_rwkv_prompt_kernel, tb=tb, pp=pp, head_dim=head_dim, delta=delta)
    return pl.pallas_call(
        kern,
        grid=(batch, width // gw, n_t),
        in_specs=[zblk(0), zblk(1), zblk(2), pl.BlockSpec((tb, lw), lambda b, g, t: (row(b, g, t), lcol)),
                  hblk(0), hblk(1), hblk(2), pl.BlockSpec((halo, lw), lambda b, g, t: (hrow(b, g, t), lcol)),
                  pl.BlockSpec((tb, gw), lambda b, g, t: (row(b, g, t), cg + g)),
                  pl.BlockSpec((tb, tail), lambda b, g, t: (row(b, g, t), (cg + g + 1) * (gw // tail))),
                  vec, vec, vec, lvec, vec, vec, vec, vec, mat, mat, vec, vec, vec],
        out_specs=[pl.BlockSpec((tb, gw), lambda b, g, t: (row(b, g, t), g)),
                   pl.BlockSpec((1, pp, _LANES, _LANES), lambda b, g, t: (b, g, 0, 0))],
        out_shape=[jax.ShapeDtypeStruct((batch * seq, width), _BF16),
                   jax.ShapeDtypeStruct((batch, width // _LANES, _LANES, _LANES), _F32)],
        scratch_shapes=[pltpu.VMEM((halo + tb, gw), _F32)] * 3 + [pltpu.VMEM((halo + tb, lw), _F32)]
                       + [big() for _ in range(7)] + [pltpu.VMEM((tb, lw), _F32),
                                                      pltpu.VMEM((pp, _LANES, _LANES), _F32)],
        compiler_params=_params(("parallel", "parallel", "arbitrary"), vmem_limit=_VMEM_LIMIT_RESIDENT_WEIGHT),
        name="rwkv_prompt",
    )(z, z, z, z, z, z, z, z, z, z, mur, muk, muv, mul, w0, a0, kks, ka, w2p, a2p, *post_consts)


def _rwkv_step_kernel(r_ref, ld_ref, k_ref, v_ref, kk_ref, be_ref, s_ref, y_ref, so_ref, *, hb, head_dim, vi):
    for h in range(hb):
        kk = kk_ref[h][None]
        dec = jnp.exp(ld_ref[h])[None]
        be = be_ref[h][None]
        k = k_ref[h][None]
        r = r_ref[h][None]

        def body(c, carry, h=h, kk=kk, dec=dec, be=be, k=k, r=r):
            rows = pl.ds(pl.multiple_of(c * vi, vi), vi)
            s = s_ref[h, rows]
            sa = -jnp.sum(s * kk, axis=1, keepdims=True)
            s_new = s * dec + sa * be + v_ref[h, rows] * k
            y_ref[h, rows] = jnp.sum(s_new * r, axis=1, keepdims=True)
            so_ref[h, rows] = s_new
            return carry

        lax.fori_loop(0, head_dim // vi, body, 0)


def _rwkv_step(arrs_t, state_t, *, hb):
    heads, hd, _, n = state_t.shape
    r, ld, k, v, kk, be = arrs_t
    per_key = lambda a: a.reshape(heads, hd, n)
    vec = pl.BlockSpec((hb, hd, n), lambda i: (i, 0, 0))
    col = pl.BlockSpec((hb, hd, 1, n), lambda i: (i, 0, 0, 0))
    mat = pl.BlockSpec((hb, hd, hd, n), lambda i: (i, 0, 0, 0))
    kern = functools.partial(_rwkv_step_kernel, hb=hb, head_dim=hd, vi=8)
    return pl.pallas_call(
        kern,
        grid=(heads // hb,),
        in_specs=[vec, vec, vec, col, vec, vec, mat],
        out_specs=[col, mat],
        out_shape=[jax.ShapeDtypeStruct((heads, hd, 1, n), _F32), jax.ShapeDtypeStruct(state_t.shape, _F32)],
        compiler_params=_params(("parallel",)),
        name="rwkv_step",
    )(per_key(r), per_key(ld), per_key(k), v.reshape(heads, hd, 1, n), per_key(kk), per_key(be), state_t)


def _rwkv_post_kernel(y_ref, r_ref, k_ref, v_ref, za_ref, zb_ref, rk_ref, lw_ref, lb_ref, o_ref, *, head_dim, delta):
    avg_bd = _head_block_matrix(1.0 / head_dim, head_dim)
    ones_bd = _head_block_matrix(1.0, head_dim)
    width = y_ref.shape[1]
    zwin = jnp.concatenate([za_ref[...], zb_ref[...]], axis=1)
    for p in range(width // _LANES):
        cols = slice(p * _LANES, (p + 1) * _LANES)
        y = y_ref[:, cols]
        d = y - _mm_exact_rhs(y, avg_bd)
        var = _mm_exact_rhs(d * d, avg_bd)
        yn = d * lax.rsqrt(var + _GN_EPS) * lw_ref[:, cols] + lb_ref[:, cols]
        bonus = _mm_exact_rhs(r_ref[:, cols] * k_ref[:, cols] * rk_ref[:, cols], ones_bd) * v_ref[:, cols]
        zg = zwin[:, delta + p * _LANES:delta + (p + 1) * _LANES]
        o_ref[:, cols] = ((yn + bonus) * _silu(zg)).astype(o_ref.dtype)


def _rwkv_post(y, r, k2, v, z, consts, *, row0, rows, width, tb, head_dim, col0):
    off = row0 // tb
    tail = 2 * _LANES
    cg, delta = col0 // width, col0 % width
    assert delta <= tail and width % tail == 0
    blk = pl.BlockSpec((tb, width), lambda i: (i, 0))
    vec = pl.BlockSpec((1, width), lambda i: (0, 0))
    kern = functools.partial(_rwkv_post_kernel, head_dim=head_dim, delta=delta)
    return pl.pallas_call(
        kern,
        grid=(rows // tb,),
        in_specs=[blk, blk, blk, blk,
                  pl.BlockSpec((tb, width), lambda i: (off + i, cg)),
                  pl.BlockSpec((tb, tail), lambda i: (off + i, (cg + 1) * (width // tail))),
                  vec, vec, vec],
        out_specs=blk,
        out_shape=jax.ShapeDtypeStruct((rows, width), _BF16),
        compiler_params=_params(("parallel",)),
        name="rwkv_post",
    )(y, r, k2, v, z, z, *consts)


def _mem_attn_prompt_kernel(za_ref, zb_ref, zc_ref, k_ref, v_ref, o_ref, *, heads, head_dim, delta):
    scale = head_dim ** -0.5
    mw = heads * head_dim
    zwin = jnp.concatenate([za_ref[...], zb_ref[...], zc_ref[...]], axis=1)
    for h in range(heads):
        cols = slice(h * head_dim, (h + 1) * head_dim)
        q = zwin[:, delta + cols.start:delta + cols.stop]
        zg = zwin[:, delta + mw + cols.start:delta + mw + cols.stop]
        s = _dot_nt(q.astype(_BF16), k_ref[:, cols].astype(_BF16)) * scale
        p = jnp.exp(s - jnp.max(s, axis=-1, keepdims=True))
        o = _dot(p.astype(_BF16), v_ref[:, cols].astype(_BF16)) / jnp.sum(p, axis=-1, keepdims=True)
        o_ref[:, cols] = (o * _silu(zg)).astype(o_ref.dtype)


def _mem_attn_prompt(z, mkv, *, batch, seq, n_mem, heads, head_dim, tq, col0):
    n_t = seq // tq
    mw = heads * head_dim
    tail = 2 * _LANES
    q_col, delta = col0 // mw, col0 % mw
    assert delta <= tail and mw % tail == 0
    kern = functools.partial(_mem_attn_prompt_kernel, heads=heads, head_dim=head_dim, delta=delta)
    return pl.pallas_call(
        kern,
        grid=(batch, n_t),
        in_specs=[
            pl.BlockSpec((tq, mw), lambda b, t: (b * n_t + t, q_col)),
            pl.BlockSpec((tq, mw), lambda b, t: (b * n_t + t, q_col + 1)),
            pl.BlockSpec((tq, tail), lambda b, t: (b * n_t + t, (q_col + 2) * (mw // tail))),
            pl.BlockSpec((n_mem, mw), lambda b, t: (b, 0)),
            pl.BlockSpec((n_mem, mw), lambda b, t: (b, 1)),
        ],
        out_specs=pl.BlockSpec((tq, mw), lambda b, t: (b * n_t + t, 0)),
        out_shape=jax.ShapeDtypeStruct((batch * seq, mw), _BF16),
        compiler_params=_params(("parallel", "parallel")),
        name="mem_attn_prompt",
    )(z, z, z, mkv, mkv)


def _mem_attn_sample_kernel(q_ref, zg_ref, k_ref, v_ref, o_ref, *, bb, head_dim):
    scale = head_dim ** -0.5
    for b in range(bb):
        s = jnp.sum(k_ref[b] * q_ref[b][None], axis=-1, keepdims=True) * scale
        p = jnp.exp(s - jnp.max(s, axis=0, keepdims=True))
        o = jnp.sum(p * v_ref[b], axis=0) / jnp.sum(p, axis=0)
        o_ref[b] = o * _silu(zg_ref[b])


def _mem_attn_sample(q, zg, mem_k, mem_v, *, bb):
    n, n_mem, heads, head_dim = mem_k.shape
    vec = pl.BlockSpec((bb, heads, head_dim), lambda i: (i, 0, 0))
    mat = pl.BlockSpec((bb, n_mem, heads, head_dim), lambda i: (i, 0, 0, 0))
    kern = functools.partial(_mem_attn_sample_kernel, bb=bb, head_dim=head_dim)
    return pl.pallas_call(
        kern,
        grid=(n // bb,),
        in_specs=[vec, vec, mat, mat],
        out_specs=vec,
        out_shape=jax.ShapeDtypeStruct((n, heads, head_dim), _F32),
        compiler_params=_params(("parallel",)),
        name="mem_attn_sample",
    )(q, zg, mem_k, mem_v)


def _branch_kernel(po_ref, ro_ref, mo_ref, pos_ref, ros_ref, mos_ref, wp_ref, wr_ref, wm_ref, gp_ref, gr_ref, gm_ref,
                   tp_ref, tr_ref, tm_ref, bp_ref, br_ref, bm_ref, o_ref, os_ref, wpb_ref, wrb_ref, wmb_ref,
                   *, delta, row_tiles):
    i = pl.program_id(1)

    @pl.when(i == 0)
    def _():
        wpb_ref[...] = wp_ref[...].astype(_BF16)
        wrb_ref[...] = wr_ref[...].astype(_BF16)
        wmb_ref[...] = wm_ref[...].astype(_BF16)

    tn = o_ref.shape[1]

    def gated_sum(acts, rows):
        h = None
        for a_ref, wb_ref, g_ref, t_ref, b_ref in zip(acts, (wpb_ref, wrb_ref, wmb_ref), (gp_ref, gr_ref, gm_ref),
                                                      (tp_ref, tr_ref, tm_ref), (bp_ref, br_ref, bm_ref)):
            win = jnp.concatenate([g_ref[0:rows, :], t_ref[0:rows, :]], axis=1)
            term = jax.nn.sigmoid(win[:, delta:delta + tn] + b_ref[...]) * _dot(a_ref[...], wb_ref[...])
            h = term if h is None else h + term
        return h

    @pl.when(i < row_tiles)
    def _():
        o_ref[...] = gated_sum((po_ref, ro_ref, mo_ref), o_ref.shape[0]).astype(o_ref.dtype)

    @pl.when(i == row_tiles)
    def _():
        os_ref[...] = gated_sum((pos_ref, ros_ref, mos_ref), os_ref.shape[0]).astype(os_ref.dtype)


def _branch(acts_p, acts_s, wp, wr, wm, z, b_gate, *, tm, tn, gate_col0):
    rows, ns = acts_p[0].shape[0], acts_s[0].shape[0]
    d = wp.shape[1]
    tail = 2 * _LANES
    delta = gate_col0 % tn
    g0 = gate_col0 // tn
    nb = d // tn
    nt = rows // tm
    assert delta <= tail and tn % tail == 0 and nt * tm == rows and ns <= tm and z.shape[0] >= rows + ns
    act = lambda a: pl.BlockSpec((tm, a.shape[1]), lambda j, i: (jnp.minimum(i, nt - 1), 0))
    act_s = lambda a: pl.BlockSpec((ns, a.shape[1]), lambda j, i: (0, 0))
    wgt = lambda w: pl.BlockSpec((w.shape[0], tn), lambda j, i: (0, j), pipeline_mode=pl.Buffered(1))
    gat = lambda k: pl.BlockSpec((tm, tn), lambda j, i, k=k: (i, g0 + k * nb + j))
    tai = lambda k: pl.BlockSpec((tm, tail), lambda j, i, k=k: (i, (g0 + k * nb + j + 1) * (tn // tail)))
    bia = lambda k: pl.BlockSpec((1, tn), lambda j, i, k=k: (0, k * nb + j))
    return pl.pallas_call(
        functools.partial(_branch_kernel, delta=delta, row_tiles=nt),
        grid=(nb, nt + 1),
        in_specs=[act(acts_p[0]), act(acts_p[1]), act(acts_p[2]), act_s(acts_s[0]), act_s(acts_s[1]), act_s(acts_s[2]),
                  wgt(wp), wgt(wr), wgt(wm), gat(0), gat(1), gat(2), tai(0), tai(1), tai(2), bia(0), bia(1), bia(2)],
        out_specs=[pl.BlockSpec((tm, tn), lambda j, i: (jnp.minimum(i, nt - 1), j)),
                   pl.BlockSpec((ns, tn), lambda j, i: (0, j))],
        out_shape=[jax.ShapeDtypeStruct((rows, d), _BF16), jax.ShapeDtypeStruct((ns, d), _BF16)],
        scratch_shapes=[pltpu.VMEM((w.shape[0], tn), _BF16) for w in (wp, wr, wm)],
        compiler_params=_params(("parallel", "arbitrary")),
        name="branch_proj",
    )(*acts_p, *acts_s, wp, wr, wm, z, z, z, z, z, z, b_gate, b_gate, b_gate)


def _out_kernel(h_ref, w_ref, x_ref, g_ref, b_ref, o_ref, *, alpha):
    xf = alpha * x_ref[...] + _dot(h_ref[...], w_ref[...])
    mu = jnp.mean(xf, axis=-1, keepdims=True)
    d = xf - mu
    var = jnp.mean(d * d, axis=-1, keepdims=True)
    o_ref[...] = d * lax.rsqrt(var + _LN_EPS) * g_ref[...] + b_ref[...]


def _out_proj(h, w_out, x, ln_g, ln_b, *, tm, alpha):
    rows, d = x.shape
    kern = functools.partial(_out_kernel, alpha=alpha)
    return pl.pallas_call(
        kern,
        grid=(rows // tm,),
        in_specs=[
            pl.BlockSpec((tm, d), lambda i: (i, 0)),
            pl.BlockSpec((d, d), lambda i: (0, 0), pipeline_mode=pl.Buffered(1)),
            pl.BlockSpec((tm, d), lambda i: (i, 0)),
            pl.BlockSpec((1, d), lambda i: (0, 0)),
            pl.BlockSpec((1, d), lambda i: (0, 0)),
        ],
        out_specs=pl.BlockSpec((tm, d), lambda i: (i, 0)),
        out_shape=jax.ShapeDtypeStruct((rows, d), _F32),
        compiler_params=_params(("parallel",), vmem_limit=_VMEM_LIMIT_RESIDENT_WEIGHT),
        name="out_proj_ln",
    )(h, w_out, x, ln_g, ln_b)


def _tiles(batch, seq, n_sample):
    n_prompt = batch * seq
    m_all = n_prompt + n_sample
    return {
        "in_tm": next(t for t in (1040, 832, 640, 512, 256, 128, 64, 32, 16) if m_all % t == 0), "in_tn": 768,
        "row_tb": min(512, seq),
        "scan_tb": min(256, seq), "scan_pp": 16,
        "attn_tq": min(1024, seq),
        "proj_tm": min(256, n_prompt), "proj_tn": 1024, "out_tm": min(256, n_prompt),
        "pool_bb": min(32, n_sample), "step_hb": 4, "attn_bb": 4,
    }


def _layer(xp, xs, mem_p, cache_k, cache_v, st_pool, st_shift, st_rwkv, lp):
    batch, seq, d = xp.shape
    ns = xs.shape[0]
    n_p = batch * seq
    m_all = n_p + ns
    pw = lp["pool_scale"].shape[-1]
    rw = lp["rwkv_w0"].shape[-1]
    heads, hd = lp["rwkv_r_k"].shape
    lora = lp["rwkv_w2"].shape[0]
    sw = lp["rwkv_mu"].shape[-1]
    n_mem, mh, md = cache_k.shape[1:]
    mw = mh * md
    tl = _tiles(batch, seq, ns)

    lo0, lo1 = 2 * pw + 3 * rw, 2 * pw + sw
    lw = 2 * _LANES
    assert 2 * lora <= lw
    q_col0 = lo1 + rw
    gate_col0 = q_col0 + 2 * mw

    if ns < tl["in_tm"] <= n_p:
        x_all = _cast_rows(xp.reshape(n_p, d), xs.reshape(ns, d), tm=tl["in_tm"])
    else:
        x_all = jnp.concatenate([xp.reshape(n_p, d), xs.reshape(ns, d)], axis=0).astype(_BF16)
    z = _in_proj(x_all, jnp.swapaxes(lp["w_in"], 0, 1), tm=tl["in_tm"], tn=tl["in_tn"])
    z_cols_s = lambda lo, hi: z[n_p:, lo:hi]

    pool_w = lp["pool_w"].astype(_BF16)
    pool_scale = lp["pool_scale"].reshape(1, pw)
    po_p = _pool_prompt(z, pool_w, pool_scale, batch=batch, seq=seq, width=pw, tb=tl["row_tb"])
    po_s = _pool_sample(z, jnp.swapaxes(st_pool, 0, 1), pool_w, pool_scale, row0=n_p, bb=tl["pool_bb"])
    nbuf = st_pool.shape[1]
    new_pool_p = jnp.stack([z[(b + 1) * seq - nbuf:(b + 1) * seq, :pw] for b in range(batch)])
    new_pool_s = jnp.concatenate([st_pool[:, 1:, :], z_cols_s(0, pw).reshape(ns, 1, pw)], axis=1)

    mu = lp["rwkv_mu"]
    pad_l = lambda a: jnp.pad(a, ((0, 0), (0, lw - 2 * lora)))
    w2p = jnp.zeros((lw, rw), _F32).at[:lora].set(lp["rwkv_w2"]).astype(_BF16)
    a2p = jnp.zeros((lw, rw), _F32).at[lora:2 * lora].set(lp["rwkv_a2"]).astype(_BF16)
    row = lambda a: a.reshape(1, -1)
    mix_consts = (row(mu[:rw]), row(mu[rw:2 * rw]), row(mu[2 * rw:3 * rw]), pad_l(row(mu[3 * rw:])),
                  row(lp["rwkv_w0"]), row(lp["rwkv_a0"]), row(lp["rwkv_k_k"]), row(lp["rwkv_k_a"]), w2p, a2p)
    post_consts = (row(lp["rwkv_r_k"]), row(lp["rwkv_ln_w"]), row(lp["rwkv_ln_b"]))

    ro_p, st_pairs = _rwkv_prompt(z, mix_consts, post_consts, batch=batch, seq=seq, width=rw, tb=tl["scan_tb"],
                                  pp=tl["scan_pp"], head_dim=hd, col0=2 * pw, gate_col0=lo1, lw=lw)
    st5 = st_pairs.reshape(batch, heads // 2, 2, hd, 2, hd)
    new_state_p = jnp.stack([st5[:, :, 0, :, 0, :], st5[:, :, 1, :, 1, :]], axis=2).reshape(batch, heads, hd, hd)

    shift_s = st_shift.reshape(ns, sw)
    prep_s = _rwkv_prep_sample(z, shift_s[:, :3 * rw], pad_l(shift_s[:, 3 * rw:]), mix_consts, row0=n_p, rows=ns,
                               width=rw, head_dim=hd, col0=2 * pw, lw=lw)
    r_s, _, k_s, v_s, _, _ = prep_s[:6]
    y_t, state_t = _rwkv_step(prep_s[6:], jnp.transpose(st_rwkv, (1, 2, 3, 0)), hb=tl["step_hb"])
    new_state_s = jnp.transpose(state_t, (3, 0, 1, 2))
    ro_s = _rwkv_post(y_t.reshape(rw, ns).T, r_s, k_s, v_s, z, post_consts, row0=n_p, rows=ns, width=rw, tb=ns,
                      head_dim=hd, col0=lo1)
    new_shift_p = jnp.stack([z[(b + 1) * seq - 1:(b + 1) * seq, 2 * pw:lo1] for b in range(batch)])
    new_shift_s = z_cols_s(2 * pw, lo1).reshape(ns, 1, sw)

    mkv = _matmul_f32w(mem_p.reshape(batch * n_mem, d).astype(_BF16), lp["w_mem_kv"],
                       tm=next(t for t in (512, 256, 128, 64, 32, 16, 8) if (batch * n_mem) % t == 0),
                       tn=min(768, 2 * mw), name="mem_kv_proj")
    mo_p = _mem_attn_prompt(z, mkv, batch=batch, seq=seq, n_mem=n_mem, heads=mh, head_dim=md, tq=tl["attn_tq"],
                            col0=q_col0)
    mo_s = _mem_attn_sample(z_cols_s(q_col0, q_col0 + mw).reshape(ns, mh, md),
                            z_cols_s(q_col0 + mw, q_col0 + 2 * mw).reshape(ns, mh, md),
                            cache_k, cache_v, bb=tl["attn_bb"]).reshape(ns, mw).astype(_BF16)
    mk_p = mkv[:, :mw].reshape(batch, n_mem, mh, md)
    mv_p = mkv[:, mw:].reshape(batch, n_mem, mh, md)

    wp, wr, wm = lp["w_branch_pool"], lp["w_branch_rwkv"], lp["w_branch_mem"]
    w_out = lp["w_out"].astype(_BF16)
    b_gate = row(lp["b_gate"])
    ln_g, ln_b = row(lp["ln_g"]), row(lp["ln_b"])
    alpha = lp["alpha"]
    h_p, h_s = _branch((po_p, ro_p, mo_p), (po_s, ro_s, mo_s), wp, wr, wm, z, b_gate, tm=tl["proj_tm"],
                       tn=tl["proj_tn"], gate_col0=gate_col0)
    y_prompt = _out_proj(h_p, w_out, xp.reshape(n_p, d), ln_g, ln_b, tm=tl["out_tm"], alpha=alpha)
    y_sample = _out_proj(h_s, w_out, xs.reshape(ns, d), ln_g, ln_b, tm=min(tl["out_tm"], ns), alpha=alpha)
    y_prompt = y_prompt.reshape(batch, seq, d)
    y_sample = y_sample.reshape(ns, 1, d)
    return (y_prompt, y_sample, mk_p, mv_p, new_pool_p, new_shift_p, new_state_p, new_pool_s, new_shift_s,
            new_state_s)


def kernel(x_prompt, x_sample, cache_mem_k, cache_mem_v, state_pool, state_shift, state_rwkv, mem_prompt, w_in,
           b_gate, pool_w, pool_scale, rwkv_mu, rwkv_w0, rwkv_w2, rwkv_a0, rwkv_a2, rwkv_k_k, rwkv_k_a, rwkv_r_k,
           rwkv_ln_w, rwkv_ln_b, w_mem_kv, w_branch_pool, w_branch_rwkv, w_branch_mem, w_out, ln_g, ln_b):
    depth = w_in.shape[0]
    alpha = (2.0 * depth) ** 0.25
    weights = dict(w_in=w_in, b_gate=b_gate, pool_w=pool_w, pool_scale=pool_scale, rwkv_mu=rwkv_mu, rwkv_w0=rwkv_w0,
                   rwkv_w2=rwkv_w2, rwkv_a0=rwkv_a0, rwkv_a2=rwkv_a2, rwkv_k_k=rwkv_k_k, rwkv_k_a=rwkv_k_a,
                   rwkv_r_k=rwkv_r_k, rwkv_ln_w=rwkv_ln_w, rwkv_ln_b=rwkv_ln_b, w_mem_kv=w_mem_kv,
                   w_branch_pool=w_branch_pool, w_branch_rwkv=w_branch_rwkv, w_branch_mem=w_branch_mem,
                   w_out=w_out, ln_g=ln_g, ln_b=ln_b)
    yp, ys = x_prompt, x_sample
    per_layer = []
    for l in range(depth):
        lp = {k: v[l] for k, v in weights.items()}
        lp["alpha"] = alpha
        res = _layer(yp, ys, mem_prompt, cache_mem_k[l], cache_mem_v[l], state_pool[l], state_shift[l],
                     state_rwkv[l], lp)
        yp, ys = res[0], res[1]
        per_layer.append(res[2:])
    if depth == 1:
        stacked = tuple(a[None] for a in per_layer[0])
    else:
        stacked = tuple(jnp.stack([res[i] for res in per_layer]) for i in range(len(per_layer[0])))
    return (yp, ys) + stacked
```
